```python
import math
import jax, jax.numpy as jnp
from jax import lax
import numpy as np

D_MODEL = 1024
BATCH = 16
SEQ = 256
DEPTH = 1
DEC_BATCH = 8
DEC_SEQ = 1024
PAST_LEN = 256

GRID_W = 64
MIX_W = D_MODEL
N_DIR = 2
SSD_W = MIX_W // 2
SSD_HEAD_DIM = 64
SSD_HEADS = SSD_W // SSD_HEAD_DIM
SSD_GROUPS = 2
SSD_STATE = 64
SSD_CHUNK = 128
SSD_CONV = 3
SSD_XBC = SSD_W + 2 * SSD_GROUPS * SSD_STATE
HY_W = MIX_W - SSD_W
HY_ORDER = 2
HY_CONV = 3
HY_EMB = 33
HY_BANDS = (HY_EMB - 1) // 2
HY_FILTER_HIDDEN = 64
HY_DECAY_TARGET = 1e-2
HY_FAST_PCT = 0.3
HY_SLOW_PCT = 1.5
HY_MIN_DECAY = math.log(HY_DECAY_TARGET) / HY_SLOW_PCT
HY_MAX_DECAY = math.log(HY_DECAY_TARGET) / HY_FAST_PCT
FFN_DIM = 2752
N_MOD = 9
RMS_EPS = 1e-6
IN_COLS = SSD_W + SSD_XBC + N_DIR * SSD_HEADS + (HY_ORDER + 1) * HY_W

kernel_name = 'hybrid_ssd_hyena_macaron_prefix_step'


def rmsnorm(x, g):
    xf = x.astype(jnp.float32)
    y = xf * lax.rsqrt(jnp.mean(xf * xf, axis=-1, keepdims=True) + RMS_EPS)
    return (y * g.astype(jnp.float32)).astype(x.dtype)


def modulate(h, shift, scale):
    return h * (1.0 + scale) + shift


def swiglu(h, w_gate, w_up, w_down):
    return jnp.dot(jax.nn.silu(jnp.dot(h, w_gate)) * jnp.dot(h, w_up), w_down)


def short_conv(x, w, b, n_rows):
    bsz, l, ch = x.shape
    k = w.shape[0]
    xr = x.reshape(bsz * n_rows, l // n_rows, ch)
    y = lax.conv_general_dilated(xr, w[:, None, :].astype(x.dtype), window_strides=(1,),
                                 padding=[((k - 1) // 2, k // 2)],
                                 dimension_numbers=('NWC', 'WIO', 'NWC'), feature_group_count=ch)
    return (y + b).reshape(bsz, l, ch)


def segsum(a):
    t = a.shape[-1]
    x = jnp.broadcast_to(a[..., :, None], a.shape + (t,))
    x = jnp.where(jnp.tril(jnp.ones((t, t), bool), -1), x, 0.0)
    s = jnp.cumsum(x, axis=-2)
    return jnp.where(jnp.tril(jnp.ones((t, t), bool)), s, -jnp.inf)


def ssd_chunked(xh, dt, a_coef, bh, ch, init):
    bsz, l, nh, hd = xh.shape
    nc = l // SSD_CHUNK

    def chunk(t):
        return t.reshape((bsz, nc, SSD_CHUNK) + t.shape[2:])

    xc = chunk(xh * dt[..., None])
    bc, cc = chunk(bh), chunk(ch)
    ac = chunk(dt * a_coef).transpose(0, 1, 3, 2)
    a_cum = jnp.cumsum(ac, axis=-1)
    scores = jnp.einsum('bclhn,bcshn->bchls', cc, bc) * jnp.exp(segsum(ac))
    y_diag = jnp.einsum('bchls,bcshp->bclhp', scores, xc)
    decay_states = jnp.exp(a_cum[..., -1:] - a_cum)
    states = jnp.einsum('bcshn,bchs,bcshp->bchpn', bc, decay_states, xc)
    states = jnp.concatenate([init[:, None], states], axis=1)
    a_chunk = jnp.pad(a_cum[..., -1], ((0, 0), (1, 0), (0, 0))).transpose(0, 2, 1)
    states = jnp.einsum('bhzc,bchpn->bzhpn', jnp.exp(segsum(a_chunk)), states)
    prev_states, final_state = states[:, :-1], states[:, -1]
    y_off = jnp.einsum('bclhn,bchpn,bchl->bclhp', cc, prev_states, jnp.exp(a_cum))
    return (y_diag + y_off).reshape(bsz, l, nh, hd), final_state


def ssd_mixer(z, xbc, dt_raw, init, n_rows, p):
    f32 = jnp.float32
    bsz, l, _ = z.shape
    xbc = jax.nn.silu(short_conv(xbc, p['ssd_conv_w'], p['ssd_conv_b'], n_rows)).astype(f32)
    xs, bm, cm = jnp.split(xbc, [SSD_W, SSD_W + SSD_GROUPS * SSD_STATE], axis=-1)
    hpg = SSD_HEADS // SSD_GROUPS
    xh = xs.reshape(bsz, l, SSD_HEADS, SSD_HEAD_DIM)
    bh = jnp.repeat(bm.reshape(bsz, l, SSD_GROUPS, SSD_STATE), hpg, axis=2)
    ch = jnp.repeat(cm.reshape(bsz, l, SSD_GROUPS, SSD_STATE), hpg, axis=2)
    dt = jax.nn.softplus(dt_raw.astype(f32).reshape(bsz, l, N_DIR, SSD_HEADS)
                         + p['ssd_dt_bias'].astype(f32))
    a_coef = -jnp.exp(p['ssd_a_log'].astype(f32))
    init = init.astype(f32)
    y_f, s_f = ssd_chunked(xh, dt[:, :, 0], a_coef[0], bh, ch, init[:, 0])
    y_b, s_b = ssd_chunked(jnp.flip(xh, 1), jnp.flip(dt[:, :, 1], 1), a_coef[1],
                           jnp.flip(bh, 1), jnp.flip(ch, 1), init[:, 1])
    y = y_f + jnp.flip(y_b, 1) + xh * p['ssd_d'].astype(f32)[:, None]
    y = y.reshape(bsz, l, SSD_W).astype(z.dtype) * jax.nn.silu(z)
    return rmsnorm(y, p['ssd_norm_w']), jnp.stack([s_f, s_b], axis=1)


def hyena_filters(l, p):
    f32 = jnp.float32
    t = jnp.linspace(0.0, 1.0, l, dtype=f32)[:, None]
    w = (2.0 * math.pi / l) * jnp.arange(l, dtype=f32)[:, None]
    f = jnp.linspace(1e-4, HY_BANDS - 1, HY_BANDS, dtype=f32)[None, :]
    feats = jnp.concatenate([t, jnp.cos(f * w), -jnp.sin(f * w)], axis=-1)
    freq = p['hy_freq'].astype(f32)
    h = jnp.sin(freq * (jnp.dot(feats, p['hy_w1'].astype(f32)) + p['hy_b1'].astype(f32)))
    h = jnp.sin(freq * (jnp.dot(h, p['hy_w2'].astype(f32)) + p['hy_b2'].astype(f32)))
    h = jnp.dot(h, p['hy_w3'].astype(f32)).reshape(l, N_DIR, HY_ORDER, HY_W)
    deltas = jnp.abs(jnp.linspace(HY_MIN_DECAY, HY_MAX_DECAY, HY_ORDER * HY_W, dtype=f32))
    window = jnp.exp(-t[:, :, None] * deltas.reshape(HY_ORDER, HY_W))
    return h * window[:, None]


def hyena_mixer(u, n_rows, p):
    bsz, l, _ = u.shape
    u = short_conv(u, p['hy_conv_w'], p['hy_conv_b'], n_rows).astype(jnp.float32)
    v, x1, x2 = jnp.split(u, 3, axis=-1)
    k = hyena_filters(l, p)
    k2 = jnp.concatenate([k[:, 0], jnp.zeros_like(k[:1, 0]), k[:0:-1, 1]], axis=0)
    kf = jnp.fft.rfft(k2, axis=0)
    skip = p['hy_skip'].astype(jnp.float32)
    zz = v
    for i, gate in enumerate((x1, x2)):
        sf = jnp.fft.rfft(zz, n=2 * l, axis=1)
        conv = jnp.fft.irfft(sf * kf[:, i], n=2 * l, axis=1)[:, :l]
        zz = gate * (conv + zz * skip[i])
    return zz


def token_mix(h, ssd_init, n_rows, p):
    proj = jnp.dot(h, p['w_in'])
    z, xbc, dt_raw, hy_in = jnp.split(
        proj, [SSD_W, SSD_W + SSD_XBC, SSD_W + SSD_XBC + N_DIR * SSD_HEADS], axis=-1)
    y_ssd, final_state = ssd_mixer(z, xbc, dt_raw, ssd_init, n_rows, p)
    y_hy = hyena_mixer(hy_in, n_rows, p).astype(h.dtype)
    return jnp.dot(jnp.concatenate([y_ssd, y_hy], axis=-1), p['w_out']), final_state


def trunk_layer(x, cond, ssd_init, n_rows, p):
    mod = jnp.dot(jax.nn.silu(cond), p['w_ada']) + p['b_ada']
    sh1, sc1, g1, sh2, sc2, g2, sh3, sc3, g3 = jnp.split(mod[:, None, :], N_MOD, axis=-1)
    h = modulate(rmsnorm(x, p['norm_ffn1']), sh1, sc1)
    x = x + 0.5 * g1 * swiglu(h, p['ffn1_w_gate'], p['ffn1_w_up'], p['ffn1_w_down'])
    h = modulate(rmsnorm(x, p['norm_mix']), sh2, sc2)
    mix, final_state = token_mix(h, ssd_init, n_rows, p)
    x = x + g2 * mix
    h = modulate(rmsnorm(x, p['norm_ffn2']), sh3, sc3)
    x = x + 0.5 * g3 * swiglu(h, p['ffn2_w_gate'], p['ffn2_w_up'], p['ffn2_w_down'])
    return x, final_state


def setup_inputs(seed: int = 0) -> dict:
    key = jax.random.key(seed)
    keys = jax.random.split(key, 34)
    f32 = jnp.float32

    def nrm(i, shape, scale):
        return scale * jax.random.normal(keys[i], shape, f32)

    def gain(i, shape):
        return 1.0 + nrm(i, shape, 0.1)

    L = DEPTH
    dt0 = jnp.exp(jax.random.uniform(keys[16], (L, N_DIR, SSD_HEADS), f32,
                                     minval=math.log(1e-3), maxval=math.log(1e-1)))
    dt_bias = dt0 + jnp.log(-jnp.expm1(-dt0))
    a_log = jnp.log(jax.random.uniform(keys[17], (L, N_DIR, SSD_HEADS), f32, minval=1.0, maxval=16.0))
    return {
        'x_prompt': nrm(0, (BATCH, SEQ, D_MODEL), 1.0),
        'x_sample': nrm(1, (DEC_BATCH, DEC_SEQ, D_MODEL), 1.0),
        'state_ssd': nrm(2, (DEC_BATCH, DEPTH, N_DIR, SSD_HEADS, SSD_HEAD_DIM, SSD_STATE), 0.5),
        'c': nrm(3, (DEC_BATCH, D_MODEL), 1.0),
        'c_ctx': nrm(4, (D_MODEL,), 1.0),
        'w_ada': nrm(5, (L, D_MODEL, N_MOD * D_MODEL), D_MODEL ** -0.5),
        'b_ada': nrm(6, (L, N_MOD * D_MODEL), 0.02),
        'norm_ffn1': gain(7, (L, D_MODEL)),
        'ffn1_w_gate': nrm(8, (L, D_MODEL, FFN_DIM), D_MODEL ** -0.5),
        'ffn1_w_up': nrm(9, (L, D_MODEL, FFN_DIM), D_MODEL ** -0.5),
        'ffn1_w_down': nrm(10, (L, FFN_DIM, D_MODEL), FFN_DIM ** -0.5),
        'norm_mix': gain(11, (L, D_MODEL)),
        'w_in': nrm(12, (L, D_MODEL, IN_COLS), D_MODEL ** -0.5),
        'w_out': nrm(13, (L, MIX_W, D_MODEL), MIX_W ** -0.5),
        'ssd_conv_w': nrm(14, (L, SSD_CONV, SSD_XBC), SSD_CONV ** -0.5),
        'ssd_conv_b': nrm(15, (L, SSD_XBC), 0.02),
        'ssd_dt_bias': dt_bias,
        'ssd_a_log': a_log,
        'ssd_d': gain(18, (L, SSD_HEADS)),
        'ssd_norm_w': gain(19, (L, SSD_W)),
        'hy_conv_w': nrm(20, (L, HY_CONV, (HY_ORDER + 1) * HY_W), HY_CONV ** -0.5),
        'hy_conv_b': nrm(21, (L, (HY_ORDER + 1) * HY_W), 0.02),
        'hy_w1': nrm(22, (L, HY_EMB, HY_FILTER_HIDDEN), HY_EMB ** -0.5),
        'hy_b1': nrm(23, (L, HY_FILTER_HIDDEN), 0.02),
        'hy_freq': gain(24, (L, HY_FILTER_HIDDEN)),
        'hy_w2': nrm(25, (L, HY_FILTER_HIDDEN, HY_FILTER_HIDDEN), HY_FILTER_HIDDEN ** -0.5),
        'hy_b2': nrm(26, (L, HY_FILTER_HIDDEN), 0.02),
        'hy_w3': nrm(27, (L, HY_FILTER_HIDDEN, N_DIR * HY_ORDER * HY_W), 0.1 * HY_FILTER_HIDDEN ** -0.5),
        'hy_skip': nrm(28, (L, HY_ORDER, HY_W), 0.5),
        'norm_ffn2': gain(29, (L, D_MODEL)),
        'ffn2_w_gate': nrm(30, (L, D_MODEL, FFN_DIM), D_MODEL ** -0.5),
        'ffn2_w_up': nrm(31, (L, D_MODEL, FFN_DIM), D_MODEL ** -0.5),
        'ffn2_w_down': nrm(32, (L, FFN_DIM, D_MODEL), FFN_DIM ** -0.5),
        'norm_final': gain(33, (D_MODEL,)),
    }


def reference(x_prompt, x_sample, state_ssd, c, c_ctx, w_ada, b_ada, norm_ffn1, ffn1_w_gate, ffn1_w_up,
              ffn1_w_down, norm_mix, w_in, w_out, ssd_conv_w, ssd_conv_b, ssd_dt_bias, ssd_a_log, ssd_d,
              ssd_norm_w, hy_conv_w, hy_conv_b, hy_w1, hy_b1, hy_freq, hy_w2, hy_b2, hy_w3, hy_skip,
              norm_ffn2, ffn2_w_gate, ffn2_w_up, ffn2_w_down, norm_final):
    rows = x_sample.shape[1] // GRID_W
    ctx_cond = c_ctx[None, :]
    h_ctx, h_lat = x_prompt, x_sample
    ctx_states = []
    for i in range(DEPTH):
        p = {
            'w_ada': w_ada[i], 'b_ada': b_ada[i], 'norm_ffn1': norm_ffn1[i],
            'ffn1_w_gate': ffn1_w_gate[i], 'ffn1_w_up': ffn1_w_up[i], 'ffn1_w_down': ffn1_w_down[i],
            'norm_mix': norm_mix[i], 'w_in': w_in[i], 'w_out': w_out[i],
            'ssd_conv_w': ssd_conv_w[i], 'ssd_conv_b': ssd_conv_b[i], 'ssd_dt_bias': ssd_dt_bias[i],
            'ssd_a_log': ssd_a_log[i], 'ssd_d': ssd_d[i], 'ssd_norm_w': ssd_norm_w[i],
            'hy_conv_w': hy_conv_w[i], 'hy_conv_b': hy_conv_b[i], 'hy_w1': hy_w1[i], 'hy_b1': hy_b1[i],
            'hy_freq': hy_freq[i], 'hy_w2': hy_w2[i], 'hy_b2': hy_b2[i], 'hy_w3': hy_w3[i],
            'hy_skip': hy_skip[i], 'norm_ffn2': norm_ffn2[i],
            'ffn2_w_gate': ffn2_w_gate[i], 'ffn2_w_up': ffn2_w_up[i], 'ffn2_w_down': ffn2_w_down[i],
        }
        ctx_init = jnp.zeros((x_prompt.shape[0], N_DIR, SSD_HEADS, SSD_HEAD_DIM, SSD_STATE), jnp.float32)
        h_ctx, ctx_final = trunk_layer(h_ctx, ctx_cond, ctx_init, 1, p)
        ctx_states.append(ctx_final)
        h_lat, _ = trunk_layer(h_lat, c, state_ssd[:, i], rows, p)
    y_prompt = rmsnorm(h_ctx, norm_final)
    y_sample = rmsnorm(h_lat, norm_final)
    new_state_ssd = jnp.stack(ctx_states, axis=1).astype(x_prompt.dtype)
    return (y_prompt, y_sample, new_state_ssd)
```

```python
import functools
import math

import numpy as np
import jax
import jax.numpy as jnp
from jax import lax
from jax.experimental import pallas as pl
from jax.experimental.pallas import tpu as pltpu

F32 = jnp.float32
BF16 = jnp.bfloat16

D_MODEL = 1024
GRID_W = 64
N_MOD = 9
RMS_EPS = 1e-6
FFN_DIM = 2752
SSD_W = 512
SSD_HEADS = 8
SSD_HEAD_DIM = 64
SSD_STATE = 64
SSD_GROUPS = 2
SSD_CHUNK = 128
SSD_XBC = SSD_W + 2 * SSD_GROUPS * SSD_STATE
N_DIR = 2
HY_W = 512
HY_ORDER = 2
HY_EMB = 33
HY_BANDS = (HY_EMB - 1) // 2
HY_HIDDEN = 64
HY_MIN_DECAY = math.log(1e-2) / 1.5
HY_MAX_DECAY = math.log(1e-2) / 0.3

LANES = 128
FFN_PAD = 2816
FFN_CHUNK = 256
DT_PAD = LANES
EMB_PAD = LANES
TOKEN_TILE = 512
HY_CH_TILE = 256
VMEM_LIMIT = 56 * 1024 * 1024
NEG_BIG = -1e30


def _silu(x):
    return x * jax.nn.sigmoid(x)


def _softplus(x):
    return jnp.maximum(x, 0.0) + jnp.log1p(jnp.exp(-jnp.abs(x)))


def _rms_mod(x, gain, shift, scale):
    ms = jnp.mean(x * x, axis=-1, keepdims=True)
    y = x * lax.rsqrt(ms + RMS_EPS) * gain
    return y * (1.0 + scale) + shift


def _dot(a, b):
    return jnp.dot(a, b, preferred_element_type=F32)


def _dot_nt(a, b):
    return lax.dot_general(a, b, (((1,), (1,)), ((), ())), preferred_element_type=F32)


def _split3(x):
    hi = x.astype(BF16)
    r = x - hi.astype(F32)
    mid = r.astype(BF16)
    lo = (r - mid.astype(F32)).astype(BF16)
    return hi, mid, lo


def _short_conv(x, w, b, first, last, l):
    prev = jnp.where(first, 0.0, pltpu.roll(x, 1, 0))
    nxt = jnp.where(last, 0.0, pltpu.roll(x, l - 1, 0))
    return prev * w[0:1, :] + x * w[1:2, :] + nxt * w[2:3, :] + b


def _seg_edges(l, seg):
    pos = lax.broadcasted_iota(jnp.int32, (l, 1), 0) & (seg - 1)
    return pos == 0, pos == seg - 1


def _swiglu_acc(h, wg_ref, wu_ref, wd_ref):
    acc = None
    for k in range(FFN_PAD // FFN_CHUNK):
        cols = slice(k * FFN_CHUNK, (k + 1) * FFN_CHUNK)
        g = _dot(h, wg_ref[:, cols])
        u = _dot(h, wu_ref[:, cols])
        a = (_silu(g) * u).astype(BF16)
        part = _dot(a, wd_ref[cols, :])
        acc = part if acc is None else acc + part
    return acc


def _ada_kernel(cond_ref, w_ref, b_ref, o_ref):
    s = _silu(cond_ref[...]).astype(BF16)
    o_ref[...] = _dot(s, w_ref[...].astype(BF16)) + b_ref[...]


def _ada(cond, w_ada, b_ada):
    rows = cond.shape[0]
    n = w_ada.shape[1]
    tn = D_MODEL
    return pl.pallas_call(
        _ada_kernel,
        grid=(n // tn,),
        in_specs=[
            pl.BlockSpec((rows, D_MODEL), lambda j: (0, 0)),
            pl.BlockSpec((D_MODEL, tn), lambda j: (0, j)),
            pl.BlockSpec((1, tn), lambda j: (0, j)),
        ],
        out_specs=pl.BlockSpec((rows, tn), lambda j: (0, j)),
        out_shape=jax.ShapeDtypeStruct((rows, n), F32),
        compiler_params=pltpu.CompilerParams(dimension_semantics=("arbitrary",), vmem_limit_bytes=VMEM_LIMIT),
        name="ada",
    )(cond, w_ada, b_ada.reshape(1, n))


def _filter_kernel(feats_ref, w1_ref, b1_ref, fr_ref, w2_ref, b2_ref, w3_ref, dl_ref, f_ref,
                   kr_ref, ki_ref, kn_ref, *, l):
    feats = feats_ref[...]
    freq = fr_ref[...]
    h = jnp.sin(freq * (_dot(feats.astype(BF16), w1_ref[...].astype(BF16)) + b1_ref[...]))
    h = jnp.sin(freq * (_dot(h.astype(BF16), w2_ref[...].astype(BF16)) + b2_ref[...]))
    hb = h.astype(BF16)
    t = feats[:, 0:1]
    row = lax.broadcasted_iota(jnp.int32, (l, 1), 0)
    sign = jnp.where((row & 1) == 0, 1.0, -1.0)
    for i in range(HY_ORDER):
        window = jnp.exp(-t * dl_ref[i:i + 1, :])
        c0 = (0 * HY_ORDER + i) * HY_W
        c1 = (1 * HY_ORDER + i) * HY_W
        k0 = _dot(hb, w3_ref[:, c0:c0 + HY_W].astype(BF16)) * window
        k1 = _dot(hb, w3_ref[:, c1:c1 + HY_W].astype(BF16)) * window
        k1 = jnp.where(row == 0, 0.0, k1)
        ks = k0 + k1
        kd = k0 - k1
        kr_ref[i] = _dot(f_ref[0:l, :], ks.astype(BF16))
        ki = _dot(f_ref[l:2 * l, :], kd.astype(BF16))
        ki_ref[i] = jnp.where(row == 0, 0.0, ki)
        kn_ref[i] = jnp.sum(ks * sign, axis=0, keepdims=True)


def _filters(l, feats, w1p, b1, freq, w2, b2, w3, deltas, fmat):
    full = lambda *shape: pl.BlockSpec(shape, lambda: (0,) * len(shape))
    return pl.pallas_call(
        functools.partial(_filter_kernel, l=l),
        in_specs=[full(l, EMB_PAD), full(EMB_PAD, HY_HIDDEN), full(1, HY_HIDDEN), full(1, HY_HIDDEN),
                  full(HY_HIDDEN, HY_HIDDEN), full(1, HY_HIDDEN), full(HY_HIDDEN, N_DIR * HY_ORDER * HY_W),
                  full(HY_ORDER, HY_W), full(2 * l, l)],
        out_specs=[full(HY_ORDER, l, HY_W), full(HY_ORDER, l, HY_W), full(HY_ORDER, 1, HY_W)],
        out_shape=[jax.ShapeDtypeStruct((HY_ORDER, l, HY_W), F32),
                   jax.ShapeDtypeStruct((HY_ORDER, l, HY_W), F32),
                   jax.ShapeDtypeStruct((HY_ORDER, 1, HY_W), F32)],
        compiler_params=pltpu.CompilerParams(vmem_limit_bytes=VMEM_LIMIT),
        name=f"filt{l}",
    )(feats, w1p, b1, freq, w2, b2, w3, deltas, fmat)


def _ffn_in_kernel(x_ref, mod_ref, n1_ref, nm_ref, wg_ref, wu_ref, wd_ref, wz_ref, wx_ref, wt_ref, wh_ref,
                   x1_ref, z_ref, xbc_ref, dt_ref, hy_ref):
    x = x_ref[...]
    mod = mod_ref[...]
    h = _rms_mod(x, n1_ref[...], mod[0:1, :], mod[1:2, :]).astype(BF16)
    x1 = x + (0.5 * mod[2:3, :]) * _swiglu_acc(h, wg_ref, wu_ref, wd_ref)
    x1_ref[...] = x1
    h2 = _rms_mod(x1, nm_ref[...], mod[3:4, :], mod[4:5, :]).astype(BF16)
    z_ref[...] = _dot(h2, wz_ref[...])
    xbc_ref[...] = _dot(h2, wx_ref[...])
    dt_ref[...] = _dot(h2, wt_ref[...])
    hy_ref[...] = _dot(h2, wh_ref[...])


def _const_spec(shape):
    return pl.BlockSpec(shape, lambda i: (0,) * len(shape), pipeline_mode=pl.Buffered(1))


def _ffn_in(x, mod, mod_row, n1, nm, wg, wu, wd, wz, wx, wt, wh):
    tokens = x.shape[0]
    tm = TOKEN_TILE
    row_spec = lambda n: pl.BlockSpec((tm, n), lambda i: (i, 0))
    widths = (D_MODEL, SSD_W, SSD_XBC, DT_PAD, 3 * HY_W)
    return pl.pallas_call(
        _ffn_in_kernel,
        grid=(tokens // tm,),
        in_specs=[row_spec(D_MODEL),
                  pl.BlockSpec((None, N_MOD, D_MODEL), lambda i: (mod_row(i), 0, 0)),
                  _const_spec((1, D_MODEL)), _const_spec((1, D_MODEL)),
                  _const_spec(wg.shape), _const_spec(wu.shape), _const_spec(wd.shape),
                  _const_spec(wz.shape), _const_spec(wx.shape), _const_spec(wt.shape), _const_spec(wh.shape)],
        out_specs=[row_spec(n) for n in widths],
        out_shape=[jax.ShapeDtypeStruct((tokens, n), F32) for n in widths],
        compiler_params=pltpu.CompilerParams(dimension_semantics=("arbitrary",), vmem_limit_bytes=VMEM_LIMIT),
        name="ffn_in",
    )(x, mod, n1, nm, wg, wu, wd, wz, wx, wt, wh)


def _out_ffn_kernel(x1_ref, ys_ref, yh_ref, mod_ref, n3_ref, nf_ref, wo_ref, wg_ref, wu_ref, wd_ref, o_ref):
    mod = mod_ref[...]
    y = jnp.concatenate([ys_ref[...], yh_ref[...]], axis=1).astype(BF16)
    x2 = x1_ref[...] + mod[5:6, :] * _dot(y, wo_ref[...])
    h = _rms_mod(x2, n3_ref[...], mod[6:7, :], mod[7:8, :]).astype(BF16)
    x3 = x2 + (0.5 * mod[8:9, :]) * _swiglu_acc(h, wg_ref, wu_ref, wd_ref)
    ms = jnp.mean(x3 * x3, axis=-1, keepdims=True)
    o_ref[...] = x3 * lax.rsqrt(ms + RMS_EPS) * nf_ref[...]


def _out_ffn(x1, ys, yh, mod, mod_row, n3, nf, wo, wg, wu, wd):
    tokens = x1.shape[0]
    tm = TOKEN_TILE
    row_spec = lambda n: pl.BlockSpec((tm, n), lambda i: (i, 0))
    return pl.pallas_call(
        _out_ffn_kernel,
        grid=(tokens // tm,),
        in_specs=[row_spec(D_MODEL), row_spec(SSD_W), row_spec(HY_W),
                  pl.BlockSpec((None, N_MOD, D_MODEL), lambda i: (mod_row(i), 0, 0)),
                  _const_spec((1, D_MODEL)), _const_spec((1, D_MODEL)),
                  _const_spec(wo.shape), _const_spec(wg.shape), _const_spec(wu.shape), _const_spec(wd.shape)],
        out_specs=row_spec(D_MODEL),
        out_shape=jax.ShapeDtypeStruct((tokens, D_MODEL), F32),
        compiler_params=pltpu.CompilerParams(dimension_semantics=("arbitrary",), vmem_limit_bytes=VMEM_LIMIT),
        name="out_ffn",
    )(x1, ys, yh, mod, n3, nf, wo, wg, wu, wd)


def _ssd_kernel(z_ref, xbc_ref, dt_ref, init_ref, cw_ref, cb_ref, dtb_ref, alog_ref, dexp_ref, nw_ref, e_ref,
                y_ref, fin_ref,
                xs_s, xdf_s, xdb_s, b_s, c_s, a_s, st_s,
                *, l, seg, zero_init):
    q = SSD_CHUNK
    nc = l // q
    hp = SSD_HEADS * SSD_HEAD_DIM

    first, last = _seg_edges(l, seg)
    u = _silu(_short_conv(xbc_ref[...], cw_ref[...], cb_ref[...], first, last, l))
    xs = u[:, :SSD_W]
    xs_s[...] = xs
    b_s[...] = u[:, SSD_W:SSD_W + LANES]
    c_s[...] = u[:, SSD_W + LANES:]

    dt = _softplus(dt_ref[...] + dtb_ref[...])
    a_s[...] = dt * (-jnp.exp(alog_ref[...]))
    e_mat = e_ref[...].astype(BF16)
    dte = _dot(dt.astype(BF16), e_mat)
    xdf_s[...] = xs * dte[:, :hp]
    xdb_s[...] = xs * dte[:, hp:]

    row_g = lax.broadcasted_iota(jnp.int32, (hp, LANES), 0) // (hp // SSD_GROUPS)
    lane_g = lax.broadcasted_iota(jnp.int32, (hp, LANES), 1) // SSD_STATE
    own_group = row_g == lane_g
    for d in range(N_DIR):
        if zero_init:
            st_s[d] = jnp.zeros((hp, LANES), F32)
        else:
            s0 = init_ref[d]
            st_s[d] = jnp.where(own_group, jnp.concatenate([s0, s0], axis=1), 0.0)

    ii = lax.broadcasted_iota(jnp.int32, (q, q), 0)
    jj = lax.broadcasted_iota(jnp.int32, (q, q), 1)
    tri_incl = (jj <= ii).astype(BF16)
    lane = lax.broadcasted_iota(jnp.int32, (q, LANES), 1)
    low_half = lane < SSD_STATE

    def chunk(c, d, xd_s, accumulate):
        r0 = pl.multiple_of(c * q, q)
        rows = pl.ds(r0, q)
        a_c = a_s[rows, :]
        a1, a2, a3 = _split3(a_c)
        cum = _dot(tri_incl, a1) + _dot(tri_incl, a2) + _dot(tri_incl, a3)
        if d == 0:
            tot = cum[q - 1:q, :]
            keep = ii >= jj
        else:
            tot = cum[q - 1:q, :]
            cum = tot - cum + a_c
            tot = cum[0:1, :]
            keep = ii <= jj
        cum_t = cum.T
        cc = c_s[rows, :]
        bc = b_s[rows, :]
        bcb = bc.astype(BF16)
        g_mats = [_dot_nt(jnp.where(low_half, cc, 0.0).astype(BF16), bcb),
                  _dot_nt(jnp.where(low_half, 0.0, cc).astype(BF16), bcb)]
        xd = xd_s[rows, :]

        y_parts = []
        for pair in range(SSD_HEADS // 2):
            g = pair // (SSD_HEADS // 2 // SSD_GROUPS)
            s_mats = []
            for hh in (2 * pair, 2 * pair + 1):
                col = d * SSD_HEADS + hh
                diff = cum[:, col:col + 1] - cum_t[col:col + 1, :]
                decay = jnp.exp(jnp.where(keep, diff, NEG_BIG))
                s_mats.append((g_mats[g] * decay).astype(BF16))
            lhs = jnp.concatenate(s_mats, axis=1)
            xp = xd[:, pair * LANES:(pair + 1) * LANES]
            rhs = jnp.concatenate([jnp.where(low_half, xp, 0.0), jnp.where(low_half, 0.0, xp)],
                                  axis=0).astype(BF16)
            y_parts.append(_dot(lhs, rhs))
        y_diag = jnp.concatenate(y_parts, axis=1)

        e_d = e_mat[:, d * hp:(d + 1) * hp]
        state = st_s[d]
        y_off = _dot_nt(cc.astype(BF16), state.astype(BF16)) * _dot(jnp.exp(cum).astype(BF16), e_d)
        y = y_diag + y_off
        if accumulate:
            y_ref[rows, :] = y_ref[rows, :] + y
        else:
            y_ref[rows, :] = y

        w = xd * _dot(jnp.exp(tot - cum).astype(BF16), e_d)
        ds = _dot(w.T.astype(BF16), bcb)
        etot = jnp.exp(tot)
        for h in range(SSD_HEADS):
            col = d * SSD_HEADS + h
            blk = slice(h * SSD_HEAD_DIM, (h + 1) * SSD_HEAD_DIM)
            st_s[d, blk, :] = (state[blk, :] * etot[:, col:col + 1]
                               + jnp.where(own_group[blk, :], ds[blk, :], 0.0))

    def fwd_body(c, carry):
        chunk(c, 0, xdf_s, False)
        return carry

    def bwd_body(k, carry):
        chunk(nc - 1 - k, 1, xdb_s, True)
        return carry

    lax.fori_loop(0, nc, fwd_body, 0)
    lax.fori_loop(0, nc, bwd_body, 0)

    y = y_ref[...] + xs_s[...] * dexp_ref[...]
    y = y * _silu(z_ref[...])
    ms = jnp.mean(y * y, axis=-1, keepdims=True)
    y_ref[...] = y * lax.rsqrt(ms + RMS_EPS) * nw_ref[...]

    half = hp // SSD_GROUPS
    for d in range(N_DIR):
        fin_ref[d, 0:half, :] = st_s[d, 0:half, 0:SSD_STATE]
        fin_ref[d, half:hp, :] = st_s[d, half:hp, SSD_STATE:2 * SSD_STATE]


def _ssd(z, xbc, dt, init, cw, cb, dtb, alog, dexp, nw, e_mat, *, seg):
    bsz, l, _ = z.shape
    hp = SSD_HEADS * SSD_HEAD_DIM
    zero_init = init is None
    if zero_init:
        init = jnp.zeros((1, N_DIR, hp, SSD_STATE), F32)
        init_map = lambda b: (0, 0, 0, 0)
    else:
        init_map = lambda b: (b, 0, 0, 0)
    seq_spec = lambda n: pl.BlockSpec((None, l, n), lambda b: (b, 0, 0))
    const = lambda *shape: pl.BlockSpec(shape, lambda b: (0,) * len(shape))
    return pl.pallas_call(
        functools.partial(_ssd_kernel, l=l, seg=seg, zero_init=zero_init),
        grid=(bsz,),
        in_specs=[seq_spec(SSD_W), seq_spec(SSD_XBC), seq_spec(DT_PAD),
                  pl.BlockSpec((None, N_DIR, hp, SSD_STATE), init_map),
                  const(3, SSD_XBC), const(1, SSD_XBC), const(1, DT_PAD), const(1, DT_PAD),
                  const(1, SSD_W), const(1, SSD_W), const(LANES, N_DIR * hp)],
        out_specs=[seq_spec(SSD_W), pl.BlockSpec((None, N_DIR, hp, SSD_STATE), lambda b: (b, 0, 0, 0))],
        out_shape=[jax.ShapeDtypeStruct((bsz, l, SSD_W), F32),
                   jax.ShapeDtypeStruct((bsz, N_DIR, hp, SSD_STATE), F32)],
        scratch_shapes=[pltpu.VMEM((l, SSD_W), F32), pltpu.VMEM((l, SSD_W), F32), pltpu.VMEM((l, SSD_W), F32),
                        pltpu.VMEM((l, LANES), F32), pltpu.VMEM((l, LANES), F32), pltpu.VMEM((l, DT_PAD), F32),
                        pltpu.VMEM((N_DIR, hp, LANES), F32)],
        compiler_params=pltpu.CompilerParams(dimension_semantics=("arbitrary",), vmem_limit_bytes=VMEM_LIMIT),
        name=f"ssd{l}",
    )(z, xbc, dt, init, cw, cb, dtb, alog, dexp, nw, e_mat)


def _hyena_kernel(v_ref, x1_ref, x2_ref, wv_ref, w1_ref, w2_ref, bv_ref, b1_ref, b2_ref,
                  f_ref, g_ref, kr_ref, ki_ref, kn_ref, skip_ref, o_ref, *, l, seg):
    first, last = _seg_edges(l, seg)
    zz = _short_conv(v_ref[...], wv_ref[...], bv_ref[...], first, last, l)
    gates = (_short_conv(x1_ref[...], w1_ref[...], b1_ref[...], first, last, l),
             _short_conv(x2_ref[...], w2_ref[...], b2_ref[...], first, last, l))
    row = lax.broadcasted_iota(jnp.int32, (l, 1), 0)
    for i in range(HY_ORDER):
        spec = _dot(f_ref[...], zz.astype(BF16))
        top = spec[:l]
        bot = spec[l:]
        kr = kr_ref[i]
        ki = ki_ref[i]
        yr = top * kr - bot * ki
        yi = jnp.where(row == 0, bot * kn_ref[i], top * ki + bot * kr)
        prod = jnp.concatenate([yr, yi], axis=0).astype(BF16)
        conv = _dot(g_ref[...], prod)
        zz = gates[i] * (conv + zz * skip_ref[i:i + 1, :])
    o_ref[...] = zz


def _hyena(hy, cw, cb, fmat, gmat, kr, ki, kn, skip, *, seg):
    bsz, l, _ = hy.shape
    ct = HY_CH_TILE
    nct = HY_W // ct
    part = lambda p: pl.BlockSpec((None, l, ct), lambda j, b: (b, 0, p * nct + j))
    wpart = lambda p: pl.BlockSpec((3, ct), lambda j, b: (0, p * nct + j))
    bpart = lambda p: pl.BlockSpec((1, ct), lambda j, b: (0, p * nct + j))
    return pl.pallas_call(
        functools.partial(_hyena_kernel, l=l, seg=seg),
        grid=(nct, bsz),
        in_specs=[part(0), part(1), part(2), wpart(0), wpart(1), wpart(2), bpart(0), bpart(1), bpart(2),
                  pl.BlockSpec((2 * l, l), lambda j, b: (0, 0)),
                  pl.BlockSpec((l, 2 * l), lambda j, b: (0, 0)),
                  pl.BlockSpec((HY_ORDER, l, ct), lambda j, b: (0, 0, j)),
                  pl.BlockSpec((HY_ORDER, l, ct), lambda j, b: (0, 0, j)),
                  pl.BlockSpec((HY_ORDER, 1, ct), lambda j, b: (0, 0, j)),
                  pl.BlockSpec((HY_ORDER, ct), lambda j, b: (0, j))],
        out_specs=pl.BlockSpec((None, l, ct), lambda j, b: (b, 0, j)),
        out_shape=jax.ShapeDtypeStruct((bsz, l, HY_W), F32),
        compiler_params=pltpu.CompilerParams(dimension_semantics=("arbitrary", "arbitrary"),
                                             vmem_limit_bytes=VMEM_LIMIT),
        name=f"hyena{l}",
    )(hy, hy, hy, cw, cw, cw, cb, cb, cb, fmat, gmat, kr, ki, kn, skip)


def _dft_mats(l):
    n = 2 * l
    f = np.arange(l, dtype=np.int64)[:, None]
    t = np.arange(l, dtype=np.int64)[None, :]
    ang = 2.0 * np.pi * ((f * t) % n).astype(np.float64) / n
    alt = np.where(np.arange(l) % 2 == 0, 1.0, -1.0)
    top = np.cos(ang)
    bot = -np.sin(ang)
    bot[0, :] = alt
    fwd = np.concatenate([top, bot], axis=0)
    wf = np.full((l,), 2.0)
    wf[0] = 1.0
    gtop = np.cos(ang).T * wf[None, :] / n
    gbot = -np.sin(ang).T * 2.0 / n
    gbot[:, 0] = alt / n
    inv = np.concatenate([gtop, gbot], axis=1)
    return fwd.astype(np.float32), inv.astype(np.float32)


def _filter_feats(l):
    t = np.linspace(0.0, 1.0, l)[:, None]
    w = (2.0 * np.pi / l) * np.arange(l, dtype=np.float64)[:, None]
    f = np.linspace(1e-4, HY_BANDS - 1, HY_BANDS)[None, :]
    feats = np.concatenate([t, np.cos(f * w), -np.sin(f * w)], axis=-1)
    out = np.zeros((l, EMB_PAD), np.float32)
    out[:, :HY_EMB] = feats
    return out


def _head_expand():
    e = np.zeros((LANES, N_DIR * SSD_HEADS * SSD_HEAD_DIM), np.float32)
    for j in range(N_DIR * SSD_HEADS):
        e[j, j * SSD_HEAD_DIM:(j + 1) * SSD_HEAD_DIM] = 1.0
    return e


def kernel(x_prompt, x_sample, state_ssd, c, c_ctx, w_ada, b_ada, norm_ffn1, ffn1_w_gate, ffn1_w_up, ffn1_w_down, norm_mix, w_in, w_out, ssd_conv_w, ssd_conv_b, ssd_dt_bias, ssd_a_log, ssd_d, ssd_norm_w, hy_conv_w, hy_conv_b, hy_w1, hy_b1, hy_freq, hy_w2, hy_b2, hy_w3, hy_skip, norm_ffn2, ffn2_w_gate, ffn2_w_up, ffn2_w_down, norm_final):
    assert w_ada.shape[0] == 1, "single layer"
    n_ctx, l_ctx, _ = x_prompt.shape
    n_lat, l_lat, _ = x_sample.shape
    hp = SSD_HEADS * SSD_HEAD_DIM

    cond = jnp.zeros((16, D_MODEL), F32).at[:n_lat].set(c).at[n_lat].set(c_ctx)
    mod = _ada(cond, w_ada[0], b_ada[0]).reshape(16, N_MOD, D_MODEL)

    def ffn_weights(wg, wu, wd):
        pad = FFN_PAD - FFN_DIM
        return (jnp.pad(wg[0].astype(BF16), ((0, 0), (0, pad))),
                jnp.pad(wu[0].astype(BF16), ((0, 0), (0, pad))),
                jnp.pad(wd[0].astype(BF16), ((0, pad), (0, 0))))

    ffn1 = ffn_weights(ffn1_w_gate, ffn1_w_up, ffn1_w_down)
    ffn2 = ffn_weights(ffn2_w_gate, ffn2_w_up, ffn2_w_down)
    wi = w_in[0].astype(BF16)
    o1, o2, o3 = SSD_W, SSD_W + SSD_XBC, SSD_W + SSD_XBC + N_DIR * SSD_HEADS
    wz, wx, wh = wi[:, :o1], wi[:, o1:o2], wi[:, o3:]
    wt = jnp.pad(wi[:, o2:o3], ((0, 0), (0, DT_PAD - N_DIR * SSD_HEADS)))
    wo = w_out[0].astype(BF16)
    row = lambda v: v.reshape(1, -1)
    pad_dt = lambda v: jnp.pad(v.reshape(1, -1), ((0, 0), (0, DT_PAD - N_DIR * SSD_HEADS)))
    dtb, alog = pad_dt(ssd_dt_bias[0]), pad_dt(ssd_a_log[0])
    dexp = jnp.repeat(ssd_d[0], SSD_HEAD_DIM).reshape(1, SSD_W)
    e_mat = jnp.asarray(_head_expand())
    w1p = jnp.pad(hy_w1[0], ((0, EMB_PAD - HY_EMB), (0, 0)))
    deltas = jnp.asarray(np.abs(np.linspace(HY_MIN_DECAY, HY_MAX_DECAY, HY_ORDER * HY_W))
                         .reshape(HY_ORDER, HY_W).astype(np.float32))

    def path(x3d, mod_row, init, seg):
        bsz, l, _ = x3d.shape
        x = x3d.reshape(bsz * l, D_MODEL)
        fwd_np, inv_np = _dft_mats(l)
        fmat = jnp.asarray(fwd_np).astype(BF16)
        gmat = jnp.asarray(inv_np).astype(BF16)
        kr, ki, kn = _filters(l, jnp.asarray(_filter_feats(l)), w1p, row(hy_b1[0]), row(hy_freq[0]), hy_w2[0],
                              row(hy_b2[0]), hy_w3[0], deltas, fmat)
        x1, z, xbc, dt, hy = _ffn_in(x, mod, mod_row, row(norm_ffn1[0]), row(norm_mix[0]), *ffn1, wz, wx, wt, wh)
        shape3 = lambda a: a.reshape(bsz, l, a.shape[-1])
        ys, fin = _ssd(shape3(z), shape3(xbc), shape3(dt), init, ssd_conv_w[0], row(ssd_conv_b[0]), dtb, alog,
                       dexp, row(ssd_norm_w[0]), e_mat, seg=seg)
        yh = _hyena(shape3(hy), hy_conv_w[0], row(hy_conv_b[0]), fmat, gmat, kr, ki, kn, hy_skip[0], seg=seg)
        y = _out_ffn(x1, ys.reshape(bsz * l, SSD_W), yh.reshape(bsz * l, HY_W), mod, mod_row,
                     row(norm_ffn2[0]), row(norm_final), wo, *ffn2)
        return y.reshape(bsz, l, D_MODEL), fin

    tiles_per_seq = l_lat // TOKEN_TILE
    y_prompt, ctx_fin = path(x_prompt, lambda i: n_lat, None, l_ctx)
    lat_init = state_ssd[:, 0].reshape(n_lat, N_DIR, hp, SSD_STATE)
    y_sample, _ = path(x_sample, lambda i: i // tiles_per_seq, lat_init, GRID_W)
    new_state = ctx_fin.reshape(n_ctx, 1, N_DIR, SSD_HEADS, SSD_HEAD_DIM, SSD_STATE).astype(x_prompt.dtype)
    return (y_prompt, y_sample, new_state)
```

```python
import functools
import math

import numpy as np
import jax
import jax.numpy as jnp
from jax import lax
from jax.experimental import pallas as pl
from jax.experimental.pallas import tpu as pltpu

F32 = jnp.float32
BF16 = jnp.bfloat16

D_MODEL = 1024
GRID_W = 64
N_MOD = 9
RMS_EPS = 1e-6
FFN_DIM = 2752
SSD_W = 512
SSD_HEADS = 8
SSD_HEAD_DIM = 64
SSD_STATE = 64
SSD_GROUPS = 2
SSD_CHUNK = 128
SSD_XBC = SSD_W + 2 * SSD_GROUPS * SSD_STATE
N_DIR = 2
HY_W = 512
HY_ORDER = 2
HY_EMB = 33
HY_BANDS = (HY_EMB - 1) // 2
HY_HIDDEN = 64
HY_MIN_DECAY = math.log(1e-2) / 1.5
HY_MAX_DECAY = math.log(1e-2) / 0.3
IN_SPLITS = (SSD_W, SSD_W + SSD_XBC, SSD_W + SSD_XBC + N_DIR * SSD_HEADS)
IN_COLS = IN_SPLITS[-1] + (HY_ORDER + 1) * HY_W

LANES = 128
SUBLANES = 8
FFN_PAD = 2816
FFN_CHUNK = 256
DT_PAD = LANES
EMB_PAD = LANES
TOKEN_TILE = 512
CAST_ROWS = 128
HY_CH_TILE = 256
VMEM_LIMIT = 56 * 1024 * 1024
NEG_BIG = -1e30
HP = SSD_HEADS * SSD_HEAD_DIM


def _silu(x):
    return x * jax.nn.sigmoid(x)


def _softplus(x):
    return jnp.maximum(x, 0.0) + jnp.log1p(jnp.exp(-jnp.abs(x)))


def _rms_mod(x, gain, shift, scale):
    ms = jnp.mean(x * x, axis=-1, keepdims=True)
    y = x * lax.rsqrt(ms + RMS_EPS) * gain
    return y * (1.0 + scale) + shift


def _dot(a, b):
    return jnp.dot(a, b, preferred_element_type=F32)


def _dot_nt(a, b):
    return lax.dot_general(a, b, (((1,), (1,)), ((), ())), preferred_element_type=F32)


def _split3(x):
    hi = x.astype(BF16)
    r = x - hi.astype(F32)
    mid = r.astype(BF16)
    lo = (r - mid.astype(F32)).astype(BF16)
    return hi, mid, lo


def _dot_exact_lhs(x, m01):
    hi, mid, lo = _split3(x)
    return _dot(hi, m01) + _dot(mid, m01) + _dot(lo, m01)


def _short_conv(x, w, b, first, last, l):
    prev = jnp.where(first, 0.0, pltpu.roll(x, 1, 0))
    nxt = jnp.where(last, 0.0, pltpu.roll(x, l - 1, 0))
    return prev * w[0:1, :] + x * w[1:2, :] + nxt * w[2:3, :] + b


def _seg_edges(l, seg):
    pos = lax.broadcasted_iota(jnp.int32, (l, 1), 0) & (seg - 1)
    return pos == 0, pos == seg - 1


def _swiglu_acc(h, wg_ref, wu_ref, wd_ref):
    acc = None
    for k in range(FFN_PAD // FFN_CHUNK):
        cols = slice(k * FFN_CHUNK, (k + 1) * FFN_CHUNK)
        g = _dot(h, wg_ref[:, cols])
        u = _dot(h, wu_ref[:, cols])
        a = (_silu(g) * u).astype(BF16)
        part = _dot(a, wd_ref[cols, :])
        acc = part if acc is None else acc + part
    return acc


def _params(*semantics):
    return pltpu.CompilerParams(dimension_semantics=semantics, vmem_limit_bytes=VMEM_LIMIT)


def _cast_pad_cols_kernel(*refs):
    n = len(refs) // 2
    for w_ref, o_ref in zip(refs[:n], refs[n:]):
        o_ref[:, :FFN_DIM] = w_ref[...].astype(BF16)
        o_ref[:, FFN_DIM:] = jnp.zeros((o_ref.shape[0], FFN_PAD - FFN_DIM), BF16)


def _cast_pad_cols(ws):
    n = len(ws)
    return pl.pallas_call(
        _cast_pad_cols_kernel,
        grid=(D_MODEL // CAST_ROWS,),
        in_specs=[pl.BlockSpec((None, CAST_ROWS, FFN_DIM), lambda i: (0, i, 0))] * n,
        out_specs=[pl.BlockSpec((CAST_ROWS, FFN_PAD), lambda i: (i, 0))] * n,
        out_shape=[jax.ShapeDtypeStruct((D_MODEL, FFN_PAD), BF16)] * n,
        compiler_params=_params("arbitrary"),
        name="wcast_cols",
    )(*ws)


def _cast_pad_rows_kernel(*refs):
    n = len(refs) // 2
    rows = refs[0].shape[0]
    row = pl.program_id(0) * rows + lax.broadcasted_iota(jnp.int32, (rows, 1), 0)
    for w_ref, o_ref in zip(refs[:n], refs[n:]):
        o_ref[...] = jnp.where(row < FFN_DIM, w_ref[...], 0.0).astype(BF16)


def _cast_pad_rows(ws):
    n = len(ws)
    return pl.pallas_call(
        _cast_pad_rows_kernel,
        grid=(FFN_PAD // FFN_CHUNK,),
        in_specs=[pl.BlockSpec((None, FFN_CHUNK, D_MODEL), lambda i: (0, i, 0))] * n,
        out_specs=[pl.BlockSpec((FFN_CHUNK, D_MODEL), lambda i: (i, 0))] * n,
        out_shape=[jax.ShapeDtypeStruct((FFN_PAD, D_MODEL), BF16)] * n,
        compiler_params=_params("arbitrary"),
        name="wcast_rows",
    )(*ws)


def _cast_mix_kernel(wi_ref, wo_ref, wz_ref, wx_ref, wt_ref, wh_ref, wob_ref):
    wi = wi_ref[...].astype(BF16)
    o1, o2, o3 = IN_SPLITS
    wz_ref[...] = wi[:, :o1]
    wx_ref[...] = wi[:, o1:o2]
    wt_ref[:, :o3 - o2] = wi[:, o2:o3]
    wt_ref[:, o3 - o2:] = jnp.zeros((wt_ref.shape[0], DT_PAD - (o3 - o2)), BF16)
    wh_ref[...] = wi[:, o3:]
    wob_ref[...] = wo_ref[...].astype(BF16)


def _cast_mix(w_in, w_out):
    widths = (SSD_W, SSD_XBC, DT_PAD, (HY_ORDER + 1) * HY_W, D_MODEL)
    rows = 2 * CAST_ROWS
    return pl.pallas_call(
        _cast_mix_kernel,
        grid=(D_MODEL // rows,),
        in_specs=[pl.BlockSpec((None, rows, IN_COLS), lambda i: (0, i, 0)),
                  pl.BlockSpec((None, rows, D_MODEL), lambda i: (0, i, 0))],
        out_specs=[pl.BlockSpec((rows, n), lambda i: (i, 0)) for n in widths],
        out_shape=[jax.ShapeDtypeStruct((D_MODEL, n), BF16) for n in widths],
        compiler_params=_params("arbitrary"),
        name="wcast_mix",
    )(w_in, w_out)


def _ada_kernel(cond_ref, w_ref, b_ref, o_ref):
    s = _silu(cond_ref[...]).astype(BF16)
    o_ref[...] = _dot(s, w_ref[...].astype(BF16)) + b_ref[...]


def _ada(cond, w_ada, b_ada):
    rows = cond.shape[0]
    n = w_ada.shape[-1]
    tn = D_MODEL
    return pl.pallas_call(
        _ada_kernel,
        grid=(n // tn,),
        in_specs=[
            pl.BlockSpec((rows, D_MODEL), lambda j: (0, 0)),
            pl.BlockSpec((None, D_MODEL, tn), lambda j: (0, 0, j)),
            pl.BlockSpec((1, tn), lambda j: (0, j)),
        ],
        out_specs=pl.BlockSpec((rows, tn), lambda j: (0, j)),
        out_shape=jax.ShapeDtypeStruct((rows, n), F32),
        compiler_params=_params("arbitrary"),
        name="ada",
    )(cond, w_ada, b_ada)


def _filter_kernel(feats_ref, w1_ref, b1_ref, fr_ref, w2_ref, b2_ref, w3_ref, dl_ref, f_ref,
                   kr_ref, ki_ref, kn_ref, *, l):
    feats = feats_ref[...]
    freq = fr_ref[...]
    h = jnp.sin(freq * (_dot(feats.astype(BF16), w1_ref[...].astype(BF16)) + b1_ref[...]))
    h = jnp.sin(freq * (_dot(h.astype(BF16), w2_ref[...].astype(BF16)) + b2_ref[...]))
    hb = h.astype(BF16)
    t = feats[:, 0:1]
    row = lax.broadcasted_iota(jnp.int32, (l, 1), 0)
    sign = jnp.where((row & 1) == 0, 1.0, -1.0)
    for i in range(HY_ORDER):
        window = jnp.exp(-t * dl_ref[i:i + 1, :])
        c0 = (0 * HY_ORDER + i) * HY_W
        c1 = (1 * HY_ORDER + i) * HY_W
        k0 = _dot(hb, w3_ref[:, c0:c0 + HY_W].astype(BF16)) * window
        k1 = _dot(hb, w3_ref[:, c1:c1 + HY_W].astype(BF16)) * window
        k1 = jnp.where(row == 0, 0.0, k1)
        ks = k0 + k1
        kd = k0 - k1
        kr_ref[i] = _dot(f_ref[0:l, :], ks.astype(BF16))
        ki = _dot(f_ref[l:2 * l, :], kd.astype(BF16))
        ki_ref[i] = jnp.where(row == 0, 0.0, ki)
        kn_ref[i] = jnp.sum(ks * sign, axis=0, keepdims=True)


def _filters(l, feats, w1p, b1, freq, w2, b2, w3, deltas, fmat):
    full = lambda *shape: pl.BlockSpec(shape, lambda: (0,) * len(shape))
    return pl.pallas_call(
        functools.partial(_filter_kernel, l=l),
        in_specs=[full(l, EMB_PAD), full(EMB_PAD, HY_HIDDEN), full(1, HY_HIDDEN), full(1, HY_HIDDEN),
                  full(HY_HIDDEN, HY_HIDDEN), full(1, HY_HIDDEN), full(HY_HIDDEN, N_DIR * HY_ORDER * HY_W),
                  full(HY_ORDER, HY_W), full(2 * l, l)],
        out_specs=[full(HY_ORDER, l, HY_W), full(HY_ORDER, l, HY_W), full(HY_ORDER, 1, HY_W)],
        out_shape=[jax.ShapeDtypeStruct((HY_ORDER, l, HY_W), F32),
                   jax.ShapeDtypeStruct((HY_ORDER, l, HY_W), F32),
                   jax.ShapeDtypeStruct((HY_ORDER, 1, HY_W), F32)],
        compiler_params=pltpu.CompilerParams(vmem_limit_bytes=VMEM_LIMIT),
        name=f"filt{l}",
    )(feats, w1p, b1, freq, w2, b2, w3, deltas, fmat)


def _const_spec(shape):
    return pl.BlockSpec(shape, lambda i: (0,) * len(shape), pipeline_mode=pl.Buffered(1))


def _group_specs(ctx_tiles, width):
    tm = TOKEN_TILE
    return [pl.BlockSpec((tm, width), lambda i: (jnp.minimum(i, ctx_tiles - 1), 0)),
            pl.BlockSpec((tm, width), lambda i: (jnp.maximum(i - ctx_tiles, 0), 0))]


def _mod_spec(ctx_tiles, n_lat, tiles_per_seq):
    return pl.BlockSpec((None, N_MOD, D_MODEL),
                        lambda i: (jnp.where(i < ctx_tiles, n_lat, (i - ctx_tiles) // tiles_per_seq), 0, 0))


def _ffn_in_kernel(xc_ref, xl_ref, mod_ref, n1_ref, nm_ref, wg_ref, wu_ref, wd_ref, wz_ref, wx_ref, wt_ref, wh_ref,
                   x1_ref, z_ref, xbc_ref, dt_ref, hy_ref, *, ctx_tiles):
    x = jnp.where(pl.program_id(0) < ctx_tiles, xc_ref[...], xl_ref[...])
    mod = mod_ref[...]
    h = _rms_mod(x, n1_ref[...], mod[0:1, :], mod[1:2, :]).astype(BF16)
    x1 = x + (0.5 * mod[2:3, :]) * _swiglu_acc(h, wg_ref, wu_ref, wd_ref)
    x1_ref[...] = x1
    h2 = _rms_mod(x1, nm_ref[...], mod[3:4, :], mod[4:5, :]).astype(BF16)
    z_ref[...] = _dot(h2, wz_ref[...])
    xbc_ref[...] = _dot(h2, wx_ref[...])
    dt_ref[...] = _dot(h2, wt_ref[...])
    hy_ref[...] = _dot(h2, wh_ref[...])


def _ffn_in(x_ctx, x_lat, mod, n_lat, l_lat, n1, nm, wg, wu, wd, wz, wx, wt, wh):
    tm = TOKEN_TILE
    ctx_tiles = x_ctx.shape[0] // tm
    tokens = x_ctx.shape[0] + x_lat.shape[0]
    row_spec = lambda n: pl.BlockSpec((tm, n), lambda i: (i, 0))
    widths = (D_MODEL, SSD_W, SSD_XBC, DT_PAD, (HY_ORDER + 1) * HY_W)
    return pl.pallas_call(
        functools.partial(_ffn_in_kernel, ctx_tiles=ctx_tiles),
        grid=(tokens // tm,),
        in_specs=_group_specs(ctx_tiles, D_MODEL) + [
            _mod_spec(ctx_tiles, n_lat, l_lat // tm),
            _const_spec((1, D_MODEL)), _const_spec((1, D_MODEL)),
            _const_spec(wg.shape), _const_spec(wu.shape), _const_spec(wd.shape),
            _const_spec(wz.shape), _const_spec(wx.shape), _const_spec(wt.shape), _const_spec(wh.shape)],
        out_specs=[row_spec(n) for n in widths],
        out_shape=[jax.ShapeDtypeStruct((tokens, n), F32) for n in widths],
        compiler_params=_params("arbitrary"),
        name="ffn_in",
    )(x_ctx, x_lat, mod, n1, nm, wg, wu, wd, wz, wx, wt, wh)


def _out_ffn_kernel(x1_ref, ysc_ref, ysl_ref, yhc_ref, yhl_ref, mod_ref, n3_ref, nf_ref,
                    wo_ref, wg_ref, wu_ref, wd_ref, oc_ref, ol_ref, *, ctx_tiles):
    is_ctx = pl.program_id(0) < ctx_tiles
    mod = mod_ref[...]
    ys = jnp.where(is_ctx, ysc_ref[...], ysl_ref[...])
    yh = jnp.where(is_ctx, yhc_ref[...], yhl_ref[...])
    y = jnp.concatenate([ys, yh], axis=1).astype(BF16)
    x2 = x1_ref[...] + mod[5:6, :] * _dot(y, wo_ref[...])
    h = _rms_mod(x2, n3_ref[...], mod[6:7, :], mod[7:8, :]).astype(BF16)
    x3 = x2 + (0.5 * mod[8:9, :]) * _swiglu_acc(h, wg_ref, wu_ref, wd_ref)
    ms = jnp.mean(x3 * x3, axis=-1, keepdims=True)
    out = x3 * lax.rsqrt(ms + RMS_EPS) * nf_ref[...]

    @pl.when(is_ctx)
    def _():
        oc_ref[...] = out

    @pl.when(jnp.logical_not(is_ctx))
    def _():
        ol_ref[...] = out


def _out_ffn(x1, ys_ctx, ys_lat, yh_ctx, yh_lat, mod, n_lat, l_lat, n3, nf, wo, wg, wu, wd):
    tm = TOKEN_TILE
    ctx_tiles = ys_ctx.shape[0] // tm
    tokens = x1.shape[0]
    return pl.pallas_call(
        functools.partial(_out_ffn_kernel, ctx_tiles=ctx_tiles),
        grid=(tokens // tm,),
        in_specs=[pl.BlockSpec((tm, D_MODEL), lambda i: (i, 0))]
        + _group_specs(ctx_tiles, SSD_W) + _group_specs(ctx_tiles, HY_W) + [
            _mod_spec(ctx_tiles, n_lat, l_lat // tm),
            _const_spec((1, D_MODEL)), _const_spec((1, D_MODEL)),
            _const_spec(wo.shape), _const_spec(wg.shape), _const_spec(wu.shape), _const_spec(wd.shape)],
        out_specs=_group_specs(ctx_tiles, D_MODEL),
        out_shape=[jax.ShapeDtypeStruct((ys_ctx.shape[0], D_MODEL), F32),
                   jax.ShapeDtypeStruct((ys_lat.shape[0], D_MODEL), F32)],
        compiler_params=_params("arbitrary"),
        name="out_ffn",
    )(x1, ys_ctx, ys_lat, yh_ctx, yh_lat, mod, n3, nf, wo, wg, wu, wd)


def _ssd_kernel(z_ref, xbc_ref, dt_ref, init_ref, cw_ref, cb_ref, dtb_ref, alog_ref, dexp_ref, nw_ref, e_ref,
                y_ref, fin_ref,
                xs_s, xd_s, b_s, c_s, a_s, ec_s, dst_s, et_s, st_s,
                *, l, seg, zero_init):
    q = SSD_CHUNK
    nc = l // q

    first, last = _seg_edges(l, seg)
    u = _silu(_short_conv(xbc_ref[...], cw_ref[...], cb_ref[...], first, last, l))
    xs = u[:, :SSD_W]
    xs_s[...] = xs
    b_s[...] = u[:, SSD_W:SSD_W + LANES]
    c_s[...] = u[:, SSD_W + LANES:]

    dt = _softplus(dt_ref[...] + dtb_ref[...])
    a_s[...] = dt * (-jnp.exp(alog_ref[...]))
    e_mat = e_ref[...].astype(BF16)
    xd_s[...] = jnp.concatenate([xs, xs], axis=1) * _dot(dt.astype(BF16), e_mat)

    row_g = lax.broadcasted_iota(jnp.int32, (LANES, N_DIR * HP), 0) // SSD_STATE
    lane_g = (lax.broadcasted_iota(jnp.int32, (LANES, N_DIR * HP), 1) % HP) // (HP // SSD_GROUPS)
    own_t = row_g == lane_g

    ii = lax.broadcasted_iota(jnp.int32, (q, q), 0)
    jj = lax.broadcasted_iota(jnp.int32, (q, q), 1)
    tri_incl = (jj <= ii).astype(BF16)
    lane = lax.broadcasted_iota(jnp.int32, (q, LANES), 1)
    low_half = lane < SSD_STATE
    is_fwd_col = lane < SSD_HEADS

    def local_pass(c, carry):
        r0 = pl.multiple_of(c * q, q)
        rows = pl.ds(r0, q)
        a_c = a_s[rows, :]
        a1, a2, a3 = _split3(a_c)
        cum_f = _dot(tri_incl, a1) + _dot(tri_incl, a2) + _dot(tri_incl, a3)
        tot = cum_f[q - 1:q, :]
        cum = jnp.where(is_fwd_col, cum_f, tot - cum_f + a_c)
        cum_t = cum.T
        ec_s[rows, :] = _dot(jnp.exp(cum).astype(BF16), e_mat)
        et_s[c] = _dot_exact_lhs(jnp.broadcast_to(jnp.exp(tot), (SUBLANES, LANES)), e_mat)
        xd = xd_s[rows, :]
        w = (xd * _dot(jnp.exp(tot - cum).astype(BF16), e_mat)).astype(BF16)
        bc = b_s[rows, :]
        bcb = bc.astype(BF16)
        dst_s[c] = jnp.where(own_t, _dot(bc.T.astype(BF16), w), 0.0)

        cc = c_s[rows, :]
        g_mats = [_dot_nt(jnp.where(low_half, cc, 0.0).astype(BF16), bcb),
                  _dot_nt(jnp.where(low_half, 0.0, cc).astype(BF16), bcb)]
        y_parts = []
        for pair in range(SSD_HEADS // 2):
            g = pair // (SSD_HEADS // 2 // SSD_GROUPS)
            acc = None
            for d in range(N_DIR):
                keep = (ii >= jj) if d == 0 else (ii <= jj)
                s_mats = []
                for hh in (2 * pair, 2 * pair + 1):
                    col = d * SSD_HEADS + hh
                    diff = cum[:, col:col + 1] - cum_t[col:col + 1, :]
                    decay = jnp.exp(jnp.where(keep, diff, NEG_BIG))
                    s_mats.append((g_mats[g] * decay).astype(BF16))
                lhs = jnp.concatenate(s_mats, axis=1)
                xp = xd[:, d * HP + pair * LANES:d * HP + (pair + 1) * LANES]
                rhs = jnp.concatenate([jnp.where(low_half, xp, 0.0), jnp.where(low_half, 0.0, xp)],
                                      axis=0).astype(BF16)
                part = _dot(lhs, rhs)
                acc = part if acc is None else acc + part
            y_parts.append(acc)
        y_ref[rows, :] = jnp.concatenate(y_parts, axis=1)
        return carry

    lax.fori_loop(0, nc, local_pass, 0, unroll=2)

    for d in range(N_DIR):
        if zero_init:
            st_s[d] = jnp.zeros((LANES, HP), F32)
        else:
            s0 = init_ref[d]
            st_s[d] = jnp.where(own_t[:, :HP], jnp.concatenate([s0, s0], axis=1).T, 0.0)

    for k in range(nc):
        for d in range(N_DIR):
            c = k if d == 0 else nc - 1 - k
            rows = slice(c * q, (c + 1) * q)
            lanes = slice(d * HP, (d + 1) * HP)
            state = st_s[d]
            y_off = _dot(c_s[rows, :].astype(BF16), state.astype(BF16)) * ec_s[rows, lanes]
            y_ref[rows, :] = y_ref[rows, :] + y_off
            st_s[d] = state * et_s[c, 0:1, lanes] + dst_s[c, :, lanes]

    y = y_ref[...] + xs_s[...] * dexp_ref[...]
    y = y * _silu(z_ref[...])
    ms = jnp.mean(y * y, axis=-1, keepdims=True)
    y_ref[...] = y * lax.rsqrt(ms + RMS_EPS) * nw_ref[...]

    half = HP // SSD_GROUPS
    for d in range(N_DIR):
        st = st_s[d].T
        fin_ref[d, 0:half, :] = st[0:half, 0:SSD_STATE]
        fin_ref[d, half:HP, :] = st[half:HP, SSD_STATE:2 * SSD_STATE]


def _ssd(z, xbc, dt, row0, bsz, l, init, cw, cb, dtb, alog, dexp, nw, e_mat, *, seg):
    zero_init = init is None
    if zero_init:
        init = jnp.zeros((1, N_DIR, HP, SSD_STATE), F32)
        init_map = lambda b: (0, 0, 0, 0)
    else:
        init_map = lambda b: (b, 0, 0, 0)
    blk0 = row0 // l
    seq_spec = lambda n: pl.BlockSpec((l, n), lambda b: (blk0 + b, 0))
    const = lambda *shape: pl.BlockSpec(shape, lambda b: (0,) * len(shape))
    nc = l // SSD_CHUNK
    return pl.pallas_call(
        functools.partial(_ssd_kernel, l=l, seg=seg, zero_init=zero_init),
        grid=(bsz,),
        in_specs=[seq_spec(SSD_W), seq_spec(SSD_XBC), seq_spec(DT_PAD),
                  pl.BlockSpec((None, N_DIR, HP, SSD_STATE), init_map),
                  const(3, SSD_XBC), const(1, SSD_XBC), const(1, DT_PAD), const(1, DT_PAD),
                  const(1, SSD_W), const(1, SSD_W), const(LANES, N_DIR * HP)],
        out_specs=[pl.BlockSpec((l, SSD_W), lambda b: (b, 0)),
                   pl.BlockSpec((None, N_DIR, HP, SSD_STATE), lambda b: (b, 0, 0, 0))],
        out_shape=[jax.ShapeDtypeStruct((bsz * l, SSD_W), F32),
                   jax.ShapeDtypeStruct((bsz, N_DIR, HP, SSD_STATE), F32)],
        scratch_shapes=[pltpu.VMEM((l, SSD_W), F32), pltpu.VMEM((l, N_DIR * HP), F32),
                        pltpu.VMEM((l, LANES), F32), pltpu.VMEM((l, LANES), F32), pltpu.VMEM((l, DT_PAD), F32),
                        pltpu.VMEM((l, N_DIR * HP), F32), pltpu.VMEM((nc, LANES, N_DIR * HP), F32),
                        pltpu.VMEM((nc, SUBLANES, N_DIR * HP), F32), pltpu.VMEM((N_DIR, LANES, HP), F32)],
        compiler_params=_params("arbitrary"),
        name=f"ssd{l}",
    )(z, xbc, dt, init, cw, cb, dtb, alog, dexp, nw, e_mat)


def _hyena_kernel(v_ref, x1_ref, x2_ref, wv_ref, w1_ref, w2_ref, bv_ref, b1_ref, b2_ref,
                  f_ref, g_ref, kr_ref, ki_ref, kn_ref, skip_ref, o_ref, *, l, seg):
    first, last = _seg_edges(l, seg)
    zz = _short_conv(v_ref[...], wv_ref[...], bv_ref[...], first, last, l)
    gates = (_short_conv(x1_ref[...], w1_ref[...], b1_ref[...], first, last, l),
             _short_conv(x2_ref[...], w2_ref[...], b2_ref[...], first, last, l))
    row = lax.broadcasted_iota(jnp.int32, (l, 1), 0)
    for i in range(HY_ORDER):
        spec = _dot(f_ref[...], zz.astype(BF16))
        top = spec[:l]
        bot = spec[l:]
        kr = kr_ref[i]
        ki = ki_ref[i]
        yr = top * kr - bot * ki
        yi = jnp.where(row == 0, bot * kn_ref[i], top * ki + bot * kr)
        prod = jnp.concatenate([yr, yi], axis=0).astype(BF16)
        conv = _dot(g_ref[...], prod)
        zz = gates[i] * (conv + zz * skip_ref[i:i + 1, :])
    o_ref[...] = zz


def _hyena(hy, row0, bsz, l, cw, cb, fmat, gmat, kr, ki, kn, skip, *, seg):
    ct = HY_CH_TILE
    nct = HY_W // ct
    blk0 = row0 // l
    part = lambda p: pl.BlockSpec((l, ct), lambda j, b: (blk0 + b, p * nct + j))
    wpart = lambda p: pl.BlockSpec((3, ct), lambda j, b: (0, p * nct + j))
    bpart = lambda p: pl.BlockSpec((1, ct), lambda j, b: (0, p * nct + j))
    return pl.pallas_call(
        functools.partial(_hyena_kernel, l=l, seg=seg),
        grid=(nct, bsz),
        in_specs=[part(0), part(1), part(2), wpart(0), wpart(1), wpart(2), bpart(0), bpart(1), bpart(2),
                  pl.BlockSpec((2 * l, l), lambda j, b: (0, 0)),
                  pl.BlockSpec((l, 2 * l), lambda j, b: (0, 0)),
                  pl.BlockSpec((HY_ORDER, l, ct), lambda j, b: (0, 0, j)),
                  pl.BlockSpec((HY_ORDER, l, ct), lambda j, b: (0, 0, j)),
                  pl.BlockSpec((HY_ORDER, 1, ct), lambda j, b: (0, 0, j)),
                  pl.BlockSpec((HY_ORDER, ct), lambda j, b: (0, j))],
        out_specs=pl.BlockSpec((l, ct), lambda j, b: (b, j)),
        out_shape=jax.ShapeDtypeStruct((bsz * l, HY_W), F32),
        compiler_params=_params("arbitrary", "arbitrary"),
        name=f"hyena{l}",
    )(hy, hy, hy, cw, cw, cw, cb, cb, cb, fmat, gmat, kr, ki, kn, skip)


def _dft_mats(l):
    n = 2 * l
    f = np.arange(l, dtype=np.int64)[:, None]
    t = np.arange(l, dtype=np.int64)[None, :]
    ang = 2.0 * np.pi * ((f * t) % n).astype(np.float64) / n
    alt = np.where(np.arange(l) % 2 == 0, 1.0, -1.0)
    top = np.cos(ang)
    bot = -np.sin(ang)
    bot[0, :] = alt
    fwd = np.concatenate([top, bot], axis=0)
    wf = np.full((l,), 2.0)
    wf[0] = 1.0
    gtop = np.cos(ang).T * wf[None, :] / n
    gbot = -np.sin(ang).T * 2.0 / n
    gbot[:, 0] = alt / n
    inv = np.concatenate([gtop, gbot], axis=1)
    return fwd.astype(np.float32), inv.astype(np.float32)


def _filter_feats(l):
    t = np.linspace(0.0, 1.0, l)[:, None]
    w = (2.0 * np.pi / l) * np.arange(l, dtype=np.float64)[:, None]
    f = np.linspace(1e-4, HY_BANDS - 1, HY_BANDS)[None, :]
    feats = np.concatenate([t, np.cos(f * w), -np.sin(f * w)], axis=-1)
    out = np.zeros((l, EMB_PAD), np.float32)
    out[:, :HY_EMB] = feats
    return out


def _head_expand():
    e = np.zeros((LANES, N_DIR * HP), np.float32)
    for j in range(N_DIR * SSD_HEADS):
        e[j, j * SSD_HEAD_DIM:(j + 1) * SSD_HEAD_DIM] = 1.0
    return e


def kernel(x_prompt, x_sample, state_ssd, c, c_ctx, w_ada, b_ada, norm_ffn1, ffn1_w_gate, ffn1_w_up, ffn1_w_down, norm_mix, w_in, w_out, ssd_conv_w, ssd_conv_b, ssd_dt_bias, ssd_a_log, ssd_d, ssd_norm_w, hy_conv_w, hy_conv_b, hy_w1, hy_b1, hy_freq, hy_w2, hy_b2, hy_w3, hy_skip, norm_ffn2, ffn2_w_gate, ffn2_w_up, ffn2_w_down, norm_final):
    assert w_ada.shape[0] == 1, "single layer"
    n_ctx, l_ctx, _ = x_prompt.shape
    n_lat, l_lat, _ = x_sample.shape
    t_ctx = n_ctx * l_ctx

    cond = jnp.zeros((16, D_MODEL), F32).at[:n_lat].set(c).at[n_lat].set(c_ctx)
    mod = _ada(cond, w_ada, b_ada).reshape(16, N_MOD, D_MODEL)

    wg1, wu1, wg2, wu2 = _cast_pad_cols([ffn1_w_gate, ffn1_w_up, ffn2_w_gate, ffn2_w_up])
    wd1, wd2 = _cast_pad_rows([ffn1_w_down, ffn2_w_down])
    wz, wx, wt, wh, wo = _cast_mix(w_in, w_out)

    row = lambda v: v.reshape(1, -1)
    pad_dt = lambda v: jnp.pad(v.reshape(1, -1), ((0, 0), (0, DT_PAD - N_DIR * SSD_HEADS)))
    dtb, alog = pad_dt(ssd_dt_bias[0]), pad_dt(ssd_a_log[0])
    dexp = jnp.repeat(ssd_d[0], SSD_HEAD_DIM).reshape(1, SSD_W)
    e_mat = jnp.asarray(_head_expand())
    w1p = jnp.pad(hy_w1[0], ((0, EMB_PAD - HY_EMB), (0, 0)))
    deltas = jnp.asarray(np.abs(np.linspace(HY_MIN_DECAY, HY_MAX_DECAY, HY_ORDER * HY_W))
                         .reshape(HY_ORDER, HY_W).astype(np.float32))

    x1, z, xbc, dt, hy = _ffn_in(x_prompt.reshape(t_ctx, D_MODEL), x_sample.reshape(n_lat * l_lat, D_MODEL), mod,
                                 n_lat, l_lat, row(norm_ffn1[0]), row(norm_mix[0]), wg1, wu1, wd1, wz, wx, wt, wh)

    def mixers(row0, bsz, l, init, seg):
        fwd_np, inv_np = _dft_mats(l)
        fmat = jnp.asarray(fwd_np).astype(BF16)
        gmat = jnp.asarray(inv_np).astype(BF16)
        kr, ki, kn = _filters(l, jnp.asarray(_filter_feats(l)), w1p, row(hy_b1[0]), row(hy_freq[0]), hy_w2[0],
                              row(hy_b2[0]), hy_w3[0], deltas, fmat)
        ys, fin = _ssd(z, xbc, dt, row0, bsz, l, init, ssd_conv_w[0], row(ssd_conv_b[0]), dtb, alog, dexp,
                       row(ssd_norm_w[0]), e_mat, seg=seg)
        yh = _hyena(hy, row0, bsz, l, hy_conv_w[0], row(hy_conv_b[0]), fmat, gmat, kr, ki, kn, hy_skip[0], seg=seg)
        return ys, yh, fin

    ys_ctx, yh_ctx, ctx_fin = mixers(0, n_ctx, l_ctx, None, l_ctx)
    lat_init = state_ssd[:, 0].reshape(n_lat, N_DIR, HP, SSD_STATE)
    ys_lat, yh_lat, _ = mixers(t_ctx, n_lat, l_lat, lat_init, GRID_W)

    y_ctx, y_lat = _out_ffn(x1, ys_ctx, ys_lat, yh_ctx, yh_lat, mod, n_lat, l_lat, row(norm_ffn2[0]),
                            row(norm_final), wo, wg2, wu2, wd2)
    new_state = ctx_fin.reshape(n_ctx, 1, N_DIR, SSD_HEADS, SSD_HEAD_DIM, SSD_STATE).astype(x_prompt.dtype)
    return (y_ctx.reshape(n_ctx, l_ctx, D_MODEL), y_lat.reshape(n_lat, l_lat, D_MODEL), new_state)
```

```python
import functools
import math

import numpy as np
import jax
import jax.numpy as jnp
from jax import lax
from jax.experimental import pallas as pl
from jax.experimental.pallas import tpu as pltpu

F32 = jnp.float32
BF16 = jnp.bfloat16

D_MODEL = 1024
GRID_W = 64
N_MOD = 9
RMS_EPS = 1e-6
FFN_DIM = 2752
SSD_W = 512
SSD_HEADS = 8
SSD_HEAD_DIM = 64
SSD_STATE = 64
SSD_GROUPS = 2
SSD_CHUNK = 128
SSD_XBC = SSD_W + 2 * SSD_GROUPS * SSD_STATE
N_DIR = 2
HY_W = 512
HY_ORDER = 2
HY_EMB = 33
HY_BANDS = (HY_EMB - 1) // 2
HY_HIDDEN = 64
HY_MIN_DECAY = math.log(1e-2) / 1.5
HY_MAX_DECAY = math.log(1e-2) / 0.3
IN_SPLITS = (SSD_W, SSD_W + SSD_XBC, SSD_W + SSD_XBC + N_DIR * SSD_HEADS)
IN_COLS = IN_SPLITS[-1] + (HY_ORDER + 1) * HY_W

LANES = 128
SUBLANES = 8
FFN_PAD = 2816
FFN_CHUNK = 256
DT_PAD = LANES
IN_ROWS = IN_COLS + DT_PAD - N_DIR * SSD_HEADS
EMB_PAD = LANES
TOKEN_TILE = 512
HY_CH_TILE = 256
VMEM_LIMIT = 56 * 1024 * 1024
NEG_BIG = -1e30
HP = SSD_HEADS * SSD_HEAD_DIM


def _silu(x):
    return x * jax.nn.sigmoid(x)


def _softplus(x):
    return jnp.maximum(x, 0.0) + jnp.log1p(jnp.exp(-jnp.abs(x)))


def _rms_mod(x, gain, shift, scale):
    ms = jnp.mean(x * x, axis=-1, keepdims=True)
    y = x * lax.rsqrt(ms + RMS_EPS) * gain
    return y * (1.0 + scale) + shift


def _dot(a, b):
    return jnp.dot(a, b, preferred_element_type=F32)


def _dot_nt(a, b):
    return lax.dot_general(a, b, (((1,), (1,)), ((), ())), preferred_element_type=F32)


def _split3(x):
    hi = x.astype(BF16)
    r = x - hi.astype(F32)
    mid = r.astype(BF16)
    lo = (r - mid.astype(F32)).astype(BF16)
    return hi, mid, lo


def _dot_exact_lhs(x, m01):
    hi, mid, lo = _split3(x)
    return _dot(hi, m01) + _dot(mid, m01) + _dot(lo, m01)


def _short_conv(x, w, b, first, last, l):
    prev = jnp.where(first, 0.0, pltpu.roll(x, 1, 0))
    nxt = jnp.where(last, 0.0, pltpu.roll(x, l - 1, 0))
    return prev * w[0:1, :] + x * w[1:2, :] + nxt * w[2:3, :] + b


def _seg_edges(l, seg):
    pos = lax.broadcasted_iota(jnp.int32, (l, 1), 0) & (seg - 1)
    return pos == 0, pos == seg - 1


def _swiglu_acc(h, wg_ref, wu_ref, wd_ref):
    acc = None
    for k in range(FFN_PAD // FFN_CHUNK):
        cols = slice(k * FFN_CHUNK, (k + 1) * FFN_CHUNK)
        g = _dot_nt(h, wg_ref[cols, :])
        u = _dot_nt(h, wu_ref[cols, :])
        a = (_silu(g) * u).astype(BF16)
        part = _dot(a, wd_ref[cols, :])
        acc = part if acc is None else acc + part
    return acc


def _params(*semantics):
    return pltpu.CompilerParams(dimension_semantics=semantics, vmem_limit_bytes=VMEM_LIMIT)


def _cast_pad_rows_kernel(*refs):
    n = len(refs) // 2
    rows = refs[0].shape[0]
    row = pl.program_id(0) * rows + lax.broadcasted_iota(jnp.int32, (rows, 1), 0)
    for w_ref, o_ref in zip(refs[:n], refs[n:]):
        o_ref[...] = jnp.where(row < FFN_DIM, w_ref[...], 0.0).astype(BF16)


def _cast_pad_rows(ws):
    n = len(ws)
    return pl.pallas_call(
        _cast_pad_rows_kernel,
        grid=(FFN_PAD // FFN_CHUNK,),
        in_specs=[pl.BlockSpec((None, FFN_CHUNK, D_MODEL), lambda i: (0, i, 0))] * n,
        out_specs=[pl.BlockSpec((FFN_CHUNK, D_MODEL), lambda i: (i, 0))] * n,
        out_shape=[jax.ShapeDtypeStruct((FFN_PAD, D_MODEL), BF16)] * n,
        compiler_params=_params("arbitrary"),
        name="wcast_rows",
    )(*ws)


def _cast_mix_kernel(wi_ref, wo_ref, wib_ref, wob_ref):
    o1, o2, o3 = IN_SPLITS
    wib_ref[0:o3, :] = wi_ref[0:o3, :].astype(BF16)
    wib_ref[o3:o2 + DT_PAD, :] = jnp.zeros((o2 + DT_PAD - o3, D_MODEL), BF16)
    wib_ref[o2 + DT_PAD:, :] = wi_ref[o3:, :].astype(BF16)
    wob_ref[...] = wo_ref[...].astype(BF16)


def _cast_mix(w_in_t, w_out):
    full = lambda *shape: pl.BlockSpec(shape, lambda: (0,) * len(shape))
    sq = lambda *shape: pl.BlockSpec((None,) + shape, lambda: (0,) * (len(shape) + 1))
    return pl.pallas_call(
        _cast_mix_kernel,
        in_specs=[sq(IN_COLS, D_MODEL), sq(D_MODEL, D_MODEL)],
        out_specs=[full(IN_ROWS, D_MODEL), full(D_MODEL, D_MODEL)],
        out_shape=[jax.ShapeDtypeStruct((IN_ROWS, D_MODEL), BF16), jax.ShapeDtypeStruct((D_MODEL, D_MODEL), BF16)],
        compiler_params=pltpu.CompilerParams(vmem_limit_bytes=VMEM_LIMIT),
        name="wcast_mix",
    )(w_in_t, w_out)


def _ada_kernel(cond_ref, w_ref, b_ref, o_ref):
    s = _silu(cond_ref[...]).astype(BF16)
    o_ref[...] = _dot(s, w_ref[...].astype(BF16)) + b_ref[...]


def _ada(cond, w_ada, b_ada):
    rows = cond.shape[0]
    n = w_ada.shape[-1]
    tn = D_MODEL
    return pl.pallas_call(
        _ada_kernel,
        grid=(n // tn,),
        in_specs=[
            pl.BlockSpec((rows, D_MODEL), lambda j: (0, 0)),
            pl.BlockSpec((None, D_MODEL, tn), lambda j: (0, 0, j)),
            pl.BlockSpec((1, tn), lambda j: (0, j)),
        ],
        out_specs=pl.BlockSpec((rows, tn), lambda j: (0, j)),
        out_shape=jax.ShapeDtypeStruct((rows, n), F32),
        compiler_params=_params("arbitrary"),
        name="ada",
    )(cond, w_ada, b_ada)


def _filter_kernel(feats_ref, w1_ref, b1_ref, fr_ref, w2_ref, b2_ref, w3_ref, dl_ref, f_ref,
                   kr_ref, ki_ref, kn_ref, *, l):
    feats = feats_ref[...]
    freq = fr_ref[...]
    h = jnp.sin(freq * (_dot(feats.astype(BF16), w1_ref[...].astype(BF16)) + b1_ref[...]))
    h = jnp.sin(freq * (_dot(h.astype(BF16), w2_ref[...].astype(BF16)) + b2_ref[...]))
    hb = h.astype(BF16)
    t = feats[:, 0:1]
    row = lax.broadcasted_iota(jnp.int32, (l, 1), 0)
    sign = jnp.where((row & 1) == 0, 1.0, -1.0)
    for i in range(HY_ORDER):
        window = jnp.exp(-t * dl_ref[i:i + 1, :])
        c0 = (0 * HY_ORDER + i) * HY_W
        c1 = (1 * HY_ORDER + i) * HY_W
        k0 = _dot(hb, w3_ref[:, c0:c0 + HY_W].astype(BF16)) * window
        k1 = _dot(hb, w3_ref[:, c1:c1 + HY_W].astype(BF16)) * window
        k1 = jnp.where(row == 0, 0.0, k1)
        ks = k0 + k1
        kd = k0 - k1
        kr_ref[i] = _dot(f_ref[0:l, :], ks.astype(BF16))
        ki = _dot(f_ref[l:2 * l, :], kd.astype(BF16))
        ki_ref[i] = jnp.where(row == 0, 0.0, ki)
        kn_ref[i] = jnp.sum(ks * sign, axis=0, keepdims=True)


def _filters(l, feats, w1p, b1, freq, w2, b2, w3, deltas, fmat):
    full = lambda *shape: pl.BlockSpec(shape, lambda: (0,) * len(shape))
    return pl.pallas_call(
        functools.partial(_filter_kernel, l=l),
        in_specs=[full(l, EMB_PAD), full(EMB_PAD, HY_HIDDEN), full(1, HY_HIDDEN), full(1, HY_HIDDEN),
                  full(HY_HIDDEN, HY_HIDDEN), full(1, HY_HIDDEN), full(HY_HIDDEN, N_DIR * HY_ORDER * HY_W),
                  full(HY_ORDER, HY_W), full(2 * l, l)],
        out_specs=[full(HY_ORDER, l, HY_W), full(HY_ORDER, l, HY_W), full(HY_ORDER, 1, HY_W)],
        out_shape=[jax.ShapeDtypeStruct((HY_ORDER, l, HY_W), F32),
                   jax.ShapeDtypeStruct((HY_ORDER, l, HY_W), F32),
                   jax.ShapeDtypeStruct((HY_ORDER, 1, HY_W), F32)],
        compiler_params=pltpu.CompilerParams(vmem_limit_bytes=VMEM_LIMIT),
        name=f"filt{l}",
    )(feats, w1p, b1, freq, w2, b2, w3, deltas, fmat)


def _const_spec(shape):
    return pl.BlockSpec(shape, lambda i: (0,) * len(shape), pipeline_mode=pl.Buffered(1))


def _group_specs(ctx_tiles, width):
    tm = TOKEN_TILE
    return [pl.BlockSpec((tm, width), lambda i: (jnp.minimum(i, ctx_tiles - 1), 0)),
            pl.BlockSpec((tm, width), lambda i: (jnp.maximum(i - ctx_tiles, 0), 0))]


def _mod_spec(ctx_tiles, n_lat, tiles_per_seq):
    return pl.BlockSpec((None, N_MOD, D_MODEL),
                        lambda i: (jnp.where(i < ctx_tiles, n_lat, (i - ctx_tiles) // tiles_per_seq), 0, 0))


def _ffn_in_kernel(xc_ref, xl_ref, mod_ref, n1_ref, nm_ref, wg_ref, wu_ref, wd_ref, wi_ref,
                   x1_ref, z_ref, xbc_ref, dt_ref, hy_ref, *, ctx_tiles):
    x = jnp.where(pl.program_id(0) < ctx_tiles, xc_ref[...], xl_ref[...])
    mod = mod_ref[...]
    h = _rms_mod(x, n1_ref[...], mod[0:1, :], mod[1:2, :]).astype(BF16)
    x1 = x + (0.5 * mod[2:3, :]) * _swiglu_acc(h, wg_ref, wu_ref, wd_ref)
    x1_ref[...] = x1
    h2 = _rms_mod(x1, nm_ref[...], mod[3:4, :], mod[4:5, :]).astype(BF16)
    o1, o2, _ = IN_SPLITS
    z_ref[...] = _dot_nt(h2, wi_ref[0:o1, :])
    xbc_ref[...] = _dot_nt(h2, wi_ref[o1:o2, :])
    dt_ref[...] = _dot_nt(h2, wi_ref[o2:o2 + DT_PAD, :])
    hy_ref[...] = _dot_nt(h2, wi_ref[o2 + DT_PAD:, :])


def _ffn_in(x_ctx, x_lat, mod, n_lat, l_lat, n1, nm, wg, wu, wd, wi):
    tm = TOKEN_TILE
    ctx_tiles = x_ctx.shape[0] // tm
    tokens = x_ctx.shape[0] + x_lat.shape[0]
    row_spec = lambda n: pl.BlockSpec((tm, n), lambda i: (i, 0))
    widths = (D_MODEL, SSD_W, SSD_XBC, DT_PAD, (HY_ORDER + 1) * HY_W)
    return pl.pallas_call(
        functools.partial(_ffn_in_kernel, ctx_tiles=ctx_tiles),
        grid=(tokens // tm,),
        in_specs=_group_specs(ctx_tiles, D_MODEL) + [
            _mod_spec(ctx_tiles, n_lat, l_lat // tm),
            _const_spec((1, D_MODEL)), _const_spec((1, D_MODEL)),
            _const_spec(wg.shape), _const_spec(wu.shape), _const_spec(wd.shape), _const_spec(wi.shape)],
        out_specs=[row_spec(n) for n in widths],
        out_shape=[jax.ShapeDtypeStruct((tokens, n), F32) for n in widths],
        compiler_params=_params("arbitrary"),
        name="ffn_in",
    )(x_ctx, x_lat, mod, n1, nm, wg, wu, wd, wi)


def _out_ffn_kernel(x1_ref, ysc_ref, ysl_ref, yhc_ref, yhl_ref, mod_ref, n3_ref, nf_ref,
                    wo_ref, wg_ref, wu_ref, wd_ref, oc_ref, ol_ref, *, ctx_tiles):
    is_ctx = pl.program_id(0) < ctx_tiles
    mod = mod_ref[...]
    ys = jnp.where(is_ctx, ysc_ref[...], ysl_ref[...])
    yh = jnp.where(is_ctx, yhc_ref[...], yhl_ref[...])
    y = jnp.concatenate([ys, yh], axis=1).astype(BF16)
    x2 = x1_ref[...] + mod[5:6, :] * _dot(y, wo_ref[...])
    h = _rms_mod(x2, n3_ref[...], mod[6:7, :], mod[7:8, :]).astype(BF16)
    x3 = x2 + (0.5 * mod[8:9, :]) * _swiglu_acc(h, wg_ref, wu_ref, wd_ref)
    ms = jnp.mean(x3 * x3, axis=-1, keepdims=True)
    out = x3 * lax.rsqrt(ms + RMS_EPS) * nf_ref[...]

    @pl.when(is_ctx)
    def _():
        oc_ref[...] = out

    @pl.when(jnp.logical_not(is_ctx))
    def _():
        ol_ref[...] = out


def _out_ffn(x1, ys_ctx, ys_lat, yh_ctx, yh_lat, mod, n_lat, l_lat, n3, nf, wo, wg, wu, wd):
    tm = TOKEN_TILE
    ctx_tiles = ys_ctx.shape[0] // tm
    tokens = x1.shape[0]
    return pl.pallas_call(
        functools.partial(_out_ffn_kernel, ctx_tiles=ctx_tiles),
        grid=(tokens // tm,),
        in_specs=[pl.BlockSpec((tm, D_MODEL), lambda i: (i, 0))]
        + _group_specs(ctx_tiles, SSD_W) + _group_specs(ctx_tiles, HY_W) + [
            _mod_spec(ctx_tiles, n_lat, l_lat // tm),
            _const_spec((1, D_MODEL)), _const_spec((1, D_MODEL)),
            _const_spec(wo.shape), _const_spec(wg.shape), _const_spec(wu.shape), _const_spec(wd.shape)],
        out_specs=_group_specs(ctx_tiles, D_MODEL),
        out_shape=[jax.ShapeDtypeStruct((ys_ctx.shape[0], D_MODEL), F32),
                   jax.ShapeDtypeStruct((ys_lat.shape[0], D_MODEL), F32)],
        compiler_params=_params("arbitrary"),
        name="out_ffn",
    )(x1, ys_ctx, ys_lat, yh_ctx, yh_lat, mod, n3, nf, wo, wg, wu, wd)


def _ssd_kernel(z_ref, xbc_ref, dt_ref, init_ref, cw_ref, cb_ref, dtb_ref, alog_ref, dexp_ref, nw_ref, e_ref,
                y_ref, fin_ref,
                xs_s, xd_s, b_s, c_s, a_s, ec_s, dst_s, et_s, st_s,
                *, l, seg, zero_init):
    q = SSD_CHUNK
    nc = l // q

    first, last = _seg_edges(l, seg)
    u = _silu(_short_conv(xbc_ref[...], cw_ref[...], cb_ref[...], first, last, l))
    xs = u[:, :SSD_W]
    xs_s[...] = xs
    b_s[...] = u[:, SSD_W:SSD_W + LANES]
    c_s[...] = u[:, SSD_W + LANES:]

    dt = _softplus(dt_ref[...] + dtb_ref[...])
    a_s[...] = dt * (-jnp.exp(alog_ref[...]))
    e_mat = e_ref[...].astype(BF16)
    xd_s[...] = jnp.concatenate([xs, xs], axis=1) * _dot(dt.astype(BF16), e_mat)

    row_g = lax.broadcasted_iota(jnp.int32, (LANES, N_DIR * HP), 0) // SSD_STATE
    lane_g = (lax.broadcasted_iota(jnp.int32, (LANES, N_DIR * HP), 1) % HP) // (HP // SSD_GROUPS)
    own_t = row_g == lane_g

    ii = lax.broadcasted_iota(jnp.int32, (q, q), 0)
    jj = lax.broadcasted_iota(jnp.int32, (q, q), 1)
    tri_incl = (jj <= ii).astype(BF16)
    lane = lax.broadcasted_iota(jnp.int32, (q, LANES), 1)
    low_half = lane < SSD_STATE
    is_fwd_col = lane < SSD_HEADS

    def local_pass(c, carry):
        r0 = pl.multiple_of(c * q, q)
        rows = pl.ds(r0, q)
        a_c = a_s[rows, :]
        a1, a2, a3 = _split3(a_c)
        cum_f = _dot(tri_incl, a1) + _dot(tri_incl, a2) + _dot(tri_incl, a3)
        tot = cum_f[q - 1:q, :]
        cum = jnp.where(is_fwd_col, cum_f, tot - cum_f + a_c)
        cum_t = cum.T
        ec_s[rows, :] = _dot(jnp.exp(cum).astype(BF16), e_mat)
        et_s[c] = _dot_exact_lhs(jnp.broadcast_to(jnp.exp(tot), (SUBLANES, LANES)), e_mat)
        xd = xd_s[rows, :]
        w = (xd * _dot(jnp.exp(tot - cum).astype(BF16), e_mat)).astype(BF16)
        bc = b_s[rows, :]
        bcb = bc.astype(BF16)
        dst_s[c] = jnp.where(own_t, _dot(bc.T.astype(BF16), w), 0.0)

        cc = c_s[rows, :]
        g_mats = [_dot_nt(jnp.where(low_half, cc, 0.0).astype(BF16), bcb),
                  _dot_nt(jnp.where(low_half, 0.0, cc).astype(BF16), bcb)]
        y_parts = []
        for pair in range(SSD_HEADS // 2):
            g = pair // (SSD_HEADS // 2 // SSD_GROUPS)
            acc = None
            for d in range(N_DIR):
                keep = (ii >= jj) if d == 0 else (ii <= jj)
                s_mats = []
                for hh in (2 * pair, 2 * pair + 1):
                    col = d * SSD_HEADS + hh
                    diff = cum[:, col:col + 1] - cum_t[col:col + 1, :]
                    decay = jnp.exp(jnp.where(keep, diff, NEG_BIG))
                    s_mats.append((g_mats[g] * decay).astype(BF16))
                lhs = jnp.concatenate(s_mats, axis=1)
                xp = xd[:, d * HP + pair * LANES:d * HP + (pair + 1) * LANES]
                rhs = jnp.concatenate([jnp.where(low_half, xp, 0.0), jnp.where(low_half, 0.0, xp)],
                                      axis=0).astype(BF16)
                part = _dot(lhs, rhs)
                acc = part if acc is None else acc + part
            y_parts.append(acc)
        y_ref[rows, :] = jnp.concatenate(y_parts, axis=1)
        return carry

    lax.fori_loop(0, nc, local_pass, 0, unroll=2)

    for d in range(N_DIR):
        if zero_init:
            st_s[d] = jnp.zeros((LANES, HP), F32)
        else:
            s0 = init_ref[d]
            st_s[d] = jnp.where(own_t[:, :HP], jnp.concatenate([s0, s0], axis=1).T, 0.0)

    for k in range(nc):
        for d in range(N_DIR):
            c = k if d == 0 else nc - 1 - k
            rows = slice(c * q, (c + 1) * q)
            lanes = slice(d * HP, (d + 1) * HP)
            state = st_s[d]
            y_off = _dot(c_s[rows, :].astype(BF16), state.astype(BF16)) * ec_s[rows, lanes]
            y_ref[rows, :] = y_ref[rows, :] + y_off
            st_s[d] = state * et_s[c, 0:1, lanes] + dst_s[c, :, lanes]

    y = y_ref[...] + xs_s[...] * dexp_ref[...]
    y = y * _silu(z_ref[...])
    ms = jnp.mean(y * y, axis=-1, keepdims=True)
    y_ref[...] = y * lax.rsqrt(ms + RMS_EPS) * nw_ref[...]

    half = HP // SSD_GROUPS
    for d in range(N_DIR):
        st = st_s[d].T
        fin_ref[d, 0:half, :] = st[0:half, 0:SSD_STATE]
        fin_ref[d, half:HP, :] = st[half:HP, SSD_STATE:2 * SSD_STATE]


def _ssd(z, xbc, dt, row0, bsz, l, init, cw, cb, dtb, alog, dexp, nw, e_mat, *, seg):
    zero_init = init is None
    if zero_init:
        init = jnp.zeros((1, N_DIR, HP, SSD_STATE), F32)
        init_map = lambda b: (0, 0, 0, 0)
    else:
        init_map = lambda b: (b, 0, 0, 0)
    blk0 = row0 // l
    seq_spec = lambda n: pl.BlockSpec((l, n), lambda b: (blk0 + b, 0))
    const = lambda *shape: pl.BlockSpec(shape, lambda b: (0,) * len(shape))
    nc = l // SSD_CHUNK
    return pl.pallas_call(
        functools.partial(_ssd_kernel, l=l, seg=seg, zero_init=zero_init),
        grid=(bsz,),
        in_specs=[seq_spec(SSD_W), seq_spec(SSD_XBC), seq_spec(DT_PAD),
                  pl.BlockSpec((None, N_DIR, HP, SSD_STATE), init_map),
                  const(3, SSD_XBC), const(1, SSD_XBC), const(1, DT_PAD), const(1, DT_PAD),
                  const(1, SSD_W), const(1, SSD_W), const(LANES, N_DIR * HP)],
        out_specs=[pl.BlockSpec((l, SSD_W), lambda b: (b, 0)),
                   pl.BlockSpec((None, N_DIR, HP, SSD_STATE), lambda b: (b, 0, 0, 0))],
        out_shape=[jax.ShapeDtypeStruct((bsz * l, SSD_W), F32),
                   jax.ShapeDtypeStruct((bsz, N_DIR, HP, SSD_STATE), F32)],
        scratch_shapes=[pltpu.VMEM((l, SSD_W), F32), pltpu.VMEM((l, N_DIR * HP), F32),
                        pltpu.VMEM((l, LANES), F32), pltpu.VMEM((l, LANES), F32), pltpu.VMEM((l, DT_PAD), F32),
                        pltpu.VMEM((l, N_DIR * HP), F32), pltpu.VMEM((nc, LANES, N_DIR * HP), F32),
                        pltpu.VMEM((nc, SUBLANES, N_DIR * HP), F32), pltpu.VMEM((N_DIR, LANES, HP), F32)],
        compiler_params=_params("arbitrary"),
        name=f"ssd{l}",
    )(z, xbc, dt, init, cw, cb, dtb, alog, dexp, nw, e_mat)


def _hyena_kernel(v_ref, x1_ref, x2_ref, wv_ref, w1_ref, w2_ref, bv_ref, b1_ref, b2_ref,
                  f_ref, g_ref, kr_ref, ki_ref, kn_ref, skip_ref, o_ref, *, l, seg):
    first, last = _seg_edges(l, seg)
    zz = _short_conv(v_ref[...], wv_ref[...], bv_ref[...], first, last, l)
    gates = (_short_conv(x1_ref[...], w1_ref[...], b1_ref[...], first, last, l),
             _short_conv(x2_ref[...], w2_ref[...], b2_ref[...], first, last, l))
    row = lax.broadcasted_iota(jnp.int32, (l, 1), 0)
    for i in range(HY_ORDER):
        spec = _dot(f_ref[...], zz.astype(BF16))
        top = spec[:l]
        bot = spec[l:]
        kr = kr_ref[i]
        ki = ki_ref[i]
        yr = top * kr - bot * ki
        yi = jnp.where(row == 0, bot * kn_ref[i], top * ki + bot * kr)
        prod = jnp.concatenate([yr, yi], axis=0).astype(BF16)
        conv = _dot(g_ref[...], prod)
        zz = gates[i] * (conv + zz * skip_ref[i:i + 1, :])
    o_ref[...] = zz


def _hyena(hy, row0, bsz, l, cw, cb, fmat, gmat, kr, ki, kn, skip, *, seg):
    ct = HY_CH_TILE
    nct = HY_W // ct
    blk0 = row0 // l
    part = lambda p: pl.BlockSpec((l, ct), lambda j, b: (blk0 + b, p * nct + j))
    wpart = lambda p: pl.BlockSpec((3, ct), lambda j, b: (0, p * nct + j))
    bpart = lambda p: pl.BlockSpec((1, ct), lambda j, b: (0, p * nct + j))
    return pl.pallas_call(
        functools.partial(_hyena_kernel, l=l, seg=seg),
        grid=(nct, bsz),
        in_specs=[part(0), part(1), part(2), wpart(0), wpart(1), wpart(2), bpart(0), bpart(1), bpart(2),
                  pl.BlockSpec((2 * l, l), lambda j, b: (0, 0)),
                  pl.BlockSpec((l, 2 * l), lambda j, b: (0, 0)),
                  pl.BlockSpec((HY_ORDER, l, ct), lambda j, b: (0, 0, j)),
                  pl.BlockSpec((HY_ORDER, l, ct), lambda j, b: (0, 0, j)),
                  pl.BlockSpec((HY_ORDER, 1, ct), lambda j, b: (0, 0, j)),
                  pl.BlockSpec((HY_ORDER, ct), lambda j, b: (0, j))],
        out_specs=pl.BlockSpec((l, ct), lambda j, b: (b, j)),
        out_shape=jax.ShapeDtypeStruct((bsz * l, HY_W), F32),
        compiler_params=_params("arbitrary", "arbitrary"),
        name=f"hyena{l}",
    )(hy, hy, hy, cw, cw, cw, cb, cb, cb, fmat, gmat, kr, ki, kn, skip)


def _dft_mats(l):
    n = 2 * l
    f = np.arange(l, dtype=np.int64)[:, None]
    t = np.arange(l, dtype=np.int64)[None, :]
    ang = 2.0 * np.pi * ((f * t) % n).astype(np.float64) / n
    alt = np.where(np.arange(l) % 2 == 0, 1.0, -1.0)
    top = np.cos(ang)
    bot = -np.sin(ang)
    bot[0, :] = alt
    fwd = np.concatenate([top, bot], axis=0)
    wf = np.full((l,), 2.0)
    wf[0] = 1.0
    gtop = np.cos(ang).T * wf[None, :] / n
    gbot = -np.sin(ang).T * 2.0 / n
    gbot[:, 0] = alt / n
    inv = np.concatenate([gtop, gbot], axis=1)
    return fwd.astype(np.float32), inv.astype(np.float32)


def _filter_feats(l):
    t = np.linspace(0.0, 1.0, l)[:, None]
    w = (2.0 * np.pi / l) * np.arange(l, dtype=np.float64)[:, None]
    f = np.linspace(1e-4, HY_BANDS - 1, HY_BANDS)[None, :]
    feats = np.concatenate([t, np.cos(f * w), -np.sin(f * w)], axis=-1)
    out = np.zeros((l, EMB_PAD), np.float32)
    out[:, :HY_EMB] = feats
    return out


def _head_expand():
    e = np.zeros((LANES, N_DIR * HP), np.float32)
    for j in range(N_DIR * SSD_HEADS):
        e[j, j * SSD_HEAD_DIM:(j + 1) * SSD_HEAD_DIM] = 1.0
    return e


def kernel(x_prompt, x_sample, state_ssd, c, c_ctx, w_ada, b_ada, norm_ffn1, ffn1_w_gate, ffn1_w_up, ffn1_w_down, norm_mix, w_in, w_out, ssd_conv_w, ssd_conv_b, ssd_dt_bias, ssd_a_log, ssd_d, ssd_norm_w, hy_conv_w, hy_conv_b, hy_w1, hy_b1, hy_freq, hy_w2, hy_b2, hy_w3, hy_skip, norm_ffn2, ffn2_w_gate, ffn2_w_up, ffn2_w_down, norm_final):
    assert w_ada.shape[0] == 1, "single layer"
    n_ctx, l_ctx, _ = x_prompt.shape
    n_lat, l_lat, _ = x_sample.shape
    t_ctx = n_ctx * l_ctx

    cond = jnp.zeros((16, D_MODEL), F32).at[:n_lat].set(c).at[n_lat].set(c_ctx)
    mod = _ada(cond, w_ada, b_ada).reshape(16, N_MOD, D_MODEL)

    tr = lambda w: jnp.swapaxes(w, 1, 2)
    wg1, wu1, wd1, wg2, wu2, wd2 = _cast_pad_rows([tr(ffn1_w_gate), tr(ffn1_w_up), ffn1_w_down,
                                                   tr(ffn2_w_gate), tr(ffn2_w_up), ffn2_w_down])
    wi, wo = _cast_mix(tr(w_in), w_out)

    row = lambda v: v.reshape(1, -1)
    pad_dt = lambda v: jnp.pad(v.reshape(1, -1), ((0, 0), (0, DT_PAD - N_DIR * SSD_HEADS)))
    dtb, alog = pad_dt(ssd_dt_bias[0]), pad_dt(ssd_a_log[0])
    dexp = jnp.repeat(ssd_d[0], SSD_HEAD_DIM).reshape(1, SSD_W)
    e_mat = jnp.asarray(_head_expand())
    w1p = jnp.pad(hy_w1[0], ((0, EMB_PAD - HY_EMB), (0, 0)))
    deltas = jnp.asarray(np.abs(np.linspace(HY_MIN_DECAY, HY_MAX_DECAY, HY_ORDER * HY_W))
                         .reshape(HY_ORDER, HY_W).astype(np.float32))

    x1, z, xbc, dt, hy = _ffn_in(x_prompt.reshape(t_ctx, D_MODEL), x_sample.reshape(n_lat * l_lat, D_MODEL), mod,
                                 n_lat, l_lat, row(norm_ffn1[0]), row(norm_mix[0]), wg1, wu1, wd1, wi)

    def mixers(row0, bsz, l, init, seg):
        fwd_np, inv_np = _dft_mats(l)
        fmat = jnp.asarray(fwd_np).astype(BF16)
        gmat = jnp.asarray(inv_np).astype(BF16)
        kr, ki, kn = _filters(l, jnp.asarray(_filter_feats(l)), w1p, row(hy_b1[0]), row(hy_freq[0]), hy_w2[0],
                              row(hy_b2[0]), hy_w3[0], deltas, fmat)
        ys, fin = _ssd(z, xbc, dt, row0, bsz, l, init, ssd_conv_w[0], row(ssd_conv_b[0]), dtb, alog, dexp,
                       row(ssd_norm_w[0]), e_mat, seg=seg)
        yh = _hyena(hy, row0, bsz, l, hy_conv_w[0], row(hy_conv_b[0]), fmat, gmat, kr, ki, kn, hy_skip[0], seg=seg)
        return ys, yh, fin

    ys_ctx, yh_ctx, ctx_fin = mixers(0, n_ctx, l_ctx, None, l_ctx)
    lat_init = state_ssd[:, 0].reshape(n_lat, N_DIR, HP, SSD_STATE)
    ys_lat, yh_lat, _ = mixers(t_ctx, n_lat, l_lat, lat_init, GRID_W)

    y_ctx, y_lat = _out_ffn(x1, ys_ctx, ys_lat, yh_ctx, yh_lat, mod, n_lat, l_lat, row(norm_ffn2[0]),
                            row(norm_final), wo, wg2, wu2, wd2)
    new_state = ctx_fin.reshape(n_ctx, 1, N_DIR, SSD_HEADS, SSD_HEAD_DIM, SSD_STATE).astype(x_prompt.dtype)
    return (y_ctx.reshape(n_ctx, l_ctx, D_MODEL), y_lat.reshape(n_lat, l_lat, D_MODEL), new_state)
```

```python
import functools
import math

import numpy as np
import jax
import jax.numpy as jnp
from jax import lax
from jax.experimental import pallas as pl
from jax.experimental.pallas import tpu as pltpu

F32 = jnp.float32
BF16 = jnp.bfloat16

D_MODEL = 1024
GRID_W = 64
N_MOD = 9
RMS_EPS = 1e-6
FFN_DIM = 2752
SSD_W = 512
SSD_HEADS = 8
SSD_HEAD_DIM = 64
SSD_STATE = 64
SSD_GROUPS = 2
SSD_CHUNK = 128
SSD_XBC = SSD_W + 2 * SSD_GROUPS * SSD_STATE
N_DIR = 2
HY_W = 512
HY_ORDER = 2
HY_EMB = 33
HY_BANDS = (HY_EMB - 1) // 2
HY_HIDDEN = 64
HY_MIN_DECAY = math.log(1e-2) / 1.5
HY_MAX_DECAY = math.log(1e-2) / 0.3
IN_SPLITS = (SSD_W, SSD_W + SSD_XBC, SSD_W + SSD_XBC + N_DIR * SSD_HEADS)
IN_COLS = IN_SPLITS[-1] + (HY_ORDER + 1) * HY_W

LANES = 128
SUBLANES = 8
FFN_PAD = 2816
FFN_CHUNK = 256
DT_PAD = LANES
IN_ROWS = IN_COLS + DT_PAD - N_DIR * SSD_HEADS
EMB_PAD = LANES
TOKEN_TILE = 512
HY_CH_TILE = 256
VMEM_LIMIT = 56 * 1024 * 1024
NEG_BIG = -1e30
HP = SSD_HEADS * SSD_HEAD_DIM


def _silu(x):
    return x * jax.nn.sigmoid(x)


def _softplus(x):
    return jnp.maximum(x, 0.0) + jnp.log1p(jnp.exp(-jnp.abs(x)))


def _rms_mod(x, gain, shift, scale):
    ms = jnp.mean(x * x, axis=-1, keepdims=True)
    y = x * lax.rsqrt(ms + RMS_EPS) * gain
    return y * (1.0 + scale) + shift


def _dot(a, b):
    return jnp.dot(a, b, preferred_element_type=F32)


def _dot_nt(a, b):
    return lax.dot_general(a, b, (((1,), (1,)), ((), ())), preferred_element_type=F32)


def _split3(x):
    hi = x.astype(BF16)
    r = x - hi.astype(F32)
    mid = r.astype(BF16)
    lo = (r - mid.astype(F32)).astype(BF16)
    return hi, mid, lo


def _dot_exact_lhs(x, m01):
    hi, mid, lo = _split3(x)
    return _dot(hi, m01) + _dot(mid, m01) + _dot(lo, m01)


def _short_conv(x, w, b, first, last, l):
    prev = jnp.where(first, 0.0, pltpu.roll(x, 1, 0))
    nxt = jnp.where(last, 0.0, pltpu.roll(x, l - 1, 0))
    return prev * w[0:1, :] + x * w[1:2, :] + nxt * w[2:3, :] + b


def _seg_edges(l, seg):
    pos = lax.broadcasted_iota(jnp.int32, (l, 1), 0) & (seg - 1)
    return pos == 0, pos == seg - 1


def _swiglu_acc(h, wg_ref, wu_ref, wd_ref):
    acc = None
    for k in range(FFN_PAD // FFN_CHUNK):
        cols = slice(k * FFN_CHUNK, (k + 1) * FFN_CHUNK)
        g = _dot_nt(h, wg_ref[cols, :])
        u = _dot_nt(h, wu_ref[cols, :])
        a = (_silu(g) * u).astype(BF16)
        part = _dot(a, wd_ref[cols, :])
        acc = part if acc is None else acc + part
    return acc


def _params(*semantics):
    return pltpu.CompilerParams(dimension_semantics=semantics, vmem_limit_bytes=VMEM_LIMIT)


def _cast_pad_rows_kernel(*refs):
    n = len(refs) // 2
    rows = refs[0].shape[0]
    row = pl.program_id(0) * rows + lax.broadcasted_iota(jnp.int32, (rows, 1), 0)
    for w_ref, o_ref in zip(refs[:n], refs[n:]):
        o_ref[...] = jnp.where(row < FFN_DIM, w_ref[...], 0.0).astype(BF16)


def _cast_pad_rows(ws):
    n = len(ws)
    return pl.pallas_call(
        _cast_pad_rows_kernel,
        grid=(FFN_PAD // FFN_CHUNK,),
        in_specs=[pl.BlockSpec((None, FFN_CHUNK, D_MODEL), lambda i: (0, i, 0))] * n,
        out_specs=[pl.BlockSpec((FFN_CHUNK, D_MODEL), lambda i: (i, 0))] * n,
        out_shape=[jax.ShapeDtypeStruct((FFN_PAD, D_MODEL), BF16)] * n,
        compiler_params=_params("arbitrary"),
        name="wcast_rows",
    )(*ws)


def _cast_mix_kernel(wi_ref, wo_ref, wib_ref, wob_ref):
    o1, o2, o3 = IN_SPLITS
    wib_ref[0:o3, :] = wi_ref[0:o3, :].astype(BF16)
    wib_ref[o3:o2 + DT_PAD, :] = jnp.zeros((o2 + DT_PAD - o3, D_MODEL), BF16)
    wib_ref[o2 + DT_PAD:, :] = wi_ref[o3:, :].astype(BF16)
    wob_ref[...] = wo_ref[...].astype(BF16)


def _cast_mix(w_in_t, w_out):
    full = lambda *shape: pl.BlockSpec(shape, lambda: (0,) * len(shape))
    sq = lambda *shape: pl.BlockSpec((None,) + shape, lambda: (0,) * (len(shape) + 1))
    return pl.pallas_call(
        _cast_mix_kernel,
        in_specs=[sq(IN_COLS, D_MODEL), sq(D_MODEL, D_MODEL)],
        out_specs=[full(IN_ROWS, D_MODEL), full(D_MODEL, D_MODEL)],
        out_shape=[jax.ShapeDtypeStruct((IN_ROWS, D_MODEL), BF16), jax.ShapeDtypeStruct((D_MODEL, D_MODEL), BF16)],
        compiler_params=pltpu.CompilerParams(vmem_limit_bytes=VMEM_LIMIT),
        name="wcast_mix",
    )(w_in_t, w_out)


def _ada_kernel(cond_ref, w_ref, b_ref, o_ref):
    s = _silu(cond_ref[...]).astype(BF16)
    o_ref[...] = _dot(s, w_ref[...].astype(BF16)) + b_ref[...]


def _ada(cond, w_ada, b_ada):
    rows = cond.shape[0]
    n = w_ada.shape[-1]
    tn = D_MODEL
    return pl.pallas_call(
        _ada_kernel,
        grid=(n // tn,),
        in_specs=[
            pl.BlockSpec((rows, D_MODEL), lambda j: (0, 0)),
            pl.BlockSpec((None, D_MODEL, tn), lambda j: (0, 0, j)),
            pl.BlockSpec((1, tn), lambda j: (0, j)),
        ],
        out_specs=pl.BlockSpec((rows, tn), lambda j: (0, j)),
        out_shape=jax.ShapeDtypeStruct((rows, n), F32),
        compiler_params=_params("arbitrary"),
        name="ada",
    )(cond, w_ada, b_ada)


N_FILT_PLANES = 6


def _filter_kernel(fe_ref, fo_ref, w1_ref, b1_ref, fr_ref, w2_ref, b2_ref, w3_ref, dl_ref, f_ref, tw_ref,
                   kf_ref, kn_ref, *, m):
    freq = fr_ref[...]
    w1 = w1_ref[...].astype(BF16)
    w2 = w2_ref[...].astype(BF16)

    def hidden(feats):
        h = jnp.sin(freq * (_dot(feats.astype(BF16), w1) + b1_ref[...]))
        h = jnp.sin(freq * (_dot(h.astype(BF16), w2) + b2_ref[...]))
        return h.astype(BF16)

    feats_e = fe_ref[...]
    feats_o = fo_ref[...]
    hb_e = hidden(feats_e)
    hb_o = hidden(feats_o)
    t_e = feats_e[:, 0:1]
    t_o = feats_o[:, 0:1]
    row = lax.broadcasted_iota(jnp.int32, (m, 1), 0)
    sign = jnp.where((row & 1) == 0, 1.0, -1.0)
    cos_t = tw_ref[:, 0:1]
    sin_t = tw_ref[:, 1:2]
    f_top = f_ref[0:m, :]
    f_bot = f_ref[m:2 * m, :]

    def spectrum(k):
        kb = k.astype(BF16)
        return _dot(f_top, kb), jnp.where(row == 0, 0.0, _dot(f_bot, kb)), jnp.sum(k * sign, axis=0, keepdims=True)

    for i in range(HY_ORDER):
        dl = dl_ref[i:i + 1, :]
        c0 = (0 * HY_ORDER + i) * HY_W
        c1 = (1 * HY_ORDER + i) * HY_W
        w3f = w3_ref[:, c0:c0 + HY_W].astype(BF16)
        w3b = w3_ref[:, c1:c1 + HY_W].astype(BF16)
        win_e = jnp.exp(-t_e * dl)
        win_o = jnp.exp(-t_o * dl)
        k0e = _dot(hb_e, w3f) * win_e
        k0o = _dot(hb_o, w3f) * win_o
        k1e = jnp.where(row == 0, 0.0, _dot(hb_e, w3b) * win_e)
        k1o = _dot(hb_o, w3b) * win_o
        ker, _, ken = spectrum(k0e + k1e)
        _, kei, _ = spectrum(k0e - k1e)
        ar, ai, an = spectrum(k0o)
        br, bi, bn = spectrum(k1o)
        kf_ref[i, 0] = ker
        kf_ref[i, 1] = kei
        kf_ref[i, 2] = ar + cos_t * br + sin_t * bi
        kf_ref[i, 3] = ai + sin_t * br - cos_t * bi
        kf_ref[i, 4] = cos_t * ar + sin_t * ai + br
        kf_ref[i, 5] = cos_t * ai - sin_t * ar - bi
        kn_ref[i] = jnp.concatenate([ken, an - bn, bn - an, jnp.zeros((SUBLANES - 3, HY_W), F32)], axis=0)


def _filters(l, feats_e, feats_o, w1p, b1, freq, w2, b2, w3, deltas, fmat, twid):
    m = l // 2
    full = lambda *shape: pl.BlockSpec(shape, lambda: (0,) * len(shape))
    return pl.pallas_call(
        functools.partial(_filter_kernel, m=m),
        in_specs=[full(m, EMB_PAD), full(m, EMB_PAD), full(EMB_PAD, HY_HIDDEN), full(1, HY_HIDDEN),
                  full(1, HY_HIDDEN), full(HY_HIDDEN, HY_HIDDEN), full(1, HY_HIDDEN),
                  full(HY_HIDDEN, N_DIR * HY_ORDER * HY_W), full(HY_ORDER, HY_W), full(l, m), full(m, 2)],
        out_specs=[full(HY_ORDER, N_FILT_PLANES, m, HY_W), full(HY_ORDER, SUBLANES, HY_W)],
        out_shape=[jax.ShapeDtypeStruct((HY_ORDER, N_FILT_PLANES, m, HY_W), F32),
                   jax.ShapeDtypeStruct((HY_ORDER, SUBLANES, HY_W), F32)],
        compiler_params=pltpu.CompilerParams(vmem_limit_bytes=VMEM_LIMIT),
        name=f"filt{l}",
    )(feats_e, feats_o, w1p, b1, freq, w2, b2, w3, deltas, fmat, twid)


def _const_spec(shape):
    return pl.BlockSpec(shape, lambda i: (0,) * len(shape), pipeline_mode=pl.Buffered(1))


def _group_specs(ctx_tiles, width):
    tm = TOKEN_TILE
    return [pl.BlockSpec((tm, width), lambda i: (jnp.minimum(i, ctx_tiles - 1), 0)),
            pl.BlockSpec((tm, width), lambda i: (jnp.maximum(i - ctx_tiles, 0), 0))]


def _mod_spec(ctx_tiles, n_lat, tiles_per_seq):
    return pl.BlockSpec((None, N_MOD, D_MODEL),
                        lambda i: (jnp.where(i < ctx_tiles, n_lat, (i - ctx_tiles) // tiles_per_seq), 0, 0))


def _ffn_in_kernel(xc_ref, xl_ref, mod_ref, n1_ref, nm_ref, wg_ref, wu_ref, wd_ref, wi_ref,
                   x1_ref, z_ref, xbc_ref, dt_ref, hy_ref, *, ctx_tiles):
    x = jnp.where(pl.program_id(0) < ctx_tiles, xc_ref[...], xl_ref[...])
    mod = mod_ref[...]
    h = _rms_mod(x, n1_ref[...], mod[0:1, :], mod[1:2, :]).astype(BF16)
    x1 = x + (0.5 * mod[2:3, :]) * _swiglu_acc(h, wg_ref, wu_ref, wd_ref)
    x1_ref[...] = x1
    h2 = _rms_mod(x1, nm_ref[...], mod[3:4, :], mod[4:5, :]).astype(BF16)
    o1, o2, _ = IN_SPLITS
    z_ref[...] = _dot_nt(h2, wi_ref[0:o1, :])
    xbc_ref[...] = _dot_nt(h2, wi_ref[o1:o2, :])
    dt_ref[...] = _dot_nt(h2, wi_ref[o2:o2 + DT_PAD, :])
    hy = _dot_nt(h2, wi_ref[o2 + DT_PAD:, :])
    for k in range(hy_ref.shape[0]):
        hy_ref[k] = hy[:, k * LANES:(k + 1) * LANES]


def _ffn_in(x_ctx, x_lat, mod, n_lat, l_lat, n1, nm, wg, wu, wd, wi):
    tm = TOKEN_TILE
    ctx_tiles = x_ctx.shape[0] // tm
    tokens = x_ctx.shape[0] + x_lat.shape[0]
    row_spec = lambda n: pl.BlockSpec((tm, n), lambda i: (i, 0))
    widths = (D_MODEL, SSD_W, SSD_XBC, DT_PAD)
    hy_tiles = (HY_ORDER + 1) * HY_W // LANES
    return pl.pallas_call(
        functools.partial(_ffn_in_kernel, ctx_tiles=ctx_tiles),
        grid=(tokens // tm,),
        in_specs=_group_specs(ctx_tiles, D_MODEL) + [
            _mod_spec(ctx_tiles, n_lat, l_lat // tm),
            _const_spec((1, D_MODEL)), _const_spec((1, D_MODEL)),
            _const_spec(wg.shape), _const_spec(wu.shape), _const_spec(wd.shape), _const_spec(wi.shape)],
        out_specs=[row_spec(n) for n in widths] + [pl.BlockSpec((hy_tiles, tm, LANES), lambda i: (0, i, 0))],
        out_shape=[jax.ShapeDtypeStruct((tokens, n), F32) for n in widths]
        + [jax.ShapeDtypeStruct((hy_tiles, tokens, LANES), F32)],
        compiler_params=_params("arbitrary"),
        name="ffn_in",
    )(x_ctx, x_lat, mod, n1, nm, wg, wu, wd, wi)


def _out_ffn_kernel(x1_ref, ysc_ref, ysl_ref, yhc_ref, yhl_ref, mod_ref, n3_ref, nf_ref,
                    wo_ref, wg_ref, wu_ref, wd_ref, oc_ref, ol_ref, *, ctx_tiles):
    is_ctx = pl.program_id(0) < ctx_tiles
    mod = mod_ref[...]
    ys = jnp.where(is_ctx, ysc_ref[...], ysl_ref[...])
    yh = [jnp.where(is_ctx, yhc_ref[k], yhl_ref[k]) for k in range(yhc_ref.shape[0])]
    y = jnp.concatenate([ys] + yh, axis=1).astype(BF16)
    x2 = x1_ref[...] + mod[5:6, :] * _dot(y, wo_ref[...])
    h = _rms_mod(x2, n3_ref[...], mod[6:7, :], mod[7:8, :]).astype(BF16)
    x3 = x2 + (0.5 * mod[8:9, :]) * _swiglu_acc(h, wg_ref, wu_ref, wd_ref)
    ms = jnp.mean(x3 * x3, axis=-1, keepdims=True)
    out = x3 * lax.rsqrt(ms + RMS_EPS) * nf_ref[...]

    @pl.when(is_ctx)
    def _():
        oc_ref[...] = out

    @pl.when(jnp.logical_not(is_ctx))
    def _():
        ol_ref[...] = out


def _out_ffn(x1, ys_ctx, ys_lat, yh_ctx, yh_lat, mod, n_lat, l_lat, n3, nf, wo, wg, wu, wd):
    tm = TOKEN_TILE
    ctx_tiles = ys_ctx.shape[0] // tm
    tokens = x1.shape[0]
    return pl.pallas_call(
        functools.partial(_out_ffn_kernel, ctx_tiles=ctx_tiles),
        grid=(tokens // tm,),
        in_specs=[pl.BlockSpec((tm, D_MODEL), lambda i: (i, 0))]
        + _group_specs(ctx_tiles, SSD_W) + [
            pl.BlockSpec((HY_W // LANES, tm, LANES), lambda i: (0, jnp.minimum(i, ctx_tiles - 1), 0)),
            pl.BlockSpec((HY_W // LANES, tm, LANES), lambda i: (0, jnp.maximum(i - ctx_tiles, 0), 0)),
            _mod_spec(ctx_tiles, n_lat, l_lat // tm),
            _const_spec((1, D_MODEL)), _const_spec((1, D_MODEL)),
            _const_spec(wo.shape), _const_spec(wg.shape), _const_spec(wu.shape), _const_spec(wd.shape)],
        out_specs=_group_specs(ctx_tiles, D_MODEL),
        out_shape=[jax.ShapeDtypeStruct((ys_ctx.shape[0], D_MODEL), F32),
                   jax.ShapeDtypeStruct((ys_lat.shape[0], D_MODEL), F32)],
        compiler_params=_params("arbitrary"),
        name="out_ffn",
    )(x1, ys_ctx, ys_lat, yh_ctx, yh_lat, mod, n3, nf, wo, wg, wu, wd)


def _ssd_kernel(z_ref, xbc_ref, dt_ref, init_ref, cw_ref, cb_ref, dtb_ref, alog_ref, dexp_ref, nw_ref, e_ref,
                y_ref, fin_ref,
                xs_s, xd_s, b_s, c_s, a_s, ec_s, dst_s, et_s, st_s,
                *, l, seg, zero_init):
    q = SSD_CHUNK
    nc = l // q

    first, last = _seg_edges(l, seg)
    u = _silu(_short_conv(xbc_ref[...], cw_ref[...], cb_ref[...], first, last, l))
    xs = u[:, :SSD_W]
    xs_s[...] = xs
    b_s[...] = u[:, SSD_W:SSD_W + LANES]
    c_s[...] = u[:, SSD_W + LANES:]

    dt = _softplus(dt_ref[...] + dtb_ref[...])
    a_s[...] = dt * (-jnp.exp(alog_ref[...]))
    e_mat = e_ref[...].astype(BF16)
    xd_s[...] = jnp.concatenate([xs, xs], axis=1) * _dot(dt.astype(BF16), e_mat)

    row_g = lax.broadcasted_iota(jnp.int32, (LANES, N_DIR * HP), 0) // SSD_STATE
    lane_g = (lax.broadcasted_iota(jnp.int32, (LANES, N_DIR * HP), 1) % HP) // (HP // SSD_GROUPS)
    own_t = row_g == lane_g

    ii = lax.broadcasted_iota(jnp.int32, (q, q), 0)
    jj = lax.broadcasted_iota(jnp.int32, (q, q), 1)
    tri_incl = (jj <= ii).astype(BF16)
    lane = lax.broadcasted_iota(jnp.int32, (q, LANES), 1)
    low_half = lane < SSD_STATE
    is_fwd_col = lane < SSD_HEADS

    def local_pass(c, carry):
        r0 = pl.multiple_of(c * q, q)
        rows = pl.ds(r0, q)
        a_c = a_s[rows, :]
        a1, a2, a3 = _split3(a_c)
        cum_f = _dot(tri_incl, a1) + _dot(tri_incl, a2) + _dot(tri_incl, a3)
        tot = cum_f[q - 1:q, :]
        cum = jnp.where(is_fwd_col, cum_f, tot - cum_f + a_c)
        cum_t = cum.T
        ec_s[rows, :] = _dot(jnp.exp(cum).astype(BF16), e_mat)
        et_s[c] = _dot_exact_lhs(jnp.broadcast_to(jnp.exp(tot), (SUBLANES, LANES)), e_mat)
        xd = xd_s[rows, :]
        w = (xd * _dot(jnp.exp(tot - cum).astype(BF16), e_mat)).astype(BF16)
        bc = b_s[rows, :]
        bcb = bc.astype(BF16)
        dst_s[c] = jnp.where(own_t, _dot(bc.T.astype(BF16), w), 0.0)

        cc = c_s[rows, :]
        g_mats = [_dot_nt(jnp.where(low_half, cc, 0.0).astype(BF16), bcb),
                  _dot_nt(jnp.where(low_half, 0.0, cc).astype(BF16), bcb)]
        y_parts = []
        for pair in range(SSD_HEADS // 2):
            g = pair // (SSD_HEADS // 2 // SSD_GROUPS)
            acc = None
            for d in range(N_DIR):
                keep = (ii >= jj) if d == 0 else (ii <= jj)
                s_mats = []
                for hh in (2 * pair, 2 * pair + 1):
                    col = d * SSD_HEADS + hh
                    diff = cum[:, col:col + 1] - cum_t[col:col + 1, :]
                    decay = jnp.exp(jnp.where(keep, diff, NEG_BIG))
                    s_mats.append((g_mats[g] * decay).astype(BF16))
                lhs = jnp.concatenate(s_mats, axis=1)
                xp = xd[:, d * HP + pair * LANES:d * HP + (pair + 1) * LANES]
                rhs = jnp.concatenate([jnp.where(low_half, xp, 0.0), jnp.where(low_half, 0.0, xp)],
                                      axis=0).astype(BF16)
                part = _dot(lhs, rhs)
                acc = part if acc is None else acc + part
            y_parts.append(acc)
        y_ref[rows, :] = jnp.concatenate(y_parts, axis=1)
        return carry

    lax.fori_loop(0, nc, local_pass, 0, unroll=2)

    for d in range(N_DIR):
        if zero_init:
            st_s[d] = jnp.zeros((LANES, HP), F32)
        else:
            s0 = init_ref[d]
            st_s[d] = jnp.where(own_t[:, :HP], jnp.concatenate([s0, s0], axis=1).T, 0.0)

    for k in range(nc):
        for d in range(N_DIR):
            c = k if d == 0 else nc - 1 - k
            rows = slice(c * q, (c + 1) * q)
            lanes = slice(d * HP, (d + 1) * HP)
            state = st_s[d]
            y_off = _dot(c_s[rows, :].astype(BF16), state.astype(BF16)) * ec_s[rows, lanes]
            y_ref[rows, :] = y_ref[rows, :] + y_off
            st_s[d] = state * et_s[c, 0:1, lanes] + dst_s[c, :, lanes]

    y = y_ref[...] + xs_s[...] * dexp_ref[...]
    y = y * _silu(z_ref[...])
    ms = jnp.mean(y * y, axis=-1, keepdims=True)
    y_ref[...] = y * lax.rsqrt(ms + RMS_EPS) * nw_ref[...]

    half = HP // SSD_GROUPS
    for d in range(N_DIR):
        st = st_s[d].T
        fin_ref[d, 0:half, :] = st[0:half, 0:SSD_STATE]
        fin_ref[d, half:HP, :] = st[half:HP, SSD_STATE:2 * SSD_STATE]


def _ssd(z, xbc, dt, row0, bsz, l, init, cw, cb, dtb, alog, dexp, nw, e_mat, *, seg):
    zero_init = init is None
    if zero_init:
        init = jnp.zeros((1, N_DIR, HP, SSD_STATE), F32)
        init_map = lambda b: (0, 0, 0, 0)
    else:
        init_map = lambda b: (b, 0, 0, 0)
    blk0 = row0 // l
    seq_spec = lambda n: pl.BlockSpec((l, n), lambda b: (blk0 + b, 0))
    const = lambda *shape: pl.BlockSpec(shape, lambda b: (0,) * len(shape))
    nc = l // SSD_CHUNK
    return pl.pallas_call(
        functools.partial(_ssd_kernel, l=l, seg=seg, zero_init=zero_init),
        grid=(bsz,),
        in_specs=[seq_spec(SSD_W), seq_spec(SSD_XBC), seq_spec(DT_PAD),
                  pl.BlockSpec((None, N_DIR, HP, SSD_STATE), init_map),
                  const(3, SSD_XBC), const(1, SSD_XBC), const(1, DT_PAD), const(1, DT_PAD),
                  const(1, SSD_W), const(1, SSD_W), const(LANES, N_DIR * HP)],
        out_specs=[pl.BlockSpec((l, SSD_W), lambda b: (b, 0)),
                   pl.BlockSpec((None, N_DIR, HP, SSD_STATE), lambda b: (b, 0, 0, 0))],
        out_shape=[jax.ShapeDtypeStruct((bsz * l, SSD_W), F32),
                   jax.ShapeDtypeStruct((bsz, N_DIR, HP, SSD_STATE), F32)],
        scratch_shapes=[pltpu.VMEM((l, SSD_W), F32), pltpu.VMEM((l, N_DIR * HP), F32),
                        pltpu.VMEM((l, LANES), F32), pltpu.VMEM((l, LANES), F32), pltpu.VMEM((l, DT_PAD), F32),
                        pltpu.VMEM((l, N_DIR * HP), F32), pltpu.VMEM((nc, LANES, N_DIR * HP), F32),
                        pltpu.VMEM((nc, SUBLANES, N_DIR * HP), F32), pltpu.VMEM((N_DIR, LANES, HP), F32)],
        compiler_params=_params("arbitrary"),
        name=f"ssd{l}",
    )(z, xbc, dt, init, cw, cb, dtb, alog, dexp, nw, e_mat)


HY_ROW_BLOCK = 16


def _hyena_kernel(v_ref, x1_ref, x2_ref, wv_ref, w1_ref, w2_ref, bv_ref, b1_ref, b2_ref,
                  f_ref, g_ref, kf_ref, kn_ref, skip_ref, o_ref, spec_s, prod_s, *, l, seg):
    m = l // 2
    tiles = o_ref.shape[0]
    ct = tiles * LANES
    rb = HY_ROW_BLOCK
    first, last = _seg_edges(m, seg // 2)

    def phase(x_ref, p):
        return jnp.concatenate([x_ref[k, pl.ds(p, m, stride=2), :] for k in range(tiles)], axis=1)

    def conv_eo(x_ref, w_ref, b_ref):
        xe = phase(x_ref, 0)
        xo = phase(x_ref, 1)
        w = w_ref[...]
        b = b_ref[...]
        xo_prev = jnp.where(first, 0.0, pltpu.roll(xo, 1, 0))
        xe_next = jnp.where(last, 0.0, pltpu.roll(xe, m - 1, 0))
        ce = xo_prev * w[0:1, :] + xe * w[1:2, :] + xo * w[2:3, :] + b
        co = xe * w[0:1, :] + xo * w[1:2, :] + xe_next * w[2:3, :] + b
        return jnp.concatenate([ce, co], axis=1)

    zz = conv_eo(v_ref, wv_ref, bv_ref)
    gates = (conv_eo(x1_ref, w1_ref, b1_ref), conv_eo(x2_ref, w2_ref, b2_ref))
    row0 = lax.broadcasted_iota(jnp.int32, (rb, 1), 0) == 0
    ev = slice(0, ct)
    od = slice(ct, 2 * ct)

    for i in range(HY_ORDER):
        spec_s[...] = _dot(f_ref[...], zz.astype(BF16))

        def pointwise(r0, nyquist, i=i):
            re = pl.ds(r0, rb)
            im = pl.ds(m + r0, rb)
            er, orr = spec_s[re, ev], spec_s[re, od]
            ei, oi = spec_s[im, ev], spec_s[im, od]
            ker, kei, kor, koi, vr, vi = (kf_ref[i, p, re, :] for p in range(N_FILT_PLANES))
            if nyquist:
                e_n, o_n = ei[0:1, :], oi[0:1, :]
                ei = jnp.where(row0, 0.0, ei)
                oi = jnp.where(row0, 0.0, oi)
            pe_r = er * ker - ei * kei + orr * vr - oi * vi
            pe_i = er * kei + ei * ker + orr * vi + oi * vr
            po_r = er * kor - ei * koi + orr * ker - oi * kei
            po_i = er * koi + ei * kor + orr * kei + oi * ker
            if nyquist:
                kn = kn_ref[i]
                pe_i = jnp.where(row0, e_n * kn[0:1, :] + o_n * kn[2:3, :], pe_i)
                po_i = jnp.where(row0, e_n * kn[1:2, :] + o_n * kn[0:1, :], po_i)
            prod_s[re, ev] = pe_r.astype(BF16)
            prod_s[re, od] = po_r.astype(BF16)
            prod_s[im, ev] = pe_i.astype(BF16)
            prod_s[im, od] = po_i.astype(BF16)

        pointwise(0, True)

        def body(k, carry):
            pointwise(pl.multiple_of(k * rb, rb), False)
            return carry

        lax.fori_loop(1, m // rb, body, 0, unroll=2)
        conv = _dot(g_ref[...], prod_s[...])
        skip = skip_ref[i:i + 1, :]
        zz = gates[i] * (conv + zz * jnp.concatenate([skip, skip], axis=1))
    for k in range(tiles):
        o_ref[k, pl.ds(0, m, stride=2), :] = zz[:, k * LANES:(k + 1) * LANES]
        o_ref[k, pl.ds(1, m, stride=2), :] = zz[:, ct + k * LANES:ct + (k + 1) * LANES]


def _hyena(hy, row0, bsz, l, cw, cb, fmat, gmat, kf, kn, skip, *, seg):
    ct = HY_CH_TILE
    nct = HY_W // ct
    tiles = ct // LANES
    m = l // 2
    blk0 = row0 // l
    part = lambda p: pl.BlockSpec((tiles, l, LANES), lambda j, b: (p * nct + j, blk0 + b, 0))
    wpart = lambda p: pl.BlockSpec((3, ct), lambda j, b: (0, p * nct + j))
    bpart = lambda p: pl.BlockSpec((1, ct), lambda j, b: (0, p * nct + j))
    return pl.pallas_call(
        functools.partial(_hyena_kernel, l=l, seg=seg),
        grid=(nct, bsz),
        in_specs=[part(0), part(1), part(2), wpart(0), wpart(1), wpart(2), bpart(0), bpart(1), bpart(2),
                  pl.BlockSpec((l, m), lambda j, b: (0, 0)),
                  pl.BlockSpec((m, l), lambda j, b: (0, 0)),
                  pl.BlockSpec((HY_ORDER, N_FILT_PLANES, m, ct), lambda j, b: (0, 0, 0, j)),
                  pl.BlockSpec((HY_ORDER, SUBLANES, ct), lambda j, b: (0, 0, j)),
                  pl.BlockSpec((HY_ORDER, ct), lambda j, b: (0, j))],
        out_specs=pl.BlockSpec((tiles, l, LANES), lambda j, b: (j, b, 0)),
        out_shape=jax.ShapeDtypeStruct((HY_W // LANES, bsz * l, LANES), F32),
        scratch_shapes=[pltpu.VMEM((l, 2 * ct), F32), pltpu.VMEM((l, 2 * ct), BF16)],
        compiler_params=_params("arbitrary", "arbitrary"),
        name=f"hyena{l}",
    )(hy, hy, hy, cw, cw, cw, cb, cb, cb, fmat, gmat, kf, kn, skip)


def _dft_mats(l):
    n = 2 * l
    f = np.arange(l, dtype=np.int64)[:, None]
    t = np.arange(l, dtype=np.int64)[None, :]
    ang = 2.0 * np.pi * ((f * t) % n).astype(np.float64) / n
    alt = np.where(np.arange(l) % 2 == 0, 1.0, -1.0)
    top = np.cos(ang)
    bot = -np.sin(ang)
    bot[0, :] = alt
    fwd = np.concatenate([top, bot], axis=0)
    wf = np.full((l,), 2.0)
    wf[0] = 1.0
    gtop = np.cos(ang).T * wf[None, :] / n
    gbot = -np.sin(ang).T * 2.0 / n
    gbot[:, 0] = alt / n
    inv = np.concatenate([gtop, gbot], axis=1)
    return fwd.astype(np.float32), inv.astype(np.float32)


def _filter_feats(l):
    t = np.linspace(0.0, 1.0, l)[:, None]
    w = (2.0 * np.pi / l) * np.arange(l, dtype=np.float64)[:, None]
    f = np.linspace(1e-4, HY_BANDS - 1, HY_BANDS)[None, :]
    feats = np.concatenate([t, np.cos(f * w), -np.sin(f * w)], axis=-1)
    out = np.zeros((l, EMB_PAD), np.float32)
    out[:, :HY_EMB] = feats
    return out[0::2], out[1::2]


def _shift_twiddles(l):
    theta = 2.0 * np.pi * np.arange(l // 2, dtype=np.float64) / l
    return np.stack([np.cos(theta), np.sin(theta)], axis=1).astype(np.float32)


def _head_expand():
    e = np.zeros((LANES, N_DIR * HP), np.float32)
    for j in range(N_DIR * SSD_HEADS):
        e[j, j * SSD_HEAD_DIM:(j + 1) * SSD_HEAD_DIM] = 1.0
    return e


def kernel(x_prompt, x_sample, state_ssd, c, c_ctx, w_ada, b_ada, norm_ffn1, ffn1_w_gate, ffn1_w_up, ffn1_w_down, norm_mix, w_in, w_out, ssd_conv_w, ssd_conv_b, ssd_dt_bias, ssd_a_log, ssd_d, ssd_norm_w, hy_conv_w, hy_conv_b, hy_w1, hy_b1, hy_freq, hy_w2, hy_b2, hy_w3, hy_skip, norm_ffn2, ffn2_w_gate, ffn2_w_up, ffn2_w_down, norm_final):
    assert w_ada.shape[0] == 1, "single layer"
    n_ctx, l_ctx, _ = x_prompt.shape
    n_lat, l_lat, _ = x_sample.shape
    t_ctx = n_ctx * l_ctx

    cond = jnp.zeros((16, D_MODEL), F32).at[:n_lat].set(c).at[n_lat].set(c_ctx)
    mod = _ada(cond, w_ada, b_ada).reshape(16, N_MOD, D_MODEL)

    tr = lambda w: jnp.swapaxes(w, 1, 2)
    wg1, wu1, wd1, wg2, wu2, wd2 = _cast_pad_rows([tr(ffn1_w_gate), tr(ffn1_w_up), ffn1_w_down,
                                                   tr(ffn2_w_gate), tr(ffn2_w_up), ffn2_w_down])
    wi, wo = _cast_mix(tr(w_in), w_out)

    row = lambda v: v.reshape(1, -1)
    pad_dt = lambda v: jnp.pad(v.reshape(1, -1), ((0, 0), (0, DT_PAD - N_DIR * SSD_HEADS)))
    dtb, alog = pad_dt(ssd_dt_bias[0]), pad_dt(ssd_a_log[0])
    dexp = jnp.repeat(ssd_d[0], SSD_HEAD_DIM).reshape(1, SSD_W)
    e_mat = jnp.asarray(_head_expand())
    w1p = jnp.pad(hy_w1[0], ((0, EMB_PAD - HY_EMB), (0, 0)))
    deltas = jnp.asarray(np.abs(np.linspace(HY_MIN_DECAY, HY_MAX_DECAY, HY_ORDER * HY_W))
                         .reshape(HY_ORDER, HY_W).astype(np.float32))

    x1, z, xbc, dt, hy = _ffn_in(x_prompt.reshape(t_ctx, D_MODEL), x_sample.reshape(n_lat * l_lat, D_MODEL), mod,
                                 n_lat, l_lat, row(norm_ffn1[0]), row(norm_mix[0]), wg1, wu1, wd1, wi)

    def mixers(row0, bsz, l, init, seg):
        fwd_np, inv_np = _dft_mats(l // 2)
        fmat = jnp.asarray(fwd_np).astype(BF16)
        gmat = jnp.asarray(inv_np).astype(BF16)
        feats_e, feats_o = _filter_feats(l)
        kf, kn = _filters(l, jnp.asarray(feats_e), jnp.asarray(feats_o), w1p, row(hy_b1[0]), row(hy_freq[0]),
                          hy_w2[0], row(hy_b2[0]), hy_w3[0], deltas, fmat, jnp.asarray(_shift_twiddles(l)))
        ys, fin = _ssd(z, xbc, dt, row0, bsz, l, init, ssd_conv_w[0], row(ssd_conv_b[0]), dtb, alog, dexp,
                       row(ssd_norm_w[0]), e_mat, seg=seg)
        yh = _hyena(hy, row0, bsz, l, hy_conv_w[0], row(hy_conv_b[0]), fmat, gmat, kf, kn, hy_skip[0], seg=seg)
        return ys, yh, fin

    ys_ctx, yh_ctx, ctx_fin = mixers(0, n_ctx, l_ctx, None, l_ctx)
    lat_init = state_ssd[:, 0].reshape(n_lat, N_DIR, HP, SSD_STATE)
    ys_lat, yh_lat, _ = mixers(t_ctx, n_lat, l_lat, lat_init, GRID_W)

    y_ctx, y_lat = _out_ffn(x1, ys_ctx, ys_lat, yh_ctx, yh_lat, mod, n_lat, l_lat, row(norm_ffn2[0]),
                            row(norm_final), wo, wg2, wu2, wd2)
    new_state = ctx_fin.reshape(n_ctx, 1, N_DIR, SSD_HEADS, SSD_HEAD_DIM, SSD_STATE).astype(x_prompt.dtype)
    return (y_ctx.reshape(n_ctx, l_ctx, D_MODEL), y_lat.reshape(n_lat, l_lat, D_MODEL), new_state)
```

```python
import functools
import math

import numpy as np
import jax
import jax.numpy as jnp
from jax import lax
from jax.experimental import pallas as pl
from jax.experimental.pallas import tpu as pltpu

F32 = jnp.float32
BF16 = jnp.bfloat16

D_MODEL = 1024
GRID_W = 64
N_MOD = 9
RMS_EPS = 1e-6
FFN_DIM = 2752
SSD_W = 512
SSD_HEADS = 8
SSD_HEAD_DIM = 64
SSD_STATE = 64
SSD_GROUPS = 2
SSD_CHUNK = 128
SSD_XBC = SSD_W + 2 * SSD_GROUPS * SSD_STATE
N_DIR = 2
HY_W = 512
HY_ORDER = 2
HY_EMB = 33
HY_BANDS = (HY_EMB - 1) // 2
HY_HIDDEN = 64
HY_MIN_DECAY = math.log(1e-2) / 1.5
HY_MAX_DECAY = math.log(1e-2) / 0.3
IN_SPLITS = (SSD_W, SSD_W + SSD_XBC, SSD_W + SSD_XBC + N_DIR * SSD_HEADS)
IN_COLS = IN_SPLITS[-1] + (HY_ORDER + 1) * HY_W

LANES = 128
SUBLANES = 8
FFN_PAD = 2816
FFN_CHUNK = 256
DT_PAD = LANES
IN_ROWS = IN_COLS + DT_PAD - N_DIR * SSD_HEADS
EMB_PAD = LANES
TOKEN_TILE = 512
HY_CH_TILE = 256
VMEM_LIMIT = 56 * 1024 * 1024
NEG_BIG = -1e30
HP = SSD_HEADS * SSD_HEAD_DIM


def _silu(x):
    return x * jax.nn.sigmoid(x)


def _softplus(x):
    return jnp.maximum(x, 0.0) + jnp.log1p(jnp.exp(-jnp.abs(x)))


def _rms_mod(x, gain, shift, scale):
    ms = jnp.mean(x * x, axis=-1, keepdims=True)
    y = x * lax.rsqrt(ms + RMS_EPS) * gain
    return y * (1.0 + scale) + shift


def _dot(a, b):
    return jnp.dot(a, b, preferred_element_type=F32)


def _dot_nt(a, b):
    return lax.dot_general(a, b, (((1,), (1,)), ((), ())), preferred_element_type=F32)


def _split3(x):
    hi = x.astype(BF16)
    r = x - hi.astype(F32)
    mid = r.astype(BF16)
    lo = (r - mid.astype(F32)).astype(BF16)
    return hi, mid, lo


def _dot_exact_lhs(x, m01):
    hi, mid, lo = _split3(x)
    return _dot(hi, m01) + _dot(mid, m01) + _dot(lo, m01)


def _short_conv(x, w, b, first, last, l):
    prev = jnp.where(first, 0.0, pltpu.roll(x, 1, 0))
    nxt = jnp.where(last, 0.0, pltpu.roll(x, l - 1, 0))
    return prev * w[0:1, :] + x * w[1:2, :] + nxt * w[2:3, :] + b


def _seg_edges(l, seg):
    pos = lax.broadcasted_iota(jnp.int32, (l, 1), 0) & (seg - 1)
    return pos == 0, pos == seg - 1


def _swiglu_acc(h, wg_ref, wu_ref, wd_ref):
    acc = None
    for k in range(FFN_PAD // FFN_CHUNK):
        cols = slice(k * FFN_CHUNK, (k + 1) * FFN_CHUNK)
        g = _dot_nt(h, wg_ref[cols, :])
        u = _dot_nt(h, wu_ref[cols, :])
        a = (_silu(g) * u).astype(BF16)
        part = _dot(a, wd_ref[cols, :])
        acc = part if acc is None else acc + part
    return acc


def _params(*semantics):
    return pltpu.CompilerParams(dimension_semantics=semantics, vmem_limit_bytes=VMEM_LIMIT)


def _cast_pad_rows_kernel(*refs):
    n = len(refs) // 2
    rows = refs[0].shape[0]
    row = pl.program_id(0) * rows + lax.broadcasted_iota(jnp.int32, (rows, 1), 0)
    for w_ref, o_ref in zip(refs[:n], refs[n:]):
        o_ref[...] = jnp.where(row < FFN_DIM, w_ref[...], 0.0).astype(BF16)


def _cast_pad_rows(ws):
    n = len(ws)
    return pl.pallas_call(
        _cast_pad_rows_kernel,
        grid=(FFN_PAD // FFN_CHUNK,),
        in_specs=[pl.BlockSpec((None, FFN_CHUNK, D_MODEL), lambda i: (0, i, 0))] * n,
        out_specs=[pl.BlockSpec((FFN_CHUNK, D_MODEL), lambda i: (i, 0))] * n,
        out_shape=[jax.ShapeDtypeStruct((FFN_PAD, D_MODEL), BF16)] * n,
        compiler_params=_params("arbitrary"),
        name="wcast_rows",
    )(*ws)


def _cast_mix_kernel(wi_ref, wo_ref, wib_ref, wob_ref):
    o1, o2, o3 = IN_SPLITS
    wib_ref[0:o3, :] = wi_ref[0:o3, :].astype(BF16)
    wib_ref[o3:o2 + DT_PAD, :] = jnp.zeros((o2 + DT_PAD - o3, D_MODEL), BF16)
    wib_ref[o2 + DT_PAD:, :] = wi_ref[o3:, :].astype(BF16)
    wob_ref[...] = wo_ref[...].astype(BF16)


def _cast_mix(w_in_t, w_out):
    full = lambda *shape: pl.BlockSpec(shape, lambda: (0,) * len(shape))
    sq = lambda *shape: pl.BlockSpec((None,) + shape, lambda: (0,) * (len(shape) + 1))
    return pl.pallas_call(
        _cast_mix_kernel,
        in_specs=[sq(IN_COLS, D_MODEL), sq(D_MODEL, D_MODEL)],
        out_specs=[full(IN_ROWS, D_MODEL), full(D_MODEL, D_MODEL)],
        out_shape=[jax.ShapeDtypeStruct((IN_ROWS, D_MODEL), BF16), jax.ShapeDtypeStruct((D_MODEL, D_MODEL), BF16)],
        compiler_params=pltpu.CompilerParams(vmem_limit_bytes=VMEM_LIMIT),
        name="wcast_mix",
    )(w_in_t, w_out)


def _ada_kernel(cond_ref, w_ref, b_ref, o_ref):
    s = _silu(cond_ref[...]).astype(BF16)
    o_ref[...] = _dot(s, w_ref[...].astype(BF16)) + b_ref[...]


def _ada(cond, w_ada, b_ada):
    rows = cond.shape[0]
    n = w_ada.shape[-1]
    tn = D_MODEL
    return pl.pallas_call(
        _ada_kernel,
        grid=(n // tn,),
        in_specs=[
            pl.BlockSpec((rows, D_MODEL), lambda j: (0, 0)),
            pl.BlockSpec((None, D_MODEL, tn), lambda j: (0, 0, j)),
            pl.BlockSpec((1, tn), lambda j: (0, j)),
        ],
        out_specs=pl.BlockSpec((rows, tn), lambda j: (0, j)),
        out_shape=jax.ShapeDtypeStruct((rows, n), F32),
        compiler_params=_params("arbitrary"),
        name="ada",
    )(cond, w_ada, b_ada)


N_FILT_PLANES = 6


def _filter_kernel(fe_ref, fo_ref, w1_ref, b1_ref, fr_ref, w2_ref, b2_ref, w3_ref, dl_ref, f_ref, tw_ref,
                   kf_ref, kn_ref, *, m):
    freq = fr_ref[...]
    w1 = w1_ref[...].astype(BF16)
    w2 = w2_ref[...].astype(BF16)

    def hidden(feats):
        h = jnp.sin(freq * (_dot(feats.astype(BF16), w1) + b1_ref[...]))
        h = jnp.sin(freq * (_dot(h.astype(BF16), w2) + b2_ref[...]))
        return h.astype(BF16)

    feats_e = fe_ref[...]
    feats_o = fo_ref[...]
    hb_e = hidden(feats_e)
    hb_o = hidden(feats_o)
    t_e = feats_e[:, 0:1]
    t_o = feats_o[:, 0:1]
    row = lax.broadcasted_iota(jnp.int32, (m, 1), 0)
    sign = jnp.where((row & 1) == 0, 1.0, -1.0)
    cos_t = tw_ref[:, 0:1]
    sin_t = tw_ref[:, 1:2]
    f_top = f_ref[0:m, :]
    f_bot = f_ref[m:2 * m, :]

    def spectrum(k):
        kb = k.astype(BF16)
        return _dot(f_top, kb), jnp.where(row == 0, 0.0, _dot(f_bot, kb)), jnp.sum(k * sign, axis=0, keepdims=True)

    for i in range(HY_ORDER):
        dl = dl_ref[i:i + 1, :]
        c0 = (0 * HY_ORDER + i) * HY_W
        c1 = (1 * HY_ORDER + i) * HY_W
        w3f = w3_ref[:, c0:c0 + HY_W].astype(BF16)
        w3b = w3_ref[:, c1:c1 + HY_W].astype(BF16)
        win_e = jnp.exp(-t_e * dl)
        win_o = jnp.exp(-t_o * dl)
        k0e = _dot(hb_e, w3f) * win_e
        k0o = _dot(hb_o, w3f) * win_o
        k1e = jnp.where(row == 0, 0.0, _dot(hb_e, w3b) * win_e)
        k1o = _dot(hb_o, w3b) * win_o
        ker, _, ken = spectrum(k0e + k1e)
        _, kei, _ = spectrum(k0e - k1e)
        ar, ai, an = spectrum(k0o)
        br, bi, bn = spectrum(k1o)
        kf_ref[i, 0] = ker
        kf_ref[i, 1] = kei
        kf_ref[i, 2] = ar + cos_t * br + sin_t * bi
        kf_ref[i, 3] = ai + sin_t * br - cos_t * bi
        kf_ref[i, 4] = cos_t * ar + sin_t * ai + br
        kf_ref[i, 5] = cos_t * ai - sin_t * ar - bi
        kn_ref[i] = jnp.concatenate([ken, an - bn, bn - an, jnp.zeros((SUBLANES - 3, HY_W), F32)], axis=0)


def _filters(l, feats_e, feats_o, w1p, b1, freq, w2, b2, w3, deltas, fmat, twid):
    m = l // 2
    full = lambda *shape: pl.BlockSpec(shape, lambda: (0,) * len(shape))
    return pl.pallas_call(
        functools.partial(_filter_kernel, m=m),
        in_specs=[full(m, EMB_PAD), full(m, EMB_PAD), full(EMB_PAD, HY_HIDDEN), full(1, HY_HIDDEN),
                  full(1, HY_HIDDEN), full(HY_HIDDEN, HY_HIDDEN), full(1, HY_HIDDEN),
                  full(HY_HIDDEN, N_DIR * HY_ORDER * HY_W), full(HY_ORDER, HY_W), full(l, m), full(m, 2)],
        out_specs=[full(HY_ORDER, N_FILT_PLANES, m, HY_W), full(HY_ORDER, SUBLANES, HY_W)],
        out_shape=[jax.ShapeDtypeStruct((HY_ORDER, N_FILT_PLANES, m, HY_W), F32),
                   jax.ShapeDtypeStruct((HY_ORDER, SUBLANES, HY_W), F32)],
        compiler_params=pltpu.CompilerParams(vmem_limit_bytes=VMEM_LIMIT),
        name=f"filt{l}",
    )(feats_e, feats_o, w1p, b1, freq, w2, b2, w3, deltas, fmat, twid)


def _const_spec(shape):
    return pl.BlockSpec(shape, lambda i: (0,) * len(shape), pipeline_mode=pl.Buffered(1))


def _group_specs(ctx_tiles, width):
    tm = TOKEN_TILE
    return [pl.BlockSpec((tm, width), lambda i: (jnp.minimum(i, ctx_tiles - 1), 0)),
            pl.BlockSpec((tm, width), lambda i: (jnp.maximum(i - ctx_tiles, 0), 0))]


def _mod_spec(ctx_tiles, n_lat, tiles_per_seq):
    return pl.BlockSpec((None, N_MOD, D_MODEL),
                        lambda i: (jnp.where(i < ctx_tiles, n_lat, (i - ctx_tiles) // tiles_per_seq), 0, 0))


def _ffn_in_kernel(xc_ref, xl_ref, mod_ref, n1_ref, nm_ref, wg_ref, wu_ref, wd_ref, wi_ref,
                   x1_ref, z_ref, xbc_ref, dt_ref, hy_ref, *, ctx_tiles):
    x = jnp.where(pl.program_id(0) < ctx_tiles, xc_ref[...], xl_ref[...])
    mod = mod_ref[...]
    h = _rms_mod(x, n1_ref[...], mod[0:1, :], mod[1:2, :]).astype(BF16)
    x1 = x + (0.5 * mod[2:3, :]) * _swiglu_acc(h, wg_ref, wu_ref, wd_ref)
    x1_ref[...] = x1
    h2 = _rms_mod(x1, nm_ref[...], mod[3:4, :], mod[4:5, :]).astype(BF16)
    o1, o2, _ = IN_SPLITS
    z_ref[...] = _dot_nt(h2, wi_ref[0:o1, :])
    xbc_ref[...] = _dot_nt(h2, wi_ref[o1:o2, :])
    dt_ref[...] = _dot_nt(h2, wi_ref[o2:o2 + DT_PAD, :])
    hy = _dot_nt(h2, wi_ref[o2 + DT_PAD:, :])
    for k in range(hy_ref.shape[0]):
        hy_ref[k] = hy[:, k * LANES:(k + 1) * LANES]


def _ffn_in(x_ctx, x_lat, mod, n_lat, l_lat, n1, nm, wg, wu, wd, wi):
    tm = TOKEN_TILE
    ctx_tiles = x_ctx.shape[0] // tm
    tokens = x_ctx.shape[0] + x_lat.shape[0]
    row_spec = lambda n: pl.BlockSpec((tm, n), lambda i: (i, 0))
    widths = (D_MODEL, SSD_W, SSD_XBC, DT_PAD)
    hy_tiles = (HY_ORDER + 1) * HY_W // LANES
    return pl.pallas_call(
        functools.partial(_ffn_in_kernel, ctx_tiles=ctx_tiles),
        grid=(tokens // tm,),
        in_specs=_group_specs(ctx_tiles, D_MODEL) + [
            _mod_spec(ctx_tiles, n_lat, l_lat // tm),
            _const_spec((1, D_MODEL)), _const_spec((1, D_MODEL)),
            _const_spec(wg.shape), _const_spec(wu.shape), _const_spec(wd.shape), _const_spec(wi.shape)],
        out_specs=[row_spec(n) for n in widths] + [pl.BlockSpec((hy_tiles, tm, LANES), lambda i: (0, i, 0))],
        out_shape=[jax.ShapeDtypeStruct((tokens, n), F32) for n in widths]
        + [jax.ShapeDtypeStruct((hy_tiles, tokens, LANES), F32)],
        compiler_params=_params("arbitrary"),
        name="ffn_in",
    )(x_ctx, x_lat, mod, n1, nm, wg, wu, wd, wi)


def _out_ffn_kernel(x1_ref, ysc_ref, ysl_ref, yhc_ref, yhl_ref, mod_ref, n3_ref, nf_ref,
                    wo_ref, wg_ref, wu_ref, wd_ref, oc_ref, ol_ref, *, ctx_tiles):
    is_ctx = pl.program_id(0) < ctx_tiles
    mod = mod_ref[...]
    ys = jnp.where(is_ctx, ysc_ref[...], ysl_ref[...])
    yh = [jnp.where(is_ctx, yhc_ref[k], yhl_ref[k]) for k in range(yhc_ref.shape[0])]
    y = jnp.concatenate([ys] + yh, axis=1).astype(BF16)
    x2 = x1_ref[...] + mod[5:6, :] * _dot(y, wo_ref[...])
    h = _rms_mod(x2, n3_ref[...], mod[6:7, :], mod[7:8, :]).astype(BF16)
    x3 = x2 + (0.5 * mod[8:9, :]) * _swiglu_acc(h, wg_ref, wu_ref, wd_ref)
    ms = jnp.mean(x3 * x3, axis=-1, keepdims=True)
    out = x3 * lax.rsqrt(ms + RMS_EPS) * nf_ref[...]

    @pl.when(is_ctx)
    def _():
        oc_ref[...] = out

    @pl.when(jnp.logical_not(is_ctx))
    def _():
        ol_ref[...] = out


def _out_ffn(x1, ys_ctx, ys_lat, yh_ctx, yh_lat, mod, n_lat, l_lat, n3, nf, wo, wg, wu, wd):
    tm = TOKEN_TILE
    ctx_tiles = ys_ctx.shape[0] // tm
    tokens = x1.shape[0]
    return pl.pallas_call(
        functools.partial(_out_ffn_kernel, ctx_tiles=ctx_tiles),
        grid=(tokens // tm,),
        in_specs=[pl.BlockSpec((tm, D_MODEL), lambda i: (i, 0))]
        + _group_specs(ctx_tiles, SSD_W) + [
            pl.BlockSpec((HY_W // LANES, tm, LANES), lambda i: (0, jnp.minimum(i, ctx_tiles - 1), 0)),
            pl.BlockSpec((HY_W // LANES, tm, LANES), lambda i: (0, jnp.maximum(i - ctx_tiles, 0), 0)),
            _mod_spec(ctx_tiles, n_lat, l_lat // tm),
            _const_spec((1, D_MODEL)), _const_spec((1, D_MODEL)),
            _const_spec(wo.shape), _const_spec(wg.shape), _const_spec(wu.shape), _const_spec(wd.shape)],
        out_specs=_group_specs(ctx_tiles, D_MODEL),
        out_shape=[jax.ShapeDtypeStruct((ys_ctx.shape[0], D_MODEL), F32),
                   jax.ShapeDtypeStruct((ys_lat.shape[0], D_MODEL), F32)],
        compiler_params=_params("arbitrary"),
        name="out_ffn",
    )(x1, ys_ctx, ys_lat, yh_ctx, yh_lat, mod, n3, nf, wo, wg, wu, wd)


def _ssd_kernel(z_ref, xbc_ref, dt_ref, init_ref, cw_ref, cb_ref, dtb_ref, alog_ref, dexp_ref, nw_ref, e_ref,
                y_ref, fin_ref,
                xs_s, xd_s, b_s, c_s, a_s, ec_s, dst_s, et_s, st_s,
                *, l, seg, zero_init):
    q = SSD_CHUNK
    nc = l // q

    first, last = _seg_edges(l, seg)
    u = _silu(_short_conv(xbc_ref[...], cw_ref[...], cb_ref[...], first, last, l))
    xs = u[:, :SSD_W]
    xs_s[...] = xs
    b_s[...] = u[:, SSD_W:SSD_W + LANES]
    c_s[...] = u[:, SSD_W + LANES:]

    dt = _softplus(dt_ref[...] + dtb_ref[...])
    a_s[...] = dt * (-jnp.exp(alog_ref[...]))
    e_mat = e_ref[...].astype(BF16)
    xd_s[...] = jnp.concatenate([xs, xs], axis=1) * _dot(dt.astype(BF16), e_mat)

    row_g = lax.broadcasted_iota(jnp.int32, (LANES, N_DIR * HP), 0) // SSD_STATE
    lane_g = (lax.broadcasted_iota(jnp.int32, (LANES, N_DIR * HP), 1) % HP) // (HP // SSD_GROUPS)
    own_t = row_g == lane_g

    ii = lax.broadcasted_iota(jnp.int32, (q, q), 0)
    jj = lax.broadcasted_iota(jnp.int32, (q, q), 1)
    tri_incl = (jj <= ii).astype(BF16)
    lane = lax.broadcasted_iota(jnp.int32, (q, LANES), 1)
    low_half = lane < SSD_STATE
    is_fwd_col = lane < SSD_HEADS

    def local_pass(c, carry):
        r0 = pl.multiple_of(c * q, q)
        rows = pl.ds(r0, q)
        a_c = a_s[rows, :]
        a1, a2, a3 = _split3(a_c)
        cum_f = _dot(tri_incl, a1) + _dot(tri_incl, a2) + _dot(tri_incl, a3)
        tot = cum_f[q - 1:q, :]
        cum = jnp.where(is_fwd_col, cum_f, tot - cum_f + a_c)
        cum_t = cum.T
        ec_s[rows, :] = _dot(jnp.exp(cum).astype(BF16), e_mat)
        et_s[c] = _dot_exact_lhs(jnp.broadcast_to(jnp.exp(tot), (SUBLANES, LANES)), e_mat)
        xd = xd_s[rows, :]
        w = (xd * _dot(jnp.exp(tot - cum).astype(BF16), e_mat)).astype(BF16)
        bc = b_s[rows, :]
        bcb = bc.astype(BF16)
        dst_s[c] = jnp.where(own_t, _dot(bc.T.astype(BF16), w), 0.0)

        cc = c_s[rows, :]
        g_mats = [_dot_nt(jnp.where(low_half, cc, 0.0).astype(BF16), bcb),
                  _dot_nt(jnp.where(low_half, 0.0, cc).astype(BF16), bcb)]
        y_parts = []
        for pair in range(SSD_HEADS // 2):
            g = pair // (SSD_HEADS // 2 // SSD_GROUPS)
            acc = None
            for d in range(N_DIR):
                keep = (ii >= jj) if d == 0 else (ii <= jj)
                s_mats = []
                for hh in (2 * pair, 2 * pair + 1):
                    col = d * SSD_HEADS + hh
                    diff = cum[:, col:col + 1] - cum_t[col:col + 1, :]
                    decay = jnp.exp(jnp.where(keep, diff, NEG_BIG))
                    s_mats.append((g_mats[g] * decay).astype(BF16))
                lhs = jnp.concatenate(s_mats, axis=1)
                xp = xd[:, d * HP + pair * LANES:d * HP + (pair + 1) * LANES]
                rhs = jnp.concatenate([jnp.where(low_half, xp, 0.0), jnp.where(low_half, 0.0, xp)],
                                      axis=0).astype(BF16)
                part = _dot(lhs, rhs)
                acc = part if acc is None else acc + part
            y_parts.append(acc)
        y_ref[rows, :] = jnp.concatenate(y_parts, axis=1)
        return carry

    lax.fori_loop(0, nc, local_pass, 0, unroll=2)

    for d in range(N_DIR):
        if zero_init:
            st_s[d] = jnp.zeros((LANES, HP), F32)
        else:
            s0 = init_ref[d]
            st_s[d] = jnp.where(own_t[:, :HP], jnp.concatenate([s0, s0], axis=1).T, 0.0)

    for k in range(nc):
        for d in range(N_DIR):
            c = k if d == 0 else nc - 1 - k
            rows = slice(c * q, (c + 1) * q)
            lanes = slice(d * HP, (d + 1) * HP)
            state = st_s[d]
            y_off = _dot(c_s[rows, :].astype(BF16), state.astype(BF16)) * ec_s[rows, lanes]
            y_ref[rows, :] = y_ref[rows, :] + y_off
            st_s[d] = state * et_s[c, 0:1, lanes] + dst_s[c, :, lanes]

    y = y_ref[...] + xs_s[...] * dexp_ref[...]
    y = y * _silu(z_ref[...])
    ms = jnp.mean(y * y, axis=-1, keepdims=True)
    y_ref[...] = y * lax.rsqrt(ms + RMS_EPS) * nw_ref[...]

    half = HP // SSD_GROUPS
    for d in range(N_DIR):
        st = st_s[d].T
        fin_ref[d, 0:half, :] = st[0:half, 0:SSD_STATE]
        fin_ref[d, half:HP, :] = st[half:HP, SSD_STATE:2 * SSD_STATE]


def _ssd(z, xbc, dt, row0, bsz, l, init, cw, cb, dtb, alog, dexp, nw, e_mat, *, seg):
    zero_init = init is None
    if zero_init:
        init = jnp.zeros((1, N_DIR, HP, SSD_STATE), F32)
        init_map = lambda b: (0, 0, 0, 0)
    else:
        init_map = lambda b: (b, 0, 0, 0)
    blk0 = row0 // l
    seq_spec = lambda n: pl.BlockSpec((l, n), lambda b: (blk0 + b, 0))
    const = lambda *shape: pl.BlockSpec(shape, lambda b: (0,) * len(shape))
    nc = l // SSD_CHUNK
    return pl.pallas_call(
        functools.partial(_ssd_kernel, l=l, seg=seg, zero_init=zero_init),
        grid=(bsz,),
        in_specs=[seq_spec(SSD_W), seq_spec(SSD_XBC), seq_spec(DT_PAD),
                  pl.BlockSpec((None, N_DIR, HP, SSD_STATE), init_map),
                  const(3, SSD_XBC), const(1, SSD_XBC), const(1, DT_PAD), const(1, DT_PAD),
                  const(1, SSD_W), const(1, SSD_W), const(LANES, N_DIR * HP)],
        out_specs=[pl.BlockSpec((l, SSD_W), lambda b: (b, 0)),
                   pl.BlockSpec((None, N_DIR, HP, SSD_STATE), lambda b: (b, 0, 0, 0))],
        out_shape=[jax.ShapeDtypeStruct((bsz * l, SSD_W), F32),
                   jax.ShapeDtypeStruct((bsz, N_DIR, HP, SSD_STATE), F32)],
        scratch_shapes=[pltpu.VMEM((l, SSD_W), F32), pltpu.VMEM((l, N_DIR * HP), F32),
                        pltpu.VMEM((l, LANES), F32), pltpu.VMEM((l, LANES), F32), pltpu.VMEM((l, DT_PAD), F32),
                        pltpu.VMEM((l, N_DIR * HP), F32), pltpu.VMEM((nc, LANES, N_DIR * HP), F32),
                        pltpu.VMEM((nc, SUBLANES, N_DIR * HP), F32), pltpu.VMEM((N_DIR, LANES, HP), F32)],
        compiler_params=_params("arbitrary"),
        name=f"ssd{l}",
    )(z, xbc, dt, init, cw, cb, dtb, alog, dexp, nw, e_mat)


HY_ROW_BLOCK = 16
HY_STEP_ROWS = 2048


def _hyena_kernel(v_ref, x1_ref, x2_ref, wv_ref, w1_ref, w2_ref, bv_ref, b1_ref, b2_ref,
                  f_ref, g_ref, kf_ref, kn_ref, skip_ref, o_ref, spec_s, prod_s, *, l, seg):
    m = l // 2
    tiles = o_ref.shape[0]
    n_seq = o_ref.shape[1] // l
    ct = tiles * LANES
    rb = HY_ROW_BLOCK
    first, last = _seg_edges(m, seg // 2)
    row0 = lax.broadcasted_iota(jnp.int32, (rb, 1), 0) == 0
    ev = slice(0, ct)
    od = slice(ct, 2 * ct)

    def conv_eo(x_ref, w_ref, b_ref, base):
        xe, xo = (jnp.concatenate([x_ref[k, pl.ds(base + p, m, stride=2), :] for k in range(tiles)], axis=1)
                  for p in range(2))
        w = w_ref[...]
        b = b_ref[...]
        xo_prev = jnp.where(first, 0.0, pltpu.roll(xo, 1, 0))
        xe_next = jnp.where(last, 0.0, pltpu.roll(xe, m - 1, 0))
        ce = xo_prev * w[0:1, :] + xe * w[1:2, :] + xo * w[2:3, :] + b
        co = xe * w[0:1, :] + xo * w[1:2, :] + xe_next * w[2:3, :] + b
        return jnp.concatenate([ce, co], axis=1)

    def pointwise(s, i, r0):
        re = slice(r0, r0 + rb)
        im = slice(m + r0, m + r0 + rb)
        er, orr = spec_s[s, re, ev], spec_s[s, re, od]
        ei, oi = spec_s[s, im, ev], spec_s[s, im, od]
        ker, kei, kor, koi, vr, vi = (kf_ref[i, p, re, :] for p in range(N_FILT_PLANES))
        if r0 == 0:
            e_n, o_n = ei[0:1, :], oi[0:1, :]
            ei = jnp.where(row0, 0.0, ei)
            oi = jnp.where(row0, 0.0, oi)
        pe_r = er * ker - ei * kei + orr * vr - oi * vi
        pe_i = er * kei + ei * ker + orr * vi + oi * vr
        po_r = er * kor - ei * koi + orr * ker - oi * kei
        po_i = er * koi + ei * kor + orr * kei + oi * ker
        if r0 == 0:
            kn = kn_ref[i]
            pe_i = jnp.where(row0, e_n * kn[0:1, :] + o_n * kn[2:3, :], pe_i)
            po_i = jnp.where(row0, e_n * kn[1:2, :] + o_n * kn[0:1, :], po_i)
        prod_s[s, re, ev] = pe_r.astype(BF16)
        prod_s[s, re, od] = po_r.astype(BF16)
        prod_s[s, im, ev] = pe_i.astype(BF16)
        prod_s[s, im, od] = po_i.astype(BF16)

    for s in range(n_seq):
        base = s * l
        zz = conv_eo(v_ref, wv_ref, bv_ref, base)
        gates = (conv_eo(x1_ref, w1_ref, b1_ref, base), conv_eo(x2_ref, w2_ref, b2_ref, base))
        for i in range(HY_ORDER):
            spec_s[s] = _dot(f_ref[...], zz.astype(BF16))
            for r0 in range(0, m, rb):
                pointwise(s, i, r0)
            conv = _dot(g_ref[...], prod_s[s])
            skip = skip_ref[i:i + 1, :]
            zz = gates[i] * (conv + zz * jnp.concatenate([skip, skip], axis=1))
        for k in range(tiles):
            o_ref[k, pl.ds(base, m, stride=2), :] = zz[:, k * LANES:(k + 1) * LANES]
            o_ref[k, pl.ds(base + 1, m, stride=2), :] = zz[:, ct + k * LANES:ct + (k + 1) * LANES]


def _hyena(hy, row0, bsz, l, n_seq, cw, cb, fmat, gmat, kf, kn, skip, *, seg):
    ct = HY_CH_TILE
    nct = HY_W // ct
    tiles = ct // LANES
    m = l // 2
    rows = n_seq * l
    blk0 = row0 // rows
    part = lambda p: pl.BlockSpec((tiles, rows, LANES), lambda j, b: (p * nct + j, blk0 + b, 0))
    wpart = lambda p: pl.BlockSpec((3, ct), lambda j, b: (0, p * nct + j))
    bpart = lambda p: pl.BlockSpec((1, ct), lambda j, b: (0, p * nct + j))
    return pl.pallas_call(
        functools.partial(_hyena_kernel, l=l, seg=seg),
        grid=(nct, bsz // n_seq),
        in_specs=[part(0), part(1), part(2), wpart(0), wpart(1), wpart(2), bpart(0), bpart(1), bpart(2),
                  pl.BlockSpec((l, m), lambda j, b: (0, 0)),
                  pl.BlockSpec((m, l), lambda j, b: (0, 0)),
                  pl.BlockSpec((HY_ORDER, N_FILT_PLANES, m, ct), lambda j, b: (0, 0, 0, j)),
                  pl.BlockSpec((HY_ORDER, SUBLANES, ct), lambda j, b: (0, 0, j)),
                  pl.BlockSpec((HY_ORDER, ct), lambda j, b: (0, j))],
        out_specs=pl.BlockSpec((tiles, rows, LANES), lambda j, b: (j, b, 0)),
        out_shape=jax.ShapeDtypeStruct((HY_W // LANES, bsz * l, LANES), F32),
        scratch_shapes=[pltpu.VMEM((n_seq, l, 2 * ct), F32), pltpu.VMEM((n_seq, l, 2 * ct), BF16)],
        compiler_params=_params("arbitrary", "arbitrary"),
        name=f"hyena{l}",
    )(hy, hy, hy, cw, cw, cw, cb, cb, cb, fmat, gmat, kf, kn, skip)


def _dft_mats(l):
    n = 2 * l
    f = np.arange(l, dtype=np.int64)[:, None]
    t = np.arange(l, dtype=np.int64)[None, :]
    ang = 2.0 * np.pi * ((f * t) % n).astype(np.float64) / n
    alt = np.where(np.arange(l) % 2 == 0, 1.0, -1.0)
    top = np.cos(ang)
    bot = -np.sin(ang)
    bot[0, :] = alt
    fwd = np.concatenate([top, bot], axis=0)
    wf = np.full((l,), 2.0)
    wf[0] = 1.0
    gtop = np.cos(ang).T * wf[None, :] / n
    gbot = -np.sin(ang).T * 2.0 / n
    gbot[:, 0] = alt / n
    inv = np.concatenate([gtop, gbot], axis=1)
    return fwd.astype(np.float32), inv.astype(np.float32)


def _filter_feats(l):
    t = np.linspace(0.0, 1.0, l)[:, None]
    w = (2.0 * np.pi / l) * np.arange(l, dtype=np.float64)[:, None]
    f = np.linspace(1e-4, HY_BANDS - 1, HY_BANDS)[None, :]
    feats = np.concatenate([t, np.cos(f * w), -np.sin(f * w)], axis=-1)
    out = np.zeros((l, EMB_PAD), np.float32)
    out[:, :HY_EMB] = feats
    return out[0::2], out[1::2]


def _shift_twiddles(l):
    theta = 2.0 * np.pi * np.arange(l // 2, dtype=np.float64) / l
    return np.stack([np.cos(theta), np.sin(theta)], axis=1).astype(np.float32)


def _head_expand():
    e = np.zeros((LANES, N_DIR * HP), np.float32)
    for j in range(N_DIR * SSD_HEADS):
        e[j, j * SSD_HEAD_DIM:(j + 1) * SSD_HEAD_DIM] = 1.0
    return e


def kernel(x_prompt, x_sample, state_ssd, c, c_ctx, w_ada, b_ada, norm_ffn1, ffn1_w_gate, ffn1_w_up, ffn1_w_down, norm_mix, w_in, w_out, ssd_conv_w, ssd_conv_b, ssd_dt_bias, ssd_a_log, ssd_d, ssd_norm_w, hy_conv_w, hy_conv_b, hy_w1, hy_b1, hy_freq, hy_w2, hy_b2, hy_w3, hy_skip, norm_ffn2, ffn2_w_gate, ffn2_w_up, ffn2_w_down, norm_final):
    assert w_ada.shape[0] == 1, "single layer"
    n_ctx, l_ctx, _ = x_prompt.shape
    n_lat, l_lat, _ = x_sample.shape
    t_ctx = n_ctx * l_ctx

    cond = jnp.zeros((16, D_MODEL), F32).at[:n_lat].set(c).at[n_lat].set(c_ctx)
    mod = _ada(cond, w_ada, b_ada).reshape(16, N_MOD, D_MODEL)

    tr = lambda w: jnp.swapaxes(w, 1, 2)
    wg1, wu1, wd1, wg2, wu2, wd2 = _cast_pad_rows([tr(ffn1_w_gate), tr(ffn1_w_up), ffn1_w_down,
                                                   tr(ffn2_w_gate), tr(ffn2_w_up), ffn2_w_down])
    wi, wo = _cast_mix(tr(w_in), w_out)

    row = lambda v: v.reshape(1, -1)
    pad_dt = lambda v: jnp.pad(v.reshape(1, -1), ((0, 0), (0, DT_PAD - N_DIR * SSD_HEADS)))
    dtb, alog = pad_dt(ssd_dt_bias[0]), pad_dt(ssd_a_log[0])
    dexp = jnp.repeat(ssd_d[0], SSD_HEAD_DIM).reshape(1, SSD_W)
    e_mat = jnp.asarray(_head_expand())
    w1p = jnp.pad(hy_w1[0], ((0, EMB_PAD - HY_EMB), (0, 0)))
    deltas = jnp.asarray(np.abs(np.linspace(HY_MIN_DECAY, HY_MAX_DECAY, HY_ORDER * HY_W))
                         .reshape(HY_ORDER, HY_W).astype(np.float32))

    x1, z, xbc, dt, hy = _ffn_in(x_prompt.reshape(t_ctx, D_MODEL), x_sample.reshape(n_lat * l_lat, D_MODEL), mod,
                                 n_lat, l_lat, row(norm_ffn1[0]), row(norm_mix[0]), wg1, wu1, wd1, wi)

    def mixers(row0, bsz, l, init, seg):
        fwd_np, inv_np = _dft_mats(l // 2)
        fmat = jnp.asarray(fwd_np).astype(BF16)
        gmat = jnp.asarray(inv_np).astype(BF16)
        feats_e, feats_o = _filter_feats(l)
        kf, kn = _filters(l, jnp.asarray(feats_e), jnp.asarray(feats_o), w1p, row(hy_b1[0]), row(hy_freq[0]),
                          hy_w2[0], row(hy_b2[0]), hy_w3[0], deltas, fmat, jnp.asarray(_shift_twiddles(l)))
        ys, fin = _ssd(z, xbc, dt, row0, bsz, l, init, ssd_conv_w[0], row(ssd_conv_b[0]), dtb, alog, dexp,
                       row(ssd_norm_w[0]), e_mat, seg=seg)
        yh = _hyena(hy, row0, bsz, l, HY_STEP_ROWS // l, hy_conv_w[0], row(hy_conv_b[0]), fmat, gmat, kf, kn,
                    hy_skip[0], seg=seg)
        return ys, yh, fin

    ys_ctx, yh_ctx, ctx_fin = mixers(0, n_ctx, l_ctx, None, l_ctx)
    lat_init = state_ssd[:, 0].reshape(n_lat, N_DIR, HP, SSD_STATE)
    ys_lat, yh_lat, _ = mixers(t_ctx, n_lat, l_lat, lat_init, GRID_W)

    y_ctx, y_lat = _out_ffn(x1, ys_ctx, ys_lat, yh_ctx, yh_lat, mod, n_lat, l_lat, row(norm_ffn2[0]),
                            row(norm_final), wo, wg2, wu2, wd2)
    new_state = ctx_fin.reshape(n_ctx, 1, N_DIR, SSD_HEADS, SSD_HEAD_DIM, SSD_STATE).astype(x_prompt.dtype)
    return (y_ctx.reshape(n_ctx, l_ctx, D_MODEL), y_lat.reshape(n_lat, l_lat, D_MODEL), new_state)
```

```python
import functools
import math

import numpy as np
import jax
import jax.numpy as jnp
from jax import lax
from jax.experimental import pallas as pl
from jax.experimental.pallas import tpu as pltpu

F32 = jnp.float32
BF16 = jnp.bfloat16

D_MODEL = 1024
GRID_W = 64
N_MOD = 9
RMS_EPS = 1e-6
FFN_DIM = 2752
SSD_W = 512
SSD_HEADS = 8
SSD_HEAD_DIM = 64
SSD_STATE = 64
SSD_GROUPS = 2
SSD_CHUNK = 128
SSD_XBC = SSD_W + 2 * SSD_GROUPS * SSD_STATE
N_DIR = 2
HY_W = 512
HY_ORDER = 2
HY_EMB = 33
HY_BANDS = (HY_EMB - 1) // 2
HY_HIDDEN = 64
HY_MIN_DECAY = math.log(1e-2) / 1.5
HY_MAX_DECAY = math.log(1e-2) / 0.3
IN_SPLITS = (SSD_W, SSD_W + SSD_XBC, SSD_W + SSD_XBC + N_DIR * SSD_HEADS)
IN_COLS = IN_SPLITS[-1] + (HY_ORDER + 1) * HY_W

LANES = 128
SUBLANES = 8
FFN_PAD = 2816
FFN_CHUNK = 256
DT_PAD = LANES
IN_ROWS = IN_COLS + DT_PAD - N_DIR * SSD_HEADS
EMB_PAD = LANES
TOKEN_TILE = 512
HY_CH_TILE = 256
VMEM_LIMIT = 56 * 1024 * 1024
NEG_BIG = -1e30
HP = SSD_HEADS * SSD_HEAD_DIM


def _silu(x):
    return x * jax.nn.sigmoid(x)


def _softplus(x):
    return jnp.maximum(x, 0.0) + jnp.log1p(jnp.exp(-jnp.abs(x)))


def _rms_mod(x, gain, shift, scale):
    ms = jnp.mean(x * x, axis=-1, keepdims=True)
    y = x * lax.rsqrt(ms + RMS_EPS) * gain
    return y * (1.0 + scale) + shift


def _dot(a, b):
    return jnp.dot(a, b, preferred_element_type=F32)


def _dot_nt(a, b):
    return lax.dot_general(a, b, (((1,), (1,)), ((), ())), preferred_element_type=F32)


def _split3(x):
    hi = x.astype(BF16)
    r = x - hi.astype(F32)
    mid = r.astype(BF16)
    lo = (r - mid.astype(F32)).astype(BF16)
    return hi, mid, lo


def _dot_exact_lhs(x, m01):
    hi, mid, lo = _split3(x)
    return _dot(hi, m01) + _dot(mid, m01) + _dot(lo, m01)


def _short_conv(x, w, b, first, last, l):
    prev = jnp.where(first, 0.0, pltpu.roll(x, 1, 0))
    nxt = jnp.where(last, 0.0, pltpu.roll(x, l - 1, 0))
    return prev * w[0:1, :] + x * w[1:2, :] + nxt * w[2:3, :] + b


def _seg_edges(l, seg):
    pos = lax.broadcasted_iota(jnp.int32, (l, 1), 0) & (seg - 1)
    return pos == 0, pos == seg - 1


def _swiglu_acc(h, wg_ref, wu_ref, wd_ref):
    acc = None
    for k in range(FFN_PAD // FFN_CHUNK):
        cols = slice(k * FFN_CHUNK, (k + 1) * FFN_CHUNK)
        g = _dot_nt(h, wg_ref[cols, :])
        u = _dot_nt(h, wu_ref[cols, :])
        a = (_silu(g) * u).astype(BF16)
        part = _dot(a, wd_ref[cols, :])
        acc = part if acc is None else acc + part
    return acc


def _params(*semantics):
    return pltpu.CompilerParams(dimension_semantics=semantics, vmem_limit_bytes=VMEM_LIMIT)


def _cast_pad_rows_kernel(*refs):
    n = len(refs) // 2
    rows = refs[0].shape[0]
    row = pl.program_id(0) * rows + lax.broadcasted_iota(jnp.int32, (rows, 1), 0)
    for w_ref, o_ref in zip(refs[:n], refs[n:]):
        o_ref[...] = jnp.where(row < FFN_DIM, w_ref[...], 0.0).astype(BF16)


def _cast_pad_rows(ws):
    n = len(ws)
    return pl.pallas_call(
        _cast_pad_rows_kernel,
        grid=(FFN_PAD // FFN_CHUNK,),
        in_specs=[pl.BlockSpec((None, FFN_CHUNK, D_MODEL), lambda i: (0, i, 0))] * n,
        out_specs=[pl.BlockSpec((FFN_CHUNK, D_MODEL), lambda i: (i, 0))] * n,
        out_shape=[jax.ShapeDtypeStruct((FFN_PAD, D_MODEL), BF16)] * n,
        compiler_params=_params("arbitrary"),
        name="wcast_rows",
    )(*ws)


def _cast_mix_kernel(wi_ref, wo_ref, wib_ref, wob_ref):
    o1, o2, o3 = IN_SPLITS
    wib_ref[0:o3, :] = wi_ref[0:o3, :].astype(BF16)
    wib_ref[o3:o2 + DT_PAD, :] = jnp.zeros((o2 + DT_PAD - o3, D_MODEL), BF16)
    wib_ref[o2 + DT_PAD:, :] = wi_ref[o3:, :].astype(BF16)
    wob_ref[...] = wo_ref[...].astype(BF16)


def _cast_mix(w_in_t, w_out):
    full = lambda *shape: pl.BlockSpec(shape, lambda: (0,) * len(shape))
    sq = lambda *shape: pl.BlockSpec((None,) + shape, lambda: (0,) * (len(shape) + 1))
    return pl.pallas_call(
        _cast_mix_kernel,
        in_specs=[sq(IN_COLS, D_MODEL), sq(D_MODEL, D_MODEL)],
        out_specs=[full(IN_ROWS, D_MODEL), full(D_MODEL, D_MODEL)],
        out_shape=[jax.ShapeDtypeStruct((IN_ROWS, D_MODEL), BF16), jax.ShapeDtypeStruct((D_MODEL, D_MODEL), BF16)],
        compiler_params=pltpu.CompilerParams(vmem_limit_bytes=VMEM_LIMIT),
        name="wcast_mix",
    )(w_in_t, w_out)


def _ada_kernel(cond_ref, w_ref, b_ref, o_ref):
    s = _silu(cond_ref[...]).astype(BF16)
    o_ref[...] = _dot(s, w_ref[...].astype(BF16)) + b_ref[...]


def _ada(cond, w_ada, b_ada):
    rows = cond.shape[0]
    n = w_ada.shape[-1]
    tn = D_MODEL
    return pl.pallas_call(
        _ada_kernel,
        grid=(n // tn,),
        in_specs=[
            pl.BlockSpec((rows, D_MODEL), lambda j: (0, 0)),
            pl.BlockSpec((None, D_MODEL, tn), lambda j: (0, 0, j)),
            pl.BlockSpec((1, tn), lambda j: (0, j)),
        ],
        out_specs=pl.BlockSpec((rows, tn), lambda j: (0, j)),
        out_shape=jax.ShapeDtypeStruct((rows, n), F32),
        compiler_params=_params("arbitrary"),
        name="ada",
    )(cond, w_ada, b_ada)


N_FILT_PLANES = 6


def _filter_kernel(fe_ref, fo_ref, w1_ref, b1_ref, fr_ref, w2_ref, b2_ref, w3_ref, dl_ref, f_ref, tw_ref,
                   kf_ref, kn_ref, *, m):
    freq = fr_ref[...]
    w1 = w1_ref[...].astype(BF16)
    w2 = w2_ref[...].astype(BF16)

    def hidden(feats):
        h = jnp.sin(freq * (_dot(feats.astype(BF16), w1) + b1_ref[...]))
        h = jnp.sin(freq * (_dot(h.astype(BF16), w2) + b2_ref[...]))
        return h.astype(BF16)

    feats_e = fe_ref[...]
    feats_o = fo_ref[...]
    hb_e = hidden(feats_e)
    hb_o = hidden(feats_o)
    t_e = feats_e[:, 0:1]
    t_o = feats_o[:, 0:1]
    row = lax.broadcasted_iota(jnp.int32, (m, 1), 0)
    sign = jnp.where((row & 1) == 0, 1.0, -1.0)
    cos_t = tw_ref[:, 0:1]
    sin_t = tw_ref[:, 1:2]
    f_top = f_ref[0:m, :]
    f_bot = f_ref[m:2 * m, :]

    def spectrum(k):
        kb = k.astype(BF16)
        return _dot(f_top, kb), jnp.where(row == 0, 0.0, _dot(f_bot, kb)), jnp.sum(k * sign, axis=0, keepdims=True)

    for i in range(HY_ORDER):
        dl = dl_ref[i:i + 1, :]
        c0 = (0 * HY_ORDER + i) * HY_W
        c1 = (1 * HY_ORDER + i) * HY_W
        w3f = w3_ref[:, c0:c0 + HY_W].astype(BF16)
        w3b = w3_ref[:, c1:c1 + HY_W].astype(BF16)
        win_e = jnp.exp(-t_e * dl)
        win_o = jnp.exp(-t_o * dl)
        k0e = _dot(hb_e, w3f) * win_e
        k0o = _dot(hb_o, w3f) * win_o
        k1e = jnp.where(row == 0, 0.0, _dot(hb_e, w3b) * win_e)
        k1o = _dot(hb_o, w3b) * win_o
        ker, _, ken = spectrum(k0e + k1e)
        _, kei, _ = spectrum(k0e - k1e)
        ar, ai, an = spectrum(k0o)
        br, bi, bn = spectrum(k1o)
        kf_ref[i, 0] = ker
        kf_ref[i, 1] = kei
        kf_ref[i, 2] = ar + cos_t * br + sin_t * bi
        kf_ref[i, 3] = ai + sin_t * br - cos_t * bi
        kf_ref[i, 4] = cos_t * ar + sin_t * ai + br
        kf_ref[i, 5] = cos_t * ai - sin_t * ar - bi
        kn_ref[i] = jnp.concatenate([ken, an - bn, bn - an, jnp.zeros((SUBLANES - 3, HY_W), F32)], axis=0)


def _filters(l, feats_e, feats_o, w1p, b1, freq, w2, b2, w3, deltas, fmat, twid):
    m = l // 2
    full = lambda *shape: pl.BlockSpec(shape, lambda: (0,) * len(shape))
    return pl.pallas_call(
        functools.partial(_filter_kernel, m=m),
        in_specs=[full(m, EMB_PAD), full(m, EMB_PAD), full(EMB_PAD, HY_HIDDEN), full(1, HY_HIDDEN),
                  full(1, HY_HIDDEN), full(HY_HIDDEN, HY_HIDDEN), full(1, HY_HIDDEN),
                  full(HY_HIDDEN, N_DIR * HY_ORDER * HY_W), full(HY_ORDER, HY_W), full(l, m), full(m, 2)],
        out_specs=[full(HY_ORDER, N_FILT_PLANES, m, HY_W), full(HY_ORDER, SUBLANES, HY_W)],
        out_shape=[jax.ShapeDtypeStruct((HY_ORDER, N_FILT_PLANES, m, HY_W), F32),
                   jax.ShapeDtypeStruct((HY_ORDER, SUBLANES, HY_W), F32)],
        compiler_params=pltpu.CompilerParams(vmem_limit_bytes=VMEM_LIMIT),
        name=f"filt{l}",
    )(feats_e, feats_o, w1p, b1, freq, w2, b2, w3, deltas, fmat, twid)


def _const_spec(shape):
    return pl.BlockSpec(shape, lambda i: (0,) * len(shape), pipeline_mode=pl.Buffered(1))


def _group_specs(ctx_tiles, width):
    tm = TOKEN_TILE
    return [pl.BlockSpec((tm, width), lambda i: (jnp.minimum(i, ctx_tiles - 1), 0)),
            pl.BlockSpec((tm, width), lambda i: (jnp.maximum(i - ctx_tiles, 0), 0))]


def _mod_spec(ctx_tiles, n_lat, tiles_per_seq):
    return pl.BlockSpec((None, N_MOD, D_MODEL),
                        lambda i: (jnp.where(i < ctx_tiles, n_lat, (i - ctx_tiles) // tiles_per_seq), 0, 0))


def _ffn_in_kernel(xc_ref, xl_ref, mod_ref, n1_ref, nm_ref, wg_ref, wu_ref, wd_ref, wi_ref,
                   x1_ref, z_ref, xbc_ref, dt_ref, hy_ref, *, ctx_tiles):
    x = jnp.where(pl.program_id(0) < ctx_tiles, xc_ref[...], xl_ref[...])
    mod = mod_ref[...]
    h = _rms_mod(x, n1_ref[...], mod[0:1, :], mod[1:2, :]).astype(BF16)
    x1 = x + (0.5 * mod[2:3, :]) * _swiglu_acc(h, wg_ref, wu_ref, wd_ref)
    x1_ref[...] = x1
    h2 = _rms_mod(x1, nm_ref[...], mod[3:4, :], mod[4:5, :]).astype(BF16)
    o1, o2, _ = IN_SPLITS
    z_ref[...] = _dot_nt(h2, wi_ref[0:o1, :])
    xbc_ref[...] = _dot_nt(h2, wi_ref[o1:o2, :])
    dt_ref[...] = _dot_nt(h2, wi_ref[o2:o2 + DT_PAD, :])
    hy = _dot_nt(h2, wi_ref[o2 + DT_PAD:, :])
    for k in range(hy_ref.shape[0]):
        hy_ref[k] = hy[:, k * LANES:(k + 1) * LANES]


def _ffn_in(x_ctx, x_lat, mod, n_lat, l_lat, n1, nm, wg, wu, wd, wi):
    tm = TOKEN_TILE
    ctx_tiles = x_ctx.shape[0] // tm
    tokens = x_ctx.shape[0] + x_lat.shape[0]
    row_spec = lambda n: pl.BlockSpec((tm, n), lambda i: (i, 0))
    widths = (D_MODEL, SSD_W, SSD_XBC, DT_PAD)
    hy_tiles = (HY_ORDER + 1) * HY_W // LANES
    return pl.pallas_call(
        functools.partial(_ffn_in_kernel, ctx_tiles=ctx_tiles),
        grid=(tokens // tm,),
        in_specs=_group_specs(ctx_tiles, D_MODEL) + [
            _mod_spec(ctx_tiles, n_lat, l_lat // tm),
            _const_spec((1, D_MODEL)), _const_spec((1, D_MODEL)),
            _const_spec(wg.shape), _const_spec(wu.shape), _const_spec(wd.shape), _const_spec(wi.shape)],
        out_specs=[row_spec(n) for n in widths] + [pl.BlockSpec((hy_tiles, tm, LANES), lambda i: (0, i, 0))],
        out_shape=[jax.ShapeDtypeStruct((tokens, n), F32) for n in widths]
        + [jax.ShapeDtypeStruct((hy_tiles, tokens, LANES), F32)],
        compiler_params=_params("arbitrary"),
        name="ffn_in",
    )(x_ctx, x_lat, mod, n1, nm, wg, wu, wd, wi)


def _out_ffn_kernel(x1_ref, ysc_ref, ysl_ref, yhc_ref, yhl_ref, mod_ref, n3_ref, nf_ref,
                    wo_ref, wg_ref, wu_ref, wd_ref, oc_ref, ol_ref, *, ctx_tiles):
    is_ctx = pl.program_id(0) < ctx_tiles
    mod = mod_ref[...]
    ys = jnp.where(is_ctx, ysc_ref[...], ysl_ref[...])
    yh = [jnp.where(is_ctx, yhc_ref[k], yhl_ref[k]) for k in range(yhc_ref.shape[0])]
    y = jnp.concatenate([ys] + yh, axis=1).astype(BF16)
    x2 = x1_ref[...] + mod[5:6, :] * _dot(y, wo_ref[...])
    h = _rms_mod(x2, n3_ref[...], mod[6:7, :], mod[7:8, :]).astype(BF16)
    x3 = x2 + (0.5 * mod[8:9, :]) * _swiglu_acc(h, wg_ref, wu_ref, wd_ref)
    ms = jnp.mean(x3 * x3, axis=-1, keepdims=True)
    out = x3 * lax.rsqrt(ms + RMS_EPS) * nf_ref[...]

    @pl.when(is_ctx)
    def _():
        oc_ref[...] = out

    @pl.when(jnp.logical_not(is_ctx))
    def _():
        ol_ref[...] = out


def _out_ffn(x1, ys_ctx, ys_lat, yh_ctx, yh_lat, mod, n_lat, l_lat, n3, nf, wo, wg, wu, wd):
    tm = TOKEN_TILE
    ctx_tiles = ys_ctx.shape[0] // tm
    tokens = x1.shape[0]
    return pl.pallas_call(
        functools.partial(_out_ffn_kernel, ctx_tiles=ctx_tiles),
        grid=(tokens // tm,),
        in_specs=[pl.BlockSpec((tm, D_MODEL), lambda i: (i, 0))]
        + _group_specs(ctx_tiles, SSD_W) + [
            pl.BlockSpec((HY_W // LANES, tm, LANES), lambda i: (0, jnp.minimum(i, ctx_tiles - 1), 0)),
            pl.BlockSpec((HY_W // LANES, tm, LANES), lambda i: (0, jnp.maximum(i - ctx_tiles, 0), 0)),
            _mod_spec(ctx_tiles, n_lat, l_lat // tm),
            _const_spec((1, D_MODEL)), _const_spec((1, D_MODEL)),
            _const_spec(wo.shape), _const_spec(wg.shape), _const_spec(wu.shape), _const_spec(wd.shape)],
        out_specs=_group_specs(ctx_tiles, D_MODEL),
        out_shape=[jax.ShapeDtypeStruct((ys_ctx.shape[0], D_MODEL), F32),
                   jax.ShapeDtypeStruct((ys_lat.shape[0], D_MODEL), F32)],
        compiler_params=_params("arbitrary"),
        name="out_ffn",
    )(x1, ys_ctx, ys_lat, yh_ctx, yh_lat, mod, n3, nf, wo, wg, wu, wd)


def _ssd_kernel(z_ref, xbc_ref, dt_ref, init_ref, cw_ref, cb_ref, dtb_ref, alog_ref, dexp_ref, nw_ref, e_ref,
                y_ref, fin_ref,
                xs_s, xd_s, b_s, c_s, a_s, ec_s, dst_s, et_s,
                *, l, seg, zero_init):
    q = SSD_CHUNK
    nc = l // q
    n_seq = y_ref.shape[0] // l
    cr = max(seg, q)
    first, last = _seg_edges(cr, seg)
    e_mat = e_ref[...].astype(BF16)
    a_coef = -jnp.exp(alog_ref[...])

    for r0 in range(0, n_seq * l, cr):
        rows = slice(r0, r0 + cr)
        u = _silu(_short_conv(xbc_ref[rows, :], cw_ref[...], cb_ref[...], first, last, cr))
        xs = u[:, :SSD_W]
        xs_s[rows, :] = xs
        b_s[rows, :] = u[:, SSD_W:SSD_W + LANES]
        c_s[rows, :] = u[:, SSD_W + LANES:]
        dt = _softplus(dt_ref[rows, :] + dtb_ref[...])
        a_s[rows, :] = dt * a_coef
        xd_s[rows, :] = jnp.concatenate([xs, xs], axis=1) * _dot(dt.astype(BF16), e_mat)

    row_g = lax.broadcasted_iota(jnp.int32, (LANES, N_DIR * HP), 0) // SSD_STATE
    lane_g = (lax.broadcasted_iota(jnp.int32, (LANES, N_DIR * HP), 1) % HP) // (HP // SSD_GROUPS)
    own_t = row_g == lane_g

    ii = lax.broadcasted_iota(jnp.int32, (q, q), 0)
    jj = lax.broadcasted_iota(jnp.int32, (q, q), 1)
    tri_incl = (jj <= ii).astype(BF16)
    lane = lax.broadcasted_iota(jnp.int32, (q, LANES), 1)
    low_half = lane < SSD_STATE
    is_fwd_col = lane < SSD_HEADS

    def local_pass(c):
        rows = slice(c * q, (c + 1) * q)
        a_c = a_s[rows, :]
        a1, a2, a3 = _split3(a_c)
        cum_f = _dot(tri_incl, a1) + _dot(tri_incl, a2) + _dot(tri_incl, a3)
        tot = cum_f[q - 1:q, :]
        cum = jnp.where(is_fwd_col, cum_f, tot - cum_f + a_c)
        cum_t = cum.T
        ec_s[rows, :] = _dot(jnp.exp(cum).astype(BF16), e_mat)
        et_s[c] = _dot_exact_lhs(jnp.broadcast_to(jnp.exp(tot), (SUBLANES, LANES)), e_mat)
        xd = xd_s[rows, :]
        w = (xd * _dot(jnp.exp(tot - cum).astype(BF16), e_mat)).astype(BF16)
        bc = b_s[rows, :]
        bcb = bc.astype(BF16)
        dst_s[c] = jnp.where(own_t, _dot(bc.T.astype(BF16), w), 0.0)

        cc = c_s[rows, :]
        g_mats = [_dot_nt(jnp.where(low_half, cc, 0.0).astype(BF16), bcb),
                  _dot_nt(jnp.where(low_half, 0.0, cc).astype(BF16), bcb)]
        y_parts = []
        for pair in range(SSD_HEADS // 2):
            g = pair // (SSD_HEADS // 2 // SSD_GROUPS)
            acc = None
            for d in range(N_DIR):
                keep = (ii >= jj) if d == 0 else (ii <= jj)
                s_mats = []
                for hh in (2 * pair, 2 * pair + 1):
                    col = d * SSD_HEADS + hh
                    diff = cum[:, col:col + 1] - cum_t[col:col + 1, :]
                    decay = jnp.exp(jnp.where(keep, diff, NEG_BIG))
                    s_mats.append((g_mats[g] * decay).astype(BF16))
                lhs = jnp.concatenate(s_mats, axis=1)
                xp = xd[:, d * HP + pair * LANES:d * HP + (pair + 1) * LANES]
                rhs = jnp.concatenate([jnp.where(low_half, xp, 0.0), jnp.where(low_half, 0.0, xp)],
                                      axis=0).astype(BF16)
                part = _dot(lhs, rhs)
                acc = part if acc is None else acc + part
            y_parts.append(acc)
        y_ref[rows, :] = jnp.concatenate(y_parts, axis=1)

    for c in range(n_seq * nc):
        local_pass(c)

    half = HP // SSD_GROUPS
    for s in range(n_seq):
        states = []
        for d in range(N_DIR):
            if zero_init:
                states.append(jnp.zeros((LANES, HP), F32))
            else:
                s0 = init_ref[s, d]
                states.append(jnp.where(own_t[:, :HP], jnp.concatenate([s0, s0], axis=1).T, 0.0))

        for k in range(nc):
            for d in range(N_DIR):
                c = s * nc + (k if d == 0 else nc - 1 - k)
                rows = slice(c * q, (c + 1) * q)
                lanes = slice(d * HP, (d + 1) * HP)
                y_off = _dot(c_s[rows, :].astype(BF16), states[d].astype(BF16)) * ec_s[rows, lanes]
                y_ref[rows, :] = y_ref[rows, :] + y_off
                states[d] = states[d] * et_s[c, 0:1, lanes] + dst_s[c, :, lanes]

        for d in range(N_DIR):
            st = states[d].T
            fin_ref[s, d, 0:half, :] = st[0:half, 0:SSD_STATE]
            fin_ref[s, d, half:HP, :] = st[half:HP, SSD_STATE:2 * SSD_STATE]

        for c in range(s * nc, (s + 1) * nc):
            rows = slice(c * q, (c + 1) * q)
            y = y_ref[rows, :] + xs_s[rows, :] * dexp_ref[...]
            y = y * _silu(z_ref[rows, :])
            ms = jnp.mean(y * y, axis=-1, keepdims=True)
            y_ref[rows, :] = y * lax.rsqrt(ms + RMS_EPS) * nw_ref[...]


SSD_STEP_ROWS = 1024


def _ssd(z, xbc, dt, row0, bsz, l, init, cw, cb, dtb, alog, dexp, nw, e_mat, *, seg):
    n_seq = SSD_STEP_ROWS // l
    rows = n_seq * l
    zero_init = init is None
    if zero_init:
        init = jnp.zeros((n_seq, N_DIR, HP, SSD_STATE), F32)
        init_map = lambda b: (0, 0, 0, 0)
    else:
        init_map = lambda b: (b, 0, 0, 0)
    blk0 = row0 // rows
    seq_spec = lambda n: pl.BlockSpec((rows, n), lambda b: (blk0 + b, 0))
    const = lambda *shape: pl.BlockSpec(shape, lambda b: (0,) * len(shape))
    nc = rows // SSD_CHUNK
    return pl.pallas_call(
        functools.partial(_ssd_kernel, l=l, seg=seg, zero_init=zero_init),
        grid=(bsz // n_seq,),
        in_specs=[seq_spec(SSD_W), seq_spec(SSD_XBC), seq_spec(DT_PAD),
                  pl.BlockSpec((n_seq, N_DIR, HP, SSD_STATE), init_map),
                  const(3, SSD_XBC), const(1, SSD_XBC), const(1, DT_PAD), const(1, DT_PAD),
                  const(1, SSD_W), const(1, SSD_W), const(LANES, N_DIR * HP)],
        out_specs=[pl.BlockSpec((rows, SSD_W), lambda b: (b, 0)),
                   pl.BlockSpec((n_seq, N_DIR, HP, SSD_STATE), lambda b: (b, 0, 0, 0))],
        out_shape=[jax.ShapeDtypeStruct((bsz * l, SSD_W), F32),
                   jax.ShapeDtypeStruct((bsz, N_DIR, HP, SSD_STATE), F32)],
        scratch_shapes=[pltpu.VMEM((rows, SSD_W), F32), pltpu.VMEM((rows, N_DIR * HP), F32),
                        pltpu.VMEM((rows, LANES), F32), pltpu.VMEM((rows, LANES), F32),
                        pltpu.VMEM((rows, DT_PAD), F32), pltpu.VMEM((rows, N_DIR * HP), F32),
                        pltpu.VMEM((nc, LANES, N_DIR * HP), F32), pltpu.VMEM((nc, SUBLANES, N_DIR * HP), F32)],
        compiler_params=_params("arbitrary"),
        name=f"ssd{l}",
    )(z, xbc, dt, init, cw, cb, dtb, alog, dexp, nw, e_mat)


HY_ROW_BLOCK = 16
HY_STEP_ROWS = 2048


def _hyena_kernel(v_ref, x1_ref, x2_ref, wv_ref, w1_ref, w2_ref, bv_ref, b1_ref, b2_ref,
                  f_ref, g_ref, kf_ref, kn_ref, skip_ref, o_ref, spec_s, prod_s, *, l, seg):
    m = l // 2
    tiles = o_ref.shape[0]
    n_seq = o_ref.shape[1] // l
    ct = tiles * LANES
    rb = HY_ROW_BLOCK
    first, last = _seg_edges(m, seg // 2)
    row0 = lax.broadcasted_iota(jnp.int32, (rb, 1), 0) == 0
    ev = slice(0, ct)
    od = slice(ct, 2 * ct)

    def conv_eo(x_ref, w_ref, b_ref, base):
        xe, xo = (jnp.concatenate([x_ref[k, pl.ds(base + p, m, stride=2), :] for k in range(tiles)], axis=1)
                  for p in range(2))
        w = w_ref[...]
        b = b_ref[...]
        xo_prev = jnp.where(first, 0.0, pltpu.roll(xo, 1, 0))
        xe_next = jnp.where(last, 0.0, pltpu.roll(xe, m - 1, 0))
        ce = xo_prev * w[0:1, :] + xe * w[1:2, :] + xo * w[2:3, :] + b
        co = xe * w[0:1, :] + xo * w[1:2, :] + xe_next * w[2:3, :] + b
        return jnp.concatenate([ce, co], axis=1)

    def pointwise(s, i, r0):
        re = slice(r0, r0 + rb)
        im = slice(m + r0, m + r0 + rb)
        er, orr = spec_s[s, re, ev], spec_s[s, re, od]
        ei, oi = spec_s[s, im, ev], spec_s[s, im, od]
        ker, kei, kor, koi, vr, vi = (kf_ref[i, p, re, :] for p in range(N_FILT_PLANES))
        if r0 == 0:
            e_n, o_n = ei[0:1, :], oi[0:1, :]
            ei = jnp.where(row0, 0.0, ei)
            oi = jnp.where(row0, 0.0, oi)
        pe_r = er * ker - ei * kei + orr * vr - oi * vi
        pe_i = er * kei + ei * ker + orr * vi + oi * vr
        po_r = er * kor - ei * koi + orr * ker - oi * kei
        po_i = er * koi + ei * kor + orr * kei + oi * ker
        if r0 == 0:
            kn = kn_ref[i]
            pe_i = jnp.where(row0, e_n * kn[0:1, :] + o_n * kn[2:3, :], pe_i)
            po_i = jnp.where(row0, e_n * kn[1:2, :] + o_n * kn[0:1, :], po_i)
        prod_s[s, re, ev] = pe_r.astype(BF16)
        prod_s[s, re, od] = po_r.astype(BF16)
        prod_s[s, im, ev] = pe_i.astype(BF16)
        prod_s[s, im, od] = po_i.astype(BF16)

    for s in range(n_seq):
        base = s * l
        zz = conv_eo(v_ref, wv_ref, bv_ref, base)
        gates = (conv_eo(x1_ref, w1_ref, b1_ref, base), conv_eo(x2_ref, w2_ref, b2_ref, base))
        for i in range(HY_ORDER):
            spec_s[s] = _dot(f_ref[...], zz.astype(BF16))
            for r0 in range(0, m, rb):
                pointwise(s, i, r0)
            conv = _dot(g_ref[...], prod_s[s])
            skip = skip_ref[i:i + 1, :]
            zz = gates[i] * (conv + zz * jnp.concatenate([skip, skip], axis=1))
        for k in range(tiles):
            o_ref[k, pl.ds(base, m, stride=2), :] = zz[:, k * LANES:(k + 1) * LANES]
            o_ref[k, pl.ds(base + 1, m, stride=2), :] = zz[:, ct + k * LANES:ct + (k + 1) * LANES]


def _hyena(hy, row0, bsz, l, n_seq, cw, cb, fmat, gmat, kf, kn, skip, *, seg):
    ct = HY_CH_TILE
    nct = HY_W // ct
    tiles = ct // LANES
    m = l // 2
    rows = n_seq * l
    blk0 = row0 // rows
    part = lambda p: pl.BlockSpec((tiles, rows, LANES), lambda j, b: (p * nct + j, blk0 + b, 0))
    wpart = lambda p: pl.BlockSpec((3, ct), lambda j, b: (0, p * nct + j))
    bpart = lambda p: pl.BlockSpec((1, ct), lambda j, b: (0, p * nct + j))
    return pl.pallas_call(
        functools.partial(_hyena_kernel, l=l, seg=seg),
        grid=(nct, bsz // n_seq),
        in_specs=[part(0), part(1), part(2), wpart(0), wpart(1), wpart(2), bpart(0), bpart(1), bpart(2),
                  pl.BlockSpec((l, m), lambda j, b: (0, 0)),
                  pl.BlockSpec((m, l), lambda j, b: (0, 0)),
                  pl.BlockSpec((HY_ORDER, N_FILT_PLANES, m, ct), lambda j, b: (0, 0, 0, j)),
                  pl.BlockSpec((HY_ORDER, SUBLANES, ct), lambda j, b: (0, 0, j)),
                  pl.BlockSpec((HY_ORDER, ct), lambda j, b: (0, j))],
        out_specs=pl.BlockSpec((tiles, rows, LANES), lambda j, b: (j, b, 0)),
        out_shape=jax.ShapeDtypeStruct((HY_W // LANES, bsz * l, LANES), F32),
        scratch_shapes=[pltpu.VMEM((n_seq, l, 2 * ct), F32), pltpu.VMEM((n_seq, l, 2 * ct), BF16)],
        compiler_params=_params("arbitrary", "arbitrary"),
        name=f"hyena{l}",
    )(hy, hy, hy, cw, cw, cw, cb, cb, cb, fmat, gmat, kf, kn, skip)


def _dft_mats(l):
    n = 2 * l
    f = np.arange(l, dtype=np.int64)[:, None]
    t = np.arange(l, dtype=np.int64)[None, :]
    ang = 2.0 * np.pi * ((f * t) % n).astype(np.float64) / n
    alt = np.where(np.arange(l) % 2 == 0, 1.0, -1.0)
    top = np.cos(ang)
    bot = -np.sin(ang)
    bot[0, :] = alt
    fwd = np.concatenate([top, bot], axis=0)
    wf = np.full((l,), 2.0)
    wf[0] = 1.0
    gtop = np.cos(ang).T * wf[None, :] / n
    gbot = -np.sin(ang).T * 2.0 / n
    gbot[:, 0] = alt / n
    inv = np.concatenate([gtop, gbot], axis=1)
    return fwd.astype(np.float32), inv.astype(np.float32)


def _filter_feats(l):
    t = np.linspace(0.0, 1.0, l)[:, None]
    w = (2.0 * np.pi / l) * np.arange(l, dtype=np.float64)[:, None]
    f = np.linspace(1e-4, HY_BANDS - 1, HY_BANDS)[None, :]
    feats = np.concatenate([t, np.cos(f * w), -np.sin(f * w)], axis=-1)
    out = np.zeros((l, EMB_PAD), np.float32)
    out[:, :HY_EMB] = feats
    return out[0::2], out[1::2]


def _shift_twiddles(l):
    theta = 2.0 * np.pi * np.arange(l // 2, dtype=np.float64) / l
    return np.stack([np.cos(theta), np.sin(theta)], axis=1).astype(np.float32)


def _head_expand():
    e = np.zeros((LANES, N_DIR * HP), np.float32)
    for j in range(N_DIR * SSD_HEADS):
        e[j, j * SSD_HEAD_DIM:(j + 1) * SSD_HEAD_DIM] = 1.0
    return e


def kernel(x_prompt, x_sample, state_ssd, c, c_ctx, w_ada, b_ada, norm_ffn1, ffn1_w_gate, ffn1_w_up, ffn1_w_down, norm_mix, w_in, w_out, ssd_conv_w, ssd_conv_b, ssd_dt_bias, ssd_a_log, ssd_d, ssd_norm_w, hy_conv_w, hy_conv_b, hy_w1, hy_b1, hy_freq, hy_w2, hy_b2, hy_w3, hy_skip, norm_ffn2, ffn2_w_gate, ffn2_w_up, ffn2_w_down, norm_final):
    assert w_ada.shape[0] == 1, "single layer"
    n_ctx, l_ctx, _ = x_prompt.shape
    n_lat, l_lat, _ = x_sample.shape
    t_ctx = n_ctx * l_ctx

    cond = jnp.zeros((16, D_MODEL), F32).at[:n_lat].set(c).at[n_lat].set(c_ctx)
    mod = _ada(cond, w_ada, b_ada).reshape(16, N_MOD, D_MODEL)

    tr = lambda w: jnp.swapaxes(w, 1, 2)
    wg1, wu1, wd1, wg2, wu2, wd2 = _cast_pad_rows([tr(ffn1_w_gate), tr(ffn1_w_up), ffn1_w_down,
                                                   tr(ffn2_w_gate), tr(ffn2_w_up), ffn2_w_down])
    wi, wo = _cast_mix(tr(w_in), w_out)

    row = lambda v: v.reshape(1, -1)
    pad_dt = lambda v: jnp.pad(v.reshape(1, -1), ((0, 0), (0, DT_PAD - N_DIR * SSD_HEADS)))
    dtb, alog = pad_dt(ssd_dt_bias[0]), pad_dt(ssd_a_log[0])
    dexp = jnp.repeat(ssd_d[0], SSD_HEAD_DIM).reshape(1, SSD_W)
    e_mat = jnp.asarray(_head_expand())
    w1p = jnp.pad(hy_w1[0], ((0, EMB_PAD - HY_EMB), (0, 0)))
    deltas = jnp.asarray(np.abs(np.linspace(HY_MIN_DECAY, HY_MAX_DECAY, HY_ORDER * HY_W))
                         .reshape(HY_ORDER, HY_W).astype(np.float32))

    x1, z, xbc, dt, hy = _ffn_in(x_prompt.reshape(t_ctx, D_MODEL), x_sample.reshape(n_lat * l_lat, D_MODEL), mod,
                                 n_lat, l_lat, row(norm_ffn1[0]), row(norm_mix[0]), wg1, wu1, wd1, wi)

    def mixers(row0, bsz, l, init, seg):
        fwd_np, inv_np = _dft_mats(l // 2)
        fmat = jnp.asarray(fwd_np).astype(BF16)
        gmat = jnp.asarray(inv_np).astype(BF16)
        feats_e, feats_o = _filter_feats(l)
        kf, kn = _filters(l, jnp.asarray(feats_e), jnp.asarray(feats_o), w1p, row(hy_b1[0]), row(hy_freq[0]),
                          hy_w2[0], row(hy_b2[0]), hy_w3[0], deltas, fmat, jnp.asarray(_shift_twiddles(l)))
        ys, fin = _ssd(z, xbc, dt, row0, bsz, l, init, ssd_conv_w[0], row(ssd_conv_b[0]), dtb, alog, dexp,
                       row(ssd_norm_w[0]), e_mat, seg=seg)
        yh = _hyena(hy, row0, bsz, l, HY_STEP_ROWS // l, hy_conv_w[0], row(hy_conv_b[0]), fmat, gmat, kf, kn,
                    hy_skip[0], seg=seg)
        return ys, yh, fin

    ys_ctx, yh_ctx, ctx_fin = mixers(0, n_ctx, l_ctx, None, l_ctx)
    lat_init = state_ssd[:, 0].reshape(n_lat, N_DIR, HP, SSD_STATE)
    ys_lat, yh_lat, _ = mixers(t_ctx, n_lat, l_lat, lat_init, GRID_W)

    y_ctx, y_lat = _out_ffn(x1, ys_ctx, ys_lat, yh_ctx, yh_lat, mod, n_lat, l_lat, row(norm_ffn2[0]),
                            row(norm_final), wo, wg2, wu2, wd2)
    new_state = ctx_fin.reshape(n_ctx, 1, N_DIR, SSD_HEADS, SSD_HEAD_DIM, SSD_STATE).astype(x_prompt.dtype)
    return (y_ctx.reshape(n_ctx, l_ctx, D_MODEL), y_lat.reshape(n_lat, l_lat, D_MODEL), new_state)
```

```python
import functools
import math

import numpy as np
import jax
import jax.numpy as jnp
from jax import lax
from jax.experimental import pallas as pl
from jax.experimental.pallas import tpu as pltpu

F32 = jnp.float32
BF16 = jnp.bfloat16

D_MODEL = 1024
GRID_W = 64
N_MOD = 9
RMS_EPS = 1e-6
FFN_DIM = 2752
SSD_W = 512
SSD_HEADS = 8
SSD_HEAD_DIM = 64
SSD_STATE = 64
SSD_GROUPS = 2
SSD_CHUNK = 128
SSD_XBC = SSD_W + 2 * SSD_GROUPS * SSD_STATE
N_DIR = 2
HY_W = 512
HY_ORDER = 2
HY_EMB = 33
HY_BANDS = (HY_EMB - 1) // 2
HY_HIDDEN = 64
HY_MIN_DECAY = math.log(1e-2) / 1.5
HY_MAX_DECAY = math.log(1e-2) / 0.3
IN_SPLITS = (SSD_W, SSD_W + SSD_XBC, SSD_W + SSD_XBC + N_DIR * SSD_HEADS)
IN_COLS = IN_SPLITS[-1] + (HY_ORDER + 1) * HY_W

LANES = 128
SUBLANES = 8
FFN_PAD = 2816
FFN_CHUNK = 256
DT_PAD = LANES
IN_ROWS = IN_COLS + DT_PAD - N_DIR * SSD_HEADS
EMB_PAD = LANES
TOKEN_TILE = 512
HY_CH_TILE = 256
VMEM_LIMIT = 56 * 1024 * 1024
NEG_BIG = -1e30
HP = SSD_HEADS * SSD_HEAD_DIM


def _silu(x):
    return x * jax.nn.sigmoid(x)


def _softplus(x):
    return jnp.maximum(x, 0.0) + jnp.log1p(jnp.exp(-jnp.abs(x)))


def _rms_mod(x, gain, shift, scale):
    ms = jnp.mean(x * x, axis=-1, keepdims=True)
    y = x * lax.rsqrt(ms + RMS_EPS) * gain
    return y * (1.0 + scale) + shift


def _dot(a, b):
    return jnp.dot(a, b, preferred_element_type=F32)


def _dot_nt(a, b):
    return lax.dot_general(a, b, (((1,), (1,)), ((), ())), preferred_element_type=F32)


def _split3(x):
    hi = x.astype(BF16)
    r = x - hi.astype(F32)
    mid = r.astype(BF16)
    lo = (r - mid.astype(F32)).astype(BF16)
    return hi, mid, lo


def _dot_exact_lhs(x, m01):
    hi, mid, lo = _split3(x)
    return _dot(hi, m01) + _dot(mid, m01) + _dot(lo, m01)


def _short_conv(x, w, b, first, last, l):
    prev = jnp.where(first, 0.0, pltpu.roll(x, 1, 0))
    nxt = jnp.where(last, 0.0, pltpu.roll(x, l - 1, 0))
    return prev * w[0:1, :] + x * w[1:2, :] + nxt * w[2:3, :] + b


def _seg_edges(l, seg):
    pos = lax.broadcasted_iota(jnp.int32, (l, 1), 0) & (seg - 1)
    return pos == 0, pos == seg - 1


def _swiglu_acc(h, wg_ref, wu_ref, wd_ref):
    acc = None
    for k in range(FFN_PAD // FFN_CHUNK):
        cols = slice(k * FFN_CHUNK, (k + 1) * FFN_CHUNK)
        g = _dot_nt(h, wg_ref[cols, :])
        u = _dot_nt(h, wu_ref[cols, :])
        a = (_silu(g) * u).astype(BF16)
        part = _dot(a, wd_ref[cols, :])
        acc = part if acc is None else acc + part
    return acc


def _params(*semantics):
    return pltpu.CompilerParams(dimension_semantics=semantics, vmem_limit_bytes=VMEM_LIMIT)


def _cast_pad_rows_kernel(*refs):
    n = len(refs) // 2
    rows = refs[0].shape[0]
    row = pl.program_id(0) * rows + lax.broadcasted_iota(jnp.int32, (rows, 1), 0)
    for w_ref, o_ref in zip(refs[:n], refs[n:]):
        o_ref[...] = jnp.where(row < FFN_DIM, w_ref[...], 0.0).astype(BF16)


def _cast_pad_rows(ws):
    n = len(ws)
    return pl.pallas_call(
        _cast_pad_rows_kernel,
        grid=(FFN_PAD // FFN_CHUNK,),
        in_specs=[pl.BlockSpec((None, FFN_CHUNK, D_MODEL), lambda i: (0, i, 0))] * n,
        out_specs=[pl.BlockSpec((FFN_CHUNK, D_MODEL), lambda i: (i, 0))] * n,
        out_shape=[jax.ShapeDtypeStruct((FFN_PAD, D_MODEL), BF16)] * n,
        compiler_params=_params("arbitrary"),
        name="wcast_rows",
    )(*ws)


def _cast_mix_kernel(wi_ref, wib_ref):
    o1, o2, o3 = IN_SPLITS
    wib_ref[0:o3, :] = wi_ref[0:o3, :].astype(BF16)
    wib_ref[o3:o2 + DT_PAD, :] = jnp.zeros((o2 + DT_PAD - o3, D_MODEL), BF16)
    wib_ref[o2 + DT_PAD:, :] = wi_ref[o3:, :].astype(BF16)


def _cast_mix(w_in_t):
    return pl.pallas_call(
        _cast_mix_kernel,
        in_specs=[pl.BlockSpec((None, IN_COLS, D_MODEL), lambda: (0, 0, 0))],
        out_specs=pl.BlockSpec((IN_ROWS, D_MODEL), lambda: (0, 0)),
        out_shape=jax.ShapeDtypeStruct((IN_ROWS, D_MODEL), BF16),
        compiler_params=pltpu.CompilerParams(vmem_limit_bytes=VMEM_LIMIT),
        name="wcast_mix",
    )(w_in_t)


def _ada_kernel(cond_ref, w_ref, b_ref, o_ref):
    s = _silu(cond_ref[...]).astype(BF16)
    o_ref[...] = _dot(s, w_ref[...].astype(BF16)) + b_ref[...]


def _ada(cond, w_ada, b_ada):
    rows = cond.shape[0]
    n = w_ada.shape[-1]
    tn = D_MODEL
    return pl.pallas_call(
        _ada_kernel,
        grid=(n // tn,),
        in_specs=[
            pl.BlockSpec((rows, D_MODEL), lambda j: (0, 0)),
            pl.BlockSpec((None, D_MODEL, tn), lambda j: (0, 0, j)),
            pl.BlockSpec((1, tn), lambda j: (0, j)),
        ],
        out_specs=pl.BlockSpec((rows, tn), lambda j: (0, j)),
        out_shape=jax.ShapeDtypeStruct((rows, n), F32),
        compiler_params=_params("arbitrary"),
        name="ada",
    )(cond, w_ada, b_ada)


N_FILT_PLANES = 6


def _filter_kernel(fe_ref, fo_ref, w1_ref, b1_ref, fr_ref, w2_ref, b2_ref, w3_ref, dl_ref, f_ref, tw_ref,
                   kf_ref, kn_ref, *, m):
    freq = fr_ref[...]
    w1 = w1_ref[...].astype(BF16)
    w2 = w2_ref[...].astype(BF16)

    def hidden(feats):
        h = jnp.sin(freq * (_dot(feats.astype(BF16), w1) + b1_ref[...]))
        h = jnp.sin(freq * (_dot(h.astype(BF16), w2) + b2_ref[...]))
        return h.astype(BF16)

    feats_e = fe_ref[...]
    feats_o = fo_ref[...]
    hb_e = hidden(feats_e)
    hb_o = hidden(feats_o)
    t_e = feats_e[:, 0:1]
    t_o = feats_o[:, 0:1]
    row = lax.broadcasted_iota(jnp.int32, (m, 1), 0)
    sign = jnp.where((row & 1) == 0, 1.0, -1.0)
    cos_t = tw_ref[:, 0:1]
    sin_t = tw_ref[:, 1:2]
    f_top = f_ref[0:m, :]
    f_bot = f_ref[m:2 * m, :]

    def spectrum(k):
        kb = k.astype(BF16)
        return _dot(f_top, kb), jnp.where(row == 0, 0.0, _dot(f_bot, kb)), jnp.sum(k * sign, axis=0, keepdims=True)

    for i in range(HY_ORDER):
        dl = dl_ref[i:i + 1, :]
        c0 = (0 * HY_ORDER + i) * HY_W
        c1 = (1 * HY_ORDER + i) * HY_W
        w3f = w3_ref[:, c0:c0 + HY_W].astype(BF16)
        w3b = w3_ref[:, c1:c1 + HY_W].astype(BF16)
        win_e = jnp.exp(-t_e * dl)
        win_o = jnp.exp(-t_o * dl)
        k0e = _dot(hb_e, w3f) * win_e
        k0o = _dot(hb_o, w3f) * win_o
        k1e = jnp.where(row == 0, 0.0, _dot(hb_e, w3b) * win_e)
        k1o = _dot(hb_o, w3b) * win_o
        ker, _, ken = spectrum(k0e + k1e)
        _, kei, _ = spectrum(k0e - k1e)
        ar, ai, an = spectrum(k0o)
        br, bi, bn = spectrum(k1o)
        kf_ref[i, 0] = ker
        kf_ref[i, 1] = kei
        kf_ref[i, 2] = ar + cos_t * br + sin_t * bi
        kf_ref[i, 3] = ai + sin_t * br - cos_t * bi
        kf_ref[i, 4] = cos_t * ar + sin_t * ai + br
        kf_ref[i, 5] = cos_t * ai - sin_t * ar - bi
        kn_ref[i] = jnp.concatenate([ken, an - bn, bn - an, jnp.zeros((SUBLANES - 3, HY_W), F32)], axis=0)


def _filters(l, feats_e, feats_o, w1p, b1, freq, w2, b2, w3, deltas, fmat, twid):
    m = l // 2
    full = lambda *shape: pl.BlockSpec(shape, lambda: (0,) * len(shape))
    return pl.pallas_call(
        functools.partial(_filter_kernel, m=m),
        in_specs=[full(m, EMB_PAD), full(m, EMB_PAD), full(EMB_PAD, HY_HIDDEN), full(1, HY_HIDDEN),
                  full(1, HY_HIDDEN), full(HY_HIDDEN, HY_HIDDEN), full(1, HY_HIDDEN),
                  full(HY_HIDDEN, N_DIR * HY_ORDER * HY_W), full(HY_ORDER, HY_W), full(l, m), full(m, 2)],
        out_specs=[full(HY_ORDER, N_FILT_PLANES, m, HY_W), full(HY_ORDER, SUBLANES, HY_W)],
        out_shape=[jax.ShapeDtypeStruct((HY_ORDER, N_FILT_PLANES, m, HY_W), F32),
                   jax.ShapeDtypeStruct((HY_ORDER, SUBLANES, HY_W), F32)],
        compiler_params=pltpu.CompilerParams(vmem_limit_bytes=VMEM_LIMIT),
        name=f"filt{l}",
    )(feats_e, feats_o, w1p, b1, freq, w2, b2, w3, deltas, fmat, twid)


def _const_spec(shape):
    return pl.BlockSpec(shape, lambda i: (0,) * len(shape), pipeline_mode=pl.Buffered(1))


def _group_specs(ctx_tiles, width):
    tm = TOKEN_TILE
    return [pl.BlockSpec((tm, width), lambda i: (jnp.minimum(i, ctx_tiles - 1), 0)),
            pl.BlockSpec((tm, width), lambda i: (jnp.maximum(i - ctx_tiles, 0), 0))]


def _mod_spec(ctx_tiles, n_lat, tiles_per_seq):
    return pl.BlockSpec((None, N_MOD, D_MODEL),
                        lambda i: (jnp.where(i < ctx_tiles, n_lat, (i - ctx_tiles) // tiles_per_seq), 0, 0))


NEXT_CAST_ROWS = 128
NEXT_CAST_ROWS_OUT = 64


def _ffn_in_kernel(xc_ref, xl_ref, mod_ref, n1_ref, nm_ref, wg_ref, wu_ref, wd_ref, wi_ref,
                   ng_ref, nu_ref, nd_ref, no_ref,
                   x1_ref, z_ref, xbc_ref, dt_ref, hy_ref, ngb_ref, nub_ref, ndb_ref, nob_ref, *, ctx_tiles):
    blk = jnp.minimum(pl.program_id(0), FFN_PAD // NEXT_CAST_ROWS - 1)
    wrow = blk * NEXT_CAST_ROWS + lax.broadcasted_iota(jnp.int32, (NEXT_CAST_ROWS, 1), 0)
    for src, dst in ((ng_ref, ngb_ref), (nu_ref, nub_ref), (nd_ref, ndb_ref)):
        dst[...] = jnp.where(wrow < FFN_DIM, src[...], 0.0).astype(BF16)
    nob_ref[...] = no_ref[...].astype(BF16)

    x = jnp.where(pl.program_id(0) < ctx_tiles, xc_ref[...], xl_ref[...])
    mod = mod_ref[...]
    h = _rms_mod(x, n1_ref[...], mod[0:1, :], mod[1:2, :]).astype(BF16)
    x1 = x + (0.5 * mod[2:3, :]) * _swiglu_acc(h, wg_ref, wu_ref, wd_ref)
    x1_ref[...] = x1
    h2 = _rms_mod(x1, nm_ref[...], mod[3:4, :], mod[4:5, :]).astype(BF16)
    o1, o2, _ = IN_SPLITS
    z_ref[...] = _dot_nt(h2, wi_ref[0:o1, :])
    xbc_ref[...] = _dot_nt(h2, wi_ref[o1:o2, :])
    dt_ref[...] = _dot_nt(h2, wi_ref[o2:o2 + DT_PAD, :])
    hy = _dot_nt(h2, wi_ref[o2 + DT_PAD:, :])
    for k in range(hy_ref.shape[0]):
        hy_ref[k] = hy[:, k * LANES:(k + 1) * LANES]


def _ffn_in(x_ctx, x_lat, mod, n_lat, l_lat, n1, nm, wg, wu, wd, wi, next_ffn, next_out):
    tm = TOKEN_TILE
    ctx_tiles = x_ctx.shape[0] // tm
    tokens = x_ctx.shape[0] + x_lat.shape[0]
    steps = tokens // tm
    ffn_blocks = FFN_PAD // NEXT_CAST_ROWS
    out_blocks = D_MODEL // NEXT_CAST_ROWS_OUT
    assert steps >= ffn_blocks and steps >= out_blocks
    row_spec = lambda n: pl.BlockSpec((tm, n), lambda i: (i, 0))
    widths = (D_MODEL, SSD_W, SSD_XBC, DT_PAD)
    hy_tiles = (HY_ORDER + 1) * HY_W // LANES
    outs = pl.pallas_call(
        functools.partial(_ffn_in_kernel, ctx_tiles=ctx_tiles),
        grid=(steps,),
        in_specs=_group_specs(ctx_tiles, D_MODEL) + [
            _mod_spec(ctx_tiles, n_lat, l_lat // tm),
            _const_spec((1, D_MODEL)), _const_spec((1, D_MODEL)),
            _const_spec(wg.shape), _const_spec(wu.shape), _const_spec(wd.shape), _const_spec(wi.shape)]
        + [pl.BlockSpec((None, NEXT_CAST_ROWS, D_MODEL), lambda i: (0, jnp.minimum(i, ffn_blocks - 1), 0))] * 3
        + [pl.BlockSpec((None, NEXT_CAST_ROWS_OUT, D_MODEL), lambda i: (0, jnp.minimum(i, out_blocks - 1), 0))],
        out_specs=[row_spec(n) for n in widths] + [pl.BlockSpec((hy_tiles, tm, LANES), lambda i: (0, i, 0))]
        + [pl.BlockSpec((NEXT_CAST_ROWS, D_MODEL), lambda i: (jnp.minimum(i, ffn_blocks - 1), 0))] * 3
        + [pl.BlockSpec((NEXT_CAST_ROWS_OUT, D_MODEL), lambda i: (jnp.minimum(i, out_blocks - 1), 0))],
        out_shape=[jax.ShapeDtypeStruct((tokens, n), F32) for n in widths]
        + [jax.ShapeDtypeStruct((hy_tiles, tokens, LANES), F32)]
        + [jax.ShapeDtypeStruct((FFN_PAD, D_MODEL), BF16)] * 3 + [jax.ShapeDtypeStruct((D_MODEL, D_MODEL), BF16)],
        compiler_params=_params("arbitrary"),
        name="ffn_in",
    )(x_ctx, x_lat, mod, n1, nm, wg, wu, wd, wi, *next_ffn, next_out)
    return outs[:5], outs[5:8], outs[8]


def _out_ffn_kernel(x1_ref, ysc_ref, ysl_ref, yhc_ref, yhl_ref, mod_ref, n3_ref, nf_ref,
                    wo_ref, wg_ref, wu_ref, wd_ref, oc_ref, ol_ref, *, ctx_tiles):
    is_ctx = pl.program_id(0) < ctx_tiles
    mod = mod_ref[...]
    ys = jnp.where(is_ctx, ysc_ref[...], ysl_ref[...])
    yh = [jnp.where(is_ctx, yhc_ref[k], yhl_ref[k]) for k in range(yhc_ref.shape[0])]
    y = jnp.concatenate([ys] + yh, axis=1).astype(BF16)
    x2 = x1_ref[...] + mod[5:6, :] * _dot(y, wo_ref[...])
    h = _rms_mod(x2, n3_ref[...], mod[6:7, :], mod[7:8, :]).astype(BF16)
    x3 = x2 + (0.5 * mod[8:9, :]) * _swiglu_acc(h, wg_ref, wu_ref, wd_ref)
    ms = jnp.mean(x3 * x3, axis=-1, keepdims=True)
    out = x3 * lax.rsqrt(ms + RMS_EPS) * nf_ref[...]

    @pl.when(is_ctx)
    def _():
        oc_ref[...] = out

    @pl.when(jnp.logical_not(is_ctx))
    def _():
        ol_ref[...] = out


def _out_ffn(x1, ys_ctx, ys_lat, yh_ctx, yh_lat, mod, n_lat, l_lat, n3, nf, wo, wg, wu, wd):
    tm = TOKEN_TILE
    ctx_tiles = ys_ctx.shape[0] // tm
    tokens = x1.shape[0]
    return pl.pallas_call(
        functools.partial(_out_ffn_kernel, ctx_tiles=ctx_tiles),
        grid=(tokens // tm,),
        in_specs=[pl.BlockSpec((tm, D_MODEL), lambda i: (i, 0))]
        + _group_specs(ctx_tiles, SSD_W) + [
            pl.BlockSpec((HY_W // LANES, tm, LANES), lambda i: (0, jnp.minimum(i, ctx_tiles - 1), 0)),
            pl.BlockSpec((HY_W // LANES, tm, LANES), lambda i: (0, jnp.maximum(i - ctx_tiles, 0), 0)),
            _mod_spec(ctx_tiles, n_lat, l_lat // tm),
            _const_spec((1, D_MODEL)), _const_spec((1, D_MODEL)),
            _const_spec(wo.shape), _const_spec(wg.shape), _const_spec(wu.shape), _const_spec(wd.shape)],
        out_specs=_group_specs(ctx_tiles, D_MODEL),
        out_shape=[jax.ShapeDtypeStruct((ys_ctx.shape[0], D_MODEL), F32),
                   jax.ShapeDtypeStruct((ys_lat.shape[0], D_MODEL), F32)],
        compiler_params=_params("arbitrary"),
        name="out_ffn",
    )(x1, ys_ctx, ys_lat, yh_ctx, yh_lat, mod, n3, nf, wo, wg, wu, wd)


def _ssd_kernel(z_ref, xbc_ref, dt_ref, init_ref, cw_ref, cb_ref, dtb_ref, alog_ref, dexp_ref, nw_ref, e_ref,
                y_ref, fin_ref,
                xs_s, xd_s, b_s, c_s, a_s, ec_s, dst_s, et_s,
                *, l, seg, zero_init):
    q = SSD_CHUNK
    nc = l // q
    n_seq = y_ref.shape[0] // l
    cr = max(seg, q)
    first, last = _seg_edges(cr, seg)
    e_mat = e_ref[...].astype(BF16)
    a_coef = -jnp.exp(alog_ref[...])

    for r0 in range(0, n_seq * l, cr):
        rows = slice(r0, r0 + cr)
        u = _silu(_short_conv(xbc_ref[rows, :], cw_ref[...], cb_ref[...], first, last, cr))
        xs = u[:, :SSD_W]
        xs_s[rows, :] = xs
        b_s[rows, :] = u[:, SSD_W:SSD_W + LANES]
        c_s[rows, :] = u[:, SSD_W + LANES:]
        dt = _softplus(dt_ref[rows, :] + dtb_ref[...])
        a_s[rows, :] = dt * a_coef
        xd_s[rows, :] = jnp.concatenate([xs, xs], axis=1) * _dot(dt.astype(BF16), e_mat)

    row_g = lax.broadcasted_iota(jnp.int32, (LANES, N_DIR * HP), 0) // SSD_STATE
    lane_g = (lax.broadcasted_iota(jnp.int32, (LANES, N_DIR * HP), 1) % HP) // (HP // SSD_GROUPS)
    own_t = row_g == lane_g

    ii = lax.broadcasted_iota(jnp.int32, (q, q), 0)
    jj = lax.broadcasted_iota(jnp.int32, (q, q), 1)
    tri_incl = (jj <= ii).astype(BF16)
    lane = lax.broadcasted_iota(jnp.int32, (q, LANES), 1)
    low_half = lane < SSD_STATE
    is_fwd_col = lane < SSD_HEADS

    def local_pass(c):
        rows = slice(c * q, (c + 1) * q)
        a_c = a_s[rows, :]
        a1, a2, a3 = _split3(a_c)
        cum_f = _dot(tri_incl, a1) + _dot(tri_incl, a2) + _dot(tri_incl, a3)
        tot = cum_f[q - 1:q, :]
        cum = jnp.where(is_fwd_col, cum_f, tot - cum_f + a_c)
        cum_t = cum.T
        ec_s[rows, :] = _dot(jnp.exp(cum).astype(BF16), e_mat)
        et_s[c] = _dot_exact_lhs(jnp.broadcast_to(jnp.exp(tot), (SUBLANES, LANES)), e_mat)
        xd = xd_s[rows, :]
        w = (xd * _dot(jnp.exp(tot - cum).astype(BF16), e_mat)).astype(BF16)
        bc = b_s[rows, :]
        bcb = bc.astype(BF16)
        dst_s[c] = jnp.where(own_t, _dot(bc.T.astype(BF16), w), 0.0)

        cc = c_s[rows, :]
        g_mats = [_dot_nt(jnp.where(low_half, cc, 0.0).astype(BF16), bcb),
                  _dot_nt(jnp.where(low_half, 0.0, cc).astype(BF16), bcb)]
        y_parts = []
        for pair in range(SSD_HEADS // 2):
            g = pair // (SSD_HEADS // 2 // SSD_GROUPS)
            acc = None
            for d in range(N_DIR):
                keep = (ii >= jj) if d == 0 else (ii <= jj)
                s_mats = []
                for hh in (2 * pair, 2 * pair + 1):
                    col = d * SSD_HEADS + hh
                    diff = cum[:, col:col + 1] - cum_t[col:col + 1, :]
                    decay = jnp.exp(jnp.where(keep, diff, NEG_BIG))
                    s_mats.append((g_mats[g] * decay).astype(BF16))
                lhs = jnp.concatenate(s_mats, axis=1)
                xp = xd[:, d * HP + pair * LANES:d * HP + (pair + 1) * LANES]
                rhs = jnp.concatenate([jnp.where(low_half, xp, 0.0), jnp.where(low_half, 0.0, xp)],
                                      axis=0).astype(BF16)
                part = _dot(lhs, rhs)
                acc = part if acc is None else acc + part
            y_parts.append(acc)
        y_ref[rows, :] = jnp.concatenate(y_parts, axis=1)

    for c in range(n_seq * nc):
        local_pass(c)

    half = HP // SSD_GROUPS
    for s in range(n_seq):
        states = []
        for d in range(N_DIR):
            if zero_init:
                states.append(jnp.zeros((LANES, HP), F32))
            else:
                s0 = init_ref[s, d]
                states.append(jnp.where(own_t[:, :HP], jnp.concatenate([s0, s0], axis=1).T, 0.0))

        for k in range(nc):
            for d in range(N_DIR):
                c = s * nc + (k if d == 0 else nc - 1 - k)
                rows = slice(c * q, (c + 1) * q)
                lanes = slice(d * HP, (d + 1) * HP)
                y_off = _dot(c_s[rows, :].astype(BF16), states[d].astype(BF16)) * ec_s[rows, lanes]
                y_ref[rows, :] = y_ref[rows, :] + y_off
                states[d] = states[d] * et_s[c, 0:1, lanes] + dst_s[c, :, lanes]

        for d in range(N_DIR):
            st = states[d].T
            fin_ref[s, d, 0:half, :] = st[0:half, 0:SSD_STATE]
            fin_ref[s, d, half:HP, :] = st[half:HP, SSD_STATE:2 * SSD_STATE]

        for c in range(s * nc, (s + 1) * nc):
            rows = slice(c * q, (c + 1) * q)
            y = y_ref[rows, :] + xs_s[rows, :] * dexp_ref[...]
            y = y * _silu(z_ref[rows, :])
            ms = jnp.mean(y * y, axis=-1, keepdims=True)
            y_ref[rows, :] = y * lax.rsqrt(ms + RMS_EPS) * nw_ref[...]


SSD_STEP_ROWS = 1024


def _ssd(z, xbc, dt, row0, bsz, l, init, cw, cb, dtb, alog, dexp, nw, e_mat, *, seg):
    n_seq = SSD_STEP_ROWS // l
    rows = n_seq * l
    zero_init = init is None
    if zero_init:
        init = jnp.zeros((n_seq, N_DIR, HP, SSD_STATE), F32)
        init_map = lambda b: (0, 0, 0, 0)
    else:
        init_map = lambda b: (b, 0, 0, 0)
    blk0 = row0 // rows
    seq_spec = lambda n: pl.BlockSpec((rows, n), lambda b: (blk0 + b, 0))
    const = lambda *shape: pl.BlockSpec(shape, lambda b: (0,) * len(shape))
    nc = rows // SSD_CHUNK
    return pl.pallas_call(
        functools.partial(_ssd_kernel, l=l, seg=seg, zero_init=zero_init),
        grid=(bsz // n_seq,),
        in_specs=[seq_spec(SSD_W), seq_spec(SSD_XBC), seq_spec(DT_PAD),
                  pl.BlockSpec((n_seq, N_DIR, HP, SSD_STATE), init_map),
                  const(3, SSD_XBC), const(1, SSD_XBC), const(1, DT_PAD), const(1, DT_PAD),
                  const(1, SSD_W), const(1, SSD_W), const(LANES, N_DIR * HP)],
        out_specs=[pl.BlockSpec((rows, SSD_W), lambda b: (b, 0)),
                   pl.BlockSpec((n_seq, N_DIR, HP, SSD_STATE), lambda b: (b, 0, 0, 0))],
        out_shape=[jax.ShapeDtypeStruct((bsz * l, SSD_W), F32),
                   jax.ShapeDtypeStruct((bsz, N_DIR, HP, SSD_STATE), F32)],
        scratch_shapes=[pltpu.VMEM((rows, SSD_W), F32), pltpu.VMEM((rows, N_DIR * HP), F32),
                        pltpu.VMEM((rows, LANES), F32), pltpu.VMEM((rows, LANES), F32),
                        pltpu.VMEM((rows, DT_PAD), F32), pltpu.VMEM((rows, N_DIR * HP), F32),
                        pltpu.VMEM((nc, LANES, N_DIR * HP), F32), pltpu.VMEM((nc, SUBLANES, N_DIR * HP), F32)],
        compiler_params=_params("arbitrary"),
        name=f"ssd{l}",
    )(z, xbc, dt, init, cw, cb, dtb, alog, dexp, nw, e_mat)


HY_ROW_BLOCK = 16
HY_STEP_ROWS = 2048


def _hyena_kernel(v_ref, x1_ref, x2_ref, wv_ref, w1_ref, w2_ref, bv_ref, b1_ref, b2_ref,
                  f_ref, g_ref, kf_ref, kn_ref, skip_ref, o_ref, spec_s, prod_s, *, l, seg):
    m = l // 2
    tiles = o_ref.shape[0]
    n_seq = o_ref.shape[1] // l
    ct = tiles * LANES
    rb = HY_ROW_BLOCK
    first, last = _seg_edges(m, seg // 2)
    row0 = lax.broadcasted_iota(jnp.int32, (rb, 1), 0) == 0
    ev = slice(0, ct)
    od = slice(ct, 2 * ct)

    def conv_eo(x_ref, w_ref, b_ref, base):
        xe, xo = (jnp.concatenate([x_ref[k, pl.ds(base + p, m, stride=2), :] for k in range(tiles)], axis=1)
                  for p in range(2))
        w = w_ref[...]
        b = b_ref[...]
        xo_prev = jnp.where(first, 0.0, pltpu.roll(xo, 1, 0))
        xe_next = jnp.where(last, 0.0, pltpu.roll(xe, m - 1, 0))
        ce = xo_prev * w[0:1, :] + xe * w[1:2, :] + xo * w[2:3, :] + b
        co = xe * w[0:1, :] + xo * w[1:2, :] + xe_next * w[2:3, :] + b
        return jnp.concatenate([ce, co], axis=1)

    def pointwise(s, i, r0):
        re = slice(r0, r0 + rb)
        im = slice(m + r0, m + r0 + rb)
        er, orr = spec_s[s, re, ev], spec_s[s, re, od]
        ei, oi = spec_s[s, im, ev], spec_s[s, im, od]
        ker, kei, kor, koi, vr, vi = (kf_ref[i, p, re, :] for p in range(N_FILT_PLANES))
        if r0 == 0:
            e_n, o_n = ei[0:1, :], oi[0:1, :]
            ei = jnp.where(row0, 0.0, ei)
            oi = jnp.where(row0, 0.0, oi)
        pe_r = er * ker - ei * kei + orr * vr - oi * vi
        pe_i = er * kei + ei * ker + orr * vi + oi * vr
        po_r = er * kor - ei * koi + orr * ker - oi * kei
        po_i = er * koi + ei * kor + orr * kei + oi * ker
        if r0 == 0:
            kn = kn_ref[i]
            pe_i = jnp.where(row0, e_n * kn[0:1, :] + o_n * kn[2:3, :], pe_i)
            po_i = jnp.where(row0, e_n * kn[1:2, :] + o_n * kn[0:1, :], po_i)
        prod_s[s, re, ev] = pe_r.astype(BF16)
        prod_s[s, re, od] = po_r.astype(BF16)
        prod_s[s, im, ev] = pe_i.astype(BF16)
        prod_s[s, im, od] = po_i.astype(BF16)

    for s in range(n_seq):
        base = s * l
        zz = conv_eo(v_ref, wv_ref, bv_ref, base)
        gates = (conv_eo(x1_ref, w1_ref, b1_ref, base), conv_eo(x2_ref, w2_ref, b2_ref, base))
        for i in range(HY_ORDER):
            spec_s[s] = _dot(f_ref[...], zz.astype(BF16))
            for r0 in range(0, m, rb):
                pointwise(s, i, r0)
            conv = _dot(g_ref[...], prod_s[s])
            skip = skip_ref[i:i + 1, :]
            zz = gates[i] * (conv + zz * jnp.concatenate([skip, skip], axis=1))
        for k in range(tiles):
            o_ref[k, pl.ds(base, m, stride=2), :] = zz[:, k * LANES:(k + 1) * LANES]
            o_ref[k, pl.ds(base + 1, m, stride=2), :] = zz[:, ct + k * LANES:ct + (k + 1) * LANES]


def _hyena(hy, row0, bsz, l, n_seq, cw, cb, fmat, gmat, kf, kn, skip, *, seg):
    ct = HY_CH_TILE
    nct = HY_W // ct
    tiles = ct // LANES
    m = l // 2
    rows = n_seq * l
    blk0 = row0 // rows
    part = lambda p: pl.BlockSpec((tiles, rows, LANES), lambda j, b: (p * nct + j, blk0 + b, 0))
    wpart = lambda p: pl.BlockSpec((3, ct), lambda j, b: (0, p * nct + j))
    bpart = lambda p: pl.BlockSpec((1, ct), lambda j, b: (0, p * nct + j))
    return pl.pallas_call(
        functools.partial(_hyena_kernel, l=l, seg=seg),
        grid=(nct, bsz // n_seq),
        in_specs=[part(0), part(1), part(2), wpart(0), wpart(1), wpart(2), bpart(0), bpart(1), bpart(2),
                  pl.BlockSpec((l, m), lambda j, b: (0, 0)),
                  pl.BlockSpec((m, l), lambda j, b: (0, 0)),
                  pl.BlockSpec((HY_ORDER, N_FILT_PLANES, m, ct), lambda j, b: (0, 0, 0, j)),
                  pl.BlockSpec((HY_ORDER, SUBLANES, ct), lambda j, b: (0, 0, j)),
                  pl.BlockSpec((HY_ORDER, ct), lambda j, b: (0, j))],
        out_specs=pl.BlockSpec((tiles, rows, LANES), lambda j, b: (j, b, 0)),
        out_shape=jax.ShapeDtypeStruct((HY_W // LANES, bsz * l, LANES), F32),
        scratch_shapes=[pltpu.VMEM((n_seq, l, 2 * ct), F32), pltpu.VMEM((n_seq, l, 2 * ct), BF16)],
        compiler_params=_params("arbitrary", "arbitrary"),
        name=f"hyena{l}",
    )(hy, hy, hy, cw, cw, cw, cb, cb, cb, fmat, gmat, kf, kn, skip)


def _dft_mats(l):
    n = 2 * l
    f = np.arange(l, dtype=np.int64)[:, None]
    t = np.arange(l, dtype=np.int64)[None, :]
    ang = 2.0 * np.pi * ((f * t) % n).astype(np.float64) / n
    alt = np.where(np.arange(l) % 2 == 0, 1.0, -1.0)
    top = np.cos(ang)
    bot = -np.sin(ang)
    bot[0, :] = alt
    fwd = np.concatenate([top, bot], axis=0)
    wf = np.full((l,), 2.0)
    wf[0] = 1.0
    gtop = np.cos(ang).T * wf[None, :] / n
    gbot = -np.sin(ang).T * 2.0 / n
    gbot[:, 0] = alt / n
    inv = np.concatenate([gtop, gbot], axis=1)
    return fwd.astype(np.float32), inv.astype(np.float32)


def _filter_feats(l):
    t = np.linspace(0.0, 1.0, l)[:, None]
    w = (2.0 * np.pi / l) * np.arange(l, dtype=np.float64)[:, None]
    f = np.linspace(1e-4, HY_BANDS - 1, HY_BANDS)[None, :]
    feats = np.concatenate([t, np.cos(f * w), -np.sin(f * w)], axis=-1)
    out = np.zeros((l, EMB_PAD), np.float32)
    out[:, :HY_EMB] = feats
    return out[0::2], out[1::2]


def _shift_twiddles(l):
    theta = 2.0 * np.pi * np.arange(l // 2, dtype=np.float64) / l
    return np.stack([np.cos(theta), np.sin(theta)], axis=1).astype(np.float32)


def _head_expand():
    e = np.zeros((LANES, N_DIR * HP), np.float32)
    for j in range(N_DIR * SSD_HEADS):
        e[j, j * SSD_HEAD_DIM:(j + 1) * SSD_HEAD_DIM] = 1.0
    return e


def kernel(x_prompt, x_sample, state_ssd, c, c_ctx, w_ada, b_ada, norm_ffn1, ffn1_w_gate, ffn1_w_up, ffn1_w_down, norm_mix, w_in, w_out, ssd_conv_w, ssd_conv_b, ssd_dt_bias, ssd_a_log, ssd_d, ssd_norm_w, hy_conv_w, hy_conv_b, hy_w1, hy_b1, hy_freq, hy_w2, hy_b2, hy_w3, hy_skip, norm_ffn2, ffn2_w_gate, ffn2_w_up, ffn2_w_down, norm_final):
    assert w_ada.shape[0] == 1, "single layer"
    n_ctx, l_ctx, _ = x_prompt.shape
    n_lat, l_lat, _ = x_sample.shape
    t_ctx = n_ctx * l_ctx

    cond = jnp.zeros((16, D_MODEL), F32).at[:n_lat].set(c).at[n_lat].set(c_ctx)
    mod = _ada(cond, w_ada, b_ada).reshape(16, N_MOD, D_MODEL)

    tr = lambda w: jnp.swapaxes(w, 1, 2)
    wg1, wu1, wd1 = _cast_pad_rows([tr(ffn1_w_gate), tr(ffn1_w_up), ffn1_w_down])
    wi = _cast_mix(tr(w_in))

    row = lambda v: v.reshape(1, -1)
    pad_dt = lambda v: jnp.pad(v.reshape(1, -1), ((0, 0), (0, DT_PAD - N_DIR * SSD_HEADS)))
    dtb, alog = pad_dt(ssd_dt_bias[0]), pad_dt(ssd_a_log[0])
    dexp = jnp.repeat(ssd_d[0], SSD_HEAD_DIM).reshape(1, SSD_W)
    e_mat = jnp.asarray(_head_expand())
    w1p = jnp.pad(hy_w1[0], ((0, EMB_PAD - HY_EMB), (0, 0)))
    deltas = jnp.asarray(np.abs(np.linspace(HY_MIN_DECAY, HY_MAX_DECAY, HY_ORDER * HY_W))
                         .reshape(HY_ORDER, HY_W).astype(np.float32))

    (x1, z, xbc, dt, hy), (wg2, wu2, wd2), wo = _ffn_in(
        x_prompt.reshape(t_ctx, D_MODEL), x_sample.reshape(n_lat * l_lat, D_MODEL), mod, n_lat, l_lat,
        row(norm_ffn1[0]), row(norm_mix[0]), wg1, wu1, wd1, wi,
        [tr(ffn2_w_gate), tr(ffn2_w_up), ffn2_w_down], w_out)

    def mixers(row0, bsz, l, init, seg):
        fwd_np, inv_np = _dft_mats(l // 2)
        fmat = jnp.asarray(fwd_np).astype(BF16)
        gmat = jnp.asarray(inv_np).astype(BF16)
        feats_e, feats_o = _filter_feats(l)
        kf, kn = _filters(l, jnp.asarray(feats_e), jnp.asarray(feats_o), w1p, row(hy_b1[0]), row(hy_freq[0]),
                          hy_w2[0], row(hy_b2[0]), hy_w3[0], deltas, fmat, jnp.asarray(_shift_twiddles(l)))
        ys, fin = _ssd(z, xbc, dt, row0, bsz, l, init, ssd_conv_w[0], row(ssd_conv_b[0]), dtb, alog, dexp,
                       row(ssd_norm_w[0]), e_mat, seg=seg)
        yh = _hyena(hy, row0, bsz, l, HY_STEP_ROWS // l, hy_conv_w[0], row(hy_conv_b[0]), fmat, gmat, kf, kn,
                    hy_skip[0], seg=seg)
        return ys, yh, fin

    ys_ctx, yh_ctx, ctx_fin = mixers(0, n_ctx, l_ctx, None, l_ctx)
    lat_init = state_ssd[:, 0].reshape(n_lat, N_DIR, HP, SSD_STATE)
    ys_lat, yh_lat, _ = mixers(t_ctx, n_lat, l_lat, lat_init, GRID_W)

    y_ctx, y_lat = _out_ffn(x1, ys_ctx, ys_lat, yh_ctx, yh_lat, mod, n_lat, l_lat, row(norm_ffn2[0]),
                            row(norm_final), wo, wg2, wu2, wd2)
    new_state = ctx_fin.reshape(n_ctx, 1, N_DIR, SSD_HEADS, SSD_HEAD_DIM, SSD_STATE).astype(x_prompt.dtype)
    return (y_ctx.reshape(n_ctx, l_ctx, D_MODEL), y_lat.reshape(n_lat, l_lat, D_MODEL), new_state)
```

```python
import functools
import math

import numpy as np
import jax
import jax.numpy as jnp
from jax import lax
from jax.experimental import pallas as pl
from jax.experimental.pallas import tpu as pltpu

F32 = jnp.float32
BF16 = jnp.bfloat16

D_MODEL = 1024
GRID_W = 64
N_MOD = 9
RMS_EPS = 1e-6
FFN_DIM = 2752
SSD_W = 512
SSD_HEADS = 8
SSD_HEAD_DIM = 64
SSD_STATE = 64
SSD_GROUPS = 2
SSD_CHUNK = 128
SSD_XBC = SSD_W + 2 * SSD_GROUPS * SSD_STATE
N_DIR = 2
HY_W = 512
HY_ORDER = 2
HY_EMB = 33
HY_BANDS = (HY_EMB - 1) // 2
HY_HIDDEN = 64
HY_MIN_DECAY = math.log(1e-2) / 1.5
HY_MAX_DECAY = math.log(1e-2) / 0.3
IN_SPLITS = (SSD_W, SSD_W + SSD_XBC, SSD_W + SSD_XBC + N_DIR * SSD_HEADS)
IN_COLS = IN_SPLITS[-1] + (HY_ORDER + 1) * HY_W

LANES = 128
SUBLANES = 8
FFN_PAD = 2816
FFN_CHUNK = 256
N_DT = N_DIR * SSD_HEADS
DT_PAD = LANES
IN_ROWS = IN_COLS + DT_PAD - N_DIR * SSD_HEADS
EMB_PAD = LANES
TOKEN_TILE = 512
HY_CH_TILE = 256
VMEM_LIMIT = 56 * 1024 * 1024
NEG_BIG = -1e30
HP = SSD_HEADS * SSD_HEAD_DIM


def _silu(x):
    return x * jax.nn.sigmoid(x)


def _softplus(x):
    return jnp.maximum(x, 0.0) + jnp.log1p(jnp.exp(-jnp.abs(x)))


def _rms_mod(x, gain, shift, scale):
    ms = jnp.mean(x * x, axis=-1, keepdims=True)
    y = x * lax.rsqrt(ms + RMS_EPS) * gain
    return y * (1.0 + scale) + shift


def _dot(a, b):
    return jnp.dot(a, b, preferred_element_type=F32)


def _dot_nt(a, b):
    return lax.dot_general(a, b, (((1,), (1,)), ((), ())), preferred_element_type=F32)


def _split3(x):
    hi = x.astype(BF16)
    r = x - hi.astype(F32)
    mid = r.astype(BF16)
    lo = (r - mid.astype(F32)).astype(BF16)
    return hi, mid, lo


def _dot_exact_lhs(x, m01):
    hi, mid, lo = _split3(x)
    return _dot(hi, m01) + _dot(mid, m01) + _dot(lo, m01)


def _seg_edges(l, seg):
    pos = lax.broadcasted_iota(jnp.int32, (l, 1), 0) & (seg - 1)
    return pos == 0, pos == seg - 1


def _swiglu_acc(h, wg_ref, wu_ref, wd_ref):
    acc = None
    for k in range(FFN_PAD // FFN_CHUNK):
        cols = slice(k * FFN_CHUNK, (k + 1) * FFN_CHUNK)
        g = _dot_nt(h, wg_ref[cols, :])
        u = _dot_nt(h, wu_ref[cols, :])
        a = (_silu(g) * u).astype(BF16)
        part = _dot(a, wd_ref[cols, :])
        acc = part if acc is None else acc + part
    return acc


def _params(*semantics):
    return pltpu.CompilerParams(dimension_semantics=semantics, vmem_limit_bytes=VMEM_LIMIT)


def _cast_pad_rows_kernel(*refs):
    n = len(refs) // 2
    rows = refs[0].shape[0]
    row = pl.program_id(0) * rows + lax.broadcasted_iota(jnp.int32, (rows, 1), 0)
    for w_ref, o_ref in zip(refs[:n], refs[n:]):
        o_ref[...] = jnp.where(row < FFN_DIM, w_ref[...], 0.0).astype(BF16)


def _cast_pad_rows(ws):
    n = len(ws)
    return pl.pallas_call(
        _cast_pad_rows_kernel,
        grid=(FFN_PAD // FFN_CHUNK,),
        in_specs=[pl.BlockSpec((None, FFN_CHUNK, D_MODEL), lambda i: (0, i, 0))] * n,
        out_specs=[pl.BlockSpec((FFN_CHUNK, D_MODEL), lambda i: (i, 0))] * n,
        out_shape=[jax.ShapeDtypeStruct((FFN_PAD, D_MODEL), BF16)] * n,
        compiler_params=_params("arbitrary"),
        name="wcast_rows",
    )(*ws)


def _cast_mix_kernel(wi_ref, wib_ref):
    o1, o2, o3 = IN_SPLITS
    wib_ref[0:o3, :] = wi_ref[0:o3, :].astype(BF16)
    wib_ref[o3:o2 + DT_PAD, :] = jnp.zeros((o2 + DT_PAD - o3, D_MODEL), BF16)
    wib_ref[o2 + DT_PAD:, :] = wi_ref[o3:, :].astype(BF16)


def _cast_mix(w_in_t):
    return pl.pallas_call(
        _cast_mix_kernel,
        in_specs=[pl.BlockSpec((None, IN_COLS, D_MODEL), lambda: (0, 0, 0))],
        out_specs=pl.BlockSpec((IN_ROWS, D_MODEL), lambda: (0, 0)),
        out_shape=jax.ShapeDtypeStruct((IN_ROWS, D_MODEL), BF16),
        compiler_params=pltpu.CompilerParams(vmem_limit_bytes=VMEM_LIMIT),
        name="wcast_mix",
    )(w_in_t)


def _ada_kernel(cond_ref, w_ref, b_ref, o_ref):
    s = _silu(cond_ref[...]).astype(BF16)
    o_ref[...] = _dot(s, w_ref[...].astype(BF16)) + b_ref[...]


def _ada(cond, w_ada, b_ada):
    rows = cond.shape[0]
    n = w_ada.shape[-1]
    tn = D_MODEL
    return pl.pallas_call(
        _ada_kernel,
        grid=(n // tn,),
        in_specs=[
            pl.BlockSpec((rows, D_MODEL), lambda j: (0, 0)),
            pl.BlockSpec((None, D_MODEL, tn), lambda j: (0, 0, j)),
            pl.BlockSpec((1, tn), lambda j: (0, j)),
        ],
        out_specs=pl.BlockSpec((rows, tn), lambda j: (0, j)),
        out_shape=jax.ShapeDtypeStruct((rows, n), F32),
        compiler_params=_params("arbitrary"),
        name="ada",
    )(cond, w_ada, b_ada)


N_FILT_PLANES = 6


def _filter_kernel(fe_ref, fo_ref, w1_ref, b1_ref, fr_ref, w2_ref, b2_ref, w3_ref, dl_ref, f_ref, tw_ref,
                   kf_ref, kn_ref, *, m):
    freq = fr_ref[...]
    w1 = w1_ref[...].astype(BF16)
    w2 = w2_ref[...].astype(BF16)

    def hidden(feats):
        h = jnp.sin(freq * (_dot(feats.astype(BF16), w1) + b1_ref[...]))
        h = jnp.sin(freq * (_dot(h.astype(BF16), w2) + b2_ref[...]))
        return h.astype(BF16)

    feats_e = fe_ref[...]
    feats_o = fo_ref[...]
    hb_e = hidden(feats_e)
    hb_o = hidden(feats_o)
    t_e = feats_e[:, 0:1]
    t_o = feats_o[:, 0:1]
    row = lax.broadcasted_iota(jnp.int32, (m, 1), 0)
    sign = jnp.where((row & 1) == 0, 1.0, -1.0)
    cos_t = tw_ref[:, 0:1]
    sin_t = tw_ref[:, 1:2]
    f_top = f_ref[0:m, :]
    f_bot = f_ref[m:2 * m, :]

    def spectrum(k):
        kb = k.astype(BF16)
        return _dot(f_top, kb), jnp.where(row == 0, 0.0, _dot(f_bot, kb)), jnp.sum(k * sign, axis=0, keepdims=True)

    for i in range(HY_ORDER):
        dl = dl_ref[i:i + 1, :]
        c0 = (0 * HY_ORDER + i) * HY_W
        c1 = (1 * HY_ORDER + i) * HY_W
        w3f = w3_ref[:, c0:c0 + HY_W].astype(BF16)
        w3b = w3_ref[:, c1:c1 + HY_W].astype(BF16)
        win_e = jnp.exp(-t_e * dl)
        win_o = jnp.exp(-t_o * dl)
        k0e = _dot(hb_e, w3f) * win_e
        k0o = _dot(hb_o, w3f) * win_o
        k1e = jnp.where(row == 0, 0.0, _dot(hb_e, w3b) * win_e)
        k1o = _dot(hb_o, w3b) * win_o
        ker, _, ken = spectrum(k0e + k1e)
        _, kei, _ = spectrum(k0e - k1e)
        ar, ai, an = spectrum(k0o)
        br, bi, bn = spectrum(k1o)
        kf_ref[i, 0] = ker
        kf_ref[i, 1] = kei
        kf_ref[i, 2] = ar + cos_t * br + sin_t * bi
        kf_ref[i, 3] = ai + sin_t * br - cos_t * bi
        kf_ref[i, 4] = cos_t * ar + sin_t * ai + br
        kf_ref[i, 5] = cos_t * ai - sin_t * ar - bi
        kn_ref[i] = jnp.concatenate([ken, an - bn, bn - an, jnp.zeros((SUBLANES - 3, HY_W), F32)], axis=0)


def _filters(l, feats_e, feats_o, w1p, b1, freq, w2, b2, w3, deltas, fmat, twid):
    m = l // 2
    full = lambda *shape: pl.BlockSpec(shape, lambda: (0,) * len(shape))
    return pl.pallas_call(
        functools.partial(_filter_kernel, m=m),
        in_specs=[full(m, EMB_PAD), full(m, EMB_PAD), full(EMB_PAD, HY_HIDDEN), full(1, HY_HIDDEN),
                  full(1, HY_HIDDEN), full(HY_HIDDEN, HY_HIDDEN), full(1, HY_HIDDEN),
                  full(HY_HIDDEN, N_DIR * HY_ORDER * HY_W), full(HY_ORDER, HY_W), full(l, m), full(m, 2)],
        out_specs=[full(HY_ORDER, N_FILT_PLANES, m, HY_W), full(HY_ORDER, SUBLANES, HY_W)],
        out_shape=[jax.ShapeDtypeStruct((HY_ORDER, N_FILT_PLANES, m, HY_W), F32),
                   jax.ShapeDtypeStruct((HY_ORDER, SUBLANES, HY_W), F32)],
        compiler_params=pltpu.CompilerParams(vmem_limit_bytes=VMEM_LIMIT),
        name=f"filt{l}",
    )(feats_e, feats_o, w1p, b1, freq, w2, b2, w3, deltas, fmat, twid)


def _const_spec(shape):
    return pl.BlockSpec(shape, lambda i: (0,) * len(shape), pipeline_mode=pl.Buffered(1))


def _group_specs(ctx_tiles, width):
    tm = TOKEN_TILE
    return [pl.BlockSpec((tm, width), lambda i: (jnp.minimum(i, ctx_tiles - 1), 0)),
            pl.BlockSpec((tm, width), lambda i: (jnp.maximum(i - ctx_tiles, 0), 0))]


def _mod_spec(ctx_tiles, n_lat, tiles_per_seq):
    return pl.BlockSpec((None, N_MOD, D_MODEL),
                        lambda i: (jnp.where(i < ctx_tiles, n_lat, (i - ctx_tiles) // tiles_per_seq), 0, 0))


NEXT_CAST_ROWS = 128
NEXT_CAST_ROWS_OUT = 64


def _ffn_in_kernel(xc_ref, xl_ref, mod_ref, n1_ref, nm_ref, wg_ref, wu_ref, wd_ref, wi_ref,
                   ng_ref, nu_ref, nd_ref, no_ref,
                   x1_ref, z_ref, xbc_ref, dt_ref, hy_ref, ngb_ref, nub_ref, ndb_ref, nob_ref, *, ctx_tiles):
    blk = jnp.minimum(pl.program_id(0), FFN_PAD // NEXT_CAST_ROWS - 1)
    wrow = blk * NEXT_CAST_ROWS + lax.broadcasted_iota(jnp.int32, (NEXT_CAST_ROWS, 1), 0)
    for src, dst in ((ng_ref, ngb_ref), (nu_ref, nub_ref), (nd_ref, ndb_ref)):
        dst[...] = jnp.where(wrow < FFN_DIM, src[...], 0.0).astype(BF16)
    nob_ref[...] = no_ref[...].astype(BF16)

    x = jnp.where(pl.program_id(0) < ctx_tiles, xc_ref[...], xl_ref[...])
    mod = mod_ref[...]
    h = _rms_mod(x, n1_ref[...], mod[0:1, :], mod[1:2, :]).astype(BF16)
    x1 = x + (0.5 * mod[2:3, :]) * _swiglu_acc(h, wg_ref, wu_ref, wd_ref)
    x1_ref[...] = x1
    h2 = _rms_mod(x1, nm_ref[...], mod[3:4, :], mod[4:5, :]).astype(BF16)
    o1, o2, _ = IN_SPLITS
    z_ref[...] = _dot_nt(h2, wi_ref[0:o1, :])
    xbc_ref[...] = _dot_nt(h2, wi_ref[o1:o2, :])
    dt_ref[...] = _dot_nt(wi_ref[o2:o2 + N_DT, :], h2)
    hy = _dot_nt(h2, wi_ref[o2 + DT_PAD:, :])
    for k in range(hy_ref.shape[0]):
        hy_ref[k] = hy[:, k * LANES:(k + 1) * LANES]


def _ffn_in(x_ctx, x_lat, mod, n_lat, l_lat, n1, nm, wg, wu, wd, wi, next_ffn, next_out):
    tm = TOKEN_TILE
    ctx_tiles = x_ctx.shape[0] // tm
    tokens = x_ctx.shape[0] + x_lat.shape[0]
    steps = tokens // tm
    ffn_blocks = FFN_PAD // NEXT_CAST_ROWS
    out_blocks = D_MODEL // NEXT_CAST_ROWS_OUT
    assert steps >= ffn_blocks and steps >= out_blocks
    row_spec = lambda n: pl.BlockSpec((tm, n), lambda i: (i, 0))
    widths = (D_MODEL, SSD_W, SSD_XBC)
    hy_tiles = (HY_ORDER + 1) * HY_W // LANES
    dt_spec = pl.BlockSpec((N_DT, tm), lambda i: (0, i))
    outs = pl.pallas_call(
        functools.partial(_ffn_in_kernel, ctx_tiles=ctx_tiles),
        grid=(steps,),
        in_specs=_group_specs(ctx_tiles, D_MODEL) + [
            _mod_spec(ctx_tiles, n_lat, l_lat // tm),
            _const_spec((1, D_MODEL)), _const_spec((1, D_MODEL)),
            _const_spec(wg.shape), _const_spec(wu.shape), _const_spec(wd.shape), _const_spec(wi.shape)]
        + [pl.BlockSpec((None, NEXT_CAST_ROWS, D_MODEL), lambda i: (0, jnp.minimum(i, ffn_blocks - 1), 0))] * 3
        + [pl.BlockSpec((None, NEXT_CAST_ROWS_OUT, D_MODEL), lambda i: (0, jnp.minimum(i, out_blocks - 1), 0))],
        out_specs=[row_spec(n) for n in widths] + [dt_spec, pl.BlockSpec((hy_tiles, tm, LANES), lambda i: (0, i, 0))]
        + [pl.BlockSpec((NEXT_CAST_ROWS, D_MODEL), lambda i: (jnp.minimum(i, ffn_blocks - 1), 0))] * 3
        + [pl.BlockSpec((NEXT_CAST_ROWS_OUT, D_MODEL), lambda i: (jnp.minimum(i, out_blocks - 1), 0))],
        out_shape=[jax.ShapeDtypeStruct((tokens, n), F32) for n in widths]
        + [jax.ShapeDtypeStruct((N_DT, tokens), F32), jax.ShapeDtypeStruct((hy_tiles, tokens, LANES), F32)]
        + [jax.ShapeDtypeStruct((FFN_PAD, D_MODEL), BF16)] * 3 + [jax.ShapeDtypeStruct((D_MODEL, D_MODEL), BF16)],
        compiler_params=_params("arbitrary"),
        name="ffn_in",
    )(x_ctx, x_lat, mod, n1, nm, wg, wu, wd, wi, *next_ffn, next_out)
    return outs[:5], outs[5:8], outs[8]


def _out_ffn_kernel(x1_ref, ysc_ref, ysl_ref, yhc_ref, yhl_ref, mod_ref, n3_ref, nf_ref,
                    wo_ref, wg_ref, wu_ref, wd_ref, oc_ref, ol_ref, *, ctx_tiles):
    is_ctx = pl.program_id(0) < ctx_tiles
    mod = mod_ref[...]
    ys = jnp.where(is_ctx, ysc_ref[...], ysl_ref[...])
    yh = [jnp.where(is_ctx, yhc_ref[k], yhl_ref[k]) for k in range(yhc_ref.shape[0])]
    y = jnp.concatenate([ys] + yh, axis=1).astype(BF16)
    x2 = x1_ref[...] + mod[5:6, :] * _dot(y, wo_ref[...])
    h = _rms_mod(x2, n3_ref[...], mod[6:7, :], mod[7:8, :]).astype(BF16)
    x3 = x2 + (0.5 * mod[8:9, :]) * _swiglu_acc(h, wg_ref, wu_ref, wd_ref)
    ms = jnp.mean(x3 * x3, axis=-1, keepdims=True)
    out = x3 * lax.rsqrt(ms + RMS_EPS) * nf_ref[...]

    @pl.when(is_ctx)
    def _():
        oc_ref[...] = out

    @pl.when(jnp.logical_not(is_ctx))
    def _():
        ol_ref[...] = out


def _out_ffn(x1, ys_ctx, ys_lat, yh_ctx, yh_lat, mod, n_lat, l_lat, n3, nf, wo, wg, wu, wd):
    tm = TOKEN_TILE
    ctx_tiles = ys_ctx.shape[0] // tm
    tokens = x1.shape[0]
    return pl.pallas_call(
        functools.partial(_out_ffn_kernel, ctx_tiles=ctx_tiles),
        grid=(tokens // tm,),
        in_specs=[pl.BlockSpec((tm, D_MODEL), lambda i: (i, 0))]
        + _group_specs(ctx_tiles, SSD_W) + [
            pl.BlockSpec((HY_W // LANES, tm, LANES), lambda i: (0, jnp.minimum(i, ctx_tiles - 1), 0)),
            pl.BlockSpec((HY_W // LANES, tm, LANES), lambda i: (0, jnp.maximum(i - ctx_tiles, 0), 0)),
            _mod_spec(ctx_tiles, n_lat, l_lat // tm),
            _const_spec((1, D_MODEL)), _const_spec((1, D_MODEL)),
            _const_spec(wo.shape), _const_spec(wg.shape), _const_spec(wu.shape), _const_spec(wd.shape)],
        out_specs=_group_specs(ctx_tiles, D_MODEL),
        out_shape=[jax.ShapeDtypeStruct((ys_ctx.shape[0], D_MODEL), F32),
                   jax.ShapeDtypeStruct((ys_lat.shape[0], D_MODEL), F32)],
        compiler_params=_params("arbitrary"),
        name="out_ffn",
    )(x1, ys_ctx, ys_lat, yh_ctx, yh_lat, mod, n3, nf, wo, wg, wu, wd)


def _ssd_kernel(z_ref, xbc_ref, dt_ref, init_ref, cw_ref, cb_ref, dtb_ref, alog_ref, dexp_ref, nw_ref, e_ref,
                y_ref, fin_ref,
                xs_s, b_s, c_s, ec_s, dst_s, et_s, cum_s, ct_s, xd_s,
                *, l, seg, zero_init):
    q = SSD_CHUNK
    nc = l // q
    n_seq = y_ref.shape[0] // l
    cr = max(seg, q)
    e_mats = [e_ref[k].astype(BF16) for k in range(e_ref.shape[0])]
    dt_t = _softplus(dt_ref[...] + dtb_ref[...])
    a_t = dt_t * (-jnp.exp(alog_ref[...]))

    first, last = _seg_edges(cr, seg)

    for r0 in range(0, n_seq * l, cr):
        rows = slice(r0, r0 + cr)
        x = xbc_ref[rows, :]
        w = cw_ref[...]
        prev = jnp.where(first, 0.0, pltpu.roll(x, 1, 0))
        nxt = jnp.where(last, 0.0, pltpu.roll(x, cr - 1, 0))
        u = _silu(prev * w[0:1, :] + x * w[1:2, :] + nxt * w[2:3, :] + cb_ref[...])
        xs_s[rows, :] = u[:, :SSD_W]
        b_s[rows, :] = u[:, SSD_W:SSD_W + LANES]
        c_s[rows, :] = u[:, SSD_W + LANES:]

    row_g = lax.broadcasted_iota(jnp.int32, (LANES, N_DIR * HP), 0) // SSD_STATE
    lane_g = (lax.broadcasted_iota(jnp.int32, (LANES, N_DIR * HP), 1) % HP) // (HP // SSD_GROUPS)
    own_t = row_g == lane_g

    ii = lax.broadcasted_iota(jnp.int32, (q, q), 0)
    jj = lax.broadcasted_iota(jnp.int32, (q, q), 1)
    tri_upper = (ii <= jj).astype(BF16)
    lane = lax.broadcasted_iota(jnp.int32, (q, LANES), 1)
    low_half = lane < SSD_STATE
    is_fwd_row = lax.broadcasted_iota(jnp.int32, (N_DT, 1), 0) < SSD_HEADS

    chunks = range(n_seq * nc)

    for c in chunks:
        rows = slice(c * q, (c + 1) * q)
        a_c = a_t[:, rows]
        cum_f = _dot_exact_lhs(a_c, tri_upper)
        tot_c = cum_f[:, q - 1:q]
        cum_t = jnp.where(is_fwd_row, cum_f, tot_c - cum_f + a_c)
        ct_s[c] = cum_t
        stack = jnp.concatenate([cum_t, jnp.exp(cum_t), jnp.exp(tot_c - cum_t), dt_t[:, rows],
                                 jnp.zeros((LANES - 4 * N_DT, q), F32)], axis=0)
        cum_s[c] = stack.T

    for c in chunks:
        rows = slice(c * q, (c + 1) * q)
        cum = cum_s[c]
        cum_b = cum.astype(BF16)
        ec_s[rows, :] = _dot(cum_b, e_mats[1])
        tot = jnp.where(lane[0:1, :] < SSD_HEADS, cum[q - 1:q, :], cum[0:1, :])
        tot = jnp.where(lane[0:1, :] < N_DT, tot, 0.0)
        et_s[c] = _dot_exact_lhs(jnp.broadcast_to(jnp.exp(tot), (SUBLANES, LANES)), e_mats[0])
        xs = xs_s[rows, :]
        xd = jnp.concatenate([xs, xs], axis=1) * _dot(cum_b, e_mats[3])
        xd_s[rows, :] = xd
        w = (xd * _dot(cum_b, e_mats[2])).astype(BF16)
        dst_s[c] = jnp.where(own_t, _dot(b_s[rows, :].T.astype(BF16), w), 0.0)

    def local_pass(c):
        rows = slice(c * q, (c + 1) * q)
        cum = cum_s[c]
        cum_t = ct_s[c]
        xd = xd_s[rows, :]
        bcb = b_s[rows, :].astype(BF16)
        cc = c_s[rows, :]
        g_mats = [_dot_nt(jnp.where(low_half, cc, 0.0).astype(BF16), bcb),
                  _dot_nt(jnp.where(low_half, 0.0, cc).astype(BF16), bcb)]
        y_parts = []
        for pair in range(SSD_HEADS // 2):
            g = pair // (SSD_HEADS // 2 // SSD_GROUPS)
            acc = None
            for d in range(N_DIR):
                keep = (ii >= jj) if d == 0 else (ii <= jj)
                s_mats = []
                for hh in (2 * pair, 2 * pair + 1):
                    col = d * SSD_HEADS + hh
                    diff = cum[:, col:col + 1] - cum_t[col:col + 1, :]
                    decay = jnp.exp(jnp.where(keep, diff, NEG_BIG))
                    s_mats.append((g_mats[g] * decay).astype(BF16))
                lhs = jnp.concatenate(s_mats, axis=1)
                xp = xd[:, d * HP + pair * LANES:d * HP + (pair + 1) * LANES]
                rhs = jnp.concatenate([jnp.where(low_half, xp, 0.0), jnp.where(low_half, 0.0, xp)],
                                      axis=0).astype(BF16)
                part = _dot(lhs, rhs)
                acc = part if acc is None else acc + part
            y_parts.append(acc)
        y_ref[rows, :] = jnp.concatenate(y_parts, axis=1)

    for c in chunks:
        local_pass(c)

    half = HP // SSD_GROUPS
    for s in range(n_seq):
        states = []
        for d in range(N_DIR):
            if zero_init:
                states.append(jnp.zeros((LANES, HP), F32))
            else:
                s0 = init_ref[s, d]
                states.append(jnp.where(own_t[:, :HP], jnp.concatenate([s0, s0], axis=1).T, 0.0))

        for k in range(nc):
            for d in range(N_DIR):
                c = s * nc + (k if d == 0 else nc - 1 - k)
                rows = slice(c * q, (c + 1) * q)
                lanes = slice(d * HP, (d + 1) * HP)
                y_off = _dot(c_s[rows, :].astype(BF16), states[d].astype(BF16)) * ec_s[rows, lanes]
                y_ref[rows, :] = y_ref[rows, :] + y_off
                states[d] = states[d] * et_s[c, 0:1, lanes] + dst_s[c, :, lanes]

        for d in range(N_DIR):
            st = states[d].T
            fin_ref[s, d, 0:half, :] = st[0:half, 0:SSD_STATE]
            fin_ref[s, d, half:HP, :] = st[half:HP, SSD_STATE:2 * SSD_STATE]

        for c in range(s * nc, (s + 1) * nc):
            rows = slice(c * q, (c + 1) * q)
            y = y_ref[rows, :] + xs_s[rows, :] * dexp_ref[...]
            y = y * _silu(z_ref[rows, :])
            ms = jnp.mean(y * y, axis=-1, keepdims=True)
            y_ref[rows, :] = y * lax.rsqrt(ms + RMS_EPS) * nw_ref[...]


SSD_STEP_ROWS = 1024


def _ssd(z, xbc, dt, row0, bsz, l, init, cw, cb, dtb, alog, dexp, nw, e_mat, *, seg):
    n_seq = SSD_STEP_ROWS // l
    rows = n_seq * l
    zero_init = init is None
    if zero_init:
        init = jnp.zeros((n_seq, N_DIR, HP, SSD_STATE), F32)
        init_map = lambda b: (0, 0, 0, 0)
    else:
        init_map = lambda b: (b, 0, 0, 0)
    blk0 = row0 // rows
    seq_spec = lambda n: pl.BlockSpec((rows, n), lambda b: (blk0 + b, 0))
    const = lambda *shape: pl.BlockSpec(shape, lambda b: (0,) * len(shape))
    nc = rows // SSD_CHUNK
    return pl.pallas_call(
        functools.partial(_ssd_kernel, l=l, seg=seg, zero_init=zero_init),
        grid=(bsz // n_seq,),
        in_specs=[seq_spec(SSD_W), seq_spec(SSD_XBC), pl.BlockSpec((N_DT, rows), lambda b: (0, blk0 + b)),
                  pl.BlockSpec((n_seq, N_DIR, HP, SSD_STATE), init_map),
                  const(3, SSD_XBC), const(1, SSD_XBC), const(N_DT, 1), const(N_DT, 1),
                  const(1, SSD_W), const(1, SSD_W), const(*e_mat.shape)],
        out_specs=[pl.BlockSpec((rows, SSD_W), lambda b: (b, 0)),
                   pl.BlockSpec((n_seq, N_DIR, HP, SSD_STATE), lambda b: (b, 0, 0, 0))],
        out_shape=[jax.ShapeDtypeStruct((bsz * l, SSD_W), F32),
                   jax.ShapeDtypeStruct((bsz, N_DIR, HP, SSD_STATE), F32)],
        scratch_shapes=[pltpu.VMEM((rows, SSD_W), F32), pltpu.VMEM((rows, LANES), F32), pltpu.VMEM((rows, LANES), F32),
                        pltpu.VMEM((rows, N_DIR * HP), F32),
                        pltpu.VMEM((nc, LANES, N_DIR * HP), F32), pltpu.VMEM((nc, SUBLANES, N_DIR * HP), F32),
                        pltpu.VMEM((nc, SSD_CHUNK, LANES), F32), pltpu.VMEM((nc, N_DT, SSD_CHUNK), F32),
                        pltpu.VMEM((rows, N_DIR * HP), F32)],
        compiler_params=_params("arbitrary"),
        name=f"ssd{l}",
    )(z, xbc, dt, init, cw, cb, dtb, alog, dexp, nw, e_mat)


HY_ROW_BLOCK = 16
HY_STEP_ROWS = 2048


def _hyena_kernel(v_ref, x1_ref, x2_ref, wv_ref, w1_ref, w2_ref, bv_ref, b1_ref, b2_ref,
                  f_ref, g_ref, kf_ref, kn_ref, skip_ref, o_ref, spec_s, prod_s, *, l, seg):
    m = l // 2
    tiles = o_ref.shape[0]
    n_seq = o_ref.shape[1] // l
    ct = tiles * LANES
    rb = HY_ROW_BLOCK
    first, last = _seg_edges(m, seg // 2)
    row0 = lax.broadcasted_iota(jnp.int32, (rb, 1), 0) == 0
    ev = slice(0, ct)
    od = slice(ct, 2 * ct)

    def conv_eo(x_ref, w_ref, b_ref, base):
        xe, xo = (jnp.concatenate([x_ref[k, pl.ds(base + p, m, stride=2), :] for k in range(tiles)], axis=1)
                  for p in range(2))
        w = w_ref[...]
        b = b_ref[...]
        xo_prev = jnp.where(first, 0.0, pltpu.roll(xo, 1, 0))
        xe_next = jnp.where(last, 0.0, pltpu.roll(xe, m - 1, 0))
        ce = xo_prev * w[0:1, :] + xe * w[1:2, :] + xo * w[2:3, :] + b
        co = xe * w[0:1, :] + xo * w[1:2, :] + xe_next * w[2:3, :] + b
        return jnp.concatenate([ce, co], axis=1)

    def pointwise(s, i, r0):
        re = slice(r0, r0 + rb)
        im = slice(m + r0, m + r0 + rb)
        er, orr = spec_s[s, re, ev], spec_s[s, re, od]
        ei, oi = spec_s[s, im, ev], spec_s[s, im, od]
        ker, kei, kor, koi, vr, vi = (kf_ref[i, p, re, :] for p in range(N_FILT_PLANES))
        if r0 == 0:
            e_n, o_n = ei[0:1, :], oi[0:1, :]
            ei = jnp.where(row0, 0.0, ei)
            oi = jnp.where(row0, 0.0, oi)
        pe_r = er * ker - ei * kei + orr * vr - oi * vi
        pe_i = er * kei + ei * ker + orr * vi + oi * vr
        po_r = er * kor - ei * koi + orr * ker - oi * kei
        po_i = er * koi + ei * kor + orr * kei + oi * ker
        if r0 == 0:
            kn = kn_ref[i]
            pe_i = jnp.where(row0, e_n * kn[0:1, :] + o_n * kn[2:3, :], pe_i)
            po_i = jnp.where(row0, e_n * kn[1:2, :] + o_n * kn[0:1, :], po_i)
        prod_s[s, re, ev] = pe_r.astype(BF16)
        prod_s[s, re, od] = po_r.astype(BF16)
        prod_s[s, im, ev] = pe_i.astype(BF16)
        prod_s[s, im, od] = po_i.astype(BF16)

    seqs = range(n_seq)
    zz = [conv_eo(v_ref, wv_ref, bv_ref, s * l) for s in seqs]
    for i, (xg_ref, wg_ref, bg_ref) in enumerate(((x1_ref, w1_ref, b1_ref), (x2_ref, w2_ref, b2_ref))):
        for s in seqs:
            spec_s[s] = _dot(f_ref[...], zz[s].astype(BF16))
        for s in seqs:
            for r0 in range(0, m, rb):
                pointwise(s, i, r0)
        skip = skip_ref[i:i + 1, :]
        skip2 = jnp.concatenate([skip, skip], axis=1)
        for s in seqs:
            conv = _dot(g_ref[...], prod_s[s])
            zz[s] = conv_eo(xg_ref, wg_ref, bg_ref, s * l) * (conv + zz[s] * skip2)
    for s in seqs:
        for k in range(tiles):
            o_ref[k, pl.ds(s * l, m, stride=2), :] = zz[s][:, k * LANES:(k + 1) * LANES]
            o_ref[k, pl.ds(s * l + 1, m, stride=2), :] = zz[s][:, ct + k * LANES:ct + (k + 1) * LANES]


def _hyena(hy, row0, bsz, l, n_seq, cw, cb, fmat, gmat, kf, kn, skip, *, seg):
    ct = HY_CH_TILE
    nct = HY_W // ct
    tiles = ct // LANES
    m = l // 2
    rows = n_seq * l
    blk0 = row0 // rows
    part = lambda p: pl.BlockSpec((tiles, rows, LANES), lambda j, b: (p * nct + j, blk0 + b, 0))
    wpart = lambda p: pl.BlockSpec((3, ct), lambda j, b: (0, p * nct + j))
    bpart = lambda p: pl.BlockSpec((1, ct), lambda j, b: (0, p * nct + j))
    return pl.pallas_call(
        functools.partial(_hyena_kernel, l=l, seg=seg),
        grid=(nct, bsz // n_seq),
        in_specs=[part(0), part(1), part(2), wpart(0), wpart(1), wpart(2), bpart(0), bpart(1), bpart(2),
                  pl.BlockSpec((l, m), lambda j, b: (0, 0)),
                  pl.BlockSpec((m, l), lambda j, b: (0, 0)),
                  pl.BlockSpec((HY_ORDER, N_FILT_PLANES, m, ct), lambda j, b: (0, 0, 0, j)),
                  pl.BlockSpec((HY_ORDER, SUBLANES, ct), lambda j, b: (0, 0, j)),
                  pl.BlockSpec((HY_ORDER, ct), lambda j, b: (0, j))],
        out_specs=pl.BlockSpec((tiles, rows, LANES), lambda j, b: (j, b, 0)),
        out_shape=jax.ShapeDtypeStruct((HY_W // LANES, bsz * l, LANES), F32),
        scratch_shapes=[pltpu.VMEM((n_seq, l, 2 * ct), F32), pltpu.VMEM((n_seq, l, 2 * ct), BF16)],
        compiler_params=_params("arbitrary", "arbitrary"),
        name=f"hyena{l}",
    )(hy, hy, hy, cw, cw, cw, cb, cb, cb, fmat, gmat, kf, kn, skip)


def _dft_mats(l):
    n = 2 * l
    f = np.arange(l, dtype=np.int64)[:, None]
    t = np.arange(l, dtype=np.int64)[None, :]
    ang = 2.0 * np.pi * ((f * t) % n).astype(np.float64) / n
    alt = np.where(np.arange(l) % 2 == 0, 1.0, -1.0)
    top = np.cos(ang)
    bot = -np.sin(ang)
    bot[0, :] = alt
    fwd = np.concatenate([top, bot], axis=0)
    wf = np.full((l,), 2.0)
    wf[0] = 1.0
    gtop = np.cos(ang).T * wf[None, :] / n
    gbot = -np.sin(ang).T * 2.0 / n
    gbot[:, 0] = alt / n
    inv = np.concatenate([gtop, gbot], axis=1)
    return fwd.astype(np.float32), inv.astype(np.float32)


def _filter_feats(l):
    t = np.linspace(0.0, 1.0, l)[:, None]
    w = (2.0 * np.pi / l) * np.arange(l, dtype=np.float64)[:, None]
    f = np.linspace(1e-4, HY_BANDS - 1, HY_BANDS)[None, :]
    feats = np.concatenate([t, np.cos(f * w), -np.sin(f * w)], axis=-1)
    out = np.zeros((l, EMB_PAD), np.float32)
    out[:, :HY_EMB] = feats
    return out[0::2], out[1::2]


def _shift_twiddles(l):
    theta = 2.0 * np.pi * np.arange(l // 2, dtype=np.float64) / l
    return np.stack([np.cos(theta), np.sin(theta)], axis=1).astype(np.float32)


def _head_expand():
    n_blocks = LANES // N_DT // 2
    e = np.zeros((n_blocks, LANES, N_DIR * HP), np.float32)
    for k in range(n_blocks):
        for j in range(N_DT):
            e[k, k * N_DT + j, j * SSD_HEAD_DIM:(j + 1) * SSD_HEAD_DIM] = 1.0
    return e


def kernel(x_prompt, x_sample, state_ssd, c, c_ctx, w_ada, b_ada, norm_ffn1, ffn1_w_gate, ffn1_w_up, ffn1_w_down, norm_mix, w_in, w_out, ssd_conv_w, ssd_conv_b, ssd_dt_bias, ssd_a_log, ssd_d, ssd_norm_w, hy_conv_w, hy_conv_b, hy_w1, hy_b1, hy_freq, hy_w2, hy_b2, hy_w3, hy_skip, norm_ffn2, ffn2_w_gate, ffn2_w_up, ffn2_w_down, norm_final):
    assert w_ada.shape[0] == 1, "single layer"
    n_ctx, l_ctx, _ = x_prompt.shape
    n_lat, l_lat, _ = x_sample.shape
    t_ctx = n_ctx * l_ctx

    cond = jnp.zeros((16, D_MODEL), F32).at[:n_lat].set(c).at[n_lat].set(c_ctx)
    mod = _ada(cond, w_ada, b_ada).reshape(16, N_MOD, D_MODEL)

    tr = lambda w: jnp.swapaxes(w, 1, 2)
    wg1, wu1, wd1 = _cast_pad_rows([tr(ffn1_w_gate), tr(ffn1_w_up), ffn1_w_down])
    wi = _cast_mix(tr(w_in))

    row = lambda v: v.reshape(1, -1)
    dtb, alog = ssd_dt_bias[0].reshape(N_DT, 1), ssd_a_log[0].reshape(N_DT, 1)
    dexp = jnp.repeat(ssd_d[0], SSD_HEAD_DIM).reshape(1, SSD_W)
    e_mat = jnp.asarray(_head_expand())
    w1p = jnp.pad(hy_w1[0], ((0, EMB_PAD - HY_EMB), (0, 0)))
    deltas = jnp.asarray(np.abs(np.linspace(HY_MIN_DECAY, HY_MAX_DECAY, HY_ORDER * HY_W))
                         .reshape(HY_ORDER, HY_W).astype(np.float32))

    (x1, z, xbc, dt, hy), (wg2, wu2, wd2), wo = _ffn_in(
        x_prompt.reshape(t_ctx, D_MODEL), x_sample.reshape(n_lat * l_lat, D_MODEL), mod, n_lat, l_lat,
        row(norm_ffn1[0]), row(norm_mix[0]), wg1, wu1, wd1, wi,
        [tr(ffn2_w_gate), tr(ffn2_w_up), ffn2_w_down], w_out)

    def mixers(row0, bsz, l, init, seg):
        fwd_np, inv_np = _dft_mats(l // 2)
        fmat = jnp.asarray(fwd_np).astype(BF16)
        gmat = jnp.asarray(inv_np).astype(BF16)
        feats_e, feats_o = _filter_feats(l)
        kf, kn = _filters(l, jnp.asarray(feats_e), jnp.asarray(feats_o), w1p, row(hy_b1[0]), row(hy_freq[0]),
                          hy_w2[0], row(hy_b2[0]), hy_w3[0], deltas, fmat, jnp.asarray(_shift_twiddles(l)))
        ys, fin = _ssd(z, xbc, dt, row0, bsz, l, init, ssd_conv_w[0], row(ssd_conv_b[0]), dtb, alog, dexp,
                       row(ssd_norm_w[0]), e_mat, seg=seg)
        yh = _hyena(hy, row0, bsz, l, HY_STEP_ROWS // l, hy_conv_w[0], row(hy_conv_b[0]), fmat, gmat, kf, kn,
                    hy_skip[0], seg=seg)
        return ys, yh, fin

    ys_ctx, yh_ctx, ctx_fin = mixers(0, n_ctx, l_ctx, None, l_ctx)
    lat_init = state_ssd[:, 0].reshape(n_lat, N_DIR, HP, SSD_STATE)
    ys_lat, yh_lat, _ = mixers(t_ctx, n_lat, l_lat, lat_init, GRID_W)

    y_ctx, y_lat = _out_ffn(x1, ys_ctx, ys_lat, yh_ctx, yh_lat, mod, n_lat, l_lat, row(norm_ffn2[0]),
                            row(norm_final), wo, wg2, wu2, wd2)
    new_state = ctx_fin.reshape(n_ctx, 1, N_DIR, SSD_HEADS, SSD_HEAD_DIM, SSD_STATE).astype(x_prompt.dtype)
    return (y_ctx.reshape(n_ctx, l_ctx, D_MODEL), y_lat.reshape(n_lat, l_lat, D_MODEL), new_state)
```

```python
import functools
import math

import numpy as np
import jax
import jax.numpy as jnp
from jax import lax
from jax.experimental import pallas as pl
from jax.experimental.pallas import tpu as pltpu

F32 = jnp.float32
BF16 = jnp.bfloat16

D_MODEL = 1024
GRID_W = 64
N_MOD = 9
RMS_EPS = 1e-6
FFN_DIM = 2752
SSD_W = 512
SSD_HEADS = 8
SSD_HEAD_DIM = 64
SSD_STATE = 64
SSD_GROUPS = 2
SSD_CHUNK = 128
SSD_XBC = SSD_W + 2 * SSD_GROUPS * SSD_STATE
N_DIR = 2
HY_W = 512
HY_ORDER = 2
HY_EMB = 33
HY_BANDS = (HY_EMB - 1) // 2
HY_HIDDEN = 64
HY_MIN_DECAY = math.log(1e-2) / 1.5
HY_MAX_DECAY = math.log(1e-2) / 0.3
IN_SPLITS = (SSD_W, SSD_W + SSD_XBC, SSD_W + SSD_XBC + N_DIR * SSD_HEADS)
IN_COLS = IN_SPLITS[-1] + (HY_ORDER + 1) * HY_W

LANES = 128
SUBLANES = 8
FFN_PAD = 2816
FFN_CHUNK = 256
N_DT = N_DIR * SSD_HEADS
DT_PAD = LANES
IN_ROWS = IN_COLS + DT_PAD - N_DIR * SSD_HEADS
EMB_PAD = LANES
TOKEN_TILE = 512
HY_CH_TILE = 256
VMEM_LIMIT = 60 * 1024 * 1024
NEG_BIG = -1e30
HP = SSD_HEADS * SSD_HEAD_DIM


def _silu(x):
    return x * jax.nn.sigmoid(x)


def _softplus(x):
    return jnp.maximum(x, 0.0) + jnp.log1p(jnp.exp(-jnp.abs(x)))


def _rms_mod(x, gain, shift, scale):
    ms = jnp.mean(x * x, axis=-1, keepdims=True)
    return x * lax.rsqrt(ms + RMS_EPS) * (gain * (1.0 + scale)) + shift


def _dot(a, b):
    return jnp.dot(a, b, preferred_element_type=F32)


def _dot_nt(a, b):
    return lax.dot_general(a, b, (((1,), (1,)), ((), ())), preferred_element_type=F32)


def _split3(x):
    hi = x.astype(BF16)
    r = x - hi.astype(F32)
    mid = r.astype(BF16)
    lo = (r - mid.astype(F32)).astype(BF16)
    return hi, mid, lo


def _dot_exact_lhs(x, m01):
    hi, mid, lo = _split3(x)
    return _dot(hi, m01) + _dot(mid, m01) + _dot(lo, m01)


def _seg_edges(l, seg):
    pos = lax.broadcasted_iota(jnp.int32, (l, 1), 0) & (seg - 1)
    return pos == 0, pos == seg - 1


def _swiglu_acc(h, wg_ref, wu_ref, wd_ref):
    acc = None
    for k in range(FFN_PAD // FFN_CHUNK):
        cols = slice(k * FFN_CHUNK, (k + 1) * FFN_CHUNK)
        g = _dot_nt(h, wg_ref[cols, :])
        u = _dot_nt(h, wu_ref[cols, :])
        a = (_silu(g) * u).astype(BF16)
        part = _dot(a, wd_ref[cols, :])
        acc = part if acc is None else acc + part
    return acc


def _params(*semantics):
    return pltpu.CompilerParams(dimension_semantics=semantics, vmem_limit_bytes=VMEM_LIMIT)


LOAD_ROWS = 256
LOAD_SLOTS = 4


def _row_pieces(src_lo, src_hi, dst_lo):
    return [(r, min(LOAD_ROWS, src_hi - r), dst_lo + r - src_lo) for r in range(src_lo, src_hi, LOAD_ROWS)]


def _load_weights_bf16(pieces, stage_ref, sem_ref):
    def load(k):
        src, r, n, _, _ = pieces[k]
        slot = k % LOAD_SLOTS
        return pltpu.make_async_copy(src.at[0, pl.ds(r, n), :], stage_ref.at[slot, pl.ds(0, n), :], sem_ref.at[slot])

    for k in range(min(LOAD_SLOTS, len(pieces))):
        load(k).start()
    for k, (_, _, n, dst, dr) in enumerate(pieces):
        load(k).wait()
        dst[dr:dr + n, :] = stage_ref[k % LOAD_SLOTS, 0:n, :].astype(BF16)
        if k + LOAD_SLOTS < len(pieces):
            load(k + LOAD_SLOTS).start()


def _ada_kernel(cond_ref, w_ref, b_ref, o_ref):
    s = _silu(cond_ref[...]).astype(BF16)
    o_ref[...] = _dot(s, w_ref[...].astype(BF16)) + b_ref[...]


def _ada(cond, w_ada, b_ada):
    rows = cond.shape[0]
    n = w_ada.shape[-1]
    tn = D_MODEL
    return pl.pallas_call(
        _ada_kernel,
        grid=(n // tn,),
        in_specs=[
            pl.BlockSpec((rows, D_MODEL), lambda j: (0, 0)),
            pl.BlockSpec((None, D_MODEL, tn), lambda j: (0, 0, j)),
            pl.BlockSpec((1, tn), lambda j: (0, j)),
        ],
        out_specs=pl.BlockSpec((rows, tn), lambda j: (0, j)),
        out_shape=jax.ShapeDtypeStruct((rows, n), F32),
        compiler_params=_params("arbitrary"),
        name="ada",
    )(cond, w_ada, b_ada)


N_FILT_PLANES = 6


def _filter_kernel(fe_ref, fo_ref, w1_ref, b1_ref, fr_ref, w2_ref, b2_ref, w3_ref, dl_ref, f_ref, tw_ref,
                   kf_ref, kn_ref, *, m):
    freq = fr_ref[...]
    w1 = w1_ref[...].astype(BF16)
    w2 = w2_ref[...].astype(BF16)

    def hidden(feats):
        h = jnp.sin(freq * (_dot(feats.astype(BF16), w1) + b1_ref[...]))
        h = jnp.sin(freq * (_dot(h.astype(BF16), w2) + b2_ref[...]))
        return h.astype(BF16)

    feats_e = fe_ref[...]
    feats_o = fo_ref[...]
    hb_e = hidden(feats_e)
    hb_o = hidden(feats_o)
    t_e = feats_e[:, 0:1]
    t_o = feats_o[:, 0:1]
    row = lax.broadcasted_iota(jnp.int32, (m, 1), 0)
    sign = jnp.where((row & 1) == 0, 1.0, -1.0)
    cos_t = tw_ref[:, 0:1]
    sin_t = tw_ref[:, 1:2]
    f_top = f_ref[0:m, :]
    f_bot = f_ref[m:2 * m, :]

    def spectrum(k):
        kb = k.astype(BF16)
        return _dot(f_top, kb), jnp.where(row == 0, 0.0, _dot(f_bot, kb)), jnp.sum(k * sign, axis=0, keepdims=True)

    for i in range(HY_ORDER):
        dl = dl_ref[i:i + 1, :]
        c0 = (0 * HY_ORDER + i) * HY_W
        c1 = (1 * HY_ORDER + i) * HY_W
        w3f = w3_ref[:, c0:c0 + HY_W].astype(BF16)
        w3b = w3_ref[:, c1:c1 + HY_W].astype(BF16)
        win_e = jnp.exp(-t_e * dl)
        win_o = jnp.exp(-t_o * dl)
        k0e = _dot(hb_e, w3f) * win_e
        k0o = _dot(hb_o, w3f) * win_o
        k1e = jnp.where(row == 0, 0.0, _dot(hb_e, w3b) * win_e)
        k1o = _dot(hb_o, w3b) * win_o
        ker, _, ken = spectrum(k0e + k1e)
        _, kei, _ = spectrum(k0e - k1e)
        ar, ai, an = spectrum(k0o)
        br, bi, bn = spectrum(k1o)
        kf_ref[i, 0] = ker
        kf_ref[i, 1] = kei
        kf_ref[i, 2] = ar + cos_t * br + sin_t * bi
        kf_ref[i, 3] = ai + sin_t * br - cos_t * bi
        kf_ref[i, 4] = cos_t * ar + sin_t * ai + br
        kf_ref[i, 5] = cos_t * ai - sin_t * ar - bi
        kn_ref[i] = jnp.concatenate([ken, an - bn, bn - an, jnp.zeros((SUBLANES - 3, HY_W), F32)], axis=0)


def _filters(l, feats_e, feats_o, w1p, b1, freq, w2, b2, w3, deltas, fmat, twid):
    m = l // 2
    full = lambda *shape: pl.BlockSpec(shape, lambda: (0,) * len(shape))
    return pl.pallas_call(
        functools.partial(_filter_kernel, m=m),
        in_specs=[full(m, EMB_PAD), full(m, EMB_PAD), full(EMB_PAD, HY_HIDDEN), full(1, HY_HIDDEN),
                  full(1, HY_HIDDEN), full(HY_HIDDEN, HY_HIDDEN), full(1, HY_HIDDEN),
                  full(HY_HIDDEN, N_DIR * HY_ORDER * HY_W), full(HY_ORDER, HY_W), full(l, m), full(m, 2)],
        out_specs=[full(HY_ORDER, N_FILT_PLANES, m, HY_W), full(HY_ORDER, SUBLANES, HY_W)],
        out_shape=[jax.ShapeDtypeStruct((HY_ORDER, N_FILT_PLANES, m, HY_W), F32),
                   jax.ShapeDtypeStruct((HY_ORDER, SUBLANES, HY_W), F32)],
        compiler_params=pltpu.CompilerParams(vmem_limit_bytes=VMEM_LIMIT),
        name=f"filt{l}",
    )(feats_e, feats_o, w1p, b1, freq, w2, b2, w3, deltas, fmat, twid)


def _const_spec(shape):
    return pl.BlockSpec(shape, lambda i: (0,) * len(shape), pipeline_mode=pl.Buffered(1))


def _group_specs(ctx_tiles, width):
    tm = TOKEN_TILE
    return [pl.BlockSpec((tm, width), lambda i: (jnp.minimum(i, ctx_tiles - 1), 0)),
            pl.BlockSpec((tm, width), lambda i: (jnp.maximum(i - ctx_tiles, 0), 0))]


def _mod_spec(ctx_tiles, n_lat, tiles_per_seq):
    return pl.BlockSpec((None, N_MOD, D_MODEL),
                        lambda i: (jnp.where(i < ctx_tiles, n_lat, (i - ctx_tiles) // tiles_per_seq), 0, 0))


NEXT_CAST_ROWS = 128
NEXT_CAST_ROWS_OUT = 64


def _ffn_in_kernel(xc_ref, xl_ref, mod_ref, n1_ref, nm_ref, wg_hbm, wu_hbm, wd_hbm, wi_hbm,
                   ng_ref, nu_ref, nd_ref, no_ref,
                   x1_ref, z_ref, xbc_ref, dt_ref, hy_ref, ngb_ref, nub_ref, ndb_ref, nob_ref,
                   wg_ref, wu_ref, wd_ref, wi_ref, stage_ref, sem_ref, *, ctx_tiles):
    @pl.when(pl.program_id(0) == 0)
    def _():
        o1, o2, o3 = IN_SPLITS
        pieces = []
        for src, dst in ((wg_hbm, wg_ref), (wu_hbm, wu_ref), (wd_hbm, wd_ref)):
            pieces += [(src, r, n, dst, dr) for r, n, dr in _row_pieces(0, FFN_DIM, 0)]
            dst[FFN_DIM:, :] = jnp.zeros((FFN_PAD - FFN_DIM, D_MODEL), BF16)
        pieces += [(wi_hbm, r, n, wi_ref, dr) for r, n, dr in _row_pieces(0, o3, 0) + _row_pieces(o3, IN_COLS, o2 + DT_PAD)]
        wi_ref[o3:o2 + DT_PAD, :] = jnp.zeros((o2 + DT_PAD - o3, D_MODEL), BF16)
        _load_weights_bf16(pieces, stage_ref, sem_ref)

    blk = jnp.minimum(pl.program_id(0), FFN_PAD // NEXT_CAST_ROWS - 1)
    wrow = blk * NEXT_CAST_ROWS + lax.broadcasted_iota(jnp.int32, (NEXT_CAST_ROWS, 1), 0)
    for src, dst in ((ng_ref, ngb_ref), (nu_ref, nub_ref), (nd_ref, ndb_ref)):
        dst[...] = jnp.where(wrow < FFN_DIM, src[...], 0.0).astype(BF16)
    nob_ref[...] = no_ref[...].astype(BF16)

    x = jnp.where(pl.program_id(0) < ctx_tiles, xc_ref[...], xl_ref[...])
    mod = mod_ref[...]
    h = _rms_mod(x, n1_ref[...], mod[0:1, :], mod[1:2, :]).astype(BF16)
    x1 = x + (0.5 * mod[2:3, :]) * _swiglu_acc(h, wg_ref, wu_ref, wd_ref)
    x1_ref[...] = x1
    h2 = _rms_mod(x1, nm_ref[...], mod[3:4, :], mod[4:5, :]).astype(BF16)
    o1, o2, _ = IN_SPLITS
    z_ref[...] = _dot_nt(h2, wi_ref[0:o1, :])
    xbc_ref[...] = _dot_nt(h2, wi_ref[o1:o2, :])
    dt_ref[...] = _dot_nt(wi_ref[o2:o2 + N_DT, :], h2)
    hy = _dot_nt(h2, wi_ref[o2 + DT_PAD:, :])
    for k in range(hy_ref.shape[0]):
        hy_ref[k] = hy[:, k * LANES:(k + 1) * LANES]


def _ffn_in(x_ctx, x_lat, mod, n_lat, l_lat, n1, nm, wg, wu, wd, wi, next_ffn, next_out):
    tm = TOKEN_TILE
    ctx_tiles = x_ctx.shape[0] // tm
    tokens = x_ctx.shape[0] + x_lat.shape[0]
    steps = tokens // tm
    ffn_blocks = FFN_PAD // NEXT_CAST_ROWS
    out_blocks = D_MODEL // NEXT_CAST_ROWS_OUT
    assert steps >= ffn_blocks and steps >= out_blocks
    row_spec = lambda n: pl.BlockSpec((tm, n), lambda i: (i, 0))
    widths = (D_MODEL, SSD_W, SSD_XBC)
    hy_tiles = (HY_ORDER + 1) * HY_W // LANES
    dt_spec = pl.BlockSpec((N_DT, tm), lambda i: (0, i))
    outs = pl.pallas_call(
        functools.partial(_ffn_in_kernel, ctx_tiles=ctx_tiles),
        grid=(steps,),
        in_specs=_group_specs(ctx_tiles, D_MODEL) + [
            _mod_spec(ctx_tiles, n_lat, l_lat // tm),
            _const_spec((1, D_MODEL)), _const_spec((1, D_MODEL))]
        + [pl.BlockSpec(memory_space=pl.ANY)] * 4
        + [pl.BlockSpec((None, NEXT_CAST_ROWS, D_MODEL), lambda i: (0, jnp.minimum(i, ffn_blocks - 1), 0))] * 3
        + [pl.BlockSpec((None, NEXT_CAST_ROWS_OUT, D_MODEL), lambda i: (0, jnp.minimum(i, out_blocks - 1), 0))],
        out_specs=[row_spec(n) for n in widths] + [dt_spec, pl.BlockSpec((hy_tiles, tm, LANES), lambda i: (0, i, 0))]
        + [pl.BlockSpec((NEXT_CAST_ROWS, D_MODEL), lambda i: (jnp.minimum(i, ffn_blocks - 1), 0))] * 3
        + [pl.BlockSpec((NEXT_CAST_ROWS_OUT, D_MODEL), lambda i: (jnp.minimum(i, out_blocks - 1), 0))],
        out_shape=[jax.ShapeDtypeStruct((tokens, n), F32) for n in widths]
        + [jax.ShapeDtypeStruct((N_DT, tokens), F32), jax.ShapeDtypeStruct((hy_tiles, tokens, LANES), F32)]
        + [jax.ShapeDtypeStruct((FFN_PAD, D_MODEL), BF16)] * 3 + [jax.ShapeDtypeStruct((D_MODEL, D_MODEL), BF16)],
        scratch_shapes=[pltpu.VMEM((FFN_PAD, D_MODEL), BF16)] * 3 + [
            pltpu.VMEM((IN_ROWS, D_MODEL), BF16), pltpu.VMEM((LOAD_SLOTS, LOAD_ROWS, D_MODEL), F32),
            pltpu.SemaphoreType.DMA((LOAD_SLOTS,))],
        compiler_params=_params("arbitrary"),
        name="ffn_in",
    )(x_ctx, x_lat, mod, n1, nm, wg, wu, wd, wi, *next_ffn, next_out)
    return outs[:5], outs[5:8], outs[8]


def _out_ffn_kernel(x1_ref, ysc_ref, ysl_ref, yhc_ref, yhl_ref, mod_ref, n3_ref, nf_ref,
                    wo_ref, wg_ref, wu_ref, wd_ref, oc_ref, ol_ref, *, ctx_tiles):
    is_ctx = pl.program_id(0) < ctx_tiles
    mod = mod_ref[...]
    ys = jnp.where(is_ctx, ysc_ref[...], ysl_ref[...])
    yh = [jnp.where(is_ctx, yhc_ref[k], yhl_ref[k]) for k in range(yhc_ref.shape[0])]
    y = jnp.concatenate([ys] + yh, axis=1).astype(BF16)
    x2 = x1_ref[...] + mod[5:6, :] * _dot(y, wo_ref[...])
    h = _rms_mod(x2, n3_ref[...], mod[6:7, :], mod[7:8, :]).astype(BF16)
    x3 = x2 + (0.5 * mod[8:9, :]) * _swiglu_acc(h, wg_ref, wu_ref, wd_ref)
    ms = jnp.mean(x3 * x3, axis=-1, keepdims=True)
    out = x3 * lax.rsqrt(ms + RMS_EPS) * nf_ref[...]

    @pl.when(is_ctx)
    def _():
        oc_ref[...] = out

    @pl.when(jnp.logical_not(is_ctx))
    def _():
        ol_ref[...] = out


def _out_ffn(x1, ys_ctx, ys_lat, yh_ctx, yh_lat, mod, n_lat, l_lat, n3, nf, wo, wg, wu, wd):
    tm = TOKEN_TILE
    ctx_tiles = ys_ctx.shape[0] // tm
    tokens = x1.shape[0]
    return pl.pallas_call(
        functools.partial(_out_ffn_kernel, ctx_tiles=ctx_tiles),
        grid=(tokens // tm,),
        in_specs=[pl.BlockSpec((tm, D_MODEL), lambda i: (i, 0))]
        + _group_specs(ctx_tiles, SSD_W) + [
            pl.BlockSpec((HY_W // LANES, tm, LANES), lambda i: (0, jnp.minimum(i, ctx_tiles - 1), 0)),
            pl.BlockSpec((HY_W // LANES, tm, LANES), lambda i: (0, jnp.maximum(i - ctx_tiles, 0), 0)),
            _mod_spec(ctx_tiles, n_lat, l_lat // tm),
            _const_spec((1, D_MODEL)), _const_spec((1, D_MODEL)),
            _const_spec(wo.shape), _const_spec(wg.shape), _const_spec(wu.shape), _const_spec(wd.shape)],
        out_specs=_group_specs(ctx_tiles, D_MODEL),
        out_shape=[jax.ShapeDtypeStruct((ys_ctx.shape[0], D_MODEL), F32),
                   jax.ShapeDtypeStruct((ys_lat.shape[0], D_MODEL), F32)],
        compiler_params=_params("arbitrary"),
        name="out_ffn",
    )(x1, ys_ctx, ys_lat, yh_ctx, yh_lat, mod, n3, nf, wo, wg, wu, wd)


def _ssd_kernel(z_ref, xbc_ref, dt_ref, init_ref, cw_ref, cb_ref, dtb_ref, alog_ref, dexp_ref, nw_ref, e_ref,
                y_ref, fin_ref,
                xs_s, b_s, c_s, ec_s, dst_s, et_s, cum_s, ct_s, xd_s,
                *, l, seg, zero_init):
    q = SSD_CHUNK
    nc = l // q
    n_seq = y_ref.shape[0] // l
    cr = max(seg, q)
    e_mats = [e_ref[k].astype(BF16) for k in range(e_ref.shape[0])]
    dt_t = _softplus(dt_ref[...] + dtb_ref[...])
    a_t = dt_t * (-jnp.exp(alog_ref[...]))

    first, last = _seg_edges(cr, seg)

    for r0 in range(0, n_seq * l, cr):
        rows = slice(r0, r0 + cr)
        x = xbc_ref[rows, :]
        w = cw_ref[...]
        prev = jnp.where(first, 0.0, pltpu.roll(x, 1, 0))
        nxt = jnp.where(last, 0.0, pltpu.roll(x, cr - 1, 0))
        u = _silu(prev * w[0:1, :] + x * w[1:2, :] + nxt * w[2:3, :] + cb_ref[...])
        xs_s[rows, :] = u[:, :SSD_W]
        b_s[rows, :] = u[:, SSD_W:SSD_W + LANES]
        c_s[rows, :] = u[:, SSD_W + LANES:]

    row_g = lax.broadcasted_iota(jnp.int32, (LANES, N_DIR * HP), 0) // SSD_STATE
    lane_g = (lax.broadcasted_iota(jnp.int32, (LANES, N_DIR * HP), 1) % HP) // (HP // SSD_GROUPS)
    own_t = row_g == lane_g

    ii = lax.broadcasted_iota(jnp.int32, (q, q), 0)
    jj = lax.broadcasted_iota(jnp.int32, (q, q), 1)
    tri_upper = (ii <= jj).astype(BF16)
    lane = lax.broadcasted_iota(jnp.int32, (q, LANES), 1)
    low_half = lane < SSD_STATE
    is_fwd_row = lax.broadcasted_iota(jnp.int32, (N_DT, 1), 0) < SSD_HEADS

    chunks = range(n_seq * nc)

    for c in chunks:
        rows = slice(c * q, (c + 1) * q)
        a_c = a_t[:, rows]
        cum_f = _dot_exact_lhs(a_c, tri_upper)
        tot_c = cum_f[:, q - 1:q]
        cum_t = jnp.where(is_fwd_row, cum_f, tot_c - cum_f + a_c)
        ct_s[c] = cum_t
        stack = jnp.concatenate([cum_t, jnp.exp(cum_t), jnp.exp(tot_c - cum_t), dt_t[:, rows],
                                 jnp.zeros((LANES - 4 * N_DT, q), F32)], axis=0)
        cum_s[c] = stack.T

    for c in chunks:
        rows = slice(c * q, (c + 1) * q)
        cum = cum_s[c]
        cum_b = cum.astype(BF16)
        ec_s[rows, :] = _dot(cum_b, e_mats[1])
        tot = jnp.where(lane[0:1, :] < SSD_HEADS, cum[q - 1:q, :], cum[0:1, :])
        tot = jnp.where(lane[0:1, :] < N_DT, tot, 0.0)
        et_s[c] = _dot_exact_lhs(jnp.broadcast_to(jnp.exp(tot), (SUBLANES, LANES)), e_mats[0])
        xs = xs_s[rows, :]
        xd = jnp.concatenate([xs, xs], axis=1) * _dot(cum_b, e_mats[3])
        xd_s[rows, :] = xd
        w = (xd * _dot(cum_b, e_mats[2])).astype(BF16)
        dst_s[c] = jnp.where(own_t, _dot(b_s[rows, :].T.astype(BF16), w), 0.0)

    def local_pass(c):
        rows = slice(c * q, (c + 1) * q)
        cum = cum_s[c]
        cum_t = ct_s[c]
        xd = xd_s[rows, :]
        bcb = b_s[rows, :].astype(BF16)
        cc = c_s[rows, :]
        g_mats = [_dot_nt(jnp.where(low_half, cc, 0.0).astype(BF16), bcb),
                  _dot_nt(jnp.where(low_half, 0.0, cc).astype(BF16), bcb)]
        y_parts = []
        for pair in range(SSD_HEADS // 2):
            g = pair // (SSD_HEADS // 2 // SSD_GROUPS)
            acc = None
            for d in range(N_DIR):
                keep = (ii >= jj) if d == 0 else (ii <= jj)
                s_mats = []
                for hh in (2 * pair, 2 * pair + 1):
                    col = d * SSD_HEADS + hh
                    diff = cum[:, col:col + 1] - cum_t[col:col + 1, :]
                    decay = jnp.exp(jnp.where(keep, diff, NEG_BIG))
                    s_mats.append((g_mats[g] * decay).astype(BF16))
                lhs = jnp.concatenate(s_mats, axis=1)
                xp = xd[:, d * HP + pair * LANES:d * HP + (pair + 1) * LANES]
                rhs = jnp.concatenate([jnp.where(low_half, xp, 0.0), jnp.where(low_half, 0.0, xp)],
                                      axis=0).astype(BF16)
                part = _dot(lhs, rhs)
                acc = part if acc is None else acc + part
            y_parts.append(acc)
        y_ref[rows, :] = jnp.concatenate(y_parts, axis=1)

    for c in chunks:
        local_pass(c)

    half = HP // SSD_GROUPS
    for s in range(n_seq):
        states = []
        for d in range(N_DIR):
            if zero_init:
                states.append(jnp.zeros((LANES, HP), F32))
            else:
                s0 = init_ref[s, d]
                states.append(jnp.where(own_t[:, :HP], jnp.concatenate([s0, s0], axis=1).T, 0.0))

        for k in range(nc):
            for d in range(N_DIR):
                c = s * nc + (k if d == 0 else nc - 1 - k)
                rows = slice(c * q, (c + 1) * q)
                lanes = slice(d * HP, (d + 1) * HP)
                y_off = _dot(c_s[rows, :].astype(BF16), states[d].astype(BF16)) * ec_s[rows, lanes]
                y_ref[rows, :] = y_ref[rows, :] + y_off
                states[d] = states[d] * et_s[c, 0:1, lanes] + dst_s[c, :, lanes]

        for d in range(N_DIR):
            st = states[d].T
            fin_ref[s, d, 0:half, :] = st[0:half, 0:SSD_STATE]
            fin_ref[s, d, half:HP, :] = st[half:HP, SSD_STATE:2 * SSD_STATE]

        for c in range(s * nc, (s + 1) * nc):
            rows = slice(c * q, (c + 1) * q)
            y = y_ref[rows, :] + xs_s[rows, :] * dexp_ref[...]
            y = y * _silu(z_ref[rows, :])
            ms = jnp.mean(y * y, axis=-1, keepdims=True)
            y_ref[rows, :] = y * lax.rsqrt(ms + RMS_EPS) * nw_ref[...]


SSD_STEP_ROWS = 1024


def _ssd(z, xbc, dt, row0, bsz, l, init, cw, cb, dtb, alog, dexp, nw, e_mat, *, seg):
    n_seq = SSD_STEP_ROWS // l
    rows = n_seq * l
    zero_init = init is None
    if zero_init:
        init = jnp.zeros((n_seq, N_DIR, HP, SSD_STATE), F32)
        init_map = lambda b: (0, 0, 0, 0)
    else:
        init_map = lambda b: (b, 0, 0, 0)
    blk0 = row0 // rows
    seq_spec = lambda n: pl.BlockSpec((rows, n), lambda b: (blk0 + b, 0))
    const = lambda *shape: pl.BlockSpec(shape, lambda b: (0,) * len(shape))
    nc = rows // SSD_CHUNK
    return pl.pallas_call(
        functools.partial(_ssd_kernel, l=l, seg=seg, zero_init=zero_init),
        grid=(bsz // n_seq,),
        in_specs=[seq_spec(SSD_W), seq_spec(SSD_XBC), pl.BlockSpec((N_DT, rows), lambda b: (0, blk0 + b)),
                  pl.BlockSpec((n_seq, N_DIR, HP, SSD_STATE), init_map),
                  const(3, SSD_XBC), const(1, SSD_XBC), const(N_DT, 1), const(N_DT, 1),
                  const(1, SSD_W), const(1, SSD_W), const(*e_mat.shape)],
        out_specs=[pl.BlockSpec((rows, SSD_W), lambda b: (b, 0)),
                   pl.BlockSpec((n_seq, N_DIR, HP, SSD_STATE), lambda b: (b, 0, 0, 0))],
        out_shape=[jax.ShapeDtypeStruct((bsz * l, SSD_W), F32),
                   jax.ShapeDtypeStruct((bsz, N_DIR, HP, SSD_STATE), F32)],
        scratch_shapes=[pltpu.VMEM((rows, SSD_W), F32), pltpu.VMEM((rows, LANES), F32), pltpu.VMEM((rows, LANES), F32),
                        pltpu.VMEM((rows, N_DIR * HP), F32),
                        pltpu.VMEM((nc, LANES, N_DIR * HP), F32), pltpu.VMEM((nc, SUBLANES, N_DIR * HP), F32),
                        pltpu.VMEM((nc, SSD_CHUNK, LANES), F32), pltpu.VMEM((nc, N_DT, SSD_CHUNK), F32),
                        pltpu.VMEM((rows, N_DIR * HP), F32)],
        compiler_params=_params("arbitrary"),
        name=f"ssd{l}",
    )(z, xbc, dt, init, cw, cb, dtb, alog, dexp, nw, e_mat)


HY_ROW_BLOCK = 16
HY_STEP_ROWS = 2048


def _hyena_kernel(v_ref, x1_ref, x2_ref, wv_ref, w1_ref, w2_ref, bv_ref, b1_ref, b2_ref,
                  f_ref, g_ref, kf_ref, kn_ref, skip_ref, o_ref, spec_s, prod_s, *, l, seg):
    m = l // 2
    tiles = o_ref.shape[0]
    n_seq = o_ref.shape[1] // l
    ct = tiles * LANES
    rb = HY_ROW_BLOCK
    first, last = _seg_edges(m, seg // 2)
    row0 = lax.broadcasted_iota(jnp.int32, (rb, 1), 0) == 0
    ev = slice(0, ct)
    od = slice(ct, 2 * ct)

    def conv_eo(x_ref, w_ref, b_ref, base):
        xe, xo = (jnp.concatenate([x_ref[k, pl.ds(base + p, m, stride=2), :] for k in range(tiles)], axis=1)
                  for p in range(2))
        w = w_ref[...]
        b = b_ref[...]
        xo_prev = jnp.where(first, 0.0, pltpu.roll(xo, 1, 0))
        xe_next = jnp.where(last, 0.0, pltpu.roll(xe, m - 1, 0))
        ce = xo_prev * w[0:1, :] + xe * w[1:2, :] + xo * w[2:3, :] + b
        co = xe * w[0:1, :] + xo * w[1:2, :] + xe_next * w[2:3, :] + b
        return jnp.concatenate([ce, co], axis=1)

    def pointwise(s, i, r0):
        re = slice(r0, r0 + rb)
        im = slice(m + r0, m + r0 + rb)
        er, orr = spec_s[s, re, ev], spec_s[s, re, od]
        ei, oi = spec_s[s, im, ev], spec_s[s, im, od]
        ker, kei, kor, koi, vr, vi = (kf_ref[i, p, re, :] for p in range(N_FILT_PLANES))
        if r0 == 0:
            e_n, o_n = ei[0:1, :], oi[0:1, :]
            ei = jnp.where(row0, 0.0, ei)
            oi = jnp.where(row0, 0.0, oi)
        pe_r = er * ker - ei * kei + orr * vr - oi * vi
        pe_i = er * kei + ei * ker + orr * vi + oi * vr
        po_r = er * kor - ei * koi + orr * ker - oi * kei
        po_i = er * koi + ei * kor + orr * kei + oi * ker
        if r0 == 0:
            kn = kn_ref[i]
            pe_i = jnp.where(row0, e_n * kn[0:1, :] + o_n * kn[2:3, :], pe_i)
            po_i = jnp.where(row0, e_n * kn[1:2, :] + o_n * kn[0:1, :], po_i)
        prod_s[s, re, ev] = pe_r.astype(BF16)
        prod_s[s, re, od] = po_r.astype(BF16)
        prod_s[s, im, ev] = pe_i.astype(BF16)
        prod_s[s, im, od] = po_i.astype(BF16)

    seqs = range(n_seq)
    zz = [conv_eo(v_ref, wv_ref, bv_ref, s * l) for s in seqs]
    for i, (xg_ref, wg_ref, bg_ref) in enumerate(((x1_ref, w1_ref, b1_ref), (x2_ref, w2_ref, b2_ref))):
        for s in seqs:
            spec_s[s] = _dot(f_ref[...], zz[s].astype(BF16))
        for s in seqs:
            for r0 in range(0, m, rb):
                pointwise(s, i, r0)
        skip = skip_ref[i:i + 1, :]
        skip2 = jnp.concatenate([skip, skip], axis=1)
        for s in seqs:
            conv = _dot(g_ref[...], prod_s[s])
            zz[s] = conv_eo(xg_ref, wg_ref, bg_ref, s * l) * (conv + zz[s] * skip2)
    for s in seqs:
        for k in range(tiles):
            o_ref[k, pl.ds(s * l, m, stride=2), :] = zz[s][:, k * LANES:(k + 1) * LANES]
            o_ref[k, pl.ds(s * l + 1, m, stride=2), :] = zz[s][:, ct + k * LANES:ct + (k + 1) * LANES]


def _hyena(hy, row0, bsz, l, n_seq, cw, cb, fmat, gmat, kf, kn, skip, *, seg):
    ct = HY_CH_TILE
    nct = HY_W // ct
    tiles = ct // LANES
    m = l // 2
    rows = n_seq * l
    blk0 = row0 // rows
    part = lambda p: pl.BlockSpec((tiles, rows, LANES), lambda j, b: (p * nct + j, blk0 + b, 0))
    wpart = lambda p: pl.BlockSpec((3, ct), lambda j, b: (0, p * nct + j))
    bpart = lambda p: pl.BlockSpec((1, ct), lambda j, b: (0, p * nct + j))
    return pl.pallas_call(
        functools.partial(_hyena_kernel, l=l, seg=seg),
        grid=(nct, bsz // n_seq),
        in_specs=[part(0), part(1), part(2), wpart(0), wpart(1), wpart(2), bpart(0), bpart(1), bpart(2),
                  pl.BlockSpec((l, m), lambda j, b: (0, 0)),
                  pl.BlockSpec((m, l), lambda j, b: (0, 0)),
                  pl.BlockSpec((HY_ORDER, N_FILT_PLANES, m, ct), lambda j, b: (0, 0, 0, j)),
                  pl.BlockSpec((HY_ORDER, SUBLANES, ct), lambda j, b: (0, 0, j)),
                  pl.BlockSpec((HY_ORDER, ct), lambda j, b: (0, j))],
        out_specs=pl.BlockSpec((tiles, rows, LANES), lambda j, b: (j, b, 0)),
        out_shape=jax.ShapeDtypeStruct((HY_W // LANES, bsz * l, LANES), F32),
        scratch_shapes=[pltpu.VMEM((n_seq, l, 2 * ct), F32), pltpu.VMEM((n_seq, l, 2 * ct), BF16)],
        compiler_params=_params("arbitrary", "arbitrary"),
        name=f"hyena{l}",
    )(hy, hy, hy, cw, cw, cw, cb, cb, cb, fmat, gmat, kf, kn, skip)


def _dft_mats(l):
    n = 2 * l
    f = np.arange(l, dtype=np.int64)[:, None]
    t = np.arange(l, dtype=np.int64)[None, :]
    ang = 2.0 * np.pi * ((f * t) % n).astype(np.float64) / n
    alt = np.where(np.arange(l) % 2 == 0, 1.0, -1.0)
    top = np.cos(ang)
    bot = -np.sin(ang)
    bot[0, :] = alt
    fwd = np.concatenate([top, bot], axis=0)
    wf = np.full((l,), 2.0)
    wf[0] = 1.0
    gtop = np.cos(ang).T * wf[None, :] / n
    gbot = -np.sin(ang).T * 2.0 / n
    gbot[:, 0] = alt / n
    inv = np.concatenate([gtop, gbot], axis=1)
    return fwd.astype(np.float32), inv.astype(np.float32)


def _filter_feats(l):
    t = np.linspace(0.0, 1.0, l)[:, None]
    w = (2.0 * np.pi / l) * np.arange(l, dtype=np.float64)[:, None]
    f = np.linspace(1e-4, HY_BANDS - 1, HY_BANDS)[None, :]
    feats = np.concatenate([t, np.cos(f * w), -np.sin(f * w)], axis=-1)
    out = np.zeros((l, EMB_PAD), np.float32)
    out[:, :HY_EMB] = feats
    return out[0::2], out[1::2]


def _shift_twiddles(l):
    theta = 2.0 * np.pi * np.arange(l // 2, dtype=np.float64) / l
    return np.stack([np.cos(theta), np.sin(theta)], axis=1).astype(np.float32)


def _head_expand():
    n_blocks = LANES // N_DT // 2
    e = np.zeros((n_blocks, LANES, N_DIR * HP), np.float32)
    for k in range(n_blocks):
        for j in range(N_DT):
            e[k, k * N_DT + j, j * SSD_HEAD_DIM:(j + 1) * SSD_HEAD_DIM] = 1.0
    return e


def kernel(x_prompt, x_sample, state_ssd, c, c_ctx, w_ada, b_ada, norm_ffn1, ffn1_w_gate, ffn1_w_up, ffn1_w_down, norm_mix, w_in, w_out, ssd_conv_w, ssd_conv_b, ssd_dt_bias, ssd_a_log, ssd_d, ssd_norm_w, hy_conv_w, hy_conv_b, hy_w1, hy_b1, hy_freq, hy_w2, hy_b2, hy_w3, hy_skip, norm_ffn2, ffn2_w_gate, ffn2_w_up, ffn2_w_down, norm_final):
    assert w_ada.shape[0] == 1, "single layer"
    n_ctx, l_ctx, _ = x_prompt.shape
    n_lat, l_lat, _ = x_sample.shape
    t_ctx = n_ctx * l_ctx

    cond = jnp.zeros((16, D_MODEL), F32).at[:n_lat].set(c).at[n_lat].set(c_ctx)
    mod = _ada(cond, w_ada, b_ada).reshape(16, N_MOD, D_MODEL)

    tr = lambda w: jnp.swapaxes(w, 1, 2)

    row = lambda v: v.reshape(1, -1)
    dtb, alog = ssd_dt_bias[0].reshape(N_DT, 1), ssd_a_log[0].reshape(N_DT, 1)
    dexp = jnp.repeat(ssd_d[0], SSD_HEAD_DIM).reshape(1, SSD_W)
    e_mat = jnp.asarray(_head_expand())
    w1p = jnp.pad(hy_w1[0], ((0, EMB_PAD - HY_EMB), (0, 0)))
    deltas = jnp.asarray(np.abs(np.linspace(HY_MIN_DECAY, HY_MAX_DECAY, HY_ORDER * HY_W))
                         .reshape(HY_ORDER, HY_W).astype(np.float32))

    (x1, z, xbc, dt, hy), (wg2, wu2, wd2), wo = _ffn_in(
        x_prompt.reshape(t_ctx, D_MODEL), x_sample.reshape(n_lat * l_lat, D_MODEL), mod, n_lat, l_lat,
        row(norm_ffn1[0]), row(norm_mix[0]), tr(ffn1_w_gate), tr(ffn1_w_up), ffn1_w_down, tr(w_in),
        [tr(ffn2_w_gate), tr(ffn2_w_up), ffn2_w_down], w_out)

    def mixers(row0, bsz, l, init, seg):
        fwd_np, inv_np = _dft_mats(l // 2)
        fmat = jnp.asarray(fwd_np).astype(BF16)
        gmat = jnp.asarray(inv_np).astype(BF16)
        feats_e, feats_o = _filter_feats(l)
        kf, kn = _filters(l, jnp.asarray(feats_e), jnp.asarray(feats_o), w1p, row(hy_b1[0]), row(hy_freq[0]),
                          hy_w2[0], row(hy_b2[0]), hy_w3[0], deltas, fmat, jnp.asarray(_shift_twiddles(l)))
        ys, fin = _ssd(z, xbc, dt, row0, bsz, l, init, ssd_conv_w[0], row(ssd_conv_b[0]), dtb, alog, dexp,
                       row(ssd_norm_w[0]), e_mat, seg=seg)
        yh = _hyena(hy, row0, bsz, l, HY_STEP_ROWS // l, hy_conv_w[0], row(hy_conv_b[0]), fmat, gmat, kf, kn,
                    hy_skip[0], seg=seg)
        return ys, yh, fin

    ys_ctx, yh_ctx, ctx_fin = mixers(0, n_ctx, l_ctx, None, l_ctx)
    lat_init = state_ssd[:, 0].reshape(n_lat, N_DIR, HP, SSD_STATE)
    ys_lat, yh_lat, _ = mixers(t_ctx, n_lat, l_lat, lat_init, GRID_W)

    y_ctx, y_lat = _out_ffn(x1, ys_ctx, ys_lat, yh_ctx, yh_lat, mod, n_lat, l_lat, row(norm_ffn2[0]),
                            row(norm_final), wo, wg2, wu2, wd2)
    new_state = ctx_fin.reshape(n_ctx, 1, N_DIR, SSD_HEADS, SSD_HEAD_DIM, SSD_STATE).astype(x_prompt.dtype)
    return (y_ctx.reshape(n_ctx, l_ctx, D_MODEL), y_lat.reshape(n_lat, l_lat, D_MODEL), new_state)
```

```python
import functools
import math

import numpy as np
import jax
import jax.numpy as jnp
from jax import lax
from jax.experimental import pallas as pl
from jax.experimental.pallas import tpu as pltpu

F32 = jnp.float32
BF16 = jnp.bfloat16

D_MODEL = 1024
GRID_W = 64
N_MOD = 9
RMS_EPS = 1e-6
FFN_DIM = 2752
SSD_W = 512
SSD_HEADS = 8
SSD_HEAD_DIM = 64
SSD_STATE = 64
SSD_GROUPS = 2
SSD_CHUNK = 128
SSD_XBC = SSD_W + 2 * SSD_GROUPS * SSD_STATE
N_DIR = 2
HY_W = 512
HY_ORDER = 2
HY_EMB = 33
HY_BANDS = (HY_EMB - 1) // 2
HY_HIDDEN = 64
HY_MIN_DECAY = math.log(1e-2) / 1.5
HY_MAX_DECAY = math.log(1e-2) / 0.3
IN_SPLITS = (SSD_W, SSD_W + SSD_XBC, SSD_W + SSD_XBC + N_DIR * SSD_HEADS)
IN_COLS = IN_SPLITS[-1] + (HY_ORDER + 1) * HY_W

LANES = 128
SUBLANES = 8
FFN_PAD = 2816
FFN_CHUNK = 256
N_DT = N_DIR * SSD_HEADS
DT_PAD = LANES
IN_ROWS = IN_COLS + DT_PAD - N_DIR * SSD_HEADS
EMB_PAD = LANES
TOKEN_TILE = 512
HY_CH_TILE = 256
VMEM_LIMIT = 60 * 1024 * 1024
NEG_BIG = -1e30
HP = SSD_HEADS * SSD_HEAD_DIM


def _silu(x):
    return x * jax.nn.sigmoid(x)


def _softplus(x):
    return jnp.maximum(x, 0.0) + jnp.log1p(jnp.exp(-jnp.abs(x)))


def _rms_mod(x, gain, shift, scale):
    ms = jnp.mean(x * x, axis=-1, keepdims=True)
    return x * lax.rsqrt(ms + RMS_EPS) * (gain * (1.0 + scale)) + shift


def _dot(a, b):
    return jnp.dot(a, b, preferred_element_type=F32)


def _dot_nt(a, b):
    return lax.dot_general(a, b, (((1,), (1,)), ((), ())), preferred_element_type=F32)


def _split3(x):
    hi = x.astype(BF16)
    r = x - hi.astype(F32)
    mid = r.astype(BF16)
    lo = (r - mid.astype(F32)).astype(BF16)
    return hi, mid, lo


def _dot_exact_lhs(x, m01):
    hi, mid, lo = _split3(x)
    return _dot(hi, m01) + _dot(mid, m01) + _dot(lo, m01)


def _seg_edges(l, seg):
    pos = lax.broadcasted_iota(jnp.int32, (l, 1), 0) & (seg - 1)
    return pos == 0, pos == seg - 1


def _swiglu_acc(h, wg_ref, wu_ref, wd_ref):
    acc = None
    for k in range(FFN_PAD // FFN_CHUNK):
        cols = slice(k * FFN_CHUNK, (k + 1) * FFN_CHUNK)
        g = _dot_nt(h, wg_ref[cols, :])
        u = _dot_nt(h, wu_ref[cols, :])
        a = (_silu(g) * u).astype(BF16)
        part = _dot(a, wd_ref[cols, :])
        acc = part if acc is None else acc + part
    return acc


def _params(*semantics):
    return pltpu.CompilerParams(dimension_semantics=semantics, vmem_limit_bytes=VMEM_LIMIT)


LOAD_ROWS = 256
LOAD_SLOTS = 4


def _row_pieces(src_lo, src_hi, dst_lo):
    return [(r, min(LOAD_ROWS, src_hi - r), dst_lo + r - src_lo) for r in range(src_lo, src_hi, LOAD_ROWS)]


def _load_weights_bf16(pieces, stage_ref, sem_ref):
    def load(k):
        src, r, n, _, _ = pieces[k]
        slot = k % LOAD_SLOTS
        return pltpu.make_async_copy(src.at[0, pl.ds(r, n), :], stage_ref.at[slot, pl.ds(0, n), :], sem_ref.at[slot])

    for k in range(min(LOAD_SLOTS, len(pieces))):
        load(k).start()
    for k, (_, _, n, dst, dr) in enumerate(pieces):
        load(k).wait()
        dst[dr:dr + n, :] = stage_ref[k % LOAD_SLOTS, 0:n, :].astype(BF16)
        if k + LOAD_SLOTS < len(pieces):
            load(k + LOAD_SLOTS).start()


def _ada_kernel(cond_ref, w_ref, b_ref, o_ref):
    s = _silu(cond_ref[...]).astype(BF16)
    o_ref[...] = _dot(s, w_ref[...].astype(BF16)) + b_ref[...]


def _ada(cond, w_ada, b_ada):
    rows = cond.shape[0]
    n = w_ada.shape[-1]
    tn = D_MODEL
    return pl.pallas_call(
        _ada_kernel,
        grid=(n // tn,),
        in_specs=[
            pl.BlockSpec((rows, D_MODEL), lambda j: (0, 0)),
            pl.BlockSpec((None, D_MODEL, tn), lambda j: (0, 0, j)),
            pl.BlockSpec((1, tn), lambda j: (0, j)),
        ],
        out_specs=pl.BlockSpec((rows, tn), lambda j: (0, j)),
        out_shape=jax.ShapeDtypeStruct((rows, n), F32),
        compiler_params=_params("arbitrary"),
        name="ada",
    )(cond, w_ada, b_ada)


N_FILT_PLANES = 6


def _filter_kernel(fe_ref, fo_ref, w1_ref, b1_ref, fr_ref, w2_ref, b2_ref, w3_ref, dl_ref, f_ref, tw_ref,
                   kf_ref, kn_ref, *, m):
    freq = fr_ref[...]
    w1 = w1_ref[...].astype(BF16)
    w2 = w2_ref[...].astype(BF16)

    def hidden(feats):
        h = jnp.sin(freq * (_dot(feats.astype(BF16), w1) + b1_ref[...]))
        h = jnp.sin(freq * (_dot(h.astype(BF16), w2) + b2_ref[...]))
        return h.astype(BF16)

    feats_e = fe_ref[...]
    feats_o = fo_ref[...]
    hb_e = hidden(feats_e)
    hb_o = hidden(feats_o)
    t_e = feats_e[:, 0:1]
    t_o = feats_o[:, 0:1]
    row = lax.broadcasted_iota(jnp.int32, (m, 1), 0)
    sign = jnp.where((row & 1) == 0, 1.0, -1.0)
    cos_t = tw_ref[:, 0:1]
    sin_t = tw_ref[:, 1:2]
    f_top = f_ref[0:m, :]
    f_bot = f_ref[m:2 * m, :]

    def spectrum(k):
        kb = k.astype(BF16)
        return _dot(f_top, kb), jnp.where(row == 0, 0.0, _dot(f_bot, kb)), jnp.sum(k * sign, axis=0, keepdims=True)

    for i in range(HY_ORDER):
        dl = dl_ref[i:i + 1, :]
        c0 = (0 * HY_ORDER + i) * HY_W
        c1 = (1 * HY_ORDER + i) * HY_W
        w3f = w3_ref[:, c0:c0 + HY_W].astype(BF16)
        w3b = w3_ref[:, c1:c1 + HY_W].astype(BF16)
        win_e = jnp.exp(-t_e * dl)
        win_o = jnp.exp(-t_o * dl)
        k0e = _dot(hb_e, w3f) * win_e
        k0o = _dot(hb_o, w3f) * win_o
        k1e = jnp.where(row == 0, 0.0, _dot(hb_e, w3b) * win_e)
        k1o = _dot(hb_o, w3b) * win_o
        ker, _, ken = spectrum(k0e + k1e)
        _, kei, _ = spectrum(k0e - k1e)
        ar, ai, an = spectrum(k0o)
        br, bi, bn = spectrum(k1o)
        kf_ref[i, 0] = ker
        kf_ref[i, 1] = kei
        kf_ref[i, 2] = ar + cos_t * br + sin_t * bi
        kf_ref[i, 3] = ai + sin_t * br - cos_t * bi
        kf_ref[i, 4] = cos_t * ar + sin_t * ai + br
        kf_ref[i, 5] = cos_t * ai - sin_t * ar - bi
        kn_ref[i] = jnp.concatenate([ken, an - bn, bn - an, jnp.zeros((SUBLANES - 3, HY_W), F32)], axis=0)


def _filters(l, feats_e, feats_o, w1p, b1, freq, w2, b2, w3, deltas, fmat, twid):
    m = l // 2
    full = lambda *shape: pl.BlockSpec(shape, lambda: (0,) * len(shape))
    return pl.pallas_call(
        functools.partial(_filter_kernel, m=m),
        in_specs=[full(m, EMB_PAD), full(m, EMB_PAD), full(EMB_PAD, HY_HIDDEN), full(1, HY_HIDDEN),
                  full(1, HY_HIDDEN), full(HY_HIDDEN, HY_HIDDEN), full(1, HY_HIDDEN),
                  full(HY_HIDDEN, N_DIR * HY_ORDER * HY_W), full(HY_ORDER, HY_W), full(l, m), full(m, 2)],
        out_specs=[full(HY_ORDER, N_FILT_PLANES, m, HY_W), full(HY_ORDER, SUBLANES, HY_W)],
        out_shape=[jax.ShapeDtypeStruct((HY_ORDER, N_FILT_PLANES, m, HY_W), F32),
                   jax.ShapeDtypeStruct((HY_ORDER, SUBLANES, HY_W), F32)],
        compiler_params=pltpu.CompilerParams(vmem_limit_bytes=VMEM_LIMIT),
        name=f"filt{l}",
    )(feats_e, feats_o, w1p, b1, freq, w2, b2, w3, deltas, fmat, twid)


def _const_spec(shape):
    return pl.BlockSpec(shape, lambda i: (0,) * len(shape), pipeline_mode=pl.Buffered(1))


def _group_specs(ctx_tiles, width):
    tm = TOKEN_TILE
    return [pl.BlockSpec((tm, width), lambda i: (jnp.minimum(i, ctx_tiles - 1), 0)),
            pl.BlockSpec((tm, width), lambda i: (jnp.maximum(i - ctx_tiles, 0), 0))]


def _mod_spec(ctx_tiles, n_lat, tiles_per_seq):
    return pl.BlockSpec((None, N_MOD, D_MODEL),
                        lambda i: (jnp.where(i < ctx_tiles, n_lat, (i - ctx_tiles) // tiles_per_seq), 0, 0))


NEXT_CAST_ROWS = 128
NEXT_CAST_ROWS_OUT = 64


def _ffn_in_kernel(xc_ref, xl_ref, mod_ref, n1_ref, nm_ref, wg_hbm, wu_hbm, wd_hbm, wi_hbm,
                   ng_ref, nu_ref, nd_ref, no_ref,
                   x1_ref, z_ref, xbc_ref, dt_ref, hy_ref, ngb_ref, nub_ref, ndb_ref, nob_ref,
                   wg_ref, wu_ref, wd_ref, wi_ref, stage_ref, sem_ref, *, ctx_tiles):
    @pl.when(pl.program_id(0) == 0)
    def _():
        o1, o2, o3 = IN_SPLITS
        pieces = []
        for src, dst in ((wg_hbm, wg_ref), (wu_hbm, wu_ref), (wd_hbm, wd_ref)):
            pieces += [(src, r, n, dst, dr) for r, n, dr in _row_pieces(0, FFN_DIM, 0)]
            dst[FFN_DIM:, :] = jnp.zeros((FFN_PAD - FFN_DIM, D_MODEL), BF16)
        pieces += [(wi_hbm, r, n, wi_ref, dr) for r, n, dr in _row_pieces(0, o3, 0) + _row_pieces(o3, IN_COLS, o2 + DT_PAD)]
        wi_ref[o3:o2 + DT_PAD, :] = jnp.zeros((o2 + DT_PAD - o3, D_MODEL), BF16)
        _load_weights_bf16(pieces, stage_ref, sem_ref)

    blk = jnp.minimum(pl.program_id(0), FFN_PAD // NEXT_CAST_ROWS - 1)
    wrow = blk * NEXT_CAST_ROWS + lax.broadcasted_iota(jnp.int32, (NEXT_CAST_ROWS, 1), 0)
    for src, dst in ((ng_ref, ngb_ref), (nu_ref, nub_ref), (nd_ref, ndb_ref)):
        dst[...] = jnp.where(wrow < FFN_DIM, src[...], 0.0).astype(BF16)
    nob_ref[...] = no_ref[...].astype(BF16)

    x = jnp.where(pl.program_id(0) < ctx_tiles, xc_ref[...], xl_ref[...])
    mod = mod_ref[...]
    h = _rms_mod(x, n1_ref[...], mod[0:1, :], mod[1:2, :]).astype(BF16)
    x1 = x + (0.5 * mod[2:3, :]) * _swiglu_acc(h, wg_ref, wu_ref, wd_ref)
    x1_ref[...] = x1
    h2 = _rms_mod(x1, nm_ref[...], mod[3:4, :], mod[4:5, :]).astype(BF16)
    o1, o2, _ = IN_SPLITS
    z_ref[...] = _dot_nt(h2, wi_ref[0:o1, :])
    xbc_ref[...] = _dot_nt(h2, wi_ref[o1:o2, :])
    dt_ref[...] = _dot_nt(wi_ref[o2:o2 + N_DT, :], h2)
    hy = _dot_nt(h2, wi_ref[o2 + DT_PAD:, :])
    for k in range(hy_ref.shape[0]):
        hy_ref[k] = hy[:, k * LANES:(k + 1) * LANES]


def _ffn_in(x_ctx, x_lat, mod, n_lat, l_lat, n1, nm, wg, wu, wd, wi, next_ffn, next_out):
    tm = TOKEN_TILE
    ctx_tiles = x_ctx.shape[0] // tm
    tokens = x_ctx.shape[0] + x_lat.shape[0]
    steps = tokens // tm
    ffn_blocks = FFN_PAD // NEXT_CAST_ROWS
    out_blocks = D_MODEL // NEXT_CAST_ROWS_OUT
    assert steps >= ffn_blocks and steps >= out_blocks
    row_spec = lambda n: pl.BlockSpec((tm, n), lambda i: (i, 0))
    widths = (D_MODEL, SSD_W, SSD_XBC)
    hy_tiles = (HY_ORDER + 1) * HY_W // LANES
    dt_spec = pl.BlockSpec((N_DT, tm), lambda i: (0, i))
    outs = pl.pallas_call(
        functools.partial(_ffn_in_kernel, ctx_tiles=ctx_tiles),
        grid=(steps,),
        in_specs=_group_specs(ctx_tiles, D_MODEL) + [
            _mod_spec(ctx_tiles, n_lat, l_lat // tm),
            _const_spec((1, D_MODEL)), _const_spec((1, D_MODEL))]
        + [pl.BlockSpec(memory_space=pl.ANY)] * 4
        + [pl.BlockSpec((None, NEXT_CAST_ROWS, D_MODEL), lambda i: (0, jnp.minimum(i, ffn_blocks - 1), 0))] * 3
        + [pl.BlockSpec((None, NEXT_CAST_ROWS_OUT, D_MODEL), lambda i: (0, jnp.minimum(i, out_blocks - 1), 0))],
        out_specs=[row_spec(n) for n in widths] + [dt_spec, pl.BlockSpec((hy_tiles, tm, LANES), lambda i: (0, i, 0))]
        + [pl.BlockSpec((NEXT_CAST_ROWS, D_MODEL), lambda i: (jnp.minimum(i, ffn_blocks - 1), 0))] * 3
        + [pl.BlockSpec((NEXT_CAST_ROWS_OUT, D_MODEL), lambda i: (jnp.minimum(i, out_blocks - 1), 0))],
        out_shape=[jax.ShapeDtypeStruct((tokens, n), F32) for n in widths]
        + [jax.ShapeDtypeStruct((N_DT, tokens), F32), jax.ShapeDtypeStruct((hy_tiles, tokens, LANES), F32)]
        + [jax.ShapeDtypeStruct((FFN_PAD, D_MODEL), BF16)] * 3 + [jax.ShapeDtypeStruct((D_MODEL, D_MODEL), BF16)],
        scratch_shapes=[pltpu.VMEM((FFN_PAD, D_MODEL), BF16)] * 3 + [
            pltpu.VMEM((IN_ROWS, D_MODEL), BF16), pltpu.VMEM((LOAD_SLOTS, LOAD_ROWS, D_MODEL), F32),
            pltpu.SemaphoreType.DMA((LOAD_SLOTS,))],
        compiler_params=_params("arbitrary"),
        name="ffn_in",
    )(x_ctx, x_lat, mod, n1, nm, wg, wu, wd, wi, *next_ffn, next_out)
    return outs[:5], outs[5:8], outs[8]


def _out_ffn_kernel(x1_ref, ysc_ref, ysl_ref, yhc_ref, yhl_ref, mod_ref, n3_ref, nf_ref,
                    wo_ref, wg_ref, wu_ref, wd_ref, oc_ref, ol_ref, *, ctx_tiles):
    is_ctx = pl.program_id(0) < ctx_tiles
    mod = mod_ref[...]
    ys = jnp.where(is_ctx, ysc_ref[...], ysl_ref[...])
    yh = [jnp.where(is_ctx, yhc_ref[k], yhl_ref[k]) for k in range(yhc_ref.shape[0])]
    y = jnp.concatenate([ys] + yh, axis=1).astype(BF16)
    x2 = x1_ref[...] + mod[5:6, :] * _dot(y, wo_ref[...])
    h = _rms_mod(x2, n3_ref[...], mod[6:7, :], mod[7:8, :]).astype(BF16)
    x3 = x2 + (0.5 * mod[8:9, :]) * _swiglu_acc(h, wg_ref, wu_ref, wd_ref)
    ms = jnp.mean(x3 * x3, axis=-1, keepdims=True)
    out = x3 * lax.rsqrt(ms + RMS_EPS) * nf_ref[...]

    @pl.when(is_ctx)
    def _():
        oc_ref[...] = out

    @pl.when(jnp.logical_not(is_ctx))
    def _():
        ol_ref[...] = out


def _out_ffn(x1, ys_ctx, ys_lat, yh_ctx, yh_lat, mod, n_lat, l_lat, n3, nf, wo, wg, wu, wd):
    tm = TOKEN_TILE
    ctx_tiles = ys_ctx.shape[0] // tm
    tokens = x1.shape[0]
    return pl.pallas_call(
        functools.partial(_out_ffn_kernel, ctx_tiles=ctx_tiles),
        grid=(tokens // tm,),
        in_specs=[pl.BlockSpec((tm, D_MODEL), lambda i: (i, 0))]
        + _group_specs(ctx_tiles, SSD_W) + [
            pl.BlockSpec((HY_W // LANES, tm, LANES), lambda i: (0, jnp.minimum(i, ctx_tiles - 1), 0)),
            pl.BlockSpec((HY_W // LANES, tm, LANES), lambda i: (0, jnp.maximum(i - ctx_tiles, 0), 0)),
            _mod_spec(ctx_tiles, n_lat, l_lat // tm),
            _const_spec((1, D_MODEL)), _const_spec((1, D_MODEL)),
            _const_spec(wo.shape), _const_spec(wg.shape), _const_spec(wu.shape), _const_spec(wd.shape)],
        out_specs=_group_specs(ctx_tiles, D_MODEL),
        out_shape=[jax.ShapeDtypeStruct((ys_ctx.shape[0], D_MODEL), F32),
                   jax.ShapeDtypeStruct((ys_lat.shape[0], D_MODEL), F32)],
        compiler_params=_params("arbitrary"),
        name="out_ffn",
    )(x1, ys_ctx, ys_lat, yh_ctx, yh_lat, mod, n3, nf, wo, wg, wu, wd)


def _ssd_kernel(z_ref, xbc_ref, dt_ref, init_ref, cw_ref, cb_ref, dtb_ref, alog_ref, dexp_ref, nw_ref, e_ref,
                y_ref, fin_ref,
                xs_s, b_s, c_s, ec_s, dst_s, et_s, cum_s, ct_s, xd_s,
                *, l, seg, zero_init):
    q = SSD_CHUNK
    nc = l // q
    n_seq = y_ref.shape[0] // l
    cr = max(seg, q)
    e_mats = [e_ref[k].astype(BF16) for k in range(e_ref.shape[0])]
    dt_t = _softplus(dt_ref[...] + dtb_ref[...])
    a_t = dt_t * (-jnp.exp(alog_ref[...]))

    first, last = _seg_edges(cr, seg)

    for r0 in range(0, n_seq * l, cr):
        rows = slice(r0, r0 + cr)
        x = xbc_ref[rows, :]
        w = cw_ref[...]
        prev = jnp.where(first, 0.0, pltpu.roll(x, 1, 0))
        nxt = jnp.where(last, 0.0, pltpu.roll(x, cr - 1, 0))
        u = _silu(prev * w[0:1, :] + x * w[1:2, :] + nxt * w[2:3, :] + cb_ref[...])
        xs_s[rows, :] = u[:, :SSD_W]
        b_s[rows, :] = u[:, SSD_W:SSD_W + LANES]
        c_s[rows, :] = u[:, SSD_W + LANES:]

    row_g = lax.broadcasted_iota(jnp.int32, (LANES, N_DIR * HP), 0) // SSD_STATE
    lane_g = (lax.broadcasted_iota(jnp.int32, (LANES, N_DIR * HP), 1) % HP) // (HP // SSD_GROUPS)
    own_t = row_g == lane_g

    ii = lax.broadcasted_iota(jnp.int32, (q, q), 0)
    jj = lax.broadcasted_iota(jnp.int32, (q, q), 1)
    tri_upper = (ii <= jj).astype(BF16)
    lane = lax.broadcasted_iota(jnp.int32, (q, LANES), 1)
    low_half = lane < SSD_STATE
    is_fwd_row = lax.broadcasted_iota(jnp.int32, (N_DT, 1), 0) < SSD_HEADS

    chunks = range(n_seq * nc)

    for c in chunks:
        rows = slice(c * q, (c + 1) * q)
        a_c = a_t[:, rows]
        cum_f = _dot_exact_lhs(a_c, tri_upper)
        tot_c = cum_f[:, q - 1:q]
        cum_t = jnp.where(is_fwd_row, cum_f, tot_c - cum_f + a_c)
        ct_s[c] = cum_t
        stack = jnp.concatenate([cum_t, jnp.exp(cum_t), jnp.exp(tot_c - cum_t), dt_t[:, rows],
                                 jnp.zeros((LANES - 4 * N_DT, q), F32)], axis=0)
        cum_s[c] = stack.T

    def expand_pass(c):
        rows = slice(c * q, (c + 1) * q)
        cum = cum_s[c]
        cum_b = cum.astype(BF16)
        ec_s[rows, :] = _dot(cum_b, e_mats[1])
        tot = jnp.where(lane[0:1, :] < SSD_HEADS, cum[q - 1:q, :], cum[0:1, :])
        tot = jnp.where(lane[0:1, :] < N_DT, tot, 0.0)
        et_s[c] = _dot_exact_lhs(jnp.broadcast_to(jnp.exp(tot), (SUBLANES, LANES)), e_mats[0])
        xs = xs_s[rows, :]
        xd = jnp.concatenate([xs, xs], axis=1) * _dot(cum_b, e_mats[3])
        xd_s[rows, :] = xd
        w = (xd * _dot(cum_b, e_mats[2])).astype(BF16)
        dst_s[c] = jnp.where(own_t, _dot(b_s[rows, :].T.astype(BF16), w), 0.0)

    def local_pass(c):
        rows = slice(c * q, (c + 1) * q)
        cum = cum_s[c]
        cum_t = ct_s[c]
        xd = xd_s[rows, :]
        bcb = b_s[rows, :].astype(BF16)
        cc = c_s[rows, :]
        g_mats = [_dot_nt(jnp.where(low_half, cc, 0.0).astype(BF16), bcb),
                  _dot_nt(jnp.where(low_half, 0.0, cc).astype(BF16), bcb)]
        y_parts = []
        for pair in range(SSD_HEADS // 2):
            g = pair // (SSD_HEADS // 2 // SSD_GROUPS)
            acc = None
            for d in range(N_DIR):
                keep = (ii >= jj) if d == 0 else (ii <= jj)
                s_mats = []
                for hh in (2 * pair, 2 * pair + 1):
                    col = d * SSD_HEADS + hh
                    diff = cum[:, col:col + 1] - cum_t[col:col + 1, :]
                    decay = jnp.exp(jnp.where(keep, diff, NEG_BIG))
                    s_mats.append((g_mats[g] * decay).astype(BF16))
                lhs = jnp.concatenate(s_mats, axis=1)
                xp = xd[:, d * HP + pair * LANES:d * HP + (pair + 1) * LANES]
                rhs = jnp.concatenate([jnp.where(low_half, xp, 0.0), jnp.where(low_half, 0.0, xp)],
                                      axis=0).astype(BF16)
                part = _dot(lhs, rhs)
                acc = part if acc is None else acc + part
            y_parts.append(acc)
        y_ref[rows, :] = jnp.concatenate(y_parts, axis=1)

    for c in chunks:
        expand_pass(c)
        local_pass(c)

    half = HP // SSD_GROUPS
    for s in range(n_seq):
        states = []
        for d in range(N_DIR):
            if zero_init:
                states.append(jnp.zeros((LANES, HP), F32))
            else:
                s0 = init_ref[s, d]
                states.append(jnp.where(own_t[:, :HP], jnp.concatenate([s0, s0], axis=1).T, 0.0))

        for k in range(nc):
            for d in range(N_DIR):
                c = s * nc + (k if d == 0 else nc - 1 - k)
                rows = slice(c * q, (c + 1) * q)
                lanes = slice(d * HP, (d + 1) * HP)
                if zero_init and k == 0:
                    states[d] = dst_s[c, :, lanes]
                    continue
                y_off = _dot(c_s[rows, :].astype(BF16), states[d].astype(BF16)) * ec_s[rows, lanes]
                y_ref[rows, :] = y_ref[rows, :] + y_off
                states[d] = states[d] * et_s[c, 0:1, lanes] + dst_s[c, :, lanes]

        for d in range(N_DIR):
            st = states[d].T
            fin_ref[s, d, 0:half, :] = st[0:half, 0:SSD_STATE]
            fin_ref[s, d, half:HP, :] = st[half:HP, SSD_STATE:2 * SSD_STATE]

        for c in range(s * nc, (s + 1) * nc):
            rows = slice(c * q, (c + 1) * q)
            y = y_ref[rows, :] + xs_s[rows, :] * dexp_ref[...]
            y = y * _silu(z_ref[rows, :])
            ms = jnp.mean(y * y, axis=-1, keepdims=True)
            y_ref[rows, :] = y * lax.rsqrt(ms + RMS_EPS) * nw_ref[...]


SSD_STEP_ROWS = 1024


def _ssd(z, xbc, dt, row0, bsz, l, init, cw, cb, dtb, alog, dexp, nw, e_mat, *, seg):
    n_seq = SSD_STEP_ROWS // l
    rows = n_seq * l
    zero_init = init is None
    if zero_init:
        init = jnp.zeros((n_seq, N_DIR, HP, SSD_STATE), F32)
        init_map = lambda b: (0, 0, 0, 0)
    else:
        init_map = lambda b: (b, 0, 0, 0)
    blk0 = row0 // rows
    seq_spec = lambda n: pl.BlockSpec((rows, n), lambda b: (blk0 + b, 0))
    const = lambda *shape: pl.BlockSpec(shape, lambda b: (0,) * len(shape))
    nc = rows // SSD_CHUNK
    return pl.pallas_call(
        functools.partial(_ssd_kernel, l=l, seg=seg, zero_init=zero_init),
        grid=(bsz // n_seq,),
        in_specs=[seq_spec(SSD_W), seq_spec(SSD_XBC), pl.BlockSpec((N_DT, rows), lambda b: (0, blk0 + b)),
                  pl.BlockSpec((n_seq, N_DIR, HP, SSD_STATE), init_map),
                  const(3, SSD_XBC), const(1, SSD_XBC), const(N_DT, 1), const(N_DT, 1),
                  const(1, SSD_W), const(1, SSD_W), const(*e_mat.shape)],
        out_specs=[pl.BlockSpec((rows, SSD_W), lambda b: (b, 0)),
                   pl.BlockSpec((n_seq, N_DIR, HP, SSD_STATE), lambda b: (b, 0, 0, 0))],
        out_shape=[jax.ShapeDtypeStruct((bsz * l, SSD_W), F32),
                   jax.ShapeDtypeStruct((bsz, N_DIR, HP, SSD_STATE), F32)],
        scratch_shapes=[pltpu.VMEM((rows, SSD_W), F32), pltpu.VMEM((rows, LANES), F32), pltpu.VMEM((rows, LANES), F32),
                        pltpu.VMEM((rows, N_DIR * HP), F32),
                        pltpu.VMEM((nc, LANES, N_DIR * HP), F32), pltpu.VMEM((nc, SUBLANES, N_DIR * HP), F32),
                        pltpu.VMEM((nc, SSD_CHUNK, LANES), F32), pltpu.VMEM((nc, N_DT, SSD_CHUNK), F32),
                        pltpu.VMEM((rows, N_DIR * HP), F32)],
        compiler_params=_params("arbitrary"),
        name=f"ssd{l}",
    )(z, xbc, dt, init, cw, cb, dtb, alog, dexp, nw, e_mat)


HY_ROW_BLOCK = 16
HY_STEP_ROWS = 2048


def _hyena_kernel(v_ref, x1_ref, x2_ref, wv_ref, w1_ref, w2_ref, bv_ref, b1_ref, b2_ref,
                  f_ref, g_ref, kf_ref, kn_ref, skip_ref, o_ref, spec_s, prod_s, *, l, seg):
    m = l // 2
    tiles = o_ref.shape[0]
    n_seq = o_ref.shape[1] // l
    ct = tiles * LANES
    rb = HY_ROW_BLOCK
    first, last = _seg_edges(m, seg // 2)
    row0 = lax.broadcasted_iota(jnp.int32, (rb, 1), 0) == 0
    ev = slice(0, ct)
    od = slice(ct, 2 * ct)

    def conv_eo(x_ref, w_ref, b_ref, base):
        xe, xo = (jnp.concatenate([x_ref[k, pl.ds(base + p, m, stride=2), :] for k in range(tiles)], axis=1)
                  for p in range(2))
        w = w_ref[...]
        b = b_ref[...]
        xo_prev = jnp.where(first, 0.0, pltpu.roll(xo, 1, 0))
        xe_next = jnp.where(last, 0.0, pltpu.roll(xe, m - 1, 0))
        ce = xo_prev * w[0:1, :] + xe * w[1:2, :] + xo * w[2:3, :] + b
        co = xe * w[0:1, :] + xo * w[1:2, :] + xe_next * w[2:3, :] + b
        return jnp.concatenate([ce, co], axis=1)

    def pointwise(s, i, r0):
        re = slice(r0, r0 + rb)
        im = slice(m + r0, m + r0 + rb)
        er, orr = spec_s[s, re, ev], spec_s[s, re, od]
        ei, oi = spec_s[s, im, ev], spec_s[s, im, od]
        ker, kei, kor, koi, vr, vi = (kf_ref[i, p, re, :] for p in range(N_FILT_PLANES))
        if r0 == 0:
            e_n, o_n = ei[0:1, :], oi[0:1, :]
            ei = jnp.where(row0, 0.0, ei)
            oi = jnp.where(row0, 0.0, oi)
        pe_r = er * ker - ei * kei + orr * vr - oi * vi
        pe_i = er * kei + ei * ker + orr * vi + oi * vr
        po_r = er * kor - ei * koi + orr * ker - oi * kei
        po_i = er * koi + ei * kor + orr * kei + oi * ker
        if r0 == 0:
            kn = kn_ref[i]
            pe_i = jnp.where(row0, e_n * kn[0:1, :] + o_n * kn[2:3, :], pe_i)
            po_i = jnp.where(row0, e_n * kn[1:2, :] + o_n * kn[0:1, :], po_i)
        prod_s[s, re, ev] = pe_r.astype(BF16)
        prod_s[s, re, od] = po_r.astype(BF16)
        prod_s[s, im, ev] = pe_i.astype(BF16)
        prod_s[s, im, od] = po_i.astype(BF16)

    seqs = range(n_seq)
    zz = [conv_eo(v_ref, wv_ref, bv_ref, s * l) for s in seqs]
    for i, (xg_ref, wg_ref, bg_ref) in enumerate(((x1_ref, w1_ref, b1_ref), (x2_ref, w2_ref, b2_ref))):
        for s in seqs:
            spec_s[s] = _dot(f_ref[...], zz[s].astype(BF16))
        for s in seqs:
            for r0 in range(0, m, rb):
                pointwise(s, i, r0)
        skip = skip_ref[i:i + 1, :]
        skip2 = jnp.concatenate([skip, skip], axis=1)
        for s in seqs:
            conv = _dot(g_ref[...], prod_s[s])
            zz[s] = conv_eo(xg_ref, wg_ref, bg_ref, s * l) * (conv + zz[s] * skip2)
    for s in seqs:
        for k in range(tiles):
            o_ref[k, pl.ds(s * l, m, stride=2), :] = zz[s][:, k * LANES:(k + 1) * LANES]
            o_ref[k, pl.ds(s * l + 1, m, stride=2), :] = zz[s][:, ct + k * LANES:ct + (k + 1) * LANES]


def _hyena(hy, row0, bsz, l, n_seq, cw, cb, fmat, gmat, kf, kn, skip, *, seg):
    ct = HY_CH_TILE
    nct = HY_W // ct
    tiles = ct // LANES
    m = l // 2
    rows = n_seq * l
    blk0 = row0 // rows
    part = lambda p: pl.BlockSpec((tiles, rows, LANES), lambda j, b: (p * nct + j, blk0 + b, 0))
    wpart = lambda p: pl.BlockSpec((3, ct), lambda j, b: (0, p * nct + j))
    bpart = lambda p: pl.BlockSpec((1, ct), lambda j, b: (0, p * nct + j))
    return pl.pallas_call(
        functools.partial(_hyena_kernel, l=l, seg=seg),
        grid=(nct, bsz // n_seq),
        in_specs=[part(0), part(1), part(2), wpart(0), wpart(1), wpart(2), bpart(0), bpart(1), bpart(2),
                  pl.BlockSpec((l, m), lambda j, b: (0, 0)),
                  pl.BlockSpec((m, l), lambda j, b: (0, 0)),
                  pl.BlockSpec((HY_ORDER, N_FILT_PLANES, m, ct), lambda j, b: (0, 0, 0, j)),
                  pl.BlockSpec((HY_ORDER, SUBLANES, ct), lambda j, b: (0, 0, j)),
                  pl.BlockSpec((HY_ORDER, ct), lambda j, b: (0, j))],
        out_specs=pl.BlockSpec((tiles, rows, LANES), lambda j, b: (j, b, 0)),
        out_shape=jax.ShapeDtypeStruct((HY_W // LANES, bsz * l, LANES), F32),
        scratch_shapes=[pltpu.VMEM((n_seq, l, 2 * ct), F32), pltpu.VMEM((n_seq, l, 2 * ct), BF16)],
        compiler_params=_params("arbitrary", "arbitrary"),
        name=f"hyena{l}",
    )(hy, hy, hy, cw, cw, cw, cb, cb, cb, fmat, gmat, kf, kn, skip)


def _dft_mats(l):
    n = 2 * l
    f = np.arange(l, dtype=np.int64)[:, None]
    t = np.arange(l, dtype=np.int64)[None, :]
    ang = 2.0 * np.pi * ((f * t) % n).astype(np.float64) / n
    alt = np.where(np.arange(l) % 2 == 0, 1.0, -1.0)
    top = np.cos(ang)
    bot = -np.sin(ang)
    bot[0, :] = alt
    fwd = np.concatenate([top, bot], axis=0)
    wf = np.full((l,), 2.0)
    wf[0] = 1.0
    gtop = np.cos(ang).T * wf[None, :] / n
    gbot = -np.sin(ang).T * 2.0 / n
    gbot[:, 0] = alt / n
    inv = np.concatenate([gtop, gbot], axis=1)
    return fwd.astype(np.float32), inv.astype(np.float32)


def _filter_feats(l):
    t = np.linspace(0.0, 1.0, l)[:, None]
    w = (2.0 * np.pi / l) * np.arange(l, dtype=np.float64)[:, None]
    f = np.linspace(1e-4, HY_BANDS - 1, HY_BANDS)[None, :]
    feats = np.concatenate([t, np.cos(f * w), -np.sin(f * w)], axis=-1)
    out = np.zeros((l, EMB_PAD), np.float32)
    out[:, :HY_EMB] = feats
    return out[0::2], out[1::2]


def _shift_twiddles(l):
    theta = 2.0 * np.pi * np.arange(l // 2, dtype=np.float64) / l
    return np.stack([np.cos(theta), np.sin(theta)], axis=1).astype(np.float32)


def _head_expand():
    n_blocks = LANES // N_DT // 2
    e = np.zeros((n_blocks, LANES, N_DIR * HP), np.float32)
    for k in range(n_blocks):
        for j in range(N_DT):
            e[k, k * N_DT + j, j * SSD_HEAD_DIM:(j + 1) * SSD_HEAD_DIM] = 1.0
    return e


def kernel(x_prompt, x_sample, state_ssd, c, c_ctx, w_ada, b_ada, norm_ffn1, ffn1_w_gate, ffn1_w_up, ffn1_w_down, norm_mix, w_in, w_out, ssd_conv_w, ssd_conv_b, ssd_dt_bias, ssd_a_log, ssd_d, ssd_norm_w, hy_conv_w, hy_conv_b, hy_w1, hy_b1, hy_freq, hy_w2, hy_b2, hy_w3, hy_skip, norm_ffn2, ffn2_w_gate, ffn2_w_up, ffn2_w_down, norm_final):
    assert w_ada.shape[0] == 1, "single layer"
    n_ctx, l_ctx, _ = x_prompt.shape
    n_lat, l_lat, _ = x_sample.shape
    t_ctx = n_ctx * l_ctx

    cond = jnp.zeros((16, D_MODEL), F32).at[:n_lat].set(c).at[n_lat].set(c_ctx)
    mod = _ada(cond, w_ada, b_ada).reshape(16, N_MOD, D_MODEL)

    tr = lambda w: jnp.swapaxes(w, 1, 2)

    row = lambda v: v.reshape(1, -1)
    dtb, alog = ssd_dt_bias[0].reshape(N_DT, 1), ssd_a_log[0].reshape(N_DT, 1)
    dexp = jnp.repeat(ssd_d[0], SSD_HEAD_DIM).reshape(1, SSD_W)
    e_mat = jnp.asarray(_head_expand())
    w1p = jnp.pad(hy_w1[0], ((0, EMB_PAD - HY_EMB), (0, 0)))
    deltas = jnp.asarray(np.abs(np.linspace(HY_MIN_DECAY, HY_MAX_DECAY, HY_ORDER * HY_W))
                         .reshape(HY_ORDER, HY_W).astype(np.float32))

    (x1, z, xbc, dt, hy), (wg2, wu2, wd2), wo = _ffn_in(
        x_prompt.reshape(t_ctx, D_MODEL), x_sample.reshape(n_lat * l_lat, D_MODEL), mod, n_lat, l_lat,
        row(norm_ffn1[0]), row(norm_mix[0]), tr(ffn1_w_gate), tr(ffn1_w_up), ffn1_w_down, tr(w_in),
        [tr(ffn2_w_gate), tr(ffn2_w_up), ffn2_w_down], w_out)

    def mixers(row0, bsz, l, init, seg):
        fwd_np, inv_np = _dft_mats(l // 2)
        fmat = jnp.asarray(fwd_np).astype(BF16)
        gmat = jnp.asarray(inv_np).astype(BF16)
        feats_e, feats_o = _filter_feats(l)
        kf, kn = _filters(l, jnp.asarray(feats_e), jnp.asarray(feats_o), w1p, row(hy_b1[0]), row(hy_freq[0]),
                          hy_w2[0], row(hy_b2[0]), hy_w3[0], deltas, fmat, jnp.asarray(_shift_twiddles(l)))
        ys, fin = _ssd(z, xbc, dt, row0, bsz, l, init, ssd_conv_w[0], row(ssd_conv_b[0]), dtb, alog, dexp,
                       row(ssd_norm_w[0]), e_mat, seg=seg)
        yh = _hyena(hy, row0, bsz, l, HY_STEP_ROWS // l, hy_conv_w[0], row(hy_conv_b[0]), fmat, gmat, kf, kn,
                    hy_skip[0], seg=seg)
        return ys, yh, fin

    ys_ctx, yh_ctx, ctx_fin = mixers(0, n_ctx, l_ctx, None, l_ctx)
    lat_init = state_ssd[:, 0].reshape(n_lat, N_DIR, HP, SSD_STATE)
    ys_lat, yh_lat, _ = mixers(t_ctx, n_lat, l_lat, lat_init, GRID_W)

    y_ctx, y_lat = _out_ffn(x1, ys_ctx, ys_lat, yh_ctx, yh_lat, mod, n_lat, l_lat, row(norm_ffn2[0]),
                            row(norm_final), wo, wg2, wu2, wd2)
    new_state = ctx_fin.reshape(n_ctx, 1, N_DIR, SSD_HEADS, SSD_HEAD_DIM, SSD_STATE).astype(x_prompt.dtype)
    return (y_ctx.reshape(n_ctx, l_ctx, D_MODEL), y_lat.reshape(n_lat, l_lat, D_MODEL), new_state)
```

```python
import functools
import math

import numpy as np
import jax
import jax.numpy as jnp
from jax import lax
from jax.experimental import pallas as pl
from jax.experimental.pallas import tpu as pltpu

F32 = jnp.float32
BF16 = jnp.bfloat16

D_MODEL = 1024
GRID_W = 64
N_MOD = 9
RMS_EPS = 1e-6
FFN_DIM = 2752
SSD_W = 512
SSD_HEADS = 8
SSD_HEAD_DIM = 64
SSD_STATE = 64
SSD_GROUPS = 2
SSD_CHUNK = 128
SSD_XBC = SSD_W + 2 * SSD_GROUPS * SSD_STATE
N_DIR = 2
HY_W = 512
HY_ORDER = 2
HY_EMB = 33
HY_BANDS = (HY_EMB - 1) // 2
HY_HIDDEN = 64
HY_MIN_DECAY = math.log(1e-2) / 1.5
HY_MAX_DECAY = math.log(1e-2) / 0.3
IN_SPLITS = (SSD_W, SSD_W + SSD_XBC, SSD_W + SSD_XBC + N_DIR * SSD_HEADS)
IN_COLS = IN_SPLITS[-1] + (HY_ORDER + 1) * HY_W

LANES = 128
SUBLANES = 8
FFN_PAD = 2816
FFN_CHUNK = 256
N_DT = N_DIR * SSD_HEADS
DT_PAD = LANES
IN_ROWS = IN_COLS + DT_PAD - N_DIR * SSD_HEADS
EMB_PAD = LANES
TOKEN_TILE = 512
HY_CH_TILE = 256
VMEM_LIMIT = 60 * 1024 * 1024
NEG_BIG = -1e30
HP = SSD_HEADS * SSD_HEAD_DIM


def _silu(x):
    return x * jax.nn.sigmoid(x)


def _softplus(x):
    return jnp.maximum(x, 0.0) + jnp.log1p(jnp.exp(-jnp.abs(x)))


def _rms_mod(x, gain, shift, scale):
    ms = jnp.mean(x * x, axis=-1, keepdims=True)
    return x * lax.rsqrt(ms + RMS_EPS) * (gain * (1.0 + scale)) + shift


def _dot(a, b):
    return jnp.dot(a, b, preferred_element_type=F32)


def _dot_nt(a, b):
    return lax.dot_general(a, b, (((1,), (1,)), ((), ())), preferred_element_type=F32)


def _split3(x):
    hi = x.astype(BF16)
    r = x - hi.astype(F32)
    mid = r.astype(BF16)
    lo = (r - mid.astype(F32)).astype(BF16)
    return hi, mid, lo


def _dot_exact_lhs(x, m01):
    hi, mid, lo = _split3(x)
    return _dot(hi, m01) + _dot(mid, m01) + _dot(lo, m01)


def _seg_edges(l, seg):
    pos = lax.broadcasted_iota(jnp.int32, (l, 1), 0) & (seg - 1)
    return pos == 0, pos == seg - 1


def _swiglu_acc(h, wg_ref, wu_ref, wd_ref):
    acc = None
    for k in range(FFN_PAD // FFN_CHUNK):
        cols = slice(k * FFN_CHUNK, (k + 1) * FFN_CHUNK)
        g = _dot_nt(h, wg_ref[cols, :])
        u = _dot_nt(h, wu_ref[cols, :])
        a = (_silu(g) * u).astype(BF16)
        part = _dot(a, wd_ref[cols, :])
        acc = part if acc is None else acc + part
    return acc


def _params(*semantics):
    return pltpu.CompilerParams(dimension_semantics=semantics, vmem_limit_bytes=VMEM_LIMIT)


LOAD_ROWS = 256
LOAD_SLOTS = 4


def _row_pieces(src_lo, src_hi, dst_lo):
    return [(r, min(LOAD_ROWS, src_hi - r), dst_lo + r - src_lo) for r in range(src_lo, src_hi, LOAD_ROWS)]


def _load_weights_bf16(pieces, stage_ref, sem_ref):
    def load(k):
        src, r, n, _, _ = pieces[k]
        slot = k % LOAD_SLOTS
        return pltpu.make_async_copy(src.at[0, pl.ds(r, n), :], stage_ref.at[slot, pl.ds(0, n), :], sem_ref.at[slot])

    for k in range(min(LOAD_SLOTS, len(pieces))):
        load(k).start()
    for k, (_, _, n, dst, dr) in enumerate(pieces):
        load(k).wait()
        dst[dr:dr + n, :] = stage_ref[k % LOAD_SLOTS, 0:n, :].astype(BF16)
        if k + LOAD_SLOTS < len(pieces):
            load(k + LOAD_SLOTS).start()


def _ada_kernel(cond_ref, w_ref, b_ref, o_ref):
    s = _silu(cond_ref[...]).astype(BF16)
    o_ref[...] = _dot(s, w_ref[...].astype(BF16)) + b_ref[...]


def _ada(cond, w_ada, b_ada):
    rows = cond.shape[0]
    n = w_ada.shape[-1]
    tn = D_MODEL
    return pl.pallas_call(
        _ada_kernel,
        grid=(n // tn,),
        in_specs=[
            pl.BlockSpec((rows, D_MODEL), lambda j: (0, 0)),
            pl.BlockSpec((None, D_MODEL, tn), lambda j: (0, 0, j)),
            pl.BlockSpec((1, tn), lambda j: (0, j)),
        ],
        out_specs=pl.BlockSpec((rows, tn), lambda j: (0, j)),
        out_shape=jax.ShapeDtypeStruct((rows, n), F32),
        compiler_params=_params("arbitrary"),
        name="ada",
    )(cond, w_ada, b_ada)


N_FILT_PLANES = 6


def _filter_kernel(fe_ref, fo_ref, w1_ref, b1_ref, fr_ref, w2_ref, b2_ref, w3_ref, dl_ref, f_ref, tw_ref,
                   kf_ref, kn_ref, *, m):
    freq = fr_ref[...]
    w1 = w1_ref[...].astype(BF16)
    w2 = w2_ref[...].astype(BF16)

    def hidden(feats):
        h = jnp.sin(freq * (_dot(feats.astype(BF16), w1) + b1_ref[...]))
        h = jnp.sin(freq * (_dot(h.astype(BF16), w2) + b2_ref[...]))
        return h.astype(BF16)

    feats_e = fe_ref[...]
    feats_o = fo_ref[...]
    hb_e = hidden(feats_e)
    hb_o = hidden(feats_o)
    t_e = feats_e[:, 0:1]
    t_o = feats_o[:, 0:1]
    row = lax.broadcasted_iota(jnp.int32, (m, 1), 0)
    sign = jnp.where((row & 1) == 0, 1.0, -1.0)
    cos_t = tw_ref[:, 0:1]
    sin_t = tw_ref[:, 1:2]
    f_top = f_ref[0:m, :]
    f_bot = f_ref[m:2 * m, :]

    def spectrum(k):
        kb = k.astype(BF16)
        return _dot(f_top, kb), jnp.where(row == 0, 0.0, _dot(f_bot, kb)), jnp.sum(k * sign, axis=0, keepdims=True)

    for i in range(HY_ORDER):
        dl = dl_ref[i:i + 1, :]
        c0 = (0 * HY_ORDER + i) * HY_W
        c1 = (1 * HY_ORDER + i) * HY_W
        w3f = w3_ref[:, c0:c0 + HY_W].astype(BF16)
        w3b = w3_ref[:, c1:c1 + HY_W].astype(BF16)
        win_e = jnp.exp(-t_e * dl)
        win_o = jnp.exp(-t_o * dl)
        k0e = _dot(hb_e, w3f) * win_e
        k0o = _dot(hb_o, w3f) * win_o
        k1e = jnp.where(row == 0, 0.0, _dot(hb_e, w3b) * win_e)
        k1o = _dot(hb_o, w3b) * win_o
        ker, _, ken = spectrum(k0e + k1e)
        _, kei, _ = spectrum(k0e - k1e)
        ar, ai, an = spectrum(k0o)
        br, bi, bn = spectrum(k1o)
        kf_ref[i, 0] = ker
        kf_ref[i, 1] = kei
        kf_ref[i, 2] = ar + cos_t * br + sin_t * bi
        kf_ref[i, 3] = ai + sin_t * br - cos_t * bi
        kf_ref[i, 4] = cos_t * ar + sin_t * ai + br
        kf_ref[i, 5] = cos_t * ai - sin_t * ar - bi
        kn_ref[i] = jnp.concatenate([ken, an - bn, bn - an, jnp.zeros((SUBLANES - 3, HY_W), F32)], axis=0)


def _filters(l, feats_e, feats_o, w1p, b1, freq, w2, b2, w3, deltas, fmat, twid):
    m = l // 2
    full = lambda *shape: pl.BlockSpec(shape, lambda: (0,) * len(shape))
    return pl.pallas_call(
        functools.partial(_filter_kernel, m=m),
        in_specs=[full(m, EMB_PAD), full(m, EMB_PAD), full(EMB_PAD, HY_HIDDEN), full(1, HY_HIDDEN),
                  full(1, HY_HIDDEN), full(HY_HIDDEN, HY_HIDDEN), full(1, HY_HIDDEN),
                  full(HY_HIDDEN, N_DIR * HY_ORDER * HY_W), full(HY_ORDER, HY_W), full(l, m), full(m, 2)],
        out_specs=[full(HY_ORDER, N_FILT_PLANES, m, HY_W), full(HY_ORDER, SUBLANES, HY_W)],
        out_shape=[jax.ShapeDtypeStruct((HY_ORDER, N_FILT_PLANES, m, HY_W), F32),
                   jax.ShapeDtypeStruct((HY_ORDER, SUBLANES, HY_W), F32)],
        compiler_params=pltpu.CompilerParams(vmem_limit_bytes=VMEM_LIMIT),
        name=f"filt{l}",
    )(feats_e, feats_o, w1p, b1, freq, w2, b2, w3, deltas, fmat, twid)


def _const_spec(shape):
    return pl.BlockSpec(shape, lambda i: (0,) * len(shape), pipeline_mode=pl.Buffered(1))


def _group_specs(ctx_tiles, width):
    tm = TOKEN_TILE
    return [pl.BlockSpec((tm, width), lambda i: (jnp.minimum(i, ctx_tiles - 1), 0)),
            pl.BlockSpec((tm, width), lambda i: (jnp.maximum(i - ctx_tiles, 0), 0))]


def _mod_spec(ctx_tiles, n_lat, tiles_per_seq):
    return pl.BlockSpec((None, N_MOD, D_MODEL),
                        lambda i: (jnp.where(i < ctx_tiles, n_lat, (i - ctx_tiles) // tiles_per_seq), 0, 0))


NEXT_CAST_ROWS = 128
NEXT_CAST_ROWS_OUT = 64


def _ffn_in_kernel(xc_ref, xl_ref, mod_ref, n1_ref, nm_ref, wg_hbm, wu_hbm, wd_hbm, wi_hbm,
                   ng_ref, nu_ref, nd_ref, no_ref,
                   x1_ref, z_ref, xbc_ref, dt_ref, hy_ref, ngb_ref, nub_ref, ndb_ref, nob_ref,
                   wg_ref, wu_ref, wd_ref, wi_ref, stage_ref, sem_ref, *, ctx_tiles):
    @pl.when(pl.program_id(0) == 0)
    def _():
        o1, o2, o3 = IN_SPLITS
        pieces = []
        for src, dst in ((wg_hbm, wg_ref), (wu_hbm, wu_ref), (wd_hbm, wd_ref)):
            pieces += [(src, r, n, dst, dr) for r, n, dr in _row_pieces(0, FFN_DIM, 0)]
            dst[FFN_DIM:, :] = jnp.zeros((FFN_PAD - FFN_DIM, D_MODEL), BF16)
        pieces += [(wi_hbm, r, n, wi_ref, dr) for r, n, dr in _row_pieces(0, o3, 0) + _row_pieces(o3, IN_COLS, o2 + DT_PAD)]
        wi_ref[o3:o2 + DT_PAD, :] = jnp.zeros((o2 + DT_PAD - o3, D_MODEL), BF16)
        _load_weights_bf16(pieces, stage_ref, sem_ref)

    blk = jnp.minimum(pl.program_id(0), FFN_PAD // NEXT_CAST_ROWS - 1)
    wrow = blk * NEXT_CAST_ROWS + lax.broadcasted_iota(jnp.int32, (NEXT_CAST_ROWS, 1), 0)
    for src, dst in ((ng_ref, ngb_ref), (nu_ref, nub_ref), (nd_ref, ndb_ref)):
        dst[...] = jnp.where(wrow < FFN_DIM, src[...], 0.0).astype(BF16)
    nob_ref[...] = no_ref[...].astype(BF16)

    x = jnp.where(pl.program_id(0) < ctx_tiles, xc_ref[...], xl_ref[...])
    mod = mod_ref[...]
    h = _rms_mod(x, n1_ref[...], mod[0:1, :], mod[1:2, :]).astype(BF16)
    x1 = x + (0.5 * mod[2:3, :]) * _swiglu_acc(h, wg_ref, wu_ref, wd_ref)
    x1_ref[...] = x1
    h2 = _rms_mod(x1, nm_ref[...], mod[3:4, :], mod[4:5, :]).astype(BF16)
    o1, o2, _ = IN_SPLITS
    z_ref[...] = _dot_nt(h2, wi_ref[0:o1, :])
    xbc_ref[...] = _dot_nt(h2, wi_ref[o1:o2, :])
    dt_ref[...] = _dot_nt(wi_ref[o2:o2 + N_DT, :], h2)
    hy = _dot_nt(h2, wi_ref[o2 + DT_PAD:, :])
    for k in range(hy_ref.shape[0]):
        hy_ref[k] = hy[:, k * LANES:(k + 1) * LANES]


def _ffn_in(x_ctx, x_lat, mod, n_lat, l_lat, n1, nm, wg, wu, wd, wi, next_ffn, next_out):
    tm = TOKEN_TILE
    ctx_tiles = x_ctx.shape[0] // tm
    tokens = x_ctx.shape[0] + x_lat.shape[0]
    steps = tokens // tm
    ffn_blocks = FFN_PAD // NEXT_CAST_ROWS
    out_blocks = D_MODEL // NEXT_CAST_ROWS_OUT
    assert steps >= ffn_blocks and steps >= out_blocks
    row_spec = lambda n: pl.BlockSpec((tm, n), lambda i: (i, 0))
    widths = (D_MODEL, SSD_W, SSD_XBC)
    hy_tiles = (HY_ORDER + 1) * HY_W // LANES
    dt_spec = pl.BlockSpec((N_DT, tm), lambda i: (0, i))
    outs = pl.pallas_call(
        functools.partial(_ffn_in_kernel, ctx_tiles=ctx_tiles),
        grid=(steps,),
        in_specs=_group_specs(ctx_tiles, D_MODEL) + [
            _mod_spec(ctx_tiles, n_lat, l_lat // tm),
            _const_spec((1, D_MODEL)), _const_spec((1, D_MODEL))]
        + [pl.BlockSpec(memory_space=pl.ANY)] * 4
        + [pl.BlockSpec((None, NEXT_CAST_ROWS, D_MODEL), lambda i: (0, jnp.minimum(i, ffn_blocks - 1), 0))] * 3
        + [pl.BlockSpec((None, NEXT_CAST_ROWS_OUT, D_MODEL), lambda i: (0, jnp.minimum(i, out_blocks - 1), 0))],
        out_specs=[row_spec(n) for n in widths] + [dt_spec, pl.BlockSpec((hy_tiles, tm, LANES), lambda i: (0, i, 0))]
        + [pl.BlockSpec((NEXT_CAST_ROWS, D_MODEL), lambda i: (jnp.minimum(i, ffn_blocks - 1), 0))] * 3
        + [pl.BlockSpec((NEXT_CAST_ROWS_OUT, D_MODEL), lambda i: (jnp.minimum(i, out_blocks - 1), 0))],
        out_shape=[jax.ShapeDtypeStruct((tokens, n), F32) for n in widths]
        + [jax.ShapeDtypeStruct((N_DT, tokens), F32), jax.ShapeDtypeStruct((hy_tiles, tokens, LANES), F32)]
        + [jax.ShapeDtypeStruct((FFN_PAD, D_MODEL), BF16)] * 3 + [jax.ShapeDtypeStruct((D_MODEL, D_MODEL), BF16)],
        scratch_shapes=[pltpu.VMEM((FFN_PAD, D_MODEL), BF16)] * 3 + [
            pltpu.VMEM((IN_ROWS, D_MODEL), BF16), pltpu.VMEM((LOAD_SLOTS, LOAD_ROWS, D_MODEL), F32),
            pltpu.SemaphoreType.DMA((LOAD_SLOTS,))],
        compiler_params=_params("arbitrary"),
        name="ffn_in",
    )(x_ctx, x_lat, mod, n1, nm, wg, wu, wd, wi, *next_ffn, next_out)
    return outs[:5], outs[5:8], outs[8]


def _out_ffn_kernel(x1_ref, ys_ref, yh_ref, mod_ref, n3_ref, nf_ref,
                    wo_ref, wg_ref, wu_ref, wd_ref, oc_ref, ol_ref, *, ctx_tiles):
    is_ctx = pl.program_id(0) < ctx_tiles
    mod = mod_ref[...]
    y = jnp.concatenate([ys_ref[...]] + [yh_ref[k] for k in range(yh_ref.shape[0])], axis=1).astype(BF16)
    x2 = x1_ref[...] + mod[5:6, :] * _dot(y, wo_ref[...])
    h = _rms_mod(x2, n3_ref[...], mod[6:7, :], mod[7:8, :]).astype(BF16)
    x3 = x2 + (0.5 * mod[8:9, :]) * _swiglu_acc(h, wg_ref, wu_ref, wd_ref)
    ms = jnp.mean(x3 * x3, axis=-1, keepdims=True)
    out = x3 * lax.rsqrt(ms + RMS_EPS) * nf_ref[...]

    @pl.when(is_ctx)
    def _():
        oc_ref[...] = out

    @pl.when(jnp.logical_not(is_ctx))
    def _():
        ol_ref[...] = out


def _out_ffn(x1, ys, yh, t_ctx, mod, n_lat, l_lat, n3, nf, wo, wg, wu, wd):
    tm = TOKEN_TILE
    ctx_tiles = t_ctx // tm
    tokens = x1.shape[0]
    return pl.pallas_call(
        functools.partial(_out_ffn_kernel, ctx_tiles=ctx_tiles),
        grid=(tokens // tm,),
        in_specs=[pl.BlockSpec((tm, D_MODEL), lambda i: (i, 0)), pl.BlockSpec((tm, SSD_W), lambda i: (i, 0)),
                  pl.BlockSpec((HY_W // LANES, tm, LANES), lambda i: (0, i, 0)),
            _mod_spec(ctx_tiles, n_lat, l_lat // tm),
            _const_spec((1, D_MODEL)), _const_spec((1, D_MODEL)),
            _const_spec(wo.shape), _const_spec(wg.shape), _const_spec(wu.shape), _const_spec(wd.shape)],
        out_specs=_group_specs(ctx_tiles, D_MODEL),
        out_shape=[jax.ShapeDtypeStruct((t_ctx, D_MODEL), F32),
                   jax.ShapeDtypeStruct((tokens - t_ctx, D_MODEL), F32)],
        compiler_params=_params("arbitrary"),
        name="out_ffn",
    )(x1, ys, yh, mod, n3, nf, wo, wg, wu, wd)


def _ssd_kernel(z_ref, xbc_ref, dt_ref, init_ref, cw_ref, cb_ref, dtb_ref, alog_ref, dexp_ref, nw_ref, e_ref,
                y_ref, fin_ref,
                xs_s, b_s, c_s, ec_s, dst_s, et_s, cum_s, ct_s, xd_s,
                *, l, seg, zero_init, write_final):
    q = SSD_CHUNK
    nc = l // q
    n_seq = y_ref.shape[0] // l
    cr = max(seg, q)
    e_mats = [e_ref[k].astype(BF16) for k in range(e_ref.shape[0])]
    dt_t = _softplus(dt_ref[...] + dtb_ref[...])
    a_t = dt_t * (-jnp.exp(alog_ref[...]))

    first, last = _seg_edges(cr, seg)

    for r0 in range(0, n_seq * l, cr):
        rows = slice(r0, r0 + cr)
        x = xbc_ref[rows, :]
        w = cw_ref[...]
        prev = jnp.where(first, 0.0, pltpu.roll(x, 1, 0))
        nxt = jnp.where(last, 0.0, pltpu.roll(x, cr - 1, 0))
        u = _silu(prev * w[0:1, :] + x * w[1:2, :] + nxt * w[2:3, :] + cb_ref[...])
        xs_s[rows, :] = u[:, :SSD_W]
        b_s[rows, :] = u[:, SSD_W:SSD_W + LANES]
        c_s[rows, :] = u[:, SSD_W + LANES:]

    row_g = lax.broadcasted_iota(jnp.int32, (LANES, N_DIR * HP), 0) // SSD_STATE
    lane_g = (lax.broadcasted_iota(jnp.int32, (LANES, N_DIR * HP), 1) % HP) // (HP // SSD_GROUPS)
    own_t = row_g == lane_g

    ii = lax.broadcasted_iota(jnp.int32, (q, q), 0)
    jj = lax.broadcasted_iota(jnp.int32, (q, q), 1)
    tri_upper = (ii <= jj).astype(BF16)
    lane = lax.broadcasted_iota(jnp.int32, (q, LANES), 1)
    low_half = lane < SSD_STATE
    is_fwd_row = lax.broadcasted_iota(jnp.int32, (N_DT, 1), 0) < SSD_HEADS

    chunks = range(n_seq * nc)

    for c in chunks:
        rows = slice(c * q, (c + 1) * q)
        a_c = a_t[:, rows]
        cum_f = _dot_exact_lhs(a_c, tri_upper)
        tot_c = cum_f[:, q - 1:q]
        cum_t = jnp.where(is_fwd_row, cum_f, tot_c - cum_f + a_c)
        ct_s[c] = cum_t
        stack = jnp.concatenate([cum_t, jnp.exp(cum_t), jnp.exp(tot_c - cum_t), dt_t[:, rows],
                                 jnp.zeros((LANES - 4 * N_DT, q), F32)], axis=0)
        cum_s[c] = stack.T

    def expand_pass(c):
        rows = slice(c * q, (c + 1) * q)
        cum = cum_s[c]
        cum_b = cum.astype(BF16)
        ec_s[rows, :] = _dot(cum_b, e_mats[1])
        tot = jnp.where(lane[0:1, :] < SSD_HEADS, cum[q - 1:q, :], cum[0:1, :])
        tot = jnp.where(lane[0:1, :] < N_DT, tot, 0.0)
        et_s[c] = _dot_exact_lhs(jnp.broadcast_to(jnp.exp(tot), (SUBLANES, LANES)), e_mats[0])
        xs = xs_s[rows, :]
        xd = jnp.concatenate([xs, xs], axis=1) * _dot(cum_b, e_mats[3])
        xd_s[rows, :] = xd
        w = (xd * _dot(cum_b, e_mats[2])).astype(BF16)
        dst_s[c] = jnp.where(own_t, _dot(b_s[rows, :].T.astype(BF16), w), 0.0)

    def local_pass(c):
        rows = slice(c * q, (c + 1) * q)
        cum = cum_s[c]
        cum_t = ct_s[c]
        xd = xd_s[rows, :]
        bcb = b_s[rows, :].astype(BF16)
        cc = c_s[rows, :]
        g_mats = [_dot_nt(jnp.where(low_half, cc, 0.0).astype(BF16), bcb),
                  _dot_nt(jnp.where(low_half, 0.0, cc).astype(BF16), bcb)]
        y_parts = []
        for pair in range(SSD_HEADS // 2):
            g = pair // (SSD_HEADS // 2 // SSD_GROUPS)
            acc = None
            for d in range(N_DIR):
                keep = (ii >= jj) if d == 0 else (ii <= jj)
                s_mats = []
                for hh in (2 * pair, 2 * pair + 1):
                    col = d * SSD_HEADS + hh
                    diff = cum[:, col:col + 1] - cum_t[col:col + 1, :]
                    decay = jnp.exp(jnp.where(keep, diff, NEG_BIG))
                    s_mats.append((g_mats[g] * decay).astype(BF16))
                lhs = jnp.concatenate(s_mats, axis=1)
                xp = xd[:, d * HP + pair * LANES:d * HP + (pair + 1) * LANES]
                rhs = jnp.concatenate([jnp.where(low_half, xp, 0.0), jnp.where(low_half, 0.0, xp)],
                                      axis=0).astype(BF16)
                part = _dot(lhs, rhs)
                acc = part if acc is None else acc + part
            y_parts.append(acc)
        y_ref[rows, :] = jnp.concatenate(y_parts, axis=1)

    for c in chunks:
        expand_pass(c)
        local_pass(c)

    half = HP // SSD_GROUPS
    for s in range(n_seq):
        states = []
        for d in range(N_DIR):
            if zero_init:
                states.append(jnp.zeros((LANES, HP), F32))
            else:
                s0 = init_ref[s, d]
                states.append(jnp.where(own_t[:, :HP], jnp.concatenate([s0, s0], axis=1).T, 0.0))

        for k in range(nc):
            for d in range(N_DIR):
                c = s * nc + (k if d == 0 else nc - 1 - k)
                rows = slice(c * q, (c + 1) * q)
                lanes = slice(d * HP, (d + 1) * HP)
                if zero_init and k == 0:
                    states[d] = dst_s[c, :, lanes]
                    continue
                y_off = _dot(c_s[rows, :].astype(BF16), states[d].astype(BF16)) * ec_s[rows, lanes]
                y_ref[rows, :] = y_ref[rows, :] + y_off
                states[d] = states[d] * et_s[c, 0:1, lanes] + dst_s[c, :, lanes]

        for d in range(N_DIR if write_final else 0):
            st = states[d].T
            fin_ref[s, d, 0:half, :] = st[0:half, 0:SSD_STATE]
            fin_ref[s, d, half:HP, :] = st[half:HP, SSD_STATE:2 * SSD_STATE]

        for c in range(s * nc, (s + 1) * nc):
            rows = slice(c * q, (c + 1) * q)
            y = y_ref[rows, :] + xs_s[rows, :] * dexp_ref[...]
            y = y * _silu(z_ref[rows, :])
            ms = jnp.mean(y * y, axis=-1, keepdims=True)
            y_ref[rows, :] = y * lax.rsqrt(ms + RMS_EPS) * nw_ref[...]


SSD_STEP_ROWS = 1024


def _ssd_groups_kernel(*refs, ctx_steps, ctx, lat):
    @pl.when(pl.program_id(0) < ctx_steps)
    def _():
        _ssd_kernel(*refs, l=ctx[0], seg=ctx[1], zero_init=True, write_final=True)

    @pl.when(pl.program_id(0) >= ctx_steps)
    def _():
        _ssd_kernel(*refs, l=lat[0], seg=lat[1], zero_init=False, write_final=False)


def _ssd(z, xbc, dt, n_ctx, l_ctx, seg_ctx, l_lat, seg_lat, init_lat, cw, cb, dtb, alog, dexp, nw, e_mat):
    rows = SSD_STEP_ROWS
    tokens = z.shape[0]
    ctx_steps = n_ctx * l_ctx // rows
    ctx_seqs, lat_seqs = rows // l_ctx, rows // l_lat
    seq_spec = lambda n: pl.BlockSpec((rows, n), lambda b: (b, 0))
    const = lambda *shape: pl.BlockSpec(shape, lambda b: (0,) * len(shape))
    nc = rows // SSD_CHUNK
    return pl.pallas_call(
        functools.partial(_ssd_groups_kernel, ctx_steps=ctx_steps, ctx=(l_ctx, seg_ctx), lat=(l_lat, seg_lat)),
        grid=(tokens // rows,),
        in_specs=[seq_spec(SSD_W), seq_spec(SSD_XBC), pl.BlockSpec((N_DT, rows), lambda b: (0, b)),
                  pl.BlockSpec((lat_seqs, N_DIR, HP, SSD_STATE), lambda b: (jnp.maximum(b - ctx_steps, 0), 0, 0, 0)),
                  const(3, SSD_XBC), const(1, SSD_XBC), const(N_DT, 1), const(N_DT, 1),
                  const(1, SSD_W), const(1, SSD_W), const(*e_mat.shape)],
        out_specs=[pl.BlockSpec((rows, SSD_W), lambda b: (b, 0)),
                   pl.BlockSpec((ctx_seqs, N_DIR, HP, SSD_STATE), lambda b: (jnp.minimum(b, ctx_steps - 1), 0, 0, 0))],
        out_shape=[jax.ShapeDtypeStruct((tokens, SSD_W), F32),
                   jax.ShapeDtypeStruct((n_ctx, N_DIR, HP, SSD_STATE), F32)],
        scratch_shapes=[pltpu.VMEM((rows, SSD_W), F32), pltpu.VMEM((rows, LANES), F32), pltpu.VMEM((rows, LANES), F32),
                        pltpu.VMEM((rows, N_DIR * HP), F32),
                        pltpu.VMEM((nc, LANES, N_DIR * HP), F32), pltpu.VMEM((nc, SUBLANES, N_DIR * HP), F32),
                        pltpu.VMEM((nc, SSD_CHUNK, LANES), F32), pltpu.VMEM((nc, N_DT, SSD_CHUNK), F32),
                        pltpu.VMEM((rows, N_DIR * HP), F32)],
        compiler_params=_params("arbitrary"),
        name="ssd",
    )(z, xbc, dt, init_lat, cw, cb, dtb, alog, dexp, nw, e_mat)


HY_ROW_BLOCK = 16
HY_STEP_ROWS = 2048


def _hyena_kernel(v_ref, x1_ref, x2_ref, wv_ref, w1_ref, w2_ref, bv_ref, b1_ref, b2_ref,
                  f_ref, g_ref, kf_ref, kn_ref, skip_ref, o_ref, spec_s, prod_s, *, l, seg):
    m = l // 2
    tiles = o_ref.shape[0]
    n_seq = o_ref.shape[1] // l
    ct = tiles * LANES
    rb = HY_ROW_BLOCK
    first, last = _seg_edges(m, seg // 2)
    row0 = lax.broadcasted_iota(jnp.int32, (rb, 1), 0) == 0
    ev = slice(0, ct)
    od = slice(ct, 2 * ct)

    def conv_eo(x_ref, w_ref, b_ref, base):
        xe, xo = (jnp.concatenate([x_ref[k, pl.ds(base + p, m, stride=2), :] for k in range(tiles)], axis=1)
                  for p in range(2))
        w = w_ref[...]
        b = b_ref[...]
        xo_prev = jnp.where(first, 0.0, pltpu.roll(xo, 1, 0))
        xe_next = jnp.where(last, 0.0, pltpu.roll(xe, m - 1, 0))
        ce = xo_prev * w[0:1, :] + xe * w[1:2, :] + xo * w[2:3, :] + b
        co = xe * w[0:1, :] + xo * w[1:2, :] + xe_next * w[2:3, :] + b
        return jnp.concatenate([ce, co], axis=1)

    def pointwise(s, i, r0):
        re = slice(s * l + r0, s * l + r0 + rb)
        im = slice(s * l + m + r0, s * l + m + r0 + rb)
        er, orr = spec_s[re, ev], spec_s[re, od]
        ei, oi = spec_s[im, ev], spec_s[im, od]
        ker, kei, kor, koi, vr, vi = (kf_ref[i, p, r0:r0 + rb, :] for p in range(N_FILT_PLANES))
        if r0 == 0:
            e_n, o_n = ei[0:1, :], oi[0:1, :]
            ei = jnp.where(row0, 0.0, ei)
            oi = jnp.where(row0, 0.0, oi)
        pe_r = er * ker - ei * kei + orr * vr - oi * vi
        pe_i = er * kei + ei * ker + orr * vi + oi * vr
        po_r = er * kor - ei * koi + orr * ker - oi * kei
        po_i = er * koi + ei * kor + orr * kei + oi * ker
        if r0 == 0:
            kn = kn_ref[i]
            pe_i = jnp.where(row0, e_n * kn[0:1, :] + o_n * kn[2:3, :], pe_i)
            po_i = jnp.where(row0, e_n * kn[1:2, :] + o_n * kn[0:1, :], po_i)
        prod_s[re, ev] = pe_r.astype(BF16)
        prod_s[re, od] = po_r.astype(BF16)
        prod_s[im, ev] = pe_i.astype(BF16)
        prod_s[im, od] = po_i.astype(BF16)

    seqs = range(n_seq)
    zz = [conv_eo(v_ref, wv_ref, bv_ref, s * l) for s in seqs]
    for i, (xg_ref, wg_ref, bg_ref) in enumerate(((x1_ref, w1_ref, b1_ref), (x2_ref, w2_ref, b2_ref))):
        for s in seqs:
            spec_s[s * l:(s + 1) * l, :] = _dot(f_ref[...], zz[s].astype(BF16))
        for s in seqs:
            for r0 in range(0, m, rb):
                pointwise(s, i, r0)
        skip = skip_ref[i:i + 1, :]
        skip2 = jnp.concatenate([skip, skip], axis=1)
        for s in seqs:
            conv = _dot(g_ref[...], prod_s[s * l:(s + 1) * l, :])
            zz[s] = conv_eo(xg_ref, wg_ref, bg_ref, s * l) * (conv + zz[s] * skip2)
    for s in seqs:
        for k in range(tiles):
            o_ref[k, pl.ds(s * l, m, stride=2), :] = zz[s][:, k * LANES:(k + 1) * LANES]
            o_ref[k, pl.ds(s * l + 1, m, stride=2), :] = zz[s][:, ct + k * LANES:ct + (k + 1) * LANES]


def _hyena_groups_kernel(v_ref, x1_ref, x2_ref, wv_ref, w1_ref, w2_ref, bv_ref, b1_ref, b2_ref,
                         fc_ref, gc_ref, kfc_ref, knc_ref, fl_ref, gl_ref, kfl_ref, knl_ref, skip_ref,
                         o_ref, spec_s, prod_s, *, ctx_steps, ctx, lat):
    common = (v_ref, x1_ref, x2_ref, wv_ref, w1_ref, w2_ref, bv_ref, b1_ref, b2_ref)

    @pl.when(pl.program_id(1) < ctx_steps)
    def _():
        _hyena_kernel(*common, fc_ref, gc_ref, kfc_ref, knc_ref, skip_ref, o_ref, spec_s, prod_s, l=ctx[0], seg=ctx[1])

    @pl.when(pl.program_id(1) >= ctx_steps)
    def _():
        _hyena_kernel(*common, fl_ref, gl_ref, kfl_ref, knl_ref, skip_ref, o_ref, spec_s, prod_s, l=lat[0], seg=lat[1])


def _hyena(hy, n_ctx, l_ctx, seg_ctx, l_lat, seg_lat, cw, cb, consts_ctx, consts_lat, skip):
    ct = HY_CH_TILE
    nct = HY_W // ct
    tiles = ct // LANES
    rows = HY_STEP_ROWS
    tokens = hy.shape[1]
    ctx_steps = n_ctx * l_ctx // rows
    part = lambda p: pl.BlockSpec((tiles, rows, LANES), lambda j, b: (p * nct + j, b, 0))
    wpart = lambda p: pl.BlockSpec((3, ct), lambda j, b: (0, p * nct + j))
    bpart = lambda p: pl.BlockSpec((1, ct), lambda j, b: (0, p * nct + j))

    def const_specs(l):
        m = l // 2
        return [pl.BlockSpec((l, m), lambda j, b: (0, 0)),
                pl.BlockSpec((m, l), lambda j, b: (0, 0)),
                pl.BlockSpec((HY_ORDER, N_FILT_PLANES, m, ct), lambda j, b: (0, 0, 0, j)),
                pl.BlockSpec((HY_ORDER, SUBLANES, ct), lambda j, b: (0, 0, j))]

    return pl.pallas_call(
        functools.partial(_hyena_groups_kernel, ctx_steps=ctx_steps, ctx=(l_ctx, seg_ctx), lat=(l_lat, seg_lat)),
        grid=(nct, tokens // rows),
        in_specs=[part(0), part(1), part(2), wpart(0), wpart(1), wpart(2), bpart(0), bpart(1), bpart(2)]
        + const_specs(l_ctx) + const_specs(l_lat) + [pl.BlockSpec((HY_ORDER, ct), lambda j, b: (0, j))],
        out_specs=pl.BlockSpec((tiles, rows, LANES), lambda j, b: (j, b, 0)),
        out_shape=jax.ShapeDtypeStruct((HY_W // LANES, tokens, LANES), F32),
        scratch_shapes=[pltpu.VMEM((rows, 2 * ct), F32), pltpu.VMEM((rows, 2 * ct), BF16)],
        compiler_params=_params("arbitrary", "arbitrary"),
        name="hyena",
    )(hy, hy, hy, cw, cw, cw, cb, cb, cb, *consts_ctx, *consts_lat, skip)


def _dft_mats(l):
    n = 2 * l
    f = np.arange(l, dtype=np.int64)[:, None]
    t = np.arange(l, dtype=np.int64)[None, :]
    ang = 2.0 * np.pi * ((f * t) % n).astype(np.float64) / n
    alt = np.where(np.arange(l) % 2 == 0, 1.0, -1.0)
    top = np.cos(ang)
    bot = -np.sin(ang)
    bot[0, :] = alt
    fwd = np.concatenate([top, bot], axis=0)
    wf = np.full((l,), 2.0)
    wf[0] = 1.0
    gtop = np.cos(ang).T * wf[None, :] / n
    gbot = -np.sin(ang).T * 2.0 / n
    gbot[:, 0] = alt / n
    inv = np.concatenate([gtop, gbot], axis=1)
    return fwd.astype(np.float32), inv.astype(np.float32)


def _filter_feats(l):
    t = np.linspace(0.0, 1.0, l)[:, None]
    w = (2.0 * np.pi / l) * np.arange(l, dtype=np.float64)[:, None]
    f = np.linspace(1e-4, HY_BANDS - 1, HY_BANDS)[None, :]
    feats = np.concatenate([t, np.cos(f * w), -np.sin(f * w)], axis=-1)
    out = np.zeros((l, EMB_PAD), np.float32)
    out[:, :HY_EMB] = feats
    return out[0::2], out[1::2]


def _shift_twiddles(l):
    theta = 2.0 * np.pi * np.arange(l // 2, dtype=np.float64) / l
    return np.stack([np.cos(theta), np.sin(theta)], axis=1).astype(np.float32)


def _head_expand():
    n_blocks = LANES // N_DT // 2
    e = np.zeros((n_blocks, LANES, N_DIR * HP), np.float32)
    for k in range(n_blocks):
        for j in range(N_DT):
            e[k, k * N_DT + j, j * SSD_HEAD_DIM:(j + 1) * SSD_HEAD_DIM] = 1.0
    return e


def kernel(x_prompt, x_sample, state_ssd, c, c_ctx, w_ada, b_ada, norm_ffn1, ffn1_w_gate, ffn1_w_up, ffn1_w_down, norm_mix, w_in, w_out, ssd_conv_w, ssd_conv_b, ssd_dt_bias, ssd_a_log, ssd_d, ssd_norm_w, hy_conv_w, hy_conv_b, hy_w1, hy_b1, hy_freq, hy_w2, hy_b2, hy_w3, hy_skip, norm_ffn2, ffn2_w_gate, ffn2_w_up, ffn2_w_down, norm_final):
    assert w_ada.shape[0] == 1, "single layer"
    n_ctx, l_ctx, _ = x_prompt.shape
    n_lat, l_lat, _ = x_sample.shape
    t_ctx = n_ctx * l_ctx

    cond = jnp.zeros((16, D_MODEL), F32).at[:n_lat].set(c).at[n_lat].set(c_ctx)
    mod = _ada(cond, w_ada, b_ada).reshape(16, N_MOD, D_MODEL)

    tr = lambda w: jnp.swapaxes(w, 1, 2)

    row = lambda v: v.reshape(1, -1)
    dtb, alog = ssd_dt_bias[0].reshape(N_DT, 1), ssd_a_log[0].reshape(N_DT, 1)
    dexp = jnp.repeat(ssd_d[0], SSD_HEAD_DIM).reshape(1, SSD_W)
    e_mat = jnp.asarray(_head_expand())
    w1p = jnp.pad(hy_w1[0], ((0, EMB_PAD - HY_EMB), (0, 0)))
    deltas = jnp.asarray(np.abs(np.linspace(HY_MIN_DECAY, HY_MAX_DECAY, HY_ORDER * HY_W))
                         .reshape(HY_ORDER, HY_W).astype(np.float32))

    (x1, z, xbc, dt, hy), (wg2, wu2, wd2), wo = _ffn_in(
        x_prompt.reshape(t_ctx, D_MODEL), x_sample.reshape(n_lat * l_lat, D_MODEL), mod, n_lat, l_lat,
        row(norm_ffn1[0]), row(norm_mix[0]), tr(ffn1_w_gate), tr(ffn1_w_up), ffn1_w_down, tr(w_in),
        [tr(ffn2_w_gate), tr(ffn2_w_up), ffn2_w_down], w_out)

    def hyena_consts(l):
        fwd_np, inv_np = _dft_mats(l // 2)
        fmat = jnp.asarray(fwd_np).astype(BF16)
        gmat = jnp.asarray(inv_np).astype(BF16)
        feats_e, feats_o = _filter_feats(l)
        kf, kn = _filters(l, jnp.asarray(feats_e), jnp.asarray(feats_o), w1p, row(hy_b1[0]), row(hy_freq[0]),
                          hy_w2[0], row(hy_b2[0]), hy_w3[0], deltas, fmat, jnp.asarray(_shift_twiddles(l)))
        return fmat, gmat, kf, kn

    lat_init = state_ssd[:, 0].reshape(n_lat, N_DIR, HP, SSD_STATE)
    ys, ctx_fin = _ssd(z, xbc, dt, n_ctx, l_ctx, l_ctx, l_lat, GRID_W, lat_init, ssd_conv_w[0], row(ssd_conv_b[0]),
                       dtb, alog, dexp, row(ssd_norm_w[0]), e_mat)
    yh = _hyena(hy, n_ctx, l_ctx, l_ctx, l_lat, GRID_W, hy_conv_w[0], row(hy_conv_b[0]),
                hyena_consts(l_ctx), hyena_consts(l_lat), hy_skip[0])

    y_ctx, y_lat = _out_ffn(x1, ys, yh, t_ctx, mod, n_lat, l_lat, row(norm_ffn2[0]), row(norm_final),
                            wo, wg2, wu2, wd2)
    new_state = ctx_fin.reshape(n_ctx, 1, N_DIR, SSD_HEADS, SSD_HEAD_DIM, SSD_STATE).astype(x_prompt.dtype)
    return (y_ctx.reshape(n_ctx, l_ctx, D_MODEL), y_lat.reshape(n_lat, l_lat, D_MODEL), new_state)
```

```python
import functools
import math

import numpy as np
import jax
import jax.numpy as jnp
from jax import lax
from jax.experimental import pallas as pl
from jax.experimental.pallas import tpu as pltpu

F32 = jnp.float32
BF16 = jnp.bfloat16

D_MODEL = 1024
GRID_W = 64
N_MOD = 9
RMS_EPS = 1e-6
FFN_DIM = 2752
SSD_W = 512
SSD_HEADS = 8
SSD_HEAD_DIM = 64
SSD_STATE = 64
SSD_GROUPS = 2
SSD_CHUNK = 128
SSD_XBC = SSD_W + 2 * SSD_GROUPS * SSD_STATE
N_DIR = 2
HY_W = 512
HY_ORDER = 2
HY_EMB = 33
HY_BANDS = (HY_EMB - 1) // 2
HY_HIDDEN = 64
HY_MIN_DECAY = math.log(1e-2) / 1.5
HY_MAX_DECAY = math.log(1e-2) / 0.3
IN_SPLITS = (SSD_W, SSD_W + SSD_XBC, SSD_W + SSD_XBC + N_DIR * SSD_HEADS)
IN_COLS = IN_SPLITS[-1] + (HY_ORDER + 1) * HY_W

LANES = 128
SUBLANES = 8
FFN_PAD = 2816
FFN_CHUNK = 256
N_DT = N_DIR * SSD_HEADS
DT_PAD = LANES
IN_ROWS = IN_COLS + DT_PAD - N_DIR * SSD_HEADS
EMB_PAD = LANES
TOKEN_TILE = 512
HY_CH_TILE = 256
VMEM_LIMIT = 60 * 1024 * 1024
NEG_BIG = -1e30
HP = SSD_HEADS * SSD_HEAD_DIM


def _silu(x):
    return x * jax.nn.sigmoid(x)


def _softplus(x):
    return jnp.maximum(x, 0.0) + jnp.log1p(jnp.exp(-jnp.abs(x)))


def _rms_mod(x, gain, shift, scale):
    ms = jnp.mean(x * x, axis=-1, keepdims=True)
    return x * lax.rsqrt(ms + RMS_EPS) * (gain * (1.0 + scale)) + shift


def _dot(a, b):
    return jnp.dot(a, b, preferred_element_type=F32)


def _dot_nt(a, b):
    return lax.dot_general(a, b, (((1,), (1,)), ((), ())), preferred_element_type=F32)


def _split3(x):
    hi = x.astype(BF16)
    r = x - hi.astype(F32)
    mid = r.astype(BF16)
    lo = (r - mid.astype(F32)).astype(BF16)
    return hi, mid, lo


def _dot_exact_lhs(x, m01):
    hi, mid, lo = _split3(x)
    return _dot(hi, m01) + _dot(mid, m01) + _dot(lo, m01)


def _seg_edges(l, seg):
    pos = lax.broadcasted_iota(jnp.int32, (l, 1), 0) & (seg - 1)
    return pos == 0, pos == seg - 1


def _swiglu_acc(h, wg_ref, wu_ref, wd_ref):
    acc = None
    for k in range(FFN_PAD // FFN_CHUNK):
        cols = slice(k * FFN_CHUNK, (k + 1) * FFN_CHUNK)
        g = _dot_nt(h, wg_ref[cols, :])
        u = _dot_nt(h, wu_ref[cols, :])
        a = (_silu(g) * u).astype(BF16)
        part = _dot(a, wd_ref[cols, :])
        acc = part if acc is None else acc + part
    return acc


def _params(*semantics):
    return pltpu.CompilerParams(dimension_semantics=semantics, vmem_limit_bytes=VMEM_LIMIT)


LOAD_ROWS = 256
LOAD_SLOTS = 4


def _row_pieces(src_lo, src_hi, dst_lo):
    return [(r, min(LOAD_ROWS, src_hi - r), dst_lo + r - src_lo) for r in range(src_lo, src_hi, LOAD_ROWS)]


def _load_weights_bf16(pieces, stage_ref, sem_ref):
    def load(k):
        src, r, n, _, _ = pieces[k]
        slot = k % LOAD_SLOTS
        return pltpu.make_async_copy(src.at[0, pl.ds(r, n), :], stage_ref.at[slot, pl.ds(0, n), :], sem_ref.at[slot])

    for k in range(min(LOAD_SLOTS, len(pieces))):
        load(k).start()
    for k, (_, _, n, dst, dr) in enumerate(pieces):
        load(k).wait()
        dst[dr:dr + n, :] = stage_ref[k % LOAD_SLOTS, 0:n, :].astype(BF16)
        if k + LOAD_SLOTS < len(pieces):
            load(k + LOAD_SLOTS).start()


def _ada_kernel(cond_ref, w_ref, b_ref, o_ref):
    s = _silu(cond_ref[...]).astype(BF16)
    o_ref[...] = _dot(s, w_ref[...].astype(BF16)) + b_ref[...]


def _ada(cond, w_ada, b_ada):
    rows = cond.shape[0]
    n = w_ada.shape[-1]
    tn = D_MODEL
    return pl.pallas_call(
        _ada_kernel,
        grid=(n // tn,),
        in_specs=[
            pl.BlockSpec((rows, D_MODEL), lambda j: (0, 0)),
            pl.BlockSpec((None, D_MODEL, tn), lambda j: (0, 0, j)),
            pl.BlockSpec((1, tn), lambda j: (0, j)),
        ],
        out_specs=pl.BlockSpec((rows, tn), lambda j: (0, j)),
        out_shape=jax.ShapeDtypeStruct((rows, n), F32),
        compiler_params=_params("arbitrary"),
        name="ada",
    )(cond, w_ada, b_ada)


N_FILT_PLANES = 6


def _filter_kernel(fe_ref, fo_ref, w1_ref, b1_ref, fr_ref, w2_ref, b2_ref, w3_ref, dl_ref, f_ref, tw_ref,
                   kf_ref, kn_ref, *, m):
    freq = fr_ref[...]
    w1 = w1_ref[...].astype(BF16)
    w2 = w2_ref[...].astype(BF16)

    def hidden(feats):
        h = jnp.sin(freq * (_dot(feats.astype(BF16), w1) + b1_ref[...]))
        h = jnp.sin(freq * (_dot(h.astype(BF16), w2) + b2_ref[...]))
        return h.astype(BF16)

    feats_e = fe_ref[...]
    feats_o = fo_ref[...]
    hb_e = hidden(feats_e)
    hb_o = hidden(feats_o)
    t_e = feats_e[:, 0:1]
    t_o = feats_o[:, 0:1]
    row = lax.broadcasted_iota(jnp.int32, (m, 1), 0)
    sign = jnp.where((row & 1) == 0, 1.0, -1.0)
    cos_t = tw_ref[:, 0:1]
    sin_t = tw_ref[:, 1:2]
    f_top = f_ref[0:m, :]
    f_bot = f_ref[m:2 * m, :]

    def spectrum(k):
        kb = k.astype(BF16)
        return _dot(f_top, kb), jnp.where(row == 0, 0.0, _dot(f_bot, kb)), jnp.sum(k * sign, axis=0, keepdims=True)

    for i in range(HY_ORDER):
        dl = dl_ref[i:i + 1, :]
        c0 = (0 * HY_ORDER + i) * HY_W
        c1 = (1 * HY_ORDER + i) * HY_W
        w3f = w3_ref[:, c0:c0 + HY_W].astype(BF16)
        w3b = w3_ref[:, c1:c1 + HY_W].astype(BF16)
        win_e = jnp.exp(-t_e * dl)
        win_o = jnp.exp(-t_o * dl)
        k0e = _dot(hb_e, w3f) * win_e
        k0o = _dot(hb_o, w3f) * win_o
        k1e = jnp.where(row == 0, 0.0, _dot(hb_e, w3b) * win_e)
        k1o = _dot(hb_o, w3b) * win_o
        ker, _, ken = spectrum(k0e + k1e)
        _, kei, _ = spectrum(k0e - k1e)
        ar, ai, an = spectrum(k0o)
        br, bi, bn = spectrum(k1o)
        kf_ref[i, 0] = ker
        kf_ref[i, 1] = kei
        kf_ref[i, 2] = ar + cos_t * br + sin_t * bi
        kf_ref[i, 3] = ai + sin_t * br - cos_t * bi
        kf_ref[i, 4] = cos_t * ar + sin_t * ai + br
        kf_ref[i, 5] = cos_t * ai - sin_t * ar - bi
        kn_ref[i] = jnp.concatenate([ken, an - bn, bn - an, jnp.zeros((SUBLANES - 3, HY_W), F32)], axis=0)


def _filters_kernel(w1_ref, b1_ref, fr_ref, w2_ref, b2_ref, w3_ref, dl_ref, *refs, ms):
    n = len(ms)
    for k, m in enumerate(ms):
        fe_ref, fo_ref, f_ref, tw_ref = refs[4 * k:4 * k + 4]
        kf_ref, kn_ref = refs[4 * n + 2 * k:4 * n + 2 * k + 2]
        _filter_kernel(fe_ref, fo_ref, w1_ref, b1_ref, fr_ref, w2_ref, b2_ref, w3_ref, dl_ref, f_ref, tw_ref,
                       kf_ref, kn_ref, m=m)


def _filters(lengths, per_length, w1p, b1, freq, w2, b2, w3, deltas):
    ms = [l // 2 for l in lengths]
    full = lambda *shape: pl.BlockSpec(shape, lambda: (0,) * len(shape))
    in_specs = [full(EMB_PAD, HY_HIDDEN), full(1, HY_HIDDEN), full(1, HY_HIDDEN), full(HY_HIDDEN, HY_HIDDEN),
                full(1, HY_HIDDEN), full(HY_HIDDEN, N_DIR * HY_ORDER * HY_W), full(HY_ORDER, HY_W)]
    out_specs, out_shape = [], []
    for m in ms:
        in_specs += [full(m, EMB_PAD), full(m, EMB_PAD), full(2 * m, m), full(m, 2)]
        out_specs += [full(HY_ORDER, N_FILT_PLANES, m, HY_W), full(HY_ORDER, SUBLANES, HY_W)]
        out_shape += [jax.ShapeDtypeStruct((HY_ORDER, N_FILT_PLANES, m, HY_W), F32),
                      jax.ShapeDtypeStruct((HY_ORDER, SUBLANES, HY_W), F32)]
    outs = pl.pallas_call(
        functools.partial(_filters_kernel, ms=ms),
        in_specs=in_specs,
        out_specs=out_specs,
        out_shape=out_shape,
        compiler_params=pltpu.CompilerParams(vmem_limit_bytes=VMEM_LIMIT),
        name="filt",
    )(w1p, b1, freq, w2, b2, w3, deltas, *[a for group in per_length for a in group])
    return [(outs[2 * k], outs[2 * k + 1]) for k in range(len(ms))]


def _const_spec(shape):
    return pl.BlockSpec(shape, lambda i: (0,) * len(shape), pipeline_mode=pl.Buffered(1))


def _group_specs(ctx_tiles, width):
    tm = TOKEN_TILE
    return [pl.BlockSpec((tm, width), lambda i: (jnp.minimum(i, ctx_tiles - 1), 0)),
            pl.BlockSpec((tm, width), lambda i: (jnp.maximum(i - ctx_tiles, 0), 0))]


def _mod_spec(ctx_tiles, n_lat, tiles_per_seq):
    return pl.BlockSpec((None, N_MOD, D_MODEL),
                        lambda i: (jnp.where(i < ctx_tiles, n_lat, (i - ctx_tiles) // tiles_per_seq), 0, 0))


NEXT_CAST_ROWS = 128
NEXT_CAST_ROWS_OUT = 64


def _ffn_in_kernel(xc_ref, xl_ref, mod_ref, n1_ref, nm_ref, wg_hbm, wu_hbm, wd_hbm, wi_hbm,
                   ng_ref, nu_ref, nd_ref, no_ref,
                   x1_ref, z_ref, xbc_ref, dt_ref, hy_ref, ngb_ref, nub_ref, ndb_ref, nob_ref,
                   wg_ref, wu_ref, wd_ref, wi_ref, stage_ref, sem_ref, *, ctx_tiles):
    @pl.when(pl.program_id(0) == 0)
    def _():
        o1, o2, o3 = IN_SPLITS
        pieces = []
        for src, dst in ((wg_hbm, wg_ref), (wu_hbm, wu_ref), (wd_hbm, wd_ref)):
            pieces += [(src, r, n, dst, dr) for r, n, dr in _row_pieces(0, FFN_DIM, 0)]
            dst[FFN_DIM:, :] = jnp.zeros((FFN_PAD - FFN_DIM, D_MODEL), BF16)
        pieces += [(wi_hbm, r, n, wi_ref, dr) for r, n, dr in _row_pieces(0, o3, 0) + _row_pieces(o3, IN_COLS, o2 + DT_PAD)]
        wi_ref[o3:o2 + DT_PAD, :] = jnp.zeros((o2 + DT_PAD - o3, D_MODEL), BF16)
        _load_weights_bf16(pieces, stage_ref, sem_ref)

    blk = jnp.minimum(pl.program_id(0), FFN_PAD // NEXT_CAST_ROWS - 1)
    wrow = blk * NEXT_CAST_ROWS + lax.broadcasted_iota(jnp.int32, (NEXT_CAST_ROWS, 1), 0)
    for src, dst in ((ng_ref, ngb_ref), (nu_ref, nub_ref), (nd_ref, ndb_ref)):
        dst[...] = jnp.where(wrow < FFN_DIM, src[...], 0.0).astype(BF16)
    nob_ref[...] = no_ref[...].astype(BF16)

    x = jnp.where(pl.program_id(0) < ctx_tiles, xc_ref[...], xl_ref[...])
    mod = mod_ref[...]
    h = _rms_mod(x, n1_ref[...], mod[0:1, :], mod[1:2, :]).astype(BF16)
    x1 = x + (0.5 * mod[2:3, :]) * _swiglu_acc(h, wg_ref, wu_ref, wd_ref)
    x1_ref[...] = x1
    h2 = _rms_mod(x1, nm_ref[...], mod[3:4, :], mod[4:5, :]).astype(BF16)
    o1, o2, _ = IN_SPLITS
    z_ref[...] = _dot_nt(h2, wi_ref[0:o1, :])
    xbc_ref[...] = _dot_nt(h2, wi_ref[o1:o2, :])
    dt_ref[...] = _dot_nt(wi_ref[o2:o2 + N_DT, :], h2)
    hy = _dot_nt(h2, wi_ref[o2 + DT_PAD:, :])
    for k in range(hy_ref.shape[0]):
        hy_ref[k] = hy[:, k * LANES:(k + 1) * LANES]


def _ffn_in(x_ctx, x_lat, mod, n_lat, l_lat, n1, nm, wg, wu, wd, wi, next_ffn, next_out):
    tm = TOKEN_TILE
    ctx_tiles = x_ctx.shape[0] // tm
    tokens = x_ctx.shape[0] + x_lat.shape[0]
    steps = tokens // tm
    ffn_blocks = FFN_PAD // NEXT_CAST_ROWS
    out_blocks = D_MODEL // NEXT_CAST_ROWS_OUT
    assert steps >= ffn_blocks and steps >= out_blocks
    row_spec = lambda n: pl.BlockSpec((tm, n), lambda i: (i, 0))
    widths = (D_MODEL, SSD_W, SSD_XBC)
    hy_tiles = (HY_ORDER + 1) * HY_W // LANES
    dt_spec = pl.BlockSpec((N_DT, tm), lambda i: (0, i))
    outs = pl.pallas_call(
        functools.partial(_ffn_in_kernel, ctx_tiles=ctx_tiles),
        grid=(steps,),
        in_specs=_group_specs(ctx_tiles, D_MODEL) + [
            _mod_spec(ctx_tiles, n_lat, l_lat // tm),
            _const_spec((1, D_MODEL)), _const_spec((1, D_MODEL))]
        + [pl.BlockSpec(memory_space=pl.ANY)] * 4
        + [pl.BlockSpec((None, NEXT_CAST_ROWS, D_MODEL), lambda i: (0, jnp.minimum(i, ffn_blocks - 1), 0))] * 3
        + [pl.BlockSpec((None, NEXT_CAST_ROWS_OUT, D_MODEL), lambda i: (0, jnp.minimum(i, out_blocks - 1), 0))],
        out_specs=[row_spec(n) for n in widths] + [dt_spec, pl.BlockSpec((hy_tiles, tm, LANES), lambda i: (0, i, 0))]
        + [pl.BlockSpec((NEXT_CAST_ROWS, D_MODEL), lambda i: (jnp.minimum(i, ffn_blocks - 1), 0))] * 3
        + [pl.BlockSpec((NEXT_CAST_ROWS_OUT, D_MODEL), lambda i: (jnp.minimum(i, out_blocks - 1), 0))],
        out_shape=[jax.ShapeDtypeStruct((tokens, n), F32) for n in widths]
        + [jax.ShapeDtypeStruct((N_DT, tokens), F32), jax.ShapeDtypeStruct((hy_tiles, tokens, LANES), F32)]
        + [jax.ShapeDtypeStruct((FFN_PAD, D_MODEL), BF16)] * 3 + [jax.ShapeDtypeStruct((D_MODEL, D_MODEL), BF16)],
        scratch_shapes=[pltpu.VMEM((FFN_PAD, D_MODEL), BF16)] * 3 + [
            pltpu.VMEM((IN_ROWS, D_MODEL), BF16), pltpu.VMEM((LOAD_SLOTS, LOAD_ROWS, D_MODEL), F32),
            pltpu.SemaphoreType.DMA((LOAD_SLOTS,))],
        compiler_params=_params("arbitrary"),
        name="ffn_in",
    )(x_ctx, x_lat, mod, n1, nm, wg, wu, wd, wi, *next_ffn, next_out)
    return outs[:5], outs[5:8], outs[8]


def _out_ffn_kernel(x1_ref, ys_ref, yh_ref, mod_ref, n3_ref, nf_ref,
                    wo_ref, wg_ref, wu_ref, wd_ref, oc_ref, ol_ref, *, ctx_tiles):
    is_ctx = pl.program_id(0) < ctx_tiles
    mod = mod_ref[...]
    y = jnp.concatenate([ys_ref[...]] + [yh_ref[k] for k in range(yh_ref.shape[0])], axis=1).astype(BF16)
    x2 = x1_ref[...] + mod[5:6, :] * _dot(y, wo_ref[...])
    h = _rms_mod(x2, n3_ref[...], mod[6:7, :], mod[7:8, :]).astype(BF16)
    x3 = x2 + (0.5 * mod[8:9, :]) * _swiglu_acc(h, wg_ref, wu_ref, wd_ref)
    ms = jnp.mean(x3 * x3, axis=-1, keepdims=True)
    out = x3 * lax.rsqrt(ms + RMS_EPS) * nf_ref[...]

    @pl.when(is_ctx)
    def _():
        oc_ref[...] = out

    @pl.when(jnp.logical_not(is_ctx))
    def _():
        ol_ref[...] = out


def _out_ffn(x1, ys, yh, t_ctx, mod, n_lat, l_lat, n3, nf, wo, wg, wu, wd):
    tm = TOKEN_TILE
    ctx_tiles = t_ctx // tm
    tokens = x1.shape[0]
    return pl.pallas_call(
        functools.partial(_out_ffn_kernel, ctx_tiles=ctx_tiles),
        grid=(tokens // tm,),
        in_specs=[pl.BlockSpec((tm, D_MODEL), lambda i: (i, 0)), pl.BlockSpec((tm, SSD_W), lambda i: (i, 0)),
                  pl.BlockSpec((HY_W // LANES, tm, LANES), lambda i: (0, i, 0)),
            _mod_spec(ctx_tiles, n_lat, l_lat // tm),
            _const_spec((1, D_MODEL)), _const_spec((1, D_MODEL)),
            _const_spec(wo.shape), _const_spec(wg.shape), _const_spec(wu.shape), _const_spec(wd.shape)],
        out_specs=_group_specs(ctx_tiles, D_MODEL),
        out_shape=[jax.ShapeDtypeStruct((t_ctx, D_MODEL), F32),
                   jax.ShapeDtypeStruct((tokens - t_ctx, D_MODEL), F32)],
        compiler_params=_params("arbitrary"),
        name="out_ffn",
    )(x1, ys, yh, mod, n3, nf, wo, wg, wu, wd)


def _ssd_kernel(z_ref, xbc_ref, dt_ref, init_ref, cw_ref, cb_ref, dtb_ref, alog_ref, dexp_ref, nw_ref, e_ref,
                y_ref, fin_ref,
                xs_s, b_s, c_s, ec_s, dst_s, et_s, cum_s, ct_s, xd_s,
                *, l, seg, zero_init, write_final):
    q = SSD_CHUNK
    nc = l // q
    n_seq = y_ref.shape[0] // l
    cr = max(seg, q)
    e_mats = [e_ref[k].astype(BF16) for k in range(e_ref.shape[0])]
    dt_t = _softplus(dt_ref[...] + dtb_ref[...])
    a_t = dt_t * (-jnp.exp(alog_ref[...]))

    first, last = _seg_edges(cr, seg)

    for r0 in range(0, n_seq * l, cr):
        rows = slice(r0, r0 + cr)
        x = xbc_ref[rows, :]
        w = cw_ref[...]
        prev = jnp.where(first, 0.0, pltpu.roll(x, 1, 0))
        nxt = jnp.where(last, 0.0, pltpu.roll(x, cr - 1, 0))
        u = _silu(prev * w[0:1, :] + x * w[1:2, :] + nxt * w[2:3, :] + cb_ref[...])
        xs_s[rows, :] = u[:, :SSD_W]
        b_s[rows, :] = u[:, SSD_W:SSD_W + LANES]
        c_s[rows, :] = u[:, SSD_W + LANES:]

    row_g = lax.broadcasted_iota(jnp.int32, (LANES, N_DIR * HP), 0) // SSD_STATE
    lane_g = (lax.broadcasted_iota(jnp.int32, (LANES, N_DIR * HP), 1) % HP) // (HP // SSD_GROUPS)
    own_t = row_g == lane_g

    ii = lax.broadcasted_iota(jnp.int32, (q, q), 0)
    jj = lax.broadcasted_iota(jnp.int32, (q, q), 1)
    tri_upper = (ii <= jj).astype(BF16)
    lane = lax.broadcasted_iota(jnp.int32, (q, LANES), 1)
    low_half = lane < SSD_STATE
    is_fwd_row = lax.broadcasted_iota(jnp.int32, (N_DT, 1), 0) < SSD_HEADS

    chunks = range(n_seq * nc)

    for c in chunks:
        rows = slice(c * q, (c + 1) * q)
        a_c = a_t[:, rows]
        cum_f = _dot_exact_lhs(a_c, tri_upper)
        tot_c = cum_f[:, q - 1:q]
        cum_t = jnp.where(is_fwd_row, cum_f, tot_c - cum_f + a_c)
        ct_s[c] = cum_t
        stack = jnp.concatenate([cum_t, jnp.exp(cum_t), jnp.exp(tot_c - cum_t), dt_t[:, rows],
                                 jnp.zeros((LANES - 4 * N_DT, q), F32)], axis=0)
        cum_s[c] = stack.T

    def expand_pass(c):
        rows = slice(c * q, (c + 1) * q)
        cum = cum_s[c]
        cum_b = cum.astype(BF16)
        ec_s[rows, :] = _dot(cum_b, e_mats[1])
        tot = jnp.where(lane[0:1, :] < SSD_HEADS, cum[q - 1:q, :], cum[0:1, :])
        tot = jnp.where(lane[0:1, :] < N_DT, tot, 0.0)
        et_s[c] = _dot_exact_lhs(jnp.broadcast_to(jnp.exp(tot), (SUBLANES, LANES)), e_mats[0])
        xs = xs_s[rows, :]
        xd = jnp.concatenate([xs, xs], axis=1) * _dot(cum_b, e_mats[3])
        xd_s[rows, :] = xd
        w = (xd * _dot(cum_b, e_mats[2])).astype(BF16)
        dst_s[c] = jnp.where(own_t, _dot(b_s[rows, :].T.astype(BF16), w), 0.0)

    def local_pass(c):
        rows = slice(c * q, (c + 1) * q)
        cum = cum_s[c]
        cum_t = ct_s[c]
        xd = xd_s[rows, :]
        bcb = b_s[rows, :].astype(BF16)
        cc = c_s[rows, :]
        g_mats = [_dot_nt(jnp.where(low_half, cc, 0.0).astype(BF16), bcb),
                  _dot_nt(jnp.where(low_half, 0.0, cc).astype(BF16), bcb)]
        y_parts = []
        for pair in range(SSD_HEADS // 2):
            g = pair // (SSD_HEADS // 2 // SSD_GROUPS)
            acc = None
            for d in range(N_DIR):
                keep = (ii >= jj) if d == 0 else (ii <= jj)
                s_mats = []
                for hh in (2 * pair, 2 * pair + 1):
                    col = d * SSD_HEADS + hh
                    diff = cum[:, col:col + 1] - cum_t[col:col + 1, :]
                    decay = jnp.exp(jnp.where(keep, diff, NEG_BIG))
                    s_mats.append((g_mats[g] * decay).astype(BF16))
                lhs = jnp.concatenate(s_mats, axis=1)
                xp = xd[:, d * HP + pair * LANES:d * HP + (pair + 1) * LANES]
                rhs = jnp.concatenate([jnp.where(low_half, xp, 0.0), jnp.where(low_half, 0.0, xp)],
                                      axis=0).astype(BF16)
                part = _dot(lhs, rhs)
                acc = part if acc is None else acc + part
            y_parts.append(acc)
        y_ref[rows, :] = jnp.concatenate(y_parts, axis=1)

    for c in chunks:
        expand_pass(c)
        local_pass(c)

    half = HP // SSD_GROUPS
    for s in range(n_seq):
        states = []
        for d in range(N_DIR):
            if zero_init:
                states.append(jnp.zeros((LANES, HP), F32))
            else:
                s0 = init_ref[s, d]
                states.append(jnp.where(own_t[:, :HP], jnp.concatenate([s0, s0], axis=1).T, 0.0))

        for k in range(nc):
            for d in range(N_DIR):
                c = s * nc + (k if d == 0 else nc - 1 - k)
                rows = slice(c * q, (c + 1) * q)
                lanes = slice(d * HP, (d + 1) * HP)
                if zero_init and k == 0:
                    states[d] = dst_s[c, :, lanes]
                    continue
                y_off = _dot(c_s[rows, :].astype(BF16), states[d].astype(BF16)) * ec_s[rows, lanes]
                y_ref[rows, :] = y_ref[rows, :] + y_off
                states[d] = states[d] * et_s[c, 0:1, lanes] + dst_s[c, :, lanes]

        for d in range(N_DIR if write_final else 0):
            st = states[d].T
            fin_ref[s, d, 0:half, :] = st[0:half, 0:SSD_STATE]
            fin_ref[s, d, half:HP, :] = st[half:HP, SSD_STATE:2 * SSD_STATE]

        for c in range(s * nc, (s + 1) * nc):
            rows = slice(c * q, (c + 1) * q)
            y = y_ref[rows, :] + xs_s[rows, :] * dexp_ref[...]
            y = y * _silu(z_ref[rows, :])
            ms = jnp.mean(y * y, axis=-1, keepdims=True)
            y_ref[rows, :] = y * lax.rsqrt(ms + RMS_EPS) * nw_ref[...]


SSD_STEP_ROWS = 1024


def _ssd_groups_kernel(*refs, ctx_steps, ctx, lat):
    @pl.when(pl.program_id(0) < ctx_steps)
    def _():
        _ssd_kernel(*refs, l=ctx[0], seg=ctx[1], zero_init=True, write_final=True)

    @pl.when(pl.program_id(0) >= ctx_steps)
    def _():
        _ssd_kernel(*refs, l=lat[0], seg=lat[1], zero_init=False, write_final=False)


def _ssd(z, xbc, dt, n_ctx, l_ctx, seg_ctx, l_lat, seg_lat, init_lat, cw, cb, dtb, alog, dexp, nw, e_mat):
    rows = SSD_STEP_ROWS
    tokens = z.shape[0]
    ctx_steps = n_ctx * l_ctx // rows
    ctx_seqs, lat_seqs = rows // l_ctx, rows // l_lat
    seq_spec = lambda n: pl.BlockSpec((rows, n), lambda b: (b, 0))
    const = lambda *shape: pl.BlockSpec(shape, lambda b: (0,) * len(shape))
    nc = rows // SSD_CHUNK
    return pl.pallas_call(
        functools.partial(_ssd_groups_kernel, ctx_steps=ctx_steps, ctx=(l_ctx, seg_ctx), lat=(l_lat, seg_lat)),
        grid=(tokens // rows,),
        in_specs=[seq_spec(SSD_W), seq_spec(SSD_XBC), pl.BlockSpec((N_DT, rows), lambda b: (0, b)),
                  pl.BlockSpec((lat_seqs, N_DIR, HP, SSD_STATE), lambda b: (jnp.maximum(b - ctx_steps, 0), 0, 0, 0)),
                  const(3, SSD_XBC), const(1, SSD_XBC), const(N_DT, 1), const(N_DT, 1),
                  const(1, SSD_W), const(1, SSD_W), const(*e_mat.shape)],
        out_specs=[pl.BlockSpec((rows, SSD_W), lambda b: (b, 0)),
                   pl.BlockSpec((ctx_seqs, N_DIR, HP, SSD_STATE), lambda b: (jnp.minimum(b, ctx_steps - 1), 0, 0, 0))],
        out_shape=[jax.ShapeDtypeStruct((tokens, SSD_W), F32),
                   jax.ShapeDtypeStruct((n_ctx, N_DIR, HP, SSD_STATE), F32)],
        scratch_shapes=[pltpu.VMEM((rows, SSD_W), F32), pltpu.VMEM((rows, LANES), F32), pltpu.VMEM((rows, LANES), F32),
                        pltpu.VMEM((rows, N_DIR * HP), F32),
                        pltpu.VMEM((nc, LANES, N_DIR * HP), F32), pltpu.VMEM((nc, SUBLANES, N_DIR * HP), F32),
                        pltpu.VMEM((nc, SSD_CHUNK, LANES), F32), pltpu.VMEM((nc, N_DT, SSD_CHUNK), F32),
                        pltpu.VMEM((rows, N_DIR * HP), F32)],
        compiler_params=_params("arbitrary"),
        name="ssd",
    )(z, xbc, dt, init_lat, cw, cb, dtb, alog, dexp, nw, e_mat)


HY_ROW_BLOCK = 16
HY_STEP_ROWS = 2048


def _hyena_kernel(v_ref, x1_ref, x2_ref, wv_ref, w1_ref, w2_ref, bv_ref, b1_ref, b2_ref,
                  f_ref, g_ref, kf_ref, kn_ref, skip_ref, o_ref, spec_s, prod_s, *, l, seg):
    m = l // 2
    tiles = o_ref.shape[0]
    n_seq = o_ref.shape[1] // l
    ct = tiles * LANES
    rb = HY_ROW_BLOCK
    first, last = _seg_edges(m, seg // 2)
    row0 = lax.broadcasted_iota(jnp.int32, (rb, 1), 0) == 0
    ev = slice(0, ct)
    od = slice(ct, 2 * ct)

    def conv_eo(x_ref, w_ref, b_ref, base):
        xe, xo = (jnp.concatenate([x_ref[k, pl.ds(base + p, m, stride=2), :] for k in range(tiles)], axis=1)
                  for p in range(2))
        w = w_ref[...]
        b = b_ref[...]
        xo_prev = jnp.where(first, 0.0, pltpu.roll(xo, 1, 0))
        xe_next = jnp.where(last, 0.0, pltpu.roll(xe, m - 1, 0))
        ce = xo_prev * w[0:1, :] + xe * w[1:2, :] + xo * w[2:3, :] + b
        co = xe * w[0:1, :] + xo * w[1:2, :] + xe_next * w[2:3, :] + b
        return jnp.concatenate([ce, co], axis=1)

    def pointwise(s, i, r0):
        re = slice(s * l + r0, s * l + r0 + rb)
        im = slice(s * l + m + r0, s * l + m + r0 + rb)
        er, orr = spec_s[re, ev], spec_s[re, od]
        ei, oi = spec_s[im, ev], spec_s[im, od]
        ker, kei, kor, koi, vr, vi = (kf_ref[i, p, r0:r0 + rb, :] for p in range(N_FILT_PLANES))
        if r0 == 0:
            e_n, o_n = ei[0:1, :], oi[0:1, :]
            ei = jnp.where(row0, 0.0, ei)
            oi = jnp.where(row0, 0.0, oi)
        pe_r = er * ker - ei * kei + orr * vr - oi * vi
        pe_i = er * kei + ei * ker + orr * vi + oi * vr
        po_r = er * kor - ei * koi + orr * ker - oi * kei
        po_i = er * koi + ei * kor + orr * kei + oi * ker
        if r0 == 0:
            kn = kn_ref[i]
            pe_i = jnp.where(row0, e_n * kn[0:1, :] + o_n * kn[2:3, :], pe_i)
            po_i = jnp.where(row0, e_n * kn[1:2, :] + o_n * kn[0:1, :], po_i)
        prod_s[re, ev] = pe_r.astype(BF16)
        prod_s[re, od] = po_r.astype(BF16)
        prod_s[im, ev] = pe_i.astype(BF16)
        prod_s[im, od] = po_i.astype(BF16)

    seqs = range(n_seq)
    zz = [conv_eo(v_ref, wv_ref, bv_ref, s * l) for s in seqs]
    for i, (xg_ref, wg_ref, bg_ref) in enumerate(((x1_ref, w1_ref, b1_ref), (x2_ref, w2_ref, b2_ref))):
        for s in seqs:
            spec_s[s * l:(s + 1) * l, :] = _dot(f_ref[...], zz[s].astype(BF16))
        for s in seqs:
            for r0 in range(0, m, rb):
                pointwise(s, i, r0)
        skip = skip_ref[i:i + 1, :]
        skip2 = jnp.concatenate([skip, skip], axis=1)
        for s in seqs:
            conv = _dot(g_ref[...], prod_s[s * l:(s + 1) * l, :])
            zz[s] = conv_eo(xg_ref, wg_ref, bg_ref, s * l) * (conv + zz[s] * skip2)
    for s in seqs:
        for k in range(tiles):
            o_ref[k, pl.ds(s * l, m, stride=2), :] = zz[s][:, k * LANES:(k + 1) * LANES]
            o_ref[k, pl.ds(s * l + 1, m, stride=2), :] = zz[s][:, ct + k * LANES:ct + (k + 1) * LANES]


def _hyena_groups_kernel(v_ref, x1_ref, x2_ref, wv_ref, w1_ref, w2_ref, bv_ref, b1_ref, b2_ref,
                         fc_ref, gc_ref, kfc_ref, knc_ref, fl_ref, gl_ref, kfl_ref, knl_ref, skip_ref,
                         o_ref, spec_s, prod_s, *, ctx_steps, ctx, lat):
    common = (v_ref, x1_ref, x2_ref, wv_ref, w1_ref, w2_ref, bv_ref, b1_ref, b2_ref)

    @pl.when(pl.program_id(1) < ctx_steps)
    def _():
        _hyena_kernel(*common, fc_ref, gc_ref, kfc_ref, knc_ref, skip_ref, o_ref, spec_s, prod_s, l=ctx[0], seg=ctx[1])

    @pl.when(pl.program_id(1) >= ctx_steps)
    def _():
        _hyena_kernel(*common, fl_ref, gl_ref, kfl_ref, knl_ref, skip_ref, o_ref, spec_s, prod_s, l=lat[0], seg=lat[1])


def _hyena(hy, n_ctx, l_ctx, seg_ctx, l_lat, seg_lat, cw, cb, consts_ctx, consts_lat, skip):
    ct = HY_CH_TILE
    nct = HY_W // ct
    tiles = ct // LANES
    rows = HY_STEP_ROWS
    tokens = hy.shape[1]
    ctx_steps = n_ctx * l_ctx // rows
    part = lambda p: pl.BlockSpec((tiles, rows, LANES), lambda j, b: (p * nct + j, b, 0))
    wpart = lambda p: pl.BlockSpec((3, ct), lambda j, b: (0, p * nct + j))
    bpart = lambda p: pl.BlockSpec((1, ct), lambda j, b: (0, p * nct + j))

    def const_specs(l):
        m = l // 2
        return [pl.BlockSpec((l, m), lambda j, b: (0, 0)),
                pl.BlockSpec((m, l), lambda j, b: (0, 0)),
                pl.BlockSpec((HY_ORDER, N_FILT_PLANES, m, ct), lambda j, b: (0, 0, 0, j)),
                pl.BlockSpec((HY_ORDER, SUBLANES, ct), lambda j, b: (0, 0, j))]

    return pl.pallas_call(
        functools.partial(_hyena_groups_kernel, ctx_steps=ctx_steps, ctx=(l_ctx, seg_ctx), lat=(l_lat, seg_lat)),
        grid=(nct, tokens // rows),
        in_specs=[part(0), part(1), part(2), wpart(0), wpart(1), wpart(2), bpart(0), bpart(1), bpart(2)]
        + const_specs(l_ctx) + const_specs(l_lat) + [pl.BlockSpec((HY_ORDER, ct), lambda j, b: (0, j))],
        out_specs=pl.BlockSpec((tiles, rows, LANES), lambda j, b: (j, b, 0)),
        out_shape=jax.ShapeDtypeStruct((HY_W // LANES, tokens, LANES), F32),
        scratch_shapes=[pltpu.VMEM((rows, 2 * ct), F32), pltpu.VMEM((rows, 2 * ct), BF16)],
        compiler_params=_params("arbitrary", "arbitrary"),
        name="hyena",
    )(hy, hy, hy, cw, cw, cw, cb, cb, cb, *consts_ctx, *consts_lat, skip)


def _dft_mats(l):
    n = 2 * l
    f = np.arange(l, dtype=np.int64)[:, None]
    t = np.arange(l, dtype=np.int64)[None, :]
    ang = 2.0 * np.pi * ((f * t) % n).astype(np.float64) / n
    alt = np.where(np.arange(l) % 2 == 0, 1.0, -1.0)
    top = np.cos(ang)
    bot = -np.sin(ang)
    bot[0, :] = alt
    fwd = np.concatenate([top, bot], axis=0)
    wf = np.full((l,), 2.0)
    wf[0] = 1.0
    gtop = np.cos(ang).T * wf[None, :] / n
    gbot = -np.sin(ang).T * 2.0 / n
    gbot[:, 0] = alt / n
    inv = np.concatenate([gtop, gbot], axis=1)
    return fwd.astype(np.float32), inv.astype(np.float32)


def _filter_feats(l):
    t = np.linspace(0.0, 1.0, l)[:, None]
    w = (2.0 * np.pi / l) * np.arange(l, dtype=np.float64)[:, None]
    f = np.linspace(1e-4, HY_BANDS - 1, HY_BANDS)[None, :]
    feats = np.concatenate([t, np.cos(f * w), -np.sin(f * w)], axis=-1)
    out = np.zeros((l, EMB_PAD), np.float32)
    out[:, :HY_EMB] = feats
    return out[0::2], out[1::2]


def _shift_twiddles(l):
    theta = 2.0 * np.pi * np.arange(l // 2, dtype=np.float64) / l
    return np.stack([np.cos(theta), np.sin(theta)], axis=1).astype(np.float32)


def _head_expand():
    n_blocks = LANES // N_DT // 2
    e = np.zeros((n_blocks, LANES, N_DIR * HP), np.float32)
    for k in range(n_blocks):
        for j in range(N_DT):
            e[k, k * N_DT + j, j * SSD_HEAD_DIM:(j + 1) * SSD_HEAD_DIM] = 1.0
    return e


def kernel(x_prompt, x_sample, state_ssd, c, c_ctx, w_ada, b_ada, norm_ffn1, ffn1_w_gate, ffn1_w_up, ffn1_w_down, norm_mix, w_in, w_out, ssd_conv_w, ssd_conv_b, ssd_dt_bias, ssd_a_log, ssd_d, ssd_norm_w, hy_conv_w, hy_conv_b, hy_w1, hy_b1, hy_freq, hy_w2, hy_b2, hy_w3, hy_skip, norm_ffn2, ffn2_w_gate, ffn2_w_up, ffn2_w_down, norm_final):
    assert w_ada.shape[0] == 1, "single layer"
    n_ctx, l_ctx, _ = x_prompt.shape
    n_lat, l_lat, _ = x_sample.shape
    t_ctx = n_ctx * l_ctx

    cond = jnp.zeros((16, D_MODEL), F32).at[:n_lat].set(c).at[n_lat].set(c_ctx)
    mod = _ada(cond, w_ada, b_ada).reshape(16, N_MOD, D_MODEL)

    tr = lambda w: jnp.swapaxes(w, 1, 2)

    row = lambda v: v.reshape(1, -1)
    dtb, alog = ssd_dt_bias[0].reshape(N_DT, 1), ssd_a_log[0].reshape(N_DT, 1)
    dexp = jnp.repeat(ssd_d[0], SSD_HEAD_DIM).reshape(1, SSD_W)
    e_mat = jnp.asarray(_head_expand())
    w1p = jnp.pad(hy_w1[0], ((0, EMB_PAD - HY_EMB), (0, 0)))
    deltas = jnp.asarray(np.abs(np.linspace(HY_MIN_DECAY, HY_MAX_DECAY, HY_ORDER * HY_W))
                         .reshape(HY_ORDER, HY_W).astype(np.float32))

    (x1, z, xbc, dt, hy), (wg2, wu2, wd2), wo = _ffn_in(
        x_prompt.reshape(t_ctx, D_MODEL), x_sample.reshape(n_lat * l_lat, D_MODEL), mod, n_lat, l_lat,
        row(norm_ffn1[0]), row(norm_mix[0]), tr(ffn1_w_gate), tr(ffn1_w_up), ffn1_w_down, tr(w_in),
        [tr(ffn2_w_gate), tr(ffn2_w_up), ffn2_w_down], w_out)

    lengths = (l_ctx, l_lat)
    dft = [tuple(jnp.asarray(a).astype(BF16) for a in _dft_mats(l // 2)) for l in lengths]
    filt = _filters(lengths,
                    [tuple(jnp.asarray(a) for a in _filter_feats(l)) + (dft[k][0], jnp.asarray(_shift_twiddles(l)))
                     for k, l in enumerate(lengths)],
                    w1p, row(hy_b1[0]), row(hy_freq[0]), hy_w2[0], row(hy_b2[0]), hy_w3[0], deltas)
    hyena_consts = [dft[k] + filt[k] for k in range(len(lengths))]

    lat_init = state_ssd[:, 0].reshape(n_lat, N_DIR, HP, SSD_STATE)
    ys, ctx_fin = _ssd(z, xbc, dt, n_ctx, l_ctx, l_ctx, l_lat, GRID_W, lat_init, ssd_conv_w[0], row(ssd_conv_b[0]),
                       dtb, alog, dexp, row(ssd_norm_w[0]), e_mat)
    yh = _hyena(hy, n_ctx, l_ctx, l_ctx, l_lat, GRID_W, hy_conv_w[0], row(hy_conv_b[0]),
                hyena_consts[0], hyena_consts[1], hy_skip[0])

    y_ctx, y_lat = _out_ffn(x1, ys, yh, t_ctx, mod, n_lat, l_lat, row(norm_ffn2[0]), row(norm_final),
                            wo, wg2, wu2, wd2)
    new_state = ctx_fin.reshape(n_ctx, 1, N_DIR, SSD_HEADS, SSD_HEAD_DIM, SSD_STATE).astype(x_prompt.dtype)
    return (y_ctx.reshape(n_ctx, l_ctx, D_MODEL), y_lat.reshape(n_lat, l_lat, D_MODEL), new_state)
```

```python
import functools
import math

import numpy as np
import jax
import jax.numpy as jnp
from jax import lax
from jax.experimental import pallas as pl
from jax.experimental.pallas import tpu as pltpu

F32 = jnp.float32
BF16 = jnp.bfloat16

D_MODEL = 1024
GRID_W = 64
N_MOD = 9
RMS_EPS = 1e-6
FFN_DIM = 2752
SSD_W = 512
SSD_HEADS = 8
SSD_HEAD_DIM = 64
SSD_STATE = 64
SSD_GROUPS = 2
SSD_CHUNK = 128
SSD_XBC = SSD_W + 2 * SSD_GROUPS * SSD_STATE
N_DIR = 2
HY_W = 512
HY_ORDER = 2
HY_EMB = 33
HY_BANDS = (HY_EMB - 1) // 2
HY_HIDDEN = 64
HY_MIN_DECAY = math.log(1e-2) / 1.5
HY_MAX_DECAY = math.log(1e-2) / 0.3
IN_SPLITS = (SSD_W, SSD_W + SSD_XBC, SSD_W + SSD_XBC + N_DIR * SSD_HEADS)
IN_COLS = IN_SPLITS[-1] + (HY_ORDER + 1) * HY_W

LANES = 128
SUBLANES = 8
FFN_PAD = 2816
FFN_CHUNK = 256
N_DT = N_DIR * SSD_HEADS
DT_PAD = LANES
IN_ROWS = IN_COLS + DT_PAD - N_DIR * SSD_HEADS
EMB_PAD = LANES
TOKEN_TILE = 512
HY_CH_TILE = 256
VMEM_LIMIT = 60 * 1024 * 1024
NEG_BIG = -1e30
HP = SSD_HEADS * SSD_HEAD_DIM


def _silu(x):
    return x * jax.nn.sigmoid(x)


def _softplus(x):
    return jnp.maximum(x, 0.0) + jnp.log1p(jnp.exp(-jnp.abs(x)))


def _rms_mod(x, gain, shift, scale):
    ms = jnp.mean(x * x, axis=-1, keepdims=True)
    return x * lax.rsqrt(ms + RMS_EPS) * (gain * (1.0 + scale)) + shift


def _dot(a, b):
    return jnp.dot(a, b, preferred_element_type=F32)


def _dot_nt(a, b):
    return lax.dot_general(a, b, (((1,), (1,)), ((), ())), preferred_element_type=F32)


def _split3(x):
    hi = x.astype(BF16)
    r = x - hi.astype(F32)
    mid = r.astype(BF16)
    lo = (r - mid.astype(F32)).astype(BF16)
    return hi, mid, lo


def _dot_exact_lhs(x, m01):
    hi, mid, lo = _split3(x)
    return _dot(hi, m01) + _dot(mid, m01) + _dot(lo, m01)


def _seg_edges(l, seg):
    pos = lax.broadcasted_iota(jnp.int32, (l, 1), 0) & (seg - 1)
    return pos == 0, pos == seg - 1


def _swiglu_acc(h, wg_ref, wu_ref, wd_ref):
    acc = None
    for k in range(FFN_PAD // FFN_CHUNK):
        cols = slice(k * FFN_CHUNK, (k + 1) * FFN_CHUNK)
        g = _dot_nt(h, wg_ref[cols, :])
        u = _dot_nt(h, wu_ref[cols, :])
        a = (_silu(g) * u).astype(BF16)
        part = _dot(a, wd_ref[cols, :])
        acc = part if acc is None else acc + part
    return acc


def _params(*semantics):
    return pltpu.CompilerParams(dimension_semantics=semantics, vmem_limit_bytes=VMEM_LIMIT)


LOAD_ROWS = 256
LOAD_SLOTS = 4


def _row_pieces(src_lo, src_hi, dst_lo):
    return [(r, min(LOAD_ROWS, src_hi - r), dst_lo + r - src_lo) for r in range(src_lo, src_hi, LOAD_ROWS)]


def _load_weights_bf16(pieces, stage_ref, sem_ref):
    def load(k):
        src, r, n, _, _ = pieces[k]
        slot = k % LOAD_SLOTS
        return pltpu.make_async_copy(src.at[0, pl.ds(r, n), :], stage_ref.at[slot, pl.ds(0, n), :], sem_ref.at[slot])

    for k in range(min(LOAD_SLOTS, len(pieces))):
        load(k).start()
    for k, (_, _, n, dst, dr) in enumerate(pieces):
        load(k).wait()
        dst[dr:dr + n, :] = stage_ref[k % LOAD_SLOTS, 0:n, :].astype(BF16)
        if k + LOAD_SLOTS < len(pieces):
            load(k + LOAD_SLOTS).start()


def _ada_kernel(cond_ref, w_ref, b_ref, o_ref):
    s = _silu(cond_ref[...]).astype(BF16)
    o_ref[...] = _dot(s, w_ref[...].astype(BF16)) + b_ref[...]


def _ada(cond, w_ada, b_ada):
    rows = cond.shape[0]
    n = w_ada.shape[-1]
    tn = D_MODEL
    return pl.pallas_call(
        _ada_kernel,
        grid=(n // tn,),
        in_specs=[
            pl.BlockSpec((rows, D_MODEL), lambda j: (0, 0)),
            pl.BlockSpec((None, D_MODEL, tn), lambda j: (0, 0, j)),
            pl.BlockSpec((1, tn), lambda j: (0, j)),
        ],
        out_specs=pl.BlockSpec((None, rows, tn), lambda j: (j, 0, 0)),
        out_shape=jax.ShapeDtypeStruct((n // tn, rows, tn), F32),
        compiler_params=_params("arbitrary"),
        name="ada",
    )(cond, w_ada, b_ada)


N_FILT_PLANES = 6


def _filter_kernel(fe_ref, fo_ref, w1_ref, b1_ref, fr_ref, w2_ref, b2_ref, w3_ref, dl_ref, f_ref, tw_ref,
                   kf_ref, kn_ref, *, m):
    freq = fr_ref[...]
    w1 = w1_ref[...].astype(BF16)
    w2 = w2_ref[...].astype(BF16)

    def hidden(feats):
        h = jnp.sin(freq * (_dot(feats.astype(BF16), w1) + b1_ref[...]))
        h = jnp.sin(freq * (_dot(h.astype(BF16), w2) + b2_ref[...]))
        return h.astype(BF16)

    feats_e = fe_ref[...]
    feats_o = fo_ref[...]
    hb_e = hidden(feats_e)
    hb_o = hidden(feats_o)
    t_e = feats_e[:, 0:1]
    t_o = feats_o[:, 0:1]
    row = lax.broadcasted_iota(jnp.int32, (m, 1), 0)
    sign = jnp.where((row & 1) == 0, 1.0, -1.0)
    cos_t = tw_ref[:, 0:1]
    sin_t = tw_ref[:, 1:2]
    f_top = f_ref[0:m, :]
    f_bot = f_ref[m:2 * m, :]

    def spectrum(k):
        kb = k.astype(BF16)
        return _dot(f_top, kb), jnp.where(row == 0, 0.0, _dot(f_bot, kb)), jnp.sum(k * sign, axis=0, keepdims=True)

    for i in range(HY_ORDER):
        dl = dl_ref[i:i + 1, :]
        c0 = (0 * HY_ORDER + i) * HY_W
        c1 = (1 * HY_ORDER + i) * HY_W
        w3f = w3_ref[:, c0:c0 + HY_W].astype(BF16)
        w3b = w3_ref[:, c1:c1 + HY_W].astype(BF16)
        win_e = jnp.exp(-t_e * dl)
        win_o = jnp.exp(-t_o * dl)
        k0e = _dot(hb_e, w3f) * win_e
        k0o = _dot(hb_o, w3f) * win_o
        k1e = jnp.where(row == 0, 0.0, _dot(hb_e, w3b) * win_e)
        k1o = _dot(hb_o, w3b) * win_o
        ker, _, ken = spectrum(k0e + k1e)
        _, kei, _ = spectrum(k0e - k1e)
        ar, ai, an = spectrum(k0o)
        br, bi, bn = spectrum(k1o)
        kf_ref[i, 0] = ker
        kf_ref[i, 1] = kei
        kf_ref[i, 2] = ar + cos_t * br + sin_t * bi
        kf_ref[i, 3] = ai + sin_t * br - cos_t * bi
        kf_ref[i, 4] = cos_t * ar + sin_t * ai + br
        kf_ref[i, 5] = cos_t * ai - sin_t * ar - bi
        kn_ref[i] = jnp.concatenate([ken, an - bn, bn - an, jnp.zeros((SUBLANES - 3, HY_W), F32)], axis=0)


def _filters_kernel(w1_ref, b1_ref, fr_ref, w2_ref, b2_ref, w3_ref, dl_ref, *refs, ms):
    n = len(ms)
    for k, m in enumerate(ms):
        fe_ref, fo_ref, f_ref, tw_ref = refs[4 * k:4 * k + 4]
        kf_ref, kn_ref = refs[4 * n + 2 * k:4 * n + 2 * k + 2]
        _filter_kernel(fe_ref, fo_ref, w1_ref, b1_ref, fr_ref, w2_ref, b2_ref, w3_ref, dl_ref, f_ref, tw_ref,
                       kf_ref, kn_ref, m=m)


def _filters(lengths, per_length, w1p, b1, freq, w2, b2, w3, deltas):
    ms = [l // 2 for l in lengths]
    full = lambda *shape: pl.BlockSpec(shape, lambda: (0,) * len(shape))
    in_specs = [full(EMB_PAD, HY_HIDDEN), full(1, HY_HIDDEN), full(1, HY_HIDDEN), full(HY_HIDDEN, HY_HIDDEN),
                full(1, HY_HIDDEN), full(HY_HIDDEN, N_DIR * HY_ORDER * HY_W), full(HY_ORDER, HY_W)]
    out_specs, out_shape = [], []
    for m in ms:
        in_specs += [full(m, EMB_PAD), full(m, EMB_PAD), full(2 * m, m), full(m, 2)]
        out_specs += [full(HY_ORDER, N_FILT_PLANES, m, HY_W), full(HY_ORDER, SUBLANES, HY_W)]
        out_shape += [jax.ShapeDtypeStruct((HY_ORDER, N_FILT_PLANES, m, HY_W), F32),
                      jax.ShapeDtypeStruct((HY_ORDER, SUBLANES, HY_W), F32)]
    outs = pl.pallas_call(
        functools.partial(_filters_kernel, ms=ms),
        in_specs=in_specs,
        out_specs=out_specs,
        out_shape=out_shape,
        compiler_params=pltpu.CompilerParams(vmem_limit_bytes=VMEM_LIMIT),
        name="filt",
    )(w1p, b1, freq, w2, b2, w3, deltas, *[a for group in per_length for a in group])
    return [(outs[2 * k], outs[2 * k + 1]) for k in range(len(ms))]


def _const_spec(shape):
    return pl.BlockSpec(shape, lambda i: (0,) * len(shape), pipeline_mode=pl.Buffered(1))


def _group_specs(ctx_tiles, width):
    tm = TOKEN_TILE
    return [pl.BlockSpec((tm, width), lambda i: (jnp.minimum(i, ctx_tiles - 1), 0)),
            pl.BlockSpec((tm, width), lambda i: (jnp.maximum(i - ctx_tiles, 0), 0))]


def _mod_rows(mod_ref, ctx_tiles, n_lat, tiles_per_seq):
    i = pl.program_id(0)
    r = jnp.where(i < ctx_tiles, n_lat, (i - ctx_tiles) // tiles_per_seq)
    return [mod_ref[k, pl.ds(r, 1), :] for k in range(N_MOD)]


NEXT_CAST_ROWS = 128
NEXT_CAST_ROWS_OUT = 64


def _ffn_in_kernel(xc_ref, xl_ref, mod_ref, n1_ref, nm_ref, wg_hbm, wu_hbm, wd_hbm, wi_hbm,
                   ng_ref, nu_ref, nd_ref, no_ref,
                   x1_ref, z_ref, xbc_ref, dt_ref, hy_ref, ngb_ref, nub_ref, ndb_ref, nob_ref,
                   wg_ref, wu_ref, wd_ref, wi_ref, stage_ref, sem_ref, *, ctx_tiles, n_lat, tiles_per_seq):
    @pl.when(pl.program_id(0) == 0)
    def _():
        o1, o2, o3 = IN_SPLITS
        pieces = []
        for src, dst in ((wg_hbm, wg_ref), (wu_hbm, wu_ref), (wd_hbm, wd_ref)):
            pieces += [(src, r, n, dst, dr) for r, n, dr in _row_pieces(0, FFN_DIM, 0)]
            dst[FFN_DIM:, :] = jnp.zeros((FFN_PAD - FFN_DIM, D_MODEL), BF16)
        pieces += [(wi_hbm, r, n, wi_ref, dr) for r, n, dr in _row_pieces(0, o3, 0) + _row_pieces(o3, IN_COLS, o2 + DT_PAD)]
        wi_ref[o3:o2 + DT_PAD, :] = jnp.zeros((o2 + DT_PAD - o3, D_MODEL), BF16)
        _load_weights_bf16(pieces, stage_ref, sem_ref)

    blk = jnp.minimum(pl.program_id(0), FFN_PAD // NEXT_CAST_ROWS - 1)
    wrow = blk * NEXT_CAST_ROWS + lax.broadcasted_iota(jnp.int32, (NEXT_CAST_ROWS, 1), 0)
    for src, dst in ((ng_ref, ngb_ref), (nu_ref, nub_ref), (nd_ref, ndb_ref)):
        dst[...] = jnp.where(wrow < FFN_DIM, src[...], 0.0).astype(BF16)
    nob_ref[...] = no_ref[...].astype(BF16)

    x = jnp.where(pl.program_id(0) < ctx_tiles, xc_ref[...], xl_ref[...])
    mod = _mod_rows(mod_ref, ctx_tiles, n_lat, tiles_per_seq)
    h = _rms_mod(x, n1_ref[...], mod[0], mod[1]).astype(BF16)
    x1 = x + (0.5 * mod[2]) * _swiglu_acc(h, wg_ref, wu_ref, wd_ref)
    x1_ref[...] = x1
    h2 = _rms_mod(x1, nm_ref[...], mod[3], mod[4]).astype(BF16)
    o1, o2, _ = IN_SPLITS
    z_ref[...] = _dot_nt(h2, wi_ref[0:o1, :])
    xbc_ref[...] = _dot_nt(h2, wi_ref[o1:o2, :])
    dt_ref[...] = _dot_nt(wi_ref[o2:o2 + N_DT, :], h2)
    hy = _dot_nt(h2, wi_ref[o2 + DT_PAD:, :])
    for k in range(hy_ref.shape[0]):
        hy_ref[k] = hy[:, k * LANES:(k + 1) * LANES]


def _ffn_in(x_ctx, x_lat, mod, n_lat, l_lat, n1, nm, wg, wu, wd, wi, next_ffn, next_out):
    tm = TOKEN_TILE
    ctx_tiles = x_ctx.shape[0] // tm
    tokens = x_ctx.shape[0] + x_lat.shape[0]
    steps = tokens // tm
    ffn_blocks = FFN_PAD // NEXT_CAST_ROWS
    out_blocks = D_MODEL // NEXT_CAST_ROWS_OUT
    assert steps >= ffn_blocks and steps >= out_blocks
    row_spec = lambda n: pl.BlockSpec((tm, n), lambda i: (i, 0))
    widths = (D_MODEL, SSD_W, SSD_XBC)
    hy_tiles = (HY_ORDER + 1) * HY_W // LANES
    dt_spec = pl.BlockSpec((N_DT, tm), lambda i: (0, i))
    outs = pl.pallas_call(
        functools.partial(_ffn_in_kernel, ctx_tiles=ctx_tiles, n_lat=n_lat, tiles_per_seq=l_lat // tm),
        grid=(steps,),
        in_specs=_group_specs(ctx_tiles, D_MODEL) + [
            _const_spec(mod.shape), _const_spec((1, D_MODEL)), _const_spec((1, D_MODEL))]
        + [pl.BlockSpec(memory_space=pl.ANY)] * 4
        + [pl.BlockSpec((None, NEXT_CAST_ROWS, D_MODEL), lambda i: (0, jnp.minimum(i, ffn_blocks - 1), 0))] * 3
        + [pl.BlockSpec((None, NEXT_CAST_ROWS_OUT, D_MODEL), lambda i: (0, jnp.minimum(i, out_blocks - 1), 0))],
        out_specs=[row_spec(n) for n in widths] + [dt_spec, pl.BlockSpec((hy_tiles, tm, LANES), lambda i: (0, i, 0))]
        + [pl.BlockSpec((NEXT_CAST_ROWS, D_MODEL), lambda i: (jnp.minimum(i, ffn_blocks - 1), 0))] * 3
        + [pl.BlockSpec((NEXT_CAST_ROWS_OUT, D_MODEL), lambda i: (jnp.minimum(i, out_blocks - 1), 0))],
        out_shape=[jax.ShapeDtypeStruct((tokens, n), F32) for n in widths]
        + [jax.ShapeDtypeStruct((N_DT, tokens), F32), jax.ShapeDtypeStruct((hy_tiles, tokens, LANES), F32)]
        + [jax.ShapeDtypeStruct((FFN_PAD, D_MODEL), BF16)] * 3 + [jax.ShapeDtypeStruct((D_MODEL, D_MODEL), BF16)],
        scratch_shapes=[pltpu.VMEM((FFN_PAD, D_MODEL), BF16)] * 3 + [
            pltpu.VMEM((IN_ROWS, D_MODEL), BF16), pltpu.VMEM((LOAD_SLOTS, LOAD_ROWS, D_MODEL), F32),
            pltpu.SemaphoreType.DMA((LOAD_SLOTS,))],
        compiler_params=_params("arbitrary"),
        name="ffn_in",
    )(x_ctx, x_lat, mod, n1, nm, wg, wu, wd, wi, *next_ffn, next_out)
    return outs[:5], outs[5:8], outs[8]


def _out_ffn_kernel(x1_ref, ys_ref, yh_ref, mod_ref, n3_ref, nf_ref,
                    wo_ref, wg_ref, wu_ref, wd_ref, oc_ref, ol_ref, *, ctx_tiles, n_lat, tiles_per_seq):
    is_ctx = pl.program_id(0) < ctx_tiles
    mod = _mod_rows(mod_ref, ctx_tiles, n_lat, tiles_per_seq)
    y = jnp.concatenate([ys_ref[...]] + [yh_ref[k] for k in range(yh_ref.shape[0])], axis=1).astype(BF16)
    x2 = x1_ref[...] + mod[5] * _dot(y, wo_ref[...])
    h = _rms_mod(x2, n3_ref[...], mod[6], mod[7]).astype(BF16)
    x3 = x2 + (0.5 * mod[8]) * _swiglu_acc(h, wg_ref, wu_ref, wd_ref)
    ms = jnp.mean(x3 * x3, axis=-1, keepdims=True)
    out = x3 * lax.rsqrt(ms + RMS_EPS) * nf_ref[...]

    @pl.when(is_ctx)
    def _():
        oc_ref[...] = out

    @pl.when(jnp.logical_not(is_ctx))
    def _():
        ol_ref[...] = out


def _out_ffn(x1, ys, yh, t_ctx, mod, n_lat, l_lat, n3, nf, wo, wg, wu, wd):
    tm = TOKEN_TILE
    ctx_tiles = t_ctx // tm
    tokens = x1.shape[0]
    return pl.pallas_call(
        functools.partial(_out_ffn_kernel, ctx_tiles=ctx_tiles, n_lat=n_lat, tiles_per_seq=l_lat // tm),
        grid=(tokens // tm,),
        in_specs=[pl.BlockSpec((tm, D_MODEL), lambda i: (i, 0)), pl.BlockSpec((tm, SSD_W), lambda i: (i, 0)),
                  pl.BlockSpec((HY_W // LANES, tm, LANES), lambda i: (0, i, 0)),
            _const_spec(mod.shape), _const_spec((1, D_MODEL)), _const_spec((1, D_MODEL)),
            _const_spec(wo.shape), _const_spec(wg.shape), _const_spec(wu.shape), _const_spec(wd.shape)],
        out_specs=_group_specs(ctx_tiles, D_MODEL),
        out_shape=[jax.ShapeDtypeStruct((t_ctx, D_MODEL), F32),
                   jax.ShapeDtypeStruct((tokens - t_ctx, D_MODEL), F32)],
        compiler_params=_params("arbitrary"),
        name="out_ffn",
    )(x1, ys, yh, mod, n3, nf, wo, wg, wu, wd)


def _ssd_kernel(z_ref, xbc_ref, dt_ref, init_ref, cw_ref, cb_ref, dtb_ref, alog_ref, dexp_ref, nw_ref, e_ref,
                y_ref, fin_ref,
                xs_s, b_s, c_s, ec_s, dst_s, et_s, cum_s, ct_s, xd_s,
                *, l, seg, zero_init, write_final):
    q = SSD_CHUNK
    nc = l // q
    n_seq = y_ref.shape[0] // l
    cr = max(seg, q)
    e_mats = [e_ref[k].astype(BF16) for k in range(e_ref.shape[0])]
    dt_t = _softplus(dt_ref[...] + dtb_ref[...])
    a_t = dt_t * (-jnp.exp(alog_ref[...]))

    first, last = _seg_edges(cr, seg)

    for r0 in range(0, n_seq * l, cr):
        rows = slice(r0, r0 + cr)
        x = xbc_ref[rows, :]
        w = cw_ref[...]
        prev = jnp.where(first, 0.0, pltpu.roll(x, 1, 0))
        nxt = jnp.where(last, 0.0, pltpu.roll(x, cr - 1, 0))
        u = _silu(prev * w[0:1, :] + x * w[1:2, :] + nxt * w[2:3, :] + cb_ref[...])
        xs_s[rows, :] = u[:, :SSD_W]
        b_s[rows, :] = u[:, SSD_W:SSD_W + LANES]
        c_s[rows, :] = u[:, SSD_W + LANES:]

    row_g = lax.broadcasted_iota(jnp.int32, (LANES, N_DIR * HP), 0) // SSD_STATE
    lane_g = (lax.broadcasted_iota(jnp.int32, (LANES, N_DIR * HP), 1) % HP) // (HP // SSD_GROUPS)
    own_t = row_g == lane_g

    ii = lax.broadcasted_iota(jnp.int32, (q, q), 0)
    jj = lax.broadcasted_iota(jnp.int32, (q, q), 1)
    tri_upper = (ii <= jj).astype(BF16)
    lane = lax.broadcasted_iota(jnp.int32, (q, LANES), 1)
    low_half = lane < SSD_STATE
    is_fwd_row = lax.broadcasted_iota(jnp.int32, (N_DT, 1), 0) < SSD_HEADS

    chunks = range(n_seq * nc)

    for c in chunks:
        rows = slice(c * q, (c + 1) * q)
        a_c = a_t[:, rows]
        cum_f = _dot_exact_lhs(a_c, tri_upper)
        tot_c = cum_f[:, q - 1:q]
        cum_t = jnp.where(is_fwd_row, cum_f, tot_c - cum_f + a_c)
        ct_s[c] = cum_t
        stack = jnp.concatenate([cum_t, jnp.exp(cum_t), jnp.exp(tot_c - cum_t), dt_t[:, rows],
                                 jnp.zeros((LANES - 4 * N_DT, q), F32)], axis=0)
        cum_s[c] = stack.T

    def expand_pass(c):
        rows = slice(c * q, (c + 1) * q)
        cum = cum_s[c]
        cum_b = cum.astype(BF16)
        ec_s[rows, :] = _dot(cum_b, e_mats[1])
        tot = jnp.where(lane[0:1, :] < SSD_HEADS, cum[q - 1:q, :], cum[0:1, :])
        tot = jnp.where(lane[0:1, :] < N_DT, tot, 0.0)
        et_s[c] = _dot_exact_lhs(jnp.broadcast_to(jnp.exp(tot), (SUBLANES, LANES)), e_mats[0])
        xs = xs_s[rows, :]
        xd = jnp.concatenate([xs, xs], axis=1) * _dot(cum_b, e_mats[3])
        xd_s[rows, :] = xd
        w = (xd * _dot(cum_b, e_mats[2])).astype(BF16)
        dst_s[c] = jnp.where(own_t, _dot(b_s[rows, :].T.astype(BF16), w), 0.0)

    def local_pass(c):
        rows = slice(c * q, (c + 1) * q)
        cum = cum_s[c]
        cum_t = ct_s[c]
        xd = xd_s[rows, :]
        bcb = b_s[rows, :].astype(BF16)
        cc = c_s[rows, :]
        g_mats = [_dot_nt(jnp.where(low_half, cc, 0.0).astype(BF16), bcb),
                  _dot_nt(jnp.where(low_half, 0.0, cc).astype(BF16), bcb)]
        y_parts = []
        for pair in range(SSD_HEADS // 2):
            g = pair // (SSD_HEADS // 2 // SSD_GROUPS)
            acc = None
            for d in range(N_DIR):
                keep = (ii >= jj) if d == 0 else (ii <= jj)
                s_mats = []
                for hh in (2 * pair, 2 * pair + 1):
                    col = d * SSD_HEADS + hh
                    diff = cum[:, col:col + 1] - cum_t[col:col + 1, :]
                    decay = jnp.exp(jnp.where(keep, diff, NEG_BIG))
                    s_mats.append((g_mats[g] * decay).astype(BF16))
                lhs = jnp.concatenate(s_mats, axis=1)
                xp = xd[:, d * HP + pair * LANES:d * HP + (pair + 1) * LANES]
                rhs = jnp.concatenate([jnp.where(low_half, xp, 0.0), jnp.where(low_half, 0.0, xp)],
                                      axis=0).astype(BF16)
                part = _dot(lhs, rhs)
                acc = part if acc is None else acc + part
            y_parts.append(acc)
        y_ref[rows, :] = jnp.concatenate(y_parts, axis=1)

    for c in chunks:
        expand_pass(c)
        local_pass(c)

    half = HP // SSD_GROUPS
    for s in range(n_seq):
        states = []
        for d in range(N_DIR):
            if zero_init:
                states.append(jnp.zeros((LANES, HP), F32))
            else:
                s0 = init_ref[s, d]
                states.append(jnp.where(own_t[:, :HP], jnp.concatenate([s0, s0], axis=1).T, 0.0))

        for k in range(nc):
            for d in range(N_DIR):
                c = s * nc + (k if d == 0 else nc - 1 - k)
                rows = slice(c * q, (c + 1) * q)
                lanes = slice(d * HP, (d + 1) * HP)
                if zero_init and k == 0:
                    states[d] = dst_s[c, :, lanes]
                    continue
                y_off = _dot(c_s[rows, :].astype(BF16), states[d].astype(BF16)) * ec_s[rows, lanes]
                y_ref[rows, :] = y_ref[rows, :] + y_off
                states[d] = states[d] * et_s[c, 0:1, lanes] + dst_s[c, :, lanes]

        for d in range(N_DIR if write_final else 0):
            st = states[d].T
            fin_ref[s, d, 0:half, :] = st[0:half, 0:SSD_STATE]
            fin_ref[s, d, half:HP, :] = st[half:HP, SSD_STATE:2 * SSD_STATE]

        for c in range(s * nc, (s + 1) * nc):
            rows = slice(c * q, (c + 1) * q)
            y = y_ref[rows, :] + xs_s[rows, :] * dexp_ref[...]
            y = y * _silu(z_ref[rows, :])
            ms = jnp.mean(y * y, axis=-1, keepdims=True)
            y_ref[rows, :] = y * lax.rsqrt(ms + RMS_EPS) * nw_ref[...]


SSD_STEP_ROWS = 1024


def _ssd_groups_kernel(*refs, ctx_steps, ctx, lat):
    @pl.when(pl.program_id(0) < ctx_steps)
    def _():
        _ssd_kernel(*refs, l=ctx[0], seg=ctx[1], zero_init=True, write_final=True)

    @pl.when(pl.program_id(0) >= ctx_steps)
    def _():
        _ssd_kernel(*refs, l=lat[0], seg=lat[1], zero_init=False, write_final=False)


def _ssd(z, xbc, dt, n_ctx, l_ctx, seg_ctx, l_lat, seg_lat, init_lat, cw, cb, dtb, alog, dexp, nw, e_mat):
    rows = SSD_STEP_ROWS
    tokens = z.shape[0]
    ctx_steps = n_ctx * l_ctx // rows
    ctx_seqs, lat_seqs = rows // l_ctx, rows // l_lat
    seq_spec = lambda n: pl.BlockSpec((rows, n), lambda b: (b, 0))
    const = lambda *shape: pl.BlockSpec(shape, lambda b: (0,) * len(shape))
    nc = rows // SSD_CHUNK
    return pl.pallas_call(
        functools.partial(_ssd_groups_kernel, ctx_steps=ctx_steps, ctx=(l_ctx, seg_ctx), lat=(l_lat, seg_lat)),
        grid=(tokens // rows,),
        in_specs=[seq_spec(SSD_W), seq_spec(SSD_XBC), pl.BlockSpec((N_DT, rows), lambda b: (0, b)),
                  pl.BlockSpec((lat_seqs, N_DIR, HP, SSD_STATE), lambda b: (jnp.maximum(b - ctx_steps, 0), 0, 0, 0)),
                  const(3, SSD_XBC), const(1, SSD_XBC), const(N_DT, 1), const(N_DT, 1),
                  const(1, SSD_W), const(1, SSD_W), const(*e_mat.shape)],
        out_specs=[pl.BlockSpec((rows, SSD_W), lambda b: (b, 0)),
                   pl.BlockSpec((ctx_seqs, N_DIR, HP, SSD_STATE), lambda b: (jnp.minimum(b, ctx_steps - 1), 0, 0, 0))],
        out_shape=[jax.ShapeDtypeStruct((tokens, SSD_W), F32),
                   jax.ShapeDtypeStruct((n_ctx, N_DIR, HP, SSD_STATE), F32)],
        scratch_shapes=[pltpu.VMEM((rows, SSD_W), F32), pltpu.VMEM((rows, LANES), F32), pltpu.VMEM((rows, LANES), F32),
                        pltpu.VMEM((rows, N_DIR * HP), F32),
                        pltpu.VMEM((nc, LANES, N_DIR * HP), F32), pltpu.VMEM((nc, SUBLANES, N_DIR * HP), F32),
                        pltpu.VMEM((nc, SSD_CHUNK, LANES), F32), pltpu.VMEM((nc, N_DT, SSD_CHUNK), F32),
                        pltpu.VMEM((rows, N_DIR * HP), F32)],
        compiler_params=_params("arbitrary"),
        name="ssd",
    )(z, xbc, dt, init_lat, cw, cb, dtb, alog, dexp, nw, e_mat)


HY_ROW_BLOCK = 16
HY_STEP_ROWS = 2048


def _hyena_kernel(v_ref, x1_ref, x2_ref, wv_ref, w1_ref, w2_ref, bv_ref, b1_ref, b2_ref,
                  f_ref, g_ref, kf_ref, kn_ref, skip_ref, o_ref, spec_s, prod_s, *, l, seg):
    m = l // 2
    tiles = o_ref.shape[0]
    n_seq = o_ref.shape[1] // l
    ct = tiles * LANES
    rb = HY_ROW_BLOCK
    first, last = _seg_edges(m, seg // 2)
    row0 = lax.broadcasted_iota(jnp.int32, (rb, 1), 0) == 0
    ev = slice(0, ct)
    od = slice(ct, 2 * ct)

    def conv_eo(x_ref, w_ref, b_ref, base):
        xe, xo = (jnp.concatenate([x_ref[k, pl.ds(base + p, m, stride=2), :] for k in range(tiles)], axis=1)
                  for p in range(2))
        w = w_ref[...]
        b = b_ref[...]
        xo_prev = jnp.where(first, 0.0, pltpu.roll(xo, 1, 0))
        xe_next = jnp.where(last, 0.0, pltpu.roll(xe, m - 1, 0))
        ce = xo_prev * w[0:1, :] + xe * w[1:2, :] + xo * w[2:3, :] + b
        co = xe * w[0:1, :] + xo * w[1:2, :] + xe_next * w[2:3, :] + b
        return jnp.concatenate([ce, co], axis=1)

    def pointwise(s, i, r0):
        re = slice(s * l + r0, s * l + r0 + rb)
        im = slice(s * l + m + r0, s * l + m + r0 + rb)
        er, orr = spec_s[re, ev], spec_s[re, od]
        ei, oi = spec_s[im, ev], spec_s[im, od]
        ker, kei, kor, koi, vr, vi = (kf_ref[i, p, r0:r0 + rb, :] for p in range(N_FILT_PLANES))
        if r0 == 0:
            e_n, o_n = ei[0:1, :], oi[0:1, :]
            ei = jnp.where(row0, 0.0, ei)
            oi = jnp.where(row0, 0.0, oi)
        pe_r = er * ker - ei * kei + orr * vr - oi * vi
        pe_i = er * kei + ei * ker + orr * vi + oi * vr
        po_r = er * kor - ei * koi + orr * ker - oi * kei
        po_i = er * koi + ei * kor + orr * kei + oi * ker
        if r0 == 0:
            kn = kn_ref[i]
            pe_i = jnp.where(row0, e_n * kn[0:1, :] + o_n * kn[2:3, :], pe_i)
            po_i = jnp.where(row0, e_n * kn[1:2, :] + o_n * kn[0:1, :], po_i)
        prod_s[re, ev] = pe_r.astype(BF16)
        prod_s[re, od] = po_r.astype(BF16)
        prod_s[im, ev] = pe_i.astype(BF16)
        prod_s[im, od] = po_i.astype(BF16)

    seqs = range(n_seq)
    zz = [conv_eo(v_ref, wv_ref, bv_ref, s * l) for s in seqs]
    for i, (xg_ref, wg_ref, bg_ref) in enumerate(((x1_ref, w1_ref, b1_ref), (x2_ref, w2_ref, b2_ref))):
        for s in seqs:
            spec_s[s * l:(s + 1) * l, :] = _dot(f_ref[...], zz[s].astype(BF16))
        for s in seqs:
            for r0 in range(0, m, rb):
                pointwise(s, i, r0)
        skip = skip_ref[i:i + 1, :]
        skip2 = jnp.concatenate([skip, skip], axis=1)
        for s in seqs:
            conv = _dot(g_ref[...], prod_s[s * l:(s + 1) * l, :])
            zz[s] = conv_eo(xg_ref, wg_ref, bg_ref, s * l) * (conv + zz[s] * skip2)
    for s in seqs:
        for k in range(tiles):
            o_ref[k, pl.ds(s * l, m, stride=2), :] = zz[s][:, k * LANES:(k + 1) * LANES]
            o_ref[k, pl.ds(s * l + 1, m, stride=2), :] = zz[s][:, ct + k * LANES:ct + (k + 1) * LANES]


def _hyena_groups_kernel(v_ref, x1_ref, x2_ref, wv_ref, w1_ref, w2_ref, bv_ref, b1_ref, b2_ref,
                         fc_ref, gc_ref, kfc_ref, knc_ref, fl_ref, gl_ref, kfl_ref, knl_ref, skip_ref,
                         o_ref, spec_s, prod_s, *, ctx_steps, ctx, lat):
    common = (v_ref, x1_ref, x2_ref, wv_ref, w1_ref, w2_ref, bv_ref, b1_ref, b2_ref)

    @pl.when(pl.program_id(1) < ctx_steps)
    def _():
        _hyena_kernel(*common, fc_ref, gc_ref, kfc_ref, knc_ref, skip_ref, o_ref, spec_s, prod_s, l=ctx[0], seg=ctx[1])

    @pl.when(pl.program_id(1) >= ctx_steps)
    def _():
        _hyena_kernel(*common, fl_ref, gl_ref, kfl_ref, knl_ref, skip_ref, o_ref, spec_s, prod_s, l=lat[0], seg=lat[1])


def _hyena(hy, n_ctx, l_ctx, seg_ctx, l_lat, seg_lat, cw, cb, consts_ctx, consts_lat, skip):
    ct = HY_CH_TILE
    nct = HY_W // ct
    tiles = ct // LANES
    rows = HY_STEP_ROWS
    tokens = hy.shape[1]
    ctx_steps = n_ctx * l_ctx // rows
    part = lambda p: pl.BlockSpec((tiles, rows, LANES), lambda j, b: (p * nct + j, b, 0))
    wpart = lambda p: pl.BlockSpec((3, ct), lambda j, b: (0, p * nct + j))
    bpart = lambda p: pl.BlockSpec((1, ct), lambda j, b: (0, p * nct + j))

    def const_specs(l):
        m = l // 2
        return [pl.BlockSpec((l, m), lambda j, b: (0, 0)),
                pl.BlockSpec((m, l), lambda j, b: (0, 0)),
                pl.BlockSpec((HY_ORDER, N_FILT_PLANES, m, ct), lambda j, b: (0, 0, 0, j)),
                pl.BlockSpec((HY_ORDER, SUBLANES, ct), lambda j, b: (0, 0, j))]

    return pl.pallas_call(
        functools.partial(_hyena_groups_kernel, ctx_steps=ctx_steps, ctx=(l_ctx, seg_ctx), lat=(l_lat, seg_lat)),
        grid=(nct, tokens // rows),
        in_specs=[part(0), part(1), part(2), wpart(0), wpart(1), wpart(2), bpart(0), bpart(1), bpart(2)]
        + const_specs(l_ctx) + const_specs(l_lat) + [pl.BlockSpec((HY_ORDER, ct), lambda j, b: (0, j))],
        out_specs=pl.BlockSpec((tiles, rows, LANES), lambda j, b: (j, b, 0)),
        out_shape=jax.ShapeDtypeStruct((HY_W // LANES, tokens, LANES), F32),
        scratch_shapes=[pltpu.VMEM((rows, 2 * ct), F32), pltpu.VMEM((rows, 2 * ct), BF16)],
        compiler_params=_params("arbitrary", "arbitrary"),
        name="hyena",
    )(hy, hy, hy, cw, cw, cw, cb, cb, cb, *consts_ctx, *consts_lat, skip)


def _dft_mats(l):
    n = 2 * l
    f = np.arange(l, dtype=np.int64)[:, None]
    t = np.arange(l, dtype=np.int64)[None, :]
    ang = 2.0 * np.pi * ((f * t) % n).astype(np.float64) / n
    alt = np.where(np.arange(l) % 2 == 0, 1.0, -1.0)
    top = np.cos(ang)
    bot = -np.sin(ang)
    bot[0, :] = alt
    fwd = np.concatenate([top, bot], axis=0)
    wf = np.full((l,), 2.0)
    wf[0] = 1.0
    gtop = np.cos(ang).T * wf[None, :] / n
    gbot = -np.sin(ang).T * 2.0 / n
    gbot[:, 0] = alt / n
    inv = np.concatenate([gtop, gbot], axis=1)
    return fwd.astype(np.float32), inv.astype(np.float32)


def _filter_feats(l):
    t = np.linspace(0.0, 1.0, l)[:, None]
    w = (2.0 * np.pi / l) * np.arange(l, dtype=np.float64)[:, None]
    f = np.linspace(1e-4, HY_BANDS - 1, HY_BANDS)[None, :]
    feats = np.concatenate([t, np.cos(f * w), -np.sin(f * w)], axis=-1)
    out = np.zeros((l, EMB_PAD), np.float32)
    out[:, :HY_EMB] = feats
    return out[0::2], out[1::2]


def _shift_twiddles(l):
    theta = 2.0 * np.pi * np.arange(l // 2, dtype=np.float64) / l
    return np.stack([np.cos(theta), np.sin(theta)], axis=1).astype(np.float32)


def _head_expand():
    n_blocks = LANES // N_DT // 2
    e = np.zeros((n_blocks, LANES, N_DIR * HP), np.float32)
    for k in range(n_blocks):
        for j in range(N_DT):
            e[k, k * N_DT + j, j * SSD_HEAD_DIM:(j + 1) * SSD_HEAD_DIM] = 1.0
    return e


def kernel(x_prompt, x_sample, state_ssd, c, c_ctx, w_ada, b_ada, norm_ffn1, ffn1_w_gate, ffn1_w_up, ffn1_w_down, norm_mix, w_in, w_out, ssd_conv_w, ssd_conv_b, ssd_dt_bias, ssd_a_log, ssd_d, ssd_norm_w, hy_conv_w, hy_conv_b, hy_w1, hy_b1, hy_freq, hy_w2, hy_b2, hy_w3, hy_skip, norm_ffn2, ffn2_w_gate, ffn2_w_up, ffn2_w_down, norm_final):
    assert w_ada.shape[0] == 1, "single layer"
    n_ctx, l_ctx, _ = x_prompt.shape
    n_lat, l_lat, _ = x_sample.shape
    t_ctx = n_ctx * l_ctx

    cond = jnp.zeros((16, D_MODEL), F32).at[:n_lat].set(c).at[n_lat].set(c_ctx)
    mod = _ada(cond, w_ada, b_ada)

    tr = lambda w: jnp.swapaxes(w, 1, 2)

    row = lambda v: v.reshape(1, -1)
    dtb, alog = ssd_dt_bias[0].reshape(N_DT, 1), ssd_a_log[0].reshape(N_DT, 1)
    dexp = jnp.repeat(ssd_d[0], SSD_HEAD_DIM).reshape(1, SSD_W)
    e_mat = jnp.asarray(_head_expand())
    w1p = jnp.pad(hy_w1[0], ((0, EMB_PAD - HY_EMB), (0, 0)))
    deltas = jnp.asarray(np.abs(np.linspace(HY_MIN_DECAY, HY_MAX_DECAY, HY_ORDER * HY_W))
                         .reshape(HY_ORDER, HY_W).astype(np.float32))

    (x1, z, xbc, dt, hy), (wg2, wu2, wd2), wo = _ffn_in(
        x_prompt.reshape(t_ctx, D_MODEL), x_sample.reshape(n_lat * l_lat, D_MODEL), mod, n_lat, l_lat,
        row(norm_ffn1[0]), row(norm_mix[0]), tr(ffn1_w_gate), tr(ffn1_w_up), ffn1_w_down, tr(w_in),
        [tr(ffn2_w_gate), tr(ffn2_w_up), ffn2_w_down], w_out)

    lengths = (l_ctx, l_lat)
    dft = [tuple(jnp.asarray(a).astype(BF16) for a in _dft_mats(l // 2)) for l in lengths]
    filt = _filters(lengths,
                    [tuple(jnp.asarray(a) for a in _filter_feats(l)) + (dft[k][0], jnp.asarray(_shift_twiddles(l)))
                     for k, l in enumerate(lengths)],
                    w1p, row(hy_b1[0]), row(hy_freq[0]), hy_w2[0], row(hy_b2[0]), hy_w3[0], deltas)
    hyena_consts = [dft[k] + filt[k] for k in range(len(lengths))]

    lat_init = state_ssd[:, 0].reshape(n_lat, N_DIR, HP, SSD_STATE)
    ys, ctx_fin = _ssd(z, xbc, dt, n_ctx, l_ctx, l_ctx, l_lat, GRID_W, lat_init, ssd_conv_w[0], row(ssd_conv_b[0]),
                       dtb, alog, dexp, row(ssd_norm_w[0]), e_mat)
    yh = _hyena(hy, n_ctx, l_ctx, l_ctx, l_lat, GRID_W, hy_conv_w[0], row(hy_conv_b[0]),
                hyena_consts[0], hyena_consts[1], hy_skip[0])

    y_ctx, y_lat = _out_ffn(x1, ys, yh, t_ctx, mod, n_lat, l_lat, row(norm_ffn2[0]), row(norm_final),
                            wo, wg2, wu2, wd2)
    new_state = ctx_fin.reshape(n_ctx, 1, N_DIR, SSD_HEADS, SSD_HEAD_DIM, SSD_STATE).astype(x_prompt.dtype)
    return (y_ctx.reshape(n_ctx, l_ctx, D_MODEL), y_lat.reshape(n_lat, l_lat, D_MODEL), new_state)
```

```python
import functools
import math

import numpy as np
import jax
import jax.numpy as jnp
from jax import lax
from jax.experimental import pallas as pl
from jax.experimental.pallas import tpu as pltpu

F32 = jnp.float32
BF16 = jnp.bfloat16

D_MODEL = 1024
GRID_W = 64
N_MOD = 9
RMS_EPS = 1e-6
FFN_DIM = 2752
SSD_W = 512
SSD_HEADS = 8
SSD_HEAD_DIM = 64
SSD_STATE = 64
SSD_GROUPS = 2
SSD_CHUNK = 128
SSD_XBC = SSD_W + 2 * SSD_GROUPS * SSD_STATE
N_DIR = 2
HY_W = 512
HY_ORDER = 2
HY_EMB = 33
HY_BANDS = (HY_EMB - 1) // 2
HY_HIDDEN = 64
HY_MIN_DECAY = math.log(1e-2) / 1.5
HY_MAX_DECAY = math.log(1e-2) / 0.3
IN_SPLITS = (SSD_W, SSD_W + SSD_XBC, SSD_W + SSD_XBC + N_DIR * SSD_HEADS)
IN_COLS = IN_SPLITS[-1] + (HY_ORDER + 1) * HY_W

LANES = 128
SUBLANES = 8
FFN_PAD = 2816
FFN_CHUNK = 256
N_DT = N_DIR * SSD_HEADS
DT_PAD = LANES
IN_ROWS = IN_COLS + DT_PAD - N_DIR * SSD_HEADS
EMB_PAD = LANES
TOKEN_TILE = 512
HY_CH_TILE = 256
VMEM_LIMIT = 60 * 1024 * 1024
COND_ROWS = 16
NEG_BIG = -1e30
HP = SSD_HEADS * SSD_HEAD_DIM


def _silu(x):
    return x * jax.nn.sigmoid(x)


def _softplus(x):
    return jnp.maximum(x, 0.0) + jnp.log1p(jnp.exp(-jnp.abs(x)))


def _rms_mod(x, gain, shift, scale):
    ms = jnp.mean(x * x, axis=-1, keepdims=True)
    return x * lax.rsqrt(ms + RMS_EPS) * (gain * (1.0 + scale)) + shift


def _dot(a, b):
    return jnp.dot(a, b, preferred_element_type=F32)


def _dot_nt(a, b):
    return lax.dot_general(a, b, (((1,), (1,)), ((), ())), preferred_element_type=F32)


def _split3(x):
    hi = x.astype(BF16)
    r = x - hi.astype(F32)
    mid = r.astype(BF16)
    lo = (r - mid.astype(F32)).astype(BF16)
    return hi, mid, lo


def _dot_exact_lhs(x, m01):
    hi, mid, lo = _split3(x)
    return _dot(hi, m01) + _dot(mid, m01) + _dot(lo, m01)


def _seg_edges(l, seg):
    pos = lax.broadcasted_iota(jnp.int32, (l, 1), 0) & (seg - 1)
    return pos == 0, pos == seg - 1


def _swiglu_acc(h, wg_ref, wu_ref, wd_ref):
    acc = None
    for k in range(FFN_PAD // FFN_CHUNK):
        cols = slice(k * FFN_CHUNK, (k + 1) * FFN_CHUNK)
        g = _dot_nt(h, wg_ref[cols, :])
        u = _dot_nt(h, wu_ref[cols, :])
        a = (_silu(g) * u).astype(BF16)
        part = _dot(a, wd_ref[cols, :])
        acc = part if acc is None else acc + part
    return acc


def _params(*semantics):
    return pltpu.CompilerParams(dimension_semantics=semantics, vmem_limit_bytes=VMEM_LIMIT)


LOAD_ROWS = 256
LOAD_SLOTS = 4


def _row_pieces(src_lo, src_hi, dst_lo):
    return [(r, min(LOAD_ROWS, src_hi - r), dst_lo + r - src_lo) for r in range(src_lo, src_hi, LOAD_ROWS)]


def _stream_pieces(pieces, stage_ref, sem_ref):
    def load(k):
        src, n, _ = pieces[k]
        slot = k % LOAD_SLOTS
        return pltpu.make_async_copy(src, stage_ref.at[slot, pl.ds(0, n), :], sem_ref.at[slot])

    for k in range(min(LOAD_SLOTS, len(pieces))):
        load(k).start()
    for k, (_, n, consume) in enumerate(pieces):
        load(k).wait()
        consume(stage_ref[k % LOAD_SLOTS, 0:n, :])
        if k + LOAD_SLOTS < len(pieces):
            load(k + LOAD_SLOTS).start()


def _cast_pieces(src_ref, dst_ref, src_lo, src_hi, dst_lo):
    def piece(r, n, dr):
        def consume(v):
            dst_ref[dr:dr + n, :] = v.astype(BF16)
        return (src_ref.at[0, pl.ds(r, n), :], n, consume)
    return [piece(*p) for p in _row_pieces(src_lo, src_hi, dst_lo)]


def _ada_pieces(cond_ref, w_hbm, b_ref, mod_ref):
    s = _silu(cond_ref[...]).astype(BF16)

    def piece(k, r):
        def consume(v):
            mod_ref[k] = mod_ref[k] + _dot(s[:, r:r + LOAD_ROWS], v.astype(BF16))
        return (w_hbm.at[0, pl.ds(r, LOAD_ROWS), pl.ds(k * D_MODEL, D_MODEL)], LOAD_ROWS, consume)

    for k in range(N_MOD):
        mod_ref[k] = jnp.broadcast_to(b_ref[:, k * D_MODEL:(k + 1) * D_MODEL], mod_ref.shape[1:])
    return [piece(k, r) for k in range(N_MOD) for r in range(0, D_MODEL, LOAD_ROWS)]


N_FILT_PLANES = 6


def _filter_kernel(fe_ref, fo_ref, w1_ref, b1_ref, fr_ref, w2_ref, b2_ref, w3_ref, dl_ref, f_ref, tw_ref,
                   kf_ref, kn_ref, *, m):
    freq = fr_ref[...]
    w1 = w1_ref[...].astype(BF16)
    w2 = w2_ref[...].astype(BF16)

    def hidden(feats):
        h = jnp.sin(freq * (_dot(feats.astype(BF16), w1) + b1_ref[...]))
        h = jnp.sin(freq * (_dot(h.astype(BF16), w2) + b2_ref[...]))
        return h.astype(BF16)

    feats_e = fe_ref[...]
    feats_o = fo_ref[...]
    hb_e = hidden(feats_e)
    hb_o = hidden(feats_o)
    t_e = feats_e[:, 0:1]
    t_o = feats_o[:, 0:1]
    row = lax.broadcasted_iota(jnp.int32, (m, 1), 0)
    sign = jnp.where((row & 1) == 0, 1.0, -1.0)
    cos_t = tw_ref[:, 0:1]
    sin_t = tw_ref[:, 1:2]
    f_top = f_ref[0:m, :]
    f_bot = f_ref[m:2 * m, :]

    def spectrum(k):
        kb = k.astype(BF16)
        return _dot(f_top, kb), jnp.where(row == 0, 0.0, _dot(f_bot, kb)), jnp.sum(k * sign, axis=0, keepdims=True)

    for i in range(HY_ORDER):
        dl = dl_ref[i:i + 1, :]
        c0 = (0 * HY_ORDER + i) * HY_W
        c1 = (1 * HY_ORDER + i) * HY_W
        w3f = w3_ref[:, c0:c0 + HY_W].astype(BF16)
        w3b = w3_ref[:, c1:c1 + HY_W].astype(BF16)
        win_e = jnp.exp(-t_e * dl)
        win_o = jnp.exp(-t_o * dl)
        k0e = _dot(hb_e, w3f) * win_e
        k0o = _dot(hb_o, w3f) * win_o
        k1e = jnp.where(row == 0, 0.0, _dot(hb_e, w3b) * win_e)
        k1o = _dot(hb_o, w3b) * win_o
        ker, _, ken = spectrum(k0e + k1e)
        _, kei, _ = spectrum(k0e - k1e)
        ar, ai, an = spectrum(k0o)
        br, bi, bn = spectrum(k1o)
        planes = (ker, kei,
                  ar + cos_t * br + sin_t * bi, ai + sin_t * br - cos_t * bi,
                  cos_t * ar + sin_t * ai + br, cos_t * ai - sin_t * ar - bi)
        for p, plane in enumerate(planes):
            kf_ref[i, p] = plane.astype(BF16)
        kn_ref[i] = jnp.concatenate([ken, an - bn, bn - an, jnp.zeros((SUBLANES - 3, HY_W), F32)], axis=0)


def _filters_kernel(w1_ref, b1_ref, fr_ref, w2_ref, b2_ref, w3_ref, dl_ref, *refs, ms):
    n = len(ms)
    for k, m in enumerate(ms):
        fe_ref, fo_ref, f_ref, tw_ref = refs[4 * k:4 * k + 4]
        kf_ref, kn_ref = refs[4 * n + 2 * k:4 * n + 2 * k + 2]
        _filter_kernel(fe_ref, fo_ref, w1_ref, b1_ref, fr_ref, w2_ref, b2_ref, w3_ref, dl_ref, f_ref, tw_ref,
                       kf_ref, kn_ref, m=m)


def _filters(lengths, per_length, w1p, b1, freq, w2, b2, w3, deltas):
    ms = [l // 2 for l in lengths]
    full = lambda *shape: pl.BlockSpec(shape, lambda: (0,) * len(shape))
    in_specs = [full(EMB_PAD, HY_HIDDEN), full(1, HY_HIDDEN), full(1, HY_HIDDEN), full(HY_HIDDEN, HY_HIDDEN),
                full(1, HY_HIDDEN), full(HY_HIDDEN, N_DIR * HY_ORDER * HY_W), full(HY_ORDER, HY_W)]
    out_specs, out_shape = [], []
    for m in ms:
        in_specs += [full(m, EMB_PAD), full(m, EMB_PAD), full(2 * m, m), full(m, 2)]
        out_specs += [full(HY_ORDER, N_FILT_PLANES, m, HY_W), full(HY_ORDER, SUBLANES, HY_W)]
        out_shape += [jax.ShapeDtypeStruct((HY_ORDER, N_FILT_PLANES, m, HY_W), BF16),
                      jax.ShapeDtypeStruct((HY_ORDER, SUBLANES, HY_W), F32)]
    outs = pl.pallas_call(
        functools.partial(_filters_kernel, ms=ms),
        in_specs=in_specs,
        out_specs=out_specs,
        out_shape=out_shape,
        compiler_params=pltpu.CompilerParams(vmem_limit_bytes=VMEM_LIMIT),
        name="filt",
    )(w1p, b1, freq, w2, b2, w3, deltas, *[a for group in per_length for a in group])
    return [(outs[2 * k], outs[2 * k + 1]) for k in range(len(ms))]


def _const_spec(shape):
    return pl.BlockSpec(shape, lambda i: (0,) * len(shape), pipeline_mode=pl.Buffered(1))


def _group_specs(ctx_tiles, width):
    tm = TOKEN_TILE
    return [pl.BlockSpec((tm, width), lambda i: (jnp.minimum(i, ctx_tiles - 1), 0)),
            pl.BlockSpec((tm, width), lambda i: (jnp.maximum(i - ctx_tiles, 0), 0))]


def _mod_rows(mod_ref, ctx_tiles, n_lat, tiles_per_seq):
    i = pl.program_id(0)
    r = jnp.where(i < ctx_tiles, n_lat, (i - ctx_tiles) // tiles_per_seq)
    return [mod_ref[k, pl.ds(r, 1), :] for k in range(N_MOD)]


NEXT_CAST_ROWS = 128
NEXT_CAST_ROWS_OUT = 64


def _ffn_in_kernel(xc_ref, xl_ref, cond_ref, ba_ref, n1_ref, nm_ref, wa_hbm, wg_hbm, wu_hbm, wd_hbm, wi_hbm,
                   ng_ref, nu_ref, nd_ref, no_ref,
                   x1_ref, z_ref, xbc_ref, dt_ref, hy_ref, mod_ref, ngb_ref, nub_ref, ndb_ref, nob_ref,
                   wg_ref, wu_ref, wd_ref, wi_ref, stage_ref, sem_ref, *, ctx_tiles, n_lat, tiles_per_seq):
    @pl.when(pl.program_id(0) == 0)
    def _():
        o1, o2, o3 = IN_SPLITS
        pieces = _ada_pieces(cond_ref, wa_hbm, ba_ref, mod_ref)
        for src, dst in ((wg_hbm, wg_ref), (wu_hbm, wu_ref), (wd_hbm, wd_ref)):
            pieces += _cast_pieces(src, dst, 0, FFN_DIM, 0)
            dst[FFN_DIM:, :] = jnp.zeros((FFN_PAD - FFN_DIM, D_MODEL), BF16)
        pieces += _cast_pieces(wi_hbm, wi_ref, 0, o3, 0) + _cast_pieces(wi_hbm, wi_ref, o3, IN_COLS, o2 + DT_PAD)
        wi_ref[o3:o2 + DT_PAD, :] = jnp.zeros((o2 + DT_PAD - o3, D_MODEL), BF16)
        _stream_pieces(pieces, stage_ref, sem_ref)

    blk = jnp.minimum(pl.program_id(0), FFN_PAD // NEXT_CAST_ROWS - 1)
    wrow = blk * NEXT_CAST_ROWS + lax.broadcasted_iota(jnp.int32, (NEXT_CAST_ROWS, 1), 0)
    for src, dst in ((ng_ref, ngb_ref), (nu_ref, nub_ref), (nd_ref, ndb_ref)):
        dst[...] = jnp.where(wrow < FFN_DIM, src[...], 0.0).astype(BF16)
    nob_ref[...] = no_ref[...].astype(BF16)

    x = jnp.where(pl.program_id(0) < ctx_tiles, xc_ref[...], xl_ref[...])
    mod = _mod_rows(mod_ref, ctx_tiles, n_lat, tiles_per_seq)
    h = _rms_mod(x, n1_ref[...], mod[0], mod[1]).astype(BF16)
    x1 = x + (0.5 * mod[2]) * _swiglu_acc(h, wg_ref, wu_ref, wd_ref)
    x1_ref[...] = x1
    h2 = _rms_mod(x1, nm_ref[...], mod[3], mod[4]).astype(BF16)
    o1, o2, _ = IN_SPLITS
    z_ref[...] = _dot_nt(h2, wi_ref[0:o1, :])
    xbc_ref[...] = _dot_nt(h2, wi_ref[o1:o2, :])
    dt_ref[...] = _dot_nt(wi_ref[o2:o2 + N_DT, :], h2)
    hy = _dot_nt(h2, wi_ref[o2 + DT_PAD:, :])
    for k in range(hy_ref.shape[0]):
        hy_ref[k] = hy[:, k * LANES:(k + 1) * LANES]


def _ffn_in(x_ctx, x_lat, cond, w_ada, b_ada, n_lat, l_lat, n1, nm, wg, wu, wd, wi, next_ffn, next_out):
    tm = TOKEN_TILE
    ctx_tiles = x_ctx.shape[0] // tm
    tokens = x_ctx.shape[0] + x_lat.shape[0]
    steps = tokens // tm
    ffn_blocks = FFN_PAD // NEXT_CAST_ROWS
    out_blocks = D_MODEL // NEXT_CAST_ROWS_OUT
    assert steps >= ffn_blocks and steps >= out_blocks
    row_spec = lambda n: pl.BlockSpec((tm, n), lambda i: (i, 0))
    widths = (D_MODEL, SSD_W, SSD_XBC)
    hy_tiles = (HY_ORDER + 1) * HY_W // LANES
    dt_spec = pl.BlockSpec((N_DT, tm), lambda i: (0, i))
    outs = pl.pallas_call(
        functools.partial(_ffn_in_kernel, ctx_tiles=ctx_tiles, n_lat=n_lat, tiles_per_seq=l_lat // tm),
        grid=(steps,),
        in_specs=_group_specs(ctx_tiles, D_MODEL) + [
            _const_spec(cond.shape), _const_spec(b_ada.shape), _const_spec((1, D_MODEL)), _const_spec((1, D_MODEL))]
        + [pl.BlockSpec(memory_space=pl.ANY)] * 5
        + [pl.BlockSpec((None, NEXT_CAST_ROWS, D_MODEL), lambda i: (0, jnp.minimum(i, ffn_blocks - 1), 0))] * 3
        + [pl.BlockSpec((None, NEXT_CAST_ROWS_OUT, D_MODEL), lambda i: (0, jnp.minimum(i, out_blocks - 1), 0))],
        out_specs=[row_spec(n) for n in widths] + [dt_spec, pl.BlockSpec((hy_tiles, tm, LANES), lambda i: (0, i, 0)),
                                                   pl.BlockSpec((N_MOD, COND_ROWS, D_MODEL), lambda i: (0, 0, 0))]
        + [pl.BlockSpec((NEXT_CAST_ROWS, D_MODEL), lambda i: (jnp.minimum(i, ffn_blocks - 1), 0))] * 3
        + [pl.BlockSpec((NEXT_CAST_ROWS_OUT, D_MODEL), lambda i: (jnp.minimum(i, out_blocks - 1), 0))],
        out_shape=[jax.ShapeDtypeStruct((tokens, n), F32) for n in widths]
        + [jax.ShapeDtypeStruct((N_DT, tokens), F32), jax.ShapeDtypeStruct((hy_tiles, tokens, LANES), F32),
           jax.ShapeDtypeStruct((N_MOD, COND_ROWS, D_MODEL), F32)]
        + [jax.ShapeDtypeStruct((FFN_PAD, D_MODEL), BF16)] * 3 + [jax.ShapeDtypeStruct((D_MODEL, D_MODEL), BF16)],
        scratch_shapes=[pltpu.VMEM((FFN_PAD, D_MODEL), BF16)] * 3 + [
            pltpu.VMEM((IN_ROWS, D_MODEL), BF16), pltpu.VMEM((LOAD_SLOTS, LOAD_ROWS, D_MODEL), F32),
            pltpu.SemaphoreType.DMA((LOAD_SLOTS,))],
        compiler_params=_params("arbitrary"),
        name="ffn_in",
    )(x_ctx, x_lat, cond, b_ada, n1, nm, w_ada, wg, wu, wd, wi, *next_ffn, next_out)
    return outs[:5], outs[5], outs[6:9], outs[9]


def _out_ffn_kernel(x1_ref, ys_ref, yh_ref, mod_ref, n3_ref, nf_ref,
                    wo_ref, wg_ref, wu_ref, wd_ref, oc_ref, ol_ref, *, ctx_tiles, n_lat, tiles_per_seq):
    is_ctx = pl.program_id(0) < ctx_tiles
    mod = _mod_rows(mod_ref, ctx_tiles, n_lat, tiles_per_seq)
    y = jnp.concatenate([ys_ref[...]] + [yh_ref[k] for k in range(yh_ref.shape[0])], axis=1).astype(BF16)
    x2 = x1_ref[...] + mod[5] * _dot(y, wo_ref[...])
    h = _rms_mod(x2, n3_ref[...], mod[6], mod[7]).astype(BF16)
    x3 = x2 + (0.5 * mod[8]) * _swiglu_acc(h, wg_ref, wu_ref, wd_ref)
    ms = jnp.mean(x3 * x3, axis=-1, keepdims=True)
    out = x3 * lax.rsqrt(ms + RMS_EPS) * nf_ref[...]

    @pl.when(is_ctx)
    def _():
        oc_ref[...] = out

    @pl.when(jnp.logical_not(is_ctx))
    def _():
        ol_ref[...] = out


def _out_ffn(x1, ys, yh, t_ctx, mod, n_lat, l_lat, n3, nf, wo, wg, wu, wd):
    tm = TOKEN_TILE
    ctx_tiles = t_ctx // tm
    tokens = x1.shape[0]
    return pl.pallas_call(
        functools.partial(_out_ffn_kernel, ctx_tiles=ctx_tiles, n_lat=n_lat, tiles_per_seq=l_lat // tm),
        grid=(tokens // tm,),
        in_specs=[pl.BlockSpec((tm, D_MODEL), lambda i: (i, 0)), pl.BlockSpec((tm, SSD_W), lambda i: (i, 0)),
                  pl.BlockSpec((HY_W // LANES, tm, LANES), lambda i: (0, i, 0)),
            _const_spec(mod.shape), _const_spec((1, D_MODEL)), _const_spec((1, D_MODEL)),
            _const_spec(wo.shape), _const_spec(wg.shape), _const_spec(wu.shape), _const_spec(wd.shape)],
        out_specs=_group_specs(ctx_tiles, D_MODEL),
        out_shape=[jax.ShapeDtypeStruct((t_ctx, D_MODEL), F32),
                   jax.ShapeDtypeStruct((tokens - t_ctx, D_MODEL), F32)],
        compiler_params=_params("arbitrary"),
        name="out_ffn",
    )(x1, ys, yh, mod, n3, nf, wo, wg, wu, wd)


def _ssd_kernel(z_ref, xbc_ref, dt_ref, init_ref, cw_ref, cb_ref, dtb_ref, alog_ref, dexp_ref, nw_ref, e_ref,
                y_ref, fin_ref,
                xs_s, b_s, c_s, ec_s, dst_s, et_s, cum_s, ct_s, xd_s,
                *, l, seg, zero_init, write_final):
    q = SSD_CHUNK
    nc = l // q
    n_seq = y_ref.shape[0] // l
    cr = max(seg, q)
    e_mats = [e_ref[k].astype(BF16) for k in range(e_ref.shape[0])]
    dt_t = _softplus(dt_ref[...] + dtb_ref[...])
    a_t = dt_t * (-jnp.exp(alog_ref[...]))

    first, last = _seg_edges(cr, seg)

    for r0 in range(0, n_seq * l, cr):
        rows = slice(r0, r0 + cr)
        x = xbc_ref[rows, :]
        w = cw_ref[...]
        prev = jnp.where(first, 0.0, pltpu.roll(x, 1, 0))
        nxt = jnp.where(last, 0.0, pltpu.roll(x, cr - 1, 0))
        u = _silu(prev * w[0:1, :] + x * w[1:2, :] + nxt * w[2:3, :] + cb_ref[...])
        xs_s[rows, :] = u[:, :SSD_W]
        b_s[rows, :] = u[:, SSD_W:SSD_W + LANES]
        c_s[rows, :] = u[:, SSD_W + LANES:]

    row_g = lax.broadcasted_iota(jnp.int32, (LANES, N_DIR * HP), 0) // SSD_STATE
    lane_g = (lax.broadcasted_iota(jnp.int32, (LANES, N_DIR * HP), 1) % HP) // (HP // SSD_GROUPS)
    own_t = row_g == lane_g

    ii = lax.broadcasted_iota(jnp.int32, (q, q), 0)
    jj = lax.broadcasted_iota(jnp.int32, (q, q), 1)
    tri_upper = (ii <= jj).astype(BF16)
    lane = lax.broadcasted_iota(jnp.int32, (q, LANES), 1)
    low_half = lane < SSD_STATE
    is_fwd_row = lax.broadcasted_iota(jnp.int32, (N_DT, 1), 0) < SSD_HEADS

    chunks = range(n_seq * nc)

    for c in chunks:
        rows = slice(c * q, (c + 1) * q)
        a_c = a_t[:, rows]
        cum_f = _dot_exact_lhs(a_c, tri_upper)
        tot_c = cum_f[:, q - 1:q]
        cum_t = jnp.where(is_fwd_row, cum_f, tot_c - cum_f + a_c)
        ct_s[c] = cum_t
        stack = jnp.concatenate([cum_t, jnp.exp(cum_t), jnp.exp(tot_c - cum_t), dt_t[:, rows],
                                 jnp.zeros((LANES - 4 * N_DT, q), F32)], axis=0)
        cum_s[c] = stack.T

    def expand_pass(c):
        rows = slice(c * q, (c + 1) * q)
        cum = cum_s[c]
        cum_b = cum.astype(BF16)
        ec_s[rows, :] = _dot(cum_b, e_mats[1])
        tot = jnp.where(lane[0:1, :] < SSD_HEADS, cum[q - 1:q, :], cum[0:1, :])
        tot = jnp.where(lane[0:1, :] < N_DT, tot, 0.0)
        et_s[c] = _dot_exact_lhs(jnp.broadcast_to(jnp.exp(tot), (SUBLANES, LANES)), e_mats[0])
        xs = xs_s[rows, :]
        xd = jnp.concatenate([xs, xs], axis=1) * _dot(cum_b, e_mats[3])
        xd_s[rows, :] = xd
        w = (xd * _dot(cum_b, e_mats[2])).astype(BF16)
        dst_s[c] = jnp.where(own_t, _dot(b_s[rows, :].T.astype(BF16), w), 0.0)

    def local_pass(c):
        rows = slice(c * q, (c + 1) * q)
        cum = cum_s[c]
        cum_t = ct_s[c]
        xd = xd_s[rows, :]
        bcb = b_s[rows, :].astype(BF16)
        cc = c_s[rows, :]
        g_mats = [_dot_nt(jnp.where(low_half, cc, 0.0).astype(BF16), bcb),
                  _dot_nt(jnp.where(low_half, 0.0, cc).astype(BF16), bcb)]
        y_parts = []
        for pair in range(SSD_HEADS // 2):
            g = pair // (SSD_HEADS // 2 // SSD_GROUPS)
            acc = None
            for d in range(N_DIR):
                keep = (ii >= jj) if d == 0 else (ii <= jj)
                s_mats = []
                for hh in (2 * pair, 2 * pair + 1):
                    col = d * SSD_HEADS + hh
                    diff = cum[:, col:col + 1] - cum_t[col:col + 1, :]
                    decay = jnp.exp(jnp.where(keep, diff, NEG_BIG))
                    s_mats.append((g_mats[g] * decay).astype(BF16))
                lhs = jnp.concatenate(s_mats, axis=1)
                xp = xd[:, d * HP + pair * LANES:d * HP + (pair + 1) * LANES]
                rhs = jnp.concatenate([jnp.where(low_half, xp, 0.0), jnp.where(low_half, 0.0, xp)],
                                      axis=0).astype(BF16)
                part = _dot(lhs, rhs)
                acc = part if acc is None else acc + part
            y_parts.append(acc)
        y_ref[rows, :] = jnp.concatenate(y_parts, axis=1)

    for c in chunks:
        expand_pass(c)
        local_pass(c)

    half = HP // SSD_GROUPS
    for s in range(n_seq):
        states = []
        for d in range(N_DIR):
            if zero_init:
                states.append(jnp.zeros((LANES, HP), F32))
            else:
                s0 = init_ref[s, d]
                states.append(jnp.where(own_t[:, :HP], jnp.concatenate([s0, s0], axis=1).T, 0.0))

        for k in range(nc):
            for d in range(N_DIR):
                c = s * nc + (k if d == 0 else nc - 1 - k)
                rows = slice(c * q, (c + 1) * q)
                lanes = slice(d * HP, (d + 1) * HP)
                if zero_init and k == 0:
                    states[d] = dst_s[c, :, lanes]
                    continue
                y_off = _dot(c_s[rows, :].astype(BF16), states[d].astype(BF16)) * ec_s[rows, lanes]
                y_ref[rows, :] = y_ref[rows, :] + y_off
                states[d] = states[d] * et_s[c, 0:1, lanes] + dst_s[c, :, lanes]

        for d in range(N_DIR if write_final else 0):
            st = states[d].T
            fin_ref[s, d, 0:half, :] = st[0:half, 0:SSD_STATE]
            fin_ref[s, d, half:HP, :] = st[half:HP, SSD_STATE:2 * SSD_STATE]

        for c in range(s * nc, (s + 1) * nc):
            rows = slice(c * q, (c + 1) * q)
            y = y_ref[rows, :] + xs_s[rows, :] * dexp_ref[...]
            y = y * _silu(z_ref[rows, :])
            ms = jnp.mean(y * y, axis=-1, keepdims=True)
            y_ref[rows, :] = y * lax.rsqrt(ms + RMS_EPS) * nw_ref[...]


SSD_STEP_ROWS = 1024


def _ssd_groups_kernel(*refs, ctx_steps, ctx, lat):
    @pl.when(pl.program_id(0) < ctx_steps)
    def _():
        _ssd_kernel(*refs, l=ctx[0], seg=ctx[1], zero_init=True, write_final=True)

    @pl.when(pl.program_id(0) >= ctx_steps)
    def _():
        _ssd_kernel(*refs, l=lat[0], seg=lat[1], zero_init=False, write_final=False)


def _ssd(z, xbc, dt, n_ctx, l_ctx, seg_ctx, l_lat, seg_lat, init_lat, cw, cb, dtb, alog, dexp, nw, e_mat):
    rows = SSD_STEP_ROWS
    tokens = z.shape[0]
    ctx_steps = n_ctx * l_ctx // rows
    ctx_seqs, lat_seqs = rows // l_ctx, rows // l_lat
    seq_spec = lambda n: pl.BlockSpec((rows, n), lambda b: (b, 0))
    const = lambda *shape: pl.BlockSpec(shape, lambda b: (0,) * len(shape))
    nc = rows // SSD_CHUNK
    return pl.pallas_call(
        functools.partial(_ssd_groups_kernel, ctx_steps=ctx_steps, ctx=(l_ctx, seg_ctx), lat=(l_lat, seg_lat)),
        grid=(tokens // rows,),
        in_specs=[seq_spec(SSD_W), seq_spec(SSD_XBC), pl.BlockSpec((N_DT, rows), lambda b: (0, b)),
                  pl.BlockSpec((lat_seqs, N_DIR, HP, SSD_STATE), lambda b: (jnp.maximum(b - ctx_steps, 0), 0, 0, 0)),
                  const(3, SSD_XBC), const(1, SSD_XBC), const(N_DT, 1), const(N_DT, 1),
                  const(1, SSD_W), const(1, SSD_W), const(*e_mat.shape)],
        out_specs=[pl.BlockSpec((rows, SSD_W), lambda b: (b, 0)),
                   pl.BlockSpec((ctx_seqs, N_DIR, HP, SSD_STATE), lambda b: (jnp.minimum(b, ctx_steps - 1), 0, 0, 0))],
        out_shape=[jax.ShapeDtypeStruct((tokens, SSD_W), F32),
                   jax.ShapeDtypeStruct((n_ctx, N_DIR, HP, SSD_STATE), F32)],
        scratch_shapes=[pltpu.VMEM((rows, SSD_W), F32), pltpu.VMEM((rows, LANES), F32), pltpu.VMEM((rows, LANES), F32),
                        pltpu.VMEM((rows, N_DIR * HP), F32),
                        pltpu.VMEM((nc, LANES, N_DIR * HP), F32), pltpu.VMEM((nc, SUBLANES, N_DIR * HP), F32),
                        pltpu.VMEM((nc, SSD_CHUNK, LANES), F32), pltpu.VMEM((nc, N_DT, SSD_CHUNK), F32),
                        pltpu.VMEM((rows, N_DIR * HP), F32)],
        compiler_params=_params("arbitrary"),
        name="ssd",
    )(z, xbc, dt, init_lat, cw, cb, dtb, alog, dexp, nw, e_mat)


HY_ROW_BLOCK = 16
HY_STEP_ROWS = 2048


def _hyena_kernel(v_ref, x1_ref, x2_ref, wv_ref, w1_ref, w2_ref, bv_ref, b1_ref, b2_ref,
                  f_ref, g_ref, kf_ref, kn_ref, skip_ref, o_ref, spec_s, prod_s, *, l, seg):
    m = l // 2
    tiles = o_ref.shape[0]
    n_seq = o_ref.shape[1] // l
    ct = tiles * LANES
    rb = HY_ROW_BLOCK
    first, last = _seg_edges(m, seg // 2)
    row0 = lax.broadcasted_iota(jnp.int32, (rb, 1), 0) == 0
    ev = slice(0, ct)
    od = slice(ct, 2 * ct)

    def conv_eo(x_ref, w_ref, b_ref, base):
        xe, xo = (jnp.concatenate([x_ref[k, pl.ds(base + p, m, stride=2), :] for k in range(tiles)], axis=1)
                  for p in range(2))
        w = w_ref[...]
        b = b_ref[...]
        xo_prev = jnp.where(first, 0.0, pltpu.roll(xo, 1, 0))
        xe_next = jnp.where(last, 0.0, pltpu.roll(xe, m - 1, 0))
        ce = xo_prev * w[0:1, :] + xe * w[1:2, :] + xo * w[2:3, :] + b
        co = xe * w[0:1, :] + xo * w[1:2, :] + xe_next * w[2:3, :] + b
        return jnp.concatenate([ce, co], axis=1)

    def pointwise(s, i, r0):
        re = slice(s * l + r0, s * l + r0 + rb)
        im = slice(s * l + m + r0, s * l + m + r0 + rb)
        er, orr = spec_s[re, ev], spec_s[re, od]
        ei, oi = spec_s[im, ev], spec_s[im, od]
        ker, kei, kor, koi, vr, vi = (kf_ref[i, p, r0:r0 + rb, :] for p in range(N_FILT_PLANES))
        if r0 == 0:
            e_n, o_n = ei[0:1, :].astype(F32), oi[0:1, :].astype(F32)
            zero = jnp.zeros((), BF16)
            ei = jnp.where(row0, zero, ei)
            oi = jnp.where(row0, zero, oi)
        pe_r = er * ker - ei * kei + orr * vr - oi * vi
        pe_i = er * kei + ei * ker + orr * vi + oi * vr
        po_r = er * kor - ei * koi + orr * ker - oi * kei
        po_i = er * koi + ei * kor + orr * kei + oi * ker
        if r0 == 0:
            kn = kn_ref[i]
            pe_i = jnp.where(row0, (e_n * kn[0:1, :] + o_n * kn[2:3, :]).astype(BF16), pe_i)
            po_i = jnp.where(row0, (e_n * kn[1:2, :] + o_n * kn[0:1, :]).astype(BF16), po_i)
        prod_s[re, ev] = pe_r
        prod_s[re, od] = po_r
        prod_s[im, ev] = pe_i
        prod_s[im, od] = po_i

    seqs = range(n_seq)
    zz = [conv_eo(v_ref, wv_ref, bv_ref, s * l) for s in seqs]
    for i, (xg_ref, wg_ref, bg_ref) in enumerate(((x1_ref, w1_ref, b1_ref), (x2_ref, w2_ref, b2_ref))):
        for s in seqs:
            spec_s[s * l:(s + 1) * l, :] = _dot(f_ref[...], zz[s].astype(BF16)).astype(BF16)
        for s in seqs:
            for r0 in range(0, m, rb):
                pointwise(s, i, r0)
        skip = skip_ref[i:i + 1, :]
        skip2 = jnp.concatenate([skip, skip], axis=1)
        for s in seqs:
            conv = _dot(g_ref[...], prod_s[s * l:(s + 1) * l, :])
            zz[s] = conv_eo(xg_ref, wg_ref, bg_ref, s * l) * (conv + zz[s] * skip2)
    for s in seqs:
        for k in range(tiles):
            o_ref[k, pl.ds(s * l, m, stride=2), :] = zz[s][:, k * LANES:(k + 1) * LANES]
            o_ref[k, pl.ds(s * l + 1, m, stride=2), :] = zz[s][:, ct + k * LANES:ct + (k + 1) * LANES]


def _hyena_groups_kernel(v_ref, x1_ref, x2_ref, wv_ref, w1_ref, w2_ref, bv_ref, b1_ref, b2_ref,
                         fc_ref, gc_ref, kfc_ref, knc_ref, fl_ref, gl_ref, kfl_ref, knl_ref, skip_ref,
                         o_ref, spec_s, prod_s, *, ctx_steps, ctx, lat):
    common = (v_ref, x1_ref, x2_ref, wv_ref, w1_ref, w2_ref, bv_ref, b1_ref, b2_ref)

    @pl.when(pl.program_id(1) < ctx_steps)
    def _():
        _hyena_kernel(*common, fc_ref, gc_ref, kfc_ref, knc_ref, skip_ref, o_ref, spec_s, prod_s, l=ctx[0], seg=ctx[1])

    @pl.when(pl.program_id(1) >= ctx_steps)
    def _():
        _hyena_kernel(*common, fl_ref, gl_ref, kfl_ref, knl_ref, skip_ref, o_ref, spec_s, prod_s, l=lat[0], seg=lat[1])


def _hyena(hy, n_ctx, l_ctx, seg_ctx, l_lat, seg_lat, cw, cb, consts_ctx, consts_lat, skip):
    ct = HY_CH_TILE
    nct = HY_W // ct
    tiles = ct // LANES
    rows = HY_STEP_ROWS
    tokens = hy.shape[1]
    ctx_steps = n_ctx * l_ctx // rows
    part = lambda p: pl.BlockSpec((tiles, rows, LANES), lambda j, b: (p * nct + j, b, 0))
    wpart = lambda p: pl.BlockSpec((3, ct), lambda j, b: (0, p * nct + j))
    bpart = lambda p: pl.BlockSpec((1, ct), lambda j, b: (0, p * nct + j))

    def const_specs(l):
        m = l // 2
        return [pl.BlockSpec((l, m), lambda j, b: (0, 0)),
                pl.BlockSpec((m, l), lambda j, b: (0, 0)),
                pl.BlockSpec((HY_ORDER, N_FILT_PLANES, m, ct), lambda j, b: (0, 0, 0, j)),
                pl.BlockSpec((HY_ORDER, SUBLANES, ct), lambda j, b: (0, 0, j))]

    return pl.pallas_call(
        functools.partial(_hyena_groups_kernel, ctx_steps=ctx_steps, ctx=(l_ctx, seg_ctx), lat=(l_lat, seg_lat)),
        grid=(nct, tokens // rows),
        in_specs=[part(0), part(1), part(2), wpart(0), wpart(1), wpart(2), bpart(0), bpart(1), bpart(2)]
        + const_specs(l_ctx) + const_specs(l_lat) + [pl.BlockSpec((HY_ORDER, ct), lambda j, b: (0, j))],
        out_specs=pl.BlockSpec((tiles, rows, LANES), lambda j, b: (j, b, 0)),
        out_shape=jax.ShapeDtypeStruct((HY_W // LANES, tokens, LANES), F32),
        scratch_shapes=[pltpu.VMEM((rows, 2 * ct), BF16), pltpu.VMEM((rows, 2 * ct), BF16)],
        compiler_params=_params("arbitrary", "arbitrary"),
        name="hyena",
    )(hy, hy, hy, cw, cw, cw, cb, cb, cb, *consts_ctx, *consts_lat, skip)


def _dft_mats(l):
    n = 2 * l
    f = np.arange(l, dtype=np.int64)[:, None]
    t = np.arange(l, dtype=np.int64)[None, :]
    ang = 2.0 * np.pi * ((f * t) % n).astype(np.float64) / n
    alt = np.where(np.arange(l) % 2 == 0, 1.0, -1.0)
    top = np.cos(ang)
    bot = -np.sin(ang)
    bot[0, :] = alt
    fwd = np.concatenate([top, bot], axis=0)
    wf = np.full((l,), 2.0)
    wf[0] = 1.0
    gtop = np.cos(ang).T * wf[None, :] / n
    gbot = -np.sin(ang).T * 2.0 / n
    gbot[:, 0] = alt / n
    inv = np.concatenate([gtop, gbot], axis=1)
    return fwd.astype(np.float32), inv.astype(np.float32)


def _filter_feats(l):
    t = np.linspace(0.0, 1.0, l)[:, None]
    w = (2.0 * np.pi / l) * np.arange(l, dtype=np.float64)[:, None]
    f = np.linspace(1e-4, HY_BANDS - 1, HY_BANDS)[None, :]
    feats = np.concatenate([t, np.cos(f * w), -np.sin(f * w)], axis=-1)
    out = np.zeros((l, EMB_PAD), np.float32)
    out[:, :HY_EMB] = feats
    return out[0::2], out[1::2]


def _shift_twiddles(l):
    theta = 2.0 * np.pi * np.arange(l // 2, dtype=np.float64) / l
    return np.stack([np.cos(theta), np.sin(theta)], axis=1).astype(np.float32)


def _head_expand():
    n_blocks = LANES // N_DT // 2
    e = np.zeros((n_blocks, LANES, N_DIR * HP), np.float32)
    for k in range(n_blocks):
        for j in range(N_DT):
            e[k, k * N_DT + j, j * SSD_HEAD_DIM:(j + 1) * SSD_HEAD_DIM] = 1.0
    return e


def kernel(x_prompt, x_sample, state_ssd, c, c_ctx, w_ada, b_ada, norm_ffn1, ffn1_w_gate, ffn1_w_up, ffn1_w_down, norm_mix, w_in, w_out, ssd_conv_w, ssd_conv_b, ssd_dt_bias, ssd_a_log, ssd_d, ssd_norm_w, hy_conv_w, hy_conv_b, hy_w1, hy_b1, hy_freq, hy_w2, hy_b2, hy_w3, hy_skip, norm_ffn2, ffn2_w_gate, ffn2_w_up, ffn2_w_down, norm_final):
    assert w_ada.shape[0] == 1, "single layer"
    n_ctx, l_ctx, _ = x_prompt.shape
    n_lat, l_lat, _ = x_sample.shape
    t_ctx = n_ctx * l_ctx

    assert n_lat < COND_ROWS
    cond = jnp.zeros((COND_ROWS, D_MODEL), F32).at[:n_lat].set(c).at[n_lat].set(c_ctx)

    tr = lambda w: jnp.swapaxes(w, 1, 2)

    row = lambda v: v.reshape(1, -1)
    dtb, alog = ssd_dt_bias[0].reshape(N_DT, 1), ssd_a_log[0].reshape(N_DT, 1)
    dexp = jnp.repeat(ssd_d[0], SSD_HEAD_DIM).reshape(1, SSD_W)
    e_mat = jnp.asarray(_head_expand())
    w1p = jnp.pad(hy_w1[0], ((0, EMB_PAD - HY_EMB), (0, 0)))
    deltas = jnp.asarray(np.abs(np.linspace(HY_MIN_DECAY, HY_MAX_DECAY, HY_ORDER * HY_W))
                         .reshape(HY_ORDER, HY_W).astype(np.float32))

    (x1, z, xbc, dt, hy), mod, (wg2, wu2, wd2), wo = _ffn_in(
        x_prompt.reshape(t_ctx, D_MODEL), x_sample.reshape(n_lat * l_lat, D_MODEL), cond, w_ada, b_ada, n_lat, l_lat,
        row(norm_ffn1[0]), row(norm_mix[0]), tr(ffn1_w_gate), tr(ffn1_w_up), ffn1_w_down, tr(w_in),
        [tr(ffn2_w_gate), tr(ffn2_w_up), ffn2_w_down], w_out)

    lengths = (l_ctx, l_lat)
    dft = [tuple(jnp.asarray(a).astype(BF16) for a in _dft_mats(l // 2)) for l in lengths]
    filt = _filters(lengths,
                    [tuple(jnp.asarray(a) for a in _filter_feats(l)) + (dft[k][0], jnp.asarray(_shift_twiddles(l)))
                     for k, l in enumerate(lengths)],
                    w1p, row(hy_b1[0]), row(hy_freq[0]), hy_w2[0], row(hy_b2[0]), hy_w3[0], deltas)
    hyena_consts = [dft[k] + filt[k] for k in range(len(lengths))]

    lat_init = state_ssd[:, 0].reshape(n_lat, N_DIR, HP, SSD_STATE)
    ys, ctx_fin = _ssd(z, xbc, dt, n_ctx, l_ctx, l_ctx, l_lat, GRID_W, lat_init, ssd_conv_w[0], row(ssd_conv_b[0]),
                       dtb, alog, dexp, row(ssd_norm_w[0]), e_mat)
    yh = _hyena(hy, n_ctx, l_ctx, l_ctx, l_lat, GRID_W, hy_conv_w[0], row(hy_conv_b[0]),
                hyena_consts[0], hyena_consts[1], hy_skip[0])

    y_ctx, y_lat = _out_ffn(x1, ys, yh, t_ctx, mod, n_lat, l_lat, row(norm_ffn2[0]), row(norm_final),
                            wo, wg2, wu2, wd2)
    new_state = ctx_fin.reshape(n_ctx, 1, N_DIR, SSD_HEADS, SSD_HEAD_DIM, SSD_STATE).astype(x_prompt.dtype)
    return (y_ctx.reshape(n_ctx, l_ctx, D_MODEL), y_lat.reshape(n_lat, l_lat, D_MODEL), new_state)
```

```python
import functools
import math

import numpy as np
import jax
import jax.numpy as jnp
from jax import lax
from jax.experimental import pallas as pl
from jax.experimental.pallas import tpu as pltpu

F32 = jnp.float32
BF16 = jnp.bfloat16

D_MODEL = 1024
GRID_W = 64
N_MOD = 9
RMS_EPS = 1e-6
FFN_DIM = 2752
SSD_W = 512
SSD_HEADS = 8
SSD_HEAD_DIM = 64
SSD_STATE = 64
SSD_GROUPS = 2
SSD_CHUNK = 128
SSD_XBC = SSD_W + 2 * SSD_GROUPS * SSD_STATE
N_DIR = 2
HY_W = 512
HY_ORDER = 2
HY_EMB = 33
HY_BANDS = (HY_EMB - 1) // 2
HY_HIDDEN = 64
HY_MIN_DECAY = math.log(1e-2) / 1.5
HY_MAX_DECAY = math.log(1e-2) / 0.3
IN_SPLITS = (SSD_W, SSD_W + SSD_XBC, SSD_W + SSD_XBC + N_DIR * SSD_HEADS)
IN_COLS = IN_SPLITS[-1] + (HY_ORDER + 1) * HY_W

LANES = 128
SUBLANES = 8
FFN_PAD = 2816
FFN_CHUNK = 256
N_DT = N_DIR * SSD_HEADS
DT_PAD = LANES
IN_ROWS = IN_COLS + DT_PAD - N_DIR * SSD_HEADS
EMB_PAD = LANES
TOKEN_TILE = 512
HY_CH_TILE = 256
VMEM_LIMIT = 60 * 1024 * 1024
COND_ROWS = 16
NEG_BIG = -1e30
HP = SSD_HEADS * SSD_HEAD_DIM


def _silu(x):
    return x * jax.nn.sigmoid(x)


def _softplus(x):
    return jnp.maximum(x, 0.0) + jnp.log1p(jnp.exp(-jnp.abs(x)))


def _rms_mod(x, gain, shift, scale):
    ms = jnp.mean(x * x, axis=-1, keepdims=True)
    return x * lax.rsqrt(ms + RMS_EPS) * (gain * (1.0 + scale)) + shift


def _dot(a, b):
    return jnp.dot(a, b, preferred_element_type=F32)


def _dot_nt(a, b):
    return lax.dot_general(a, b, (((1,), (1,)), ((), ())), preferred_element_type=F32)


def _split3(x):
    hi = x.astype(BF16)
    r = x - hi.astype(F32)
    mid = r.astype(BF16)
    lo = (r - mid.astype(F32)).astype(BF16)
    return hi, mid, lo


def _dot_exact_lhs(x, m01):
    hi, mid, lo = _split3(x)
    return _dot(hi, m01) + _dot(mid, m01) + _dot(lo, m01)


def _seg_edges(l, seg):
    pos = lax.broadcasted_iota(jnp.int32, (l, 1), 0) & (seg - 1)
    return pos == 0, pos == seg - 1


def _swiglu_acc(h, wg_ref, wu_ref, wd_ref):
    acc = None
    for k in range(FFN_PAD // FFN_CHUNK):
        cols = slice(k * FFN_CHUNK, (k + 1) * FFN_CHUNK)
        g = _dot_nt(h, wg_ref[cols, :])
        u = _dot_nt(h, wu_ref[cols, :])
        a = (_silu(g) * u).astype(BF16)
        part = _dot(a, wd_ref[cols, :])
        acc = part if acc is None else acc + part
    return acc


def _params(*semantics):
    return pltpu.CompilerParams(dimension_semantics=semantics, vmem_limit_bytes=VMEM_LIMIT)


LOAD_ROWS = 256
LOAD_SLOTS = 4


def _row_pieces(src_lo, src_hi, dst_lo):
    return [(r, min(LOAD_ROWS, src_hi - r), dst_lo + r - src_lo) for r in range(src_lo, src_hi, LOAD_ROWS)]


def _stream_pieces(pieces, stage_ref, sem_ref):
    def load(k):
        src, n, _ = pieces[k]
        slot = k % LOAD_SLOTS
        return pltpu.make_async_copy(src, stage_ref.at[slot, pl.ds(0, n), :], sem_ref.at[slot])

    for k in range(min(LOAD_SLOTS, len(pieces))):
        load(k).start()
    for k, (_, n, consume) in enumerate(pieces):
        load(k).wait()
        consume(stage_ref[k % LOAD_SLOTS, 0:n, :])
        if k + LOAD_SLOTS < len(pieces):
            load(k + LOAD_SLOTS).start()


def _cast_pieces(src_ref, dst_ref, src_lo, src_hi, dst_lo):
    def piece(r, n, dr):
        def consume(v):
            dst_ref[dr:dr + n, :] = v.astype(BF16)
        return (src_ref.at[0, pl.ds(r, n), :], n, consume)
    return [piece(*p) for p in _row_pieces(src_lo, src_hi, dst_lo)]


def _ada_pieces(cond_ref, w_hbm, b_ref, mod_ref):
    s = _silu(cond_ref[...]).astype(BF16)

    def piece(k, r):
        def consume(v):
            mod_ref[k] = mod_ref[k] + _dot(s[:, r:r + LOAD_ROWS], v.astype(BF16))
        return (w_hbm.at[0, pl.ds(r, LOAD_ROWS), pl.ds(k * D_MODEL, D_MODEL)], LOAD_ROWS, consume)

    for k in range(N_MOD):
        mod_ref[k] = jnp.broadcast_to(b_ref[:, k * D_MODEL:(k + 1) * D_MODEL], mod_ref.shape[1:])
    return [piece(k, r) for k in range(N_MOD) for r in range(0, D_MODEL, LOAD_ROWS)]


N_FILT_PLANES = 6


def _filter_kernel(fe_ref, fo_ref, w1_ref, b1_ref, fr_ref, w2_ref, b2_ref, w3_ref, dl_ref, f_ref, tw_ref,
                   kf_ref, kn_ref, *, m):
    freq = fr_ref[...]
    w1 = w1_ref[...].astype(BF16)
    w2 = w2_ref[...].astype(BF16)

    def hidden(feats):
        h = jnp.sin(freq * (_dot(feats.astype(BF16), w1) + b1_ref[...]))
        h = jnp.sin(freq * (_dot(h.astype(BF16), w2) + b2_ref[...]))
        return h.astype(BF16)

    feats_e = fe_ref[...]
    feats_o = fo_ref[...]
    hb_e = hidden(feats_e)
    hb_o = hidden(feats_o)
    t_e = feats_e[:, 0:1]
    t_o = feats_o[:, 0:1]
    row = lax.broadcasted_iota(jnp.int32, (m, 1), 0)
    sign = jnp.where((row & 1) == 0, 1.0, -1.0)
    cos_t = tw_ref[:, 0:1]
    sin_t = tw_ref[:, 1:2]
    f_top = f_ref[0:m, :]
    f_bot = f_ref[m:2 * m, :]

    def spectrum(k):
        kb = k.astype(BF16)
        return _dot(f_top, kb), jnp.where(row == 0, 0.0, _dot(f_bot, kb)), jnp.sum(k * sign, axis=0, keepdims=True)

    for i in range(HY_ORDER):
        dl = dl_ref[i:i + 1, :]
        c0 = (0 * HY_ORDER + i) * HY_W
        c1 = (1 * HY_ORDER + i) * HY_W
        w3f = w3_ref[:, c0:c0 + HY_W].astype(BF16)
        w3b = w3_ref[:, c1:c1 + HY_W].astype(BF16)
        win_e = jnp.exp(-t_e * dl)
        win_o = jnp.exp(-t_o * dl)
        k0e = _dot(hb_e, w3f) * win_e
        k0o = _dot(hb_o, w3f) * win_o
        k1e = jnp.where(row == 0, 0.0, _dot(hb_e, w3b) * win_e)
        k1o = _dot(hb_o, w3b) * win_o
        ker, _, ken = spectrum(k0e + k1e)
        _, kei, _ = spectrum(k0e - k1e)
        ar, ai, an = spectrum(k0o)
        br, bi, bn = spectrum(k1o)
        planes = (ker, kei,
                  ar + cos_t * br + sin_t * bi, ai + sin_t * br - cos_t * bi,
                  cos_t * ar + sin_t * ai + br, cos_t * ai - sin_t * ar - bi)
        for p, plane in enumerate(planes):
            kf_ref[i, p] = plane.astype(BF16)
        kn_ref[i] = jnp.concatenate([ken, an - bn, bn - an, jnp.zeros((SUBLANES - 3, HY_W), F32)], axis=0)


def _filters_kernel(w1_ref, b1_ref, fr_ref, w2_ref, b2_ref, w3_ref, dl_ref, *refs, ms):
    n = len(ms)
    for k, m in enumerate(ms):
        fe_ref, fo_ref, f_ref, tw_ref = refs[4 * k:4 * k + 4]
        kf_ref, kn_ref = refs[4 * n + 2 * k:4 * n + 2 * k + 2]
        _filter_kernel(fe_ref, fo_ref, w1_ref, b1_ref, fr_ref, w2_ref, b2_ref, w3_ref, dl_ref, f_ref, tw_ref,
                       kf_ref, kn_ref, m=m)


def _filters(lengths, per_length, w1p, b1, freq, w2, b2, w3, deltas):
    ms = [l // 2 for l in lengths]
    full = lambda *shape: pl.BlockSpec(shape, lambda: (0,) * len(shape))
    in_specs = [full(EMB_PAD, HY_HIDDEN), full(1, HY_HIDDEN), full(1, HY_HIDDEN), full(HY_HIDDEN, HY_HIDDEN),
                full(1, HY_HIDDEN), full(HY_HIDDEN, N_DIR * HY_ORDER * HY_W), full(HY_ORDER, HY_W)]
    out_specs, out_shape = [], []
    for m in ms:
        in_specs += [full(m, EMB_PAD), full(m, EMB_PAD), full(2 * m, m), full(m, 2)]
        out_specs += [full(HY_ORDER, N_FILT_PLANES, m, HY_W), full(HY_ORDER, SUBLANES, HY_W)]
        out_shape += [jax.ShapeDtypeStruct((HY_ORDER, N_FILT_PLANES, m, HY_W), BF16),
                      jax.ShapeDtypeStruct((HY_ORDER, SUBLANES, HY_W), F32)]
    outs = pl.pallas_call(
        functools.partial(_filters_kernel, ms=ms),
        in_specs=in_specs,
        out_specs=out_specs,
        out_shape=out_shape,
        compiler_params=pltpu.CompilerParams(vmem_limit_bytes=VMEM_LIMIT),
        name="filt",
    )(w1p, b1, freq, w2, b2, w3, deltas, *[a for group in per_length for a in group])
    return [(outs[2 * k], outs[2 * k + 1]) for k in range(len(ms))]


def _const_spec(shape):
    return pl.BlockSpec(shape, lambda i: (0,) * len(shape), pipeline_mode=pl.Buffered(1))


def _group_specs(ctx_tiles, width):
    tm = TOKEN_TILE
    return [pl.BlockSpec((tm, width), lambda i: (jnp.minimum(i, ctx_tiles - 1), 0)),
            pl.BlockSpec((tm, width), lambda i: (jnp.maximum(i - ctx_tiles, 0), 0))]


def _mod_rows(mod_ref, ctx_tiles, n_lat, tiles_per_seq):
    i = pl.program_id(0)
    r = jnp.where(i < ctx_tiles, n_lat, (i - ctx_tiles) // tiles_per_seq)
    return [mod_ref[k, pl.ds(r, 1), :] for k in range(N_MOD)]


NEXT_CAST_ROWS = 128
NEXT_CAST_ROWS_OUT = 64


def _ffn_in_kernel(xc_ref, xl_ref, cond_ref, ba_ref, n1_ref, nm_ref, wa_hbm, wg_hbm, wu_hbm, wd_hbm, wi_hbm,
                   ng_ref, nu_ref, nd_ref, no_ref,
                   x1_ref, z_ref, xbc_ref, dt_ref, hy_ref, mod_ref, ngb_ref, nub_ref, ndb_ref, nob_ref,
                   wg_ref, wu_ref, wd_ref, wi_ref, stage_ref, sem_ref, *, ctx_tiles, n_lat, tiles_per_seq):
    @pl.when(pl.program_id(0) == 0)
    def _():
        o1, o2, o3 = IN_SPLITS
        pieces = _ada_pieces(cond_ref, wa_hbm, ba_ref, mod_ref)
        for src, dst in ((wg_hbm, wg_ref), (wu_hbm, wu_ref), (wd_hbm, wd_ref)):
            pieces += _cast_pieces(src, dst, 0, FFN_DIM, 0)
            dst[FFN_DIM:, :] = jnp.zeros((FFN_PAD - FFN_DIM, D_MODEL), BF16)
        pieces += _cast_pieces(wi_hbm, wi_ref, 0, o3, 0) + _cast_pieces(wi_hbm, wi_ref, o3, IN_COLS, o2 + DT_PAD)
        wi_ref[o3:o2 + DT_PAD, :] = jnp.zeros((o2 + DT_PAD - o3, D_MODEL), BF16)
        _stream_pieces(pieces, stage_ref, sem_ref)

    blk = jnp.minimum(pl.program_id(0), FFN_PAD // NEXT_CAST_ROWS - 1)
    wrow = blk * NEXT_CAST_ROWS + lax.broadcasted_iota(jnp.int32, (NEXT_CAST_ROWS, 1), 0)
    for src, dst in ((ng_ref, ngb_ref), (nu_ref, nub_ref), (nd_ref, ndb_ref)):
        dst[...] = jnp.where(wrow < FFN_DIM, src[...], 0.0).astype(BF16)
    nob_ref[...] = no_ref[...].astype(BF16)

    x = jnp.where(pl.program_id(0) < ctx_tiles, xc_ref[...], xl_ref[...])
    mod = _mod_rows(mod_ref, ctx_tiles, n_lat, tiles_per_seq)
    h = _rms_mod(x, n1_ref[...], mod[0], mod[1]).astype(BF16)
    x1 = x + (0.5 * mod[2]) * _swiglu_acc(h, wg_ref, wu_ref, wd_ref)
    x1_ref[...] = x1
    h2 = _rms_mod(x1, nm_ref[...], mod[3], mod[4]).astype(BF16)
    o1, o2, _ = IN_SPLITS
    z_ref[...] = _dot_nt(h2, wi_ref[0:o1, :])
    xbc_ref[...] = _dot_nt(h2, wi_ref[o1:o2, :])
    dt_ref[...] = _dot_nt(wi_ref[o2:o2 + N_DT, :], h2)
    hy = _dot_nt(h2, wi_ref[o2 + DT_PAD:, :])
    for k in range(hy_ref.shape[0]):
        hy_ref[k] = hy[:, k * LANES:(k + 1) * LANES]


def _ffn_in(x_ctx, x_lat, cond, w_ada, b_ada, n_lat, l_lat, n1, nm, wg, wu, wd, wi, next_ffn, next_out):
    tm = TOKEN_TILE
    ctx_tiles = x_ctx.shape[0] // tm
    tokens = x_ctx.shape[0] + x_lat.shape[0]
    steps = tokens // tm
    ffn_blocks = FFN_PAD // NEXT_CAST_ROWS
    out_blocks = D_MODEL // NEXT_CAST_ROWS_OUT
    assert steps >= ffn_blocks and steps >= out_blocks
    row_spec = lambda n: pl.BlockSpec((tm, n), lambda i: (i, 0))
    widths = (D_MODEL, SSD_W, SSD_XBC)
    hy_tiles = (HY_ORDER + 1) * HY_W // LANES
    dt_spec = pl.BlockSpec((N_DT, tm), lambda i: (0, i))
    outs = pl.pallas_call(
        functools.partial(_ffn_in_kernel, ctx_tiles=ctx_tiles, n_lat=n_lat, tiles_per_seq=l_lat // tm),
        grid=(steps,),
        in_specs=_group_specs(ctx_tiles, D_MODEL) + [
            _const_spec(cond.shape), _const_spec(b_ada.shape), _const_spec((1, D_MODEL)), _const_spec((1, D_MODEL))]
        + [pl.BlockSpec(memory_space=pl.ANY)] * 5
        + [pl.BlockSpec((None, NEXT_CAST_ROWS, D_MODEL), lambda i: (0, jnp.minimum(i, ffn_blocks - 1), 0))] * 3
        + [pl.BlockSpec((None, NEXT_CAST_ROWS_OUT, D_MODEL), lambda i: (0, jnp.minimum(i, out_blocks - 1), 0))],
        out_specs=[row_spec(n) for n in widths] + [dt_spec, pl.BlockSpec((hy_tiles, tm, LANES), lambda i: (0, i, 0)),
                                                   pl.BlockSpec((N_MOD, COND_ROWS, D_MODEL), lambda i: (0, 0, 0))]
        + [pl.BlockSpec((NEXT_CAST_ROWS, D_MODEL), lambda i: (jnp.minimum(i, ffn_blocks - 1), 0))] * 3
        + [pl.BlockSpec((NEXT_CAST_ROWS_OUT, D_MODEL), lambda i: (jnp.minimum(i, out_blocks - 1), 0))],
        out_shape=[jax.ShapeDtypeStruct((tokens, n), F32) for n in widths]
        + [jax.ShapeDtypeStruct((N_DT, tokens), F32), jax.ShapeDtypeStruct((hy_tiles, tokens, LANES), F32),
           jax.ShapeDtypeStruct((N_MOD, COND_ROWS, D_MODEL), F32)]
        + [jax.ShapeDtypeStruct((FFN_PAD, D_MODEL), BF16)] * 3 + [jax.ShapeDtypeStruct((D_MODEL, D_MODEL), BF16)],
        scratch_shapes=[pltpu.VMEM((FFN_PAD, D_MODEL), BF16)] * 3 + [
            pltpu.VMEM((IN_ROWS, D_MODEL), BF16), pltpu.VMEM((LOAD_SLOTS, LOAD_ROWS, D_MODEL), F32),
            pltpu.SemaphoreType.DMA((LOAD_SLOTS,))],
        compiler_params=_params("arbitrary"),
        name="ffn_in",
    )(x_ctx, x_lat, cond, b_ada, n1, nm, w_ada, wg, wu, wd, wi, *next_ffn, next_out)
    return outs[:5], outs[5], outs[6:9], outs[9]


def _out_ffn_kernel(x1_ref, ys_ref, yh_ref, mod_ref, n3_ref, nf_ref,
                    wo_ref, wg_ref, wu_ref, wd_ref, oc_ref, ol_ref, *, ctx_tiles, n_lat, tiles_per_seq):
    is_ctx = pl.program_id(0) < ctx_tiles
    mod = _mod_rows(mod_ref, ctx_tiles, n_lat, tiles_per_seq)
    y = jnp.concatenate([ys_ref[...]] + [yh_ref[k] for k in range(yh_ref.shape[0])], axis=1).astype(BF16)
    x2 = x1_ref[...] + mod[5] * _dot(y, wo_ref[...])
    h = _rms_mod(x2, n3_ref[...], mod[6], mod[7]).astype(BF16)
    x3 = x2 + (0.5 * mod[8]) * _swiglu_acc(h, wg_ref, wu_ref, wd_ref)
    ms = jnp.mean(x3 * x3, axis=-1, keepdims=True)
    out = x3 * lax.rsqrt(ms + RMS_EPS) * nf_ref[...]

    @pl.when(is_ctx)
    def _():
        oc_ref[...] = out

    @pl.when(jnp.logical_not(is_ctx))
    def _():
        ol_ref[...] = out


def _out_ffn(x1, ys, yh, t_ctx, mod, n_lat, l_lat, n3, nf, wo, wg, wu, wd):
    tm = TOKEN_TILE
    ctx_tiles = t_ctx // tm
    tokens = x1.shape[0]
    return pl.pallas_call(
        functools.partial(_out_ffn_kernel, ctx_tiles=ctx_tiles, n_lat=n_lat, tiles_per_seq=l_lat // tm),
        grid=(tokens // tm,),
        in_specs=[pl.BlockSpec((tm, D_MODEL), lambda i: (i, 0)), pl.BlockSpec((tm, SSD_W), lambda i: (i, 0)),
                  pl.BlockSpec((HY_W // LANES, tm, LANES), lambda i: (0, i, 0)),
            _const_spec(mod.shape), _const_spec((1, D_MODEL)), _const_spec((1, D_MODEL)),
            _const_spec(wo.shape), _const_spec(wg.shape), _const_spec(wu.shape), _const_spec(wd.shape)],
        out_specs=_group_specs(ctx_tiles, D_MODEL),
        out_shape=[jax.ShapeDtypeStruct((t_ctx, D_MODEL), F32),
                   jax.ShapeDtypeStruct((tokens - t_ctx, D_MODEL), F32)],
        compiler_params=_params("arbitrary"),
        name="out_ffn",
    )(x1, ys, yh, mod, n3, nf, wo, wg, wu, wd)


def _ssd_kernel(z_ref, xbc_ref, dt_ref, init_ref, cw_ref, cb_ref, dtb_ref, alog_ref, dexp_ref, nw_ref, e_ref,
                y_ref, fin_ref,
                xs_s, b_s, c_s, ec_s, dst_s, et_s, cum_s, ct_s,
                *, l, seg, zero_init, write_final):
    q = SSD_CHUNK
    nc = l // q
    n_seq = y_ref.shape[0] // l
    cr = max(seg, q)
    e_mats = [e_ref[k].astype(BF16) for k in range(e_ref.shape[0])]
    dt_t = _softplus(dt_ref[...] + dtb_ref[...])
    a_t = dt_t * (-jnp.exp(alog_ref[...]))

    first, last = _seg_edges(cr, seg)

    for r0 in range(0, n_seq * l, cr):
        rows = slice(r0, r0 + cr)
        x = xbc_ref[rows, :]
        w = cw_ref[...]
        prev = jnp.where(first, 0.0, pltpu.roll(x, 1, 0))
        nxt = jnp.where(last, 0.0, pltpu.roll(x, cr - 1, 0))
        u = _silu(prev * w[0:1, :] + x * w[1:2, :] + nxt * w[2:3, :] + cb_ref[...])
        xs_s[rows, :] = u[:, :SSD_W]
        b_s[rows, :] = u[:, SSD_W:SSD_W + LANES]
        c_s[rows, :] = u[:, SSD_W + LANES:]

    row_g = lax.broadcasted_iota(jnp.int32, (LANES, N_DIR * HP), 0) // SSD_STATE
    lane_g = (lax.broadcasted_iota(jnp.int32, (LANES, N_DIR * HP), 1) % HP) // (HP // SSD_GROUPS)
    own_t = row_g == lane_g

    ii = lax.broadcasted_iota(jnp.int32, (q, q), 0)
    jj = lax.broadcasted_iota(jnp.int32, (q, q), 1)
    tri_upper = (ii <= jj).astype(BF16)
    lane = lax.broadcasted_iota(jnp.int32, (q, LANES), 1)
    low_half = lane < SSD_STATE
    is_fwd_row = lax.broadcasted_iota(jnp.int32, (N_DT, 1), 0) < SSD_HEADS

    chunks = range(n_seq * nc)

    for c in chunks:
        rows = slice(c * q, (c + 1) * q)
        a_c = a_t[:, rows]
        cum_f = _dot_exact_lhs(a_c, tri_upper)
        tot_c = cum_f[:, q - 1:q]
        cum_t = jnp.where(is_fwd_row, cum_f, tot_c - cum_f + a_c)
        ct_s[c] = cum_t
        stack = jnp.concatenate([cum_t, jnp.exp(cum_t), dt_t[:, rows] * jnp.exp(tot_c - cum_t),
                                 jnp.zeros((LANES - 3 * N_DT, q), F32)], axis=0)
        cum_s[c] = stack.T

    def expand_pass(c):
        rows = slice(c * q, (c + 1) * q)
        cum = cum_s[c]
        cum_b = cum.astype(BF16)
        ec_s[rows, :] = _dot(cum_b, e_mats[1])
        tot = jnp.where(lane[0:1, :] < SSD_HEADS, cum[q - 1:q, :], cum[0:1, :])
        tot = jnp.where(lane[0:1, :] < N_DT, tot, 0.0)
        et_s[c] = _dot_exact_lhs(jnp.broadcast_to(jnp.exp(tot), (SUBLANES, LANES)), e_mats[0])
        xs = xs_s[rows, :]
        w = (jnp.concatenate([xs, xs], axis=1) * _dot(cum_b, e_mats[2])).astype(BF16)
        dst_s[c] = jnp.where(own_t, _dot(b_s[rows, :].T.astype(BF16), w), 0.0)

    def local_pass(c):
        rows = slice(c * q, (c + 1) * q)
        cum = cum_s[c]
        cum_t = ct_s[c]
        dt_c = dt_t[:, rows]
        bcb = b_s[rows, :].astype(BF16)
        cc = c_s[rows, :]
        g_mats = [_dot_nt(jnp.where(low_half, cc, 0.0).astype(BF16), bcb),
                  _dot_nt(jnp.where(low_half, 0.0, cc).astype(BF16), bcb)]
        y_parts = []
        for pair in range(SSD_HEADS // 2):
            g = pair // (SSD_HEADS // 2 // SSD_GROUPS)
            s_mats = []
            for hh in (2 * pair, 2 * pair + 1):
                weight = None
                for d in range(N_DIR):
                    keep = (ii >= jj) if d == 0 else (ii <= jj)
                    col = d * SSD_HEADS + hh
                    diff = cum[:, col:col + 1] - cum_t[col:col + 1, :]
                    term = jnp.exp(jnp.where(keep, diff, NEG_BIG)) * dt_c[col:col + 1, :]
                    weight = term if weight is None else weight + term
                s_mats.append((g_mats[g] * weight).astype(BF16))
            lhs = jnp.concatenate(s_mats, axis=1)
            xp = xs_s[rows, pair * LANES:(pair + 1) * LANES]
            rhs = jnp.concatenate([jnp.where(low_half, xp, 0.0), jnp.where(low_half, 0.0, xp)],
                                  axis=0).astype(BF16)
            y_parts.append(_dot(lhs, rhs))
        y_ref[rows, :] = jnp.concatenate(y_parts, axis=1)

    for c in chunks:
        expand_pass(c)
        local_pass(c)

    half = HP // SSD_GROUPS
    for s in range(n_seq):
        states = []
        for d in range(N_DIR):
            if zero_init:
                states.append(jnp.zeros((LANES, HP), F32))
            else:
                s0 = init_ref[s, d]
                states.append(jnp.where(own_t[:, :HP], jnp.concatenate([s0, s0], axis=1).T, 0.0))

        for k in range(nc):
            for d in range(N_DIR):
                c = s * nc + (k if d == 0 else nc - 1 - k)
                rows = slice(c * q, (c + 1) * q)
                lanes = slice(d * HP, (d + 1) * HP)
                if zero_init and k == 0:
                    states[d] = dst_s[c, :, lanes]
                    continue
                y_off = _dot(c_s[rows, :].astype(BF16), states[d].astype(BF16)) * ec_s[rows, lanes]
                y_ref[rows, :] = y_ref[rows, :] + y_off
                states[d] = states[d] * et_s[c, 0:1, lanes] + dst_s[c, :, lanes]

        for d in range(N_DIR if write_final else 0):
            st = states[d].T
            fin_ref[s, d, 0:half, :] = st[0:half, 0:SSD_STATE]
            fin_ref[s, d, half:HP, :] = st[half:HP, SSD_STATE:2 * SSD_STATE]

        for c in range(s * nc, (s + 1) * nc):
            rows = slice(c * q, (c + 1) * q)
            y = y_ref[rows, :] + xs_s[rows, :] * dexp_ref[...]
            y = y * _silu(z_ref[rows, :])
            ms = jnp.mean(y * y, axis=-1, keepdims=True)
            y_ref[rows, :] = y * lax.rsqrt(ms + RMS_EPS) * nw_ref[...]


SSD_STEP_ROWS = 1024


def _ssd_groups_kernel(*refs, ctx_steps, ctx, lat):
    @pl.when(pl.program_id(0) < ctx_steps)
    def _():
        _ssd_kernel(*refs, l=ctx[0], seg=ctx[1], zero_init=True, write_final=True)

    @pl.when(pl.program_id(0) >= ctx_steps)
    def _():
        _ssd_kernel(*refs, l=lat[0], seg=lat[1], zero_init=False, write_final=False)


def _ssd(z, xbc, dt, n_ctx, l_ctx, seg_ctx, l_lat, seg_lat, init_lat, cw, cb, dtb, alog, dexp, nw, e_mat):
    rows = SSD_STEP_ROWS
    tokens = z.shape[0]
    ctx_steps = n_ctx * l_ctx // rows
    ctx_seqs, lat_seqs = rows // l_ctx, rows // l_lat
    seq_spec = lambda n: pl.BlockSpec((rows, n), lambda b: (b, 0))
    const = lambda *shape: pl.BlockSpec(shape, lambda b: (0,) * len(shape))
    nc = rows // SSD_CHUNK
    return pl.pallas_call(
        functools.partial(_ssd_groups_kernel, ctx_steps=ctx_steps, ctx=(l_ctx, seg_ctx), lat=(l_lat, seg_lat)),
        grid=(tokens // rows,),
        in_specs=[seq_spec(SSD_W), seq_spec(SSD_XBC), pl.BlockSpec((N_DT, rows), lambda b: (0, b)),
                  pl.BlockSpec((lat_seqs, N_DIR, HP, SSD_STATE), lambda b: (jnp.maximum(b - ctx_steps, 0), 0, 0, 0)),
                  const(3, SSD_XBC), const(1, SSD_XBC), const(N_DT, 1), const(N_DT, 1),
                  const(1, SSD_W), const(1, SSD_W), const(*e_mat.shape)],
        out_specs=[pl.BlockSpec((rows, SSD_W), lambda b: (b, 0)),
                   pl.BlockSpec((ctx_seqs, N_DIR, HP, SSD_STATE), lambda b: (jnp.minimum(b, ctx_steps - 1), 0, 0, 0))],
        out_shape=[jax.ShapeDtypeStruct((tokens, SSD_W), F32),
                   jax.ShapeDtypeStruct((n_ctx, N_DIR, HP, SSD_STATE), F32)],
        scratch_shapes=[pltpu.VMEM((rows, SSD_W), F32), pltpu.VMEM((rows, LANES), F32), pltpu.VMEM((rows, LANES), F32),
                        pltpu.VMEM((rows, N_DIR * HP), F32),
                        pltpu.VMEM((nc, LANES, N_DIR * HP), F32), pltpu.VMEM((nc, SUBLANES, N_DIR * HP), F32),
                        pltpu.VMEM((nc, SSD_CHUNK, LANES), F32), pltpu.VMEM((nc, N_DT, SSD_CHUNK), F32)],
        compiler_params=_params("arbitrary"),
        name="ssd",
    )(z, xbc, dt, init_lat, cw, cb, dtb, alog, dexp, nw, e_mat)


HY_ROW_BLOCK = 16
HY_STEP_ROWS = 2048


def _hyena_kernel(v_ref, x1_ref, x2_ref, wv_ref, w1_ref, w2_ref, bv_ref, b1_ref, b2_ref,
                  f_ref, g_ref, kf_ref, kn_ref, skip_ref, o_ref, spec_s, prod_s, *, l, seg):
    m = l // 2
    tiles = o_ref.shape[0]
    n_seq = o_ref.shape[1] // l
    ct = tiles * LANES
    rb = HY_ROW_BLOCK
    first, last = _seg_edges(m, seg // 2)
    row0 = lax.broadcasted_iota(jnp.int32, (rb, 1), 0) == 0
    ev = slice(0, ct)
    od = slice(ct, 2 * ct)

    def conv_eo(x_ref, w_ref, b_ref, base):
        xe, xo = (jnp.concatenate([x_ref[k, pl.ds(base + p, m, stride=2), :] for k in range(tiles)], axis=1)
                  for p in range(2))
        w = w_ref[...]
        b = b_ref[...]
        xo_prev = jnp.where(first, 0.0, pltpu.roll(xo, 1, 0))
        xe_next = jnp.where(last, 0.0, pltpu.roll(xe, m - 1, 0))
        ce = xo_prev * w[0:1, :] + xe * w[1:2, :] + xo * w[2:3, :] + b
        co = xe * w[0:1, :] + xo * w[1:2, :] + xe_next * w[2:3, :] + b
        return jnp.concatenate([ce, co], axis=1)

    def pointwise(s, i, r0):
        re = slice(s * l + r0, s * l + r0 + rb)
        im = slice(s * l + m + r0, s * l + m + r0 + rb)
        er, orr = spec_s[re, ev], spec_s[re, od]
        ei, oi = spec_s[im, ev], spec_s[im, od]
        ker, kei, kor, koi, vr, vi = (kf_ref[i, p, r0:r0 + rb, :] for p in range(N_FILT_PLANES))
        if r0 == 0:
            e_n, o_n = ei[0:1, :].astype(F32), oi[0:1, :].astype(F32)
            zero = jnp.zeros((), BF16)
            ei = jnp.where(row0, zero, ei)
            oi = jnp.where(row0, zero, oi)
        pe_r = er * ker - ei * kei + orr * vr - oi * vi
        pe_i = er * kei + ei * ker + orr * vi + oi * vr
        po_r = er * kor - ei * koi + orr * ker - oi * kei
        po_i = er * koi + ei * kor + orr * kei + oi * ker
        if r0 == 0:
            kn = kn_ref[i]
            pe_i = jnp.where(row0, (e_n * kn[0:1, :] + o_n * kn[2:3, :]).astype(BF16), pe_i)
            po_i = jnp.where(row0, (e_n * kn[1:2, :] + o_n * kn[0:1, :]).astype(BF16), po_i)
        prod_s[re, ev] = pe_r
        prod_s[re, od] = po_r
        prod_s[im, ev] = pe_i
        prod_s[im, od] = po_i

    seqs = range(n_seq)
    zz = [conv_eo(v_ref, wv_ref, bv_ref, s * l) for s in seqs]
    for i, (xg_ref, wg_ref, bg_ref) in enumerate(((x1_ref, w1_ref, b1_ref), (x2_ref, w2_ref, b2_ref))):
        for s in seqs:
            spec_s[s * l:(s + 1) * l, :] = _dot(f_ref[...], zz[s].astype(BF16)).astype(BF16)
        for s in seqs:
            for r0 in range(0, m, rb):
                pointwise(s, i, r0)
        skip = skip_ref[i:i + 1, :]
        skip2 = jnp.concatenate([skip, skip], axis=1)
        for s in seqs:
            conv = _dot(g_ref[...], prod_s[s * l:(s + 1) * l, :])
            zz[s] = conv_eo(xg_ref, wg_ref, bg_ref, s * l) * (conv + zz[s] * skip2)
    for s in seqs:
        for k in range(tiles):
            o_ref[k, pl.ds(s * l, m, stride=2), :] = zz[s][:, k * LANES:(k + 1) * LANES]
            o_ref[k, pl.ds(s * l + 1, m, stride=2), :] = zz[s][:, ct + k * LANES:ct + (k + 1) * LANES]


def _hyena_groups_kernel(v_ref, x1_ref, x2_ref, wv_ref, w1_ref, w2_ref, bv_ref, b1_ref, b2_ref,
                         fc_ref, gc_ref, kfc_ref, knc_ref, fl_ref, gl_ref, kfl_ref, knl_ref, skip_ref,
                         o_ref, spec_s, prod_s, *, ctx_steps, ctx, lat):
    common = (v_ref, x1_ref, x2_ref, wv_ref, w1_ref, w2_ref, bv_ref, b1_ref, b2_ref)

    @pl.when(pl.program_id(1) < ctx_steps)
    def _():
        _hyena_kernel(*common, fc_ref, gc_ref, kfc_ref, knc_ref, skip_ref, o_ref, spec_s, prod_s, l=ctx[0], seg=ctx[1])

    @pl.when(pl.program_id(1) >= ctx_steps)
    def _():
        _hyena_kernel(*common, fl_ref, gl_ref, kfl_ref, knl_ref, skip_ref, o_ref, spec_s, prod_s, l=lat[0], seg=lat[1])


def _hyena(hy, n_ctx, l_ctx, seg_ctx, l_lat, seg_lat, cw, cb, consts_ctx, consts_lat, skip):
    ct = HY_CH_TILE
    nct = HY_W // ct
    tiles = ct // LANES
    rows = HY_STEP_ROWS
    tokens = hy.shape[1]
    ctx_steps = n_ctx * l_ctx // rows
    part = lambda p: pl.BlockSpec((tiles, rows, LANES), lambda j, b: (p * nct + j, b, 0))
    wpart = lambda p: pl.BlockSpec((3, ct), lambda j, b: (0, p * nct + j))
    bpart = lambda p: pl.BlockSpec((1, ct), lambda j, b: (0, p * nct + j))

    def const_specs(l):
        m = l // 2
        return [pl.BlockSpec((l, m), lambda j, b: (0, 0)),
                pl.BlockSpec((m, l), lambda j, b: (0, 0)),
                pl.BlockSpec((HY_ORDER, N_FILT_PLANES, m, ct), lambda j, b: (0, 0, 0, j)),
                pl.BlockSpec((HY_ORDER, SUBLANES, ct), lambda j, b: (0, 0, j))]

    return pl.pallas_call(
        functools.partial(_hyena_groups_kernel, ctx_steps=ctx_steps, ctx=(l_ctx, seg_ctx), lat=(l_lat, seg_lat)),
        grid=(nct, tokens // rows),
        in_specs=[part(0), part(1), part(2), wpart(0), wpart(1), wpart(2), bpart(0), bpart(1), bpart(2)]
        + const_specs(l_ctx) + const_specs(l_lat) + [pl.BlockSpec((HY_ORDER, ct), lambda j, b: (0, j))],
        out_specs=pl.BlockSpec((tiles, rows, LANES), lambda j, b: (j, b, 0)),
        out_shape=jax.ShapeDtypeStruct((HY_W // LANES, tokens, LANES), F32),
        scratch_shapes=[pltpu.VMEM((rows, 2 * ct), BF16), pltpu.VMEM((rows, 2 * ct), BF16)],
        compiler_params=_params("arbitrary", "arbitrary"),
        name="hyena",
    )(hy, hy, hy, cw, cw, cw, cb, cb, cb, *consts_ctx, *consts_lat, skip)


def _dft_mats(l):
    n = 2 * l
    f = np.arange(l, dtype=np.int64)[:, None]
    t = np.arange(l, dtype=np.int64)[None, :]
    ang = 2.0 * np.pi * ((f * t) % n).astype(np.float64) / n
    alt = np.where(np.arange(l) % 2 == 0, 1.0, -1.0)
    top = np.cos(ang)
    bot = -np.sin(ang)
    bot[0, :] = alt
    fwd = np.concatenate([top, bot], axis=0)
    wf = np.full((l,), 2.0)
    wf[0] = 1.0
    gtop = np.cos(ang).T * wf[None, :] / n
    gbot = -np.sin(ang).T * 2.0 / n
    gbot[:, 0] = alt / n
    inv = np.concatenate([gtop, gbot], axis=1)
    return fwd.astype(np.float32), inv.astype(np.float32)


def _filter_feats(l):
    t = np.linspace(0.0, 1.0, l)[:, None]
    w = (2.0 * np.pi / l) * np.arange(l, dtype=np.float64)[:, None]
    f = np.linspace(1e-4, HY_BANDS - 1, HY_BANDS)[None, :]
    feats = np.concatenate([t, np.cos(f * w), -np.sin(f * w)], axis=-1)
    out = np.zeros((l, EMB_PAD), np.float32)
    out[:, :HY_EMB] = feats
    return out[0::2], out[1::2]


def _shift_twiddles(l):
    theta = 2.0 * np.pi * np.arange(l // 2, dtype=np.float64) / l
    return np.stack([np.cos(theta), np.sin(theta)], axis=1).astype(np.float32)


def _head_expand():
    n_blocks = LANES // N_DT // 2
    e = np.zeros((n_blocks, LANES, N_DIR * HP), np.float32)
    for k in range(n_blocks):
        for j in range(N_DT):
            e[k, k * N_DT + j, j * SSD_HEAD_DIM:(j + 1) * SSD_HEAD_DIM] = 1.0
    return e


def kernel(x_prompt, x_sample, state_ssd, c, c_ctx, w_ada, b_ada, norm_ffn1, ffn1_w_gate, ffn1_w_up, ffn1_w_down, norm_mix, w_in, w_out, ssd_conv_w, ssd_conv_b, ssd_dt_bias, ssd_a_log, ssd_d, ssd_norm_w, hy_conv_w, hy_conv_b, hy_w1, hy_b1, hy_freq, hy_w2, hy_b2, hy_w3, hy_skip, norm_ffn2, ffn2_w_gate, ffn2_w_up, ffn2_w_down, norm_final):
    assert w_ada.shape[0] == 1, "single layer"
    n_ctx, l_ctx, _ = x_prompt.shape
    n_lat, l_lat, _ = x_sample.shape
    t_ctx = n_ctx * l_ctx

    assert n_lat < COND_ROWS
    cond = jnp.zeros((COND_ROWS, D_MODEL), F32).at[:n_lat].set(c).at[n_lat].set(c_ctx)

    tr = lambda w: jnp.swapaxes(w, 1, 2)

    row = lambda v: v.reshape(1, -1)
    dtb, alog = ssd_dt_bias[0].reshape(N_DT, 1), ssd_a_log[0].reshape(N_DT, 1)
    dexp = jnp.repeat(ssd_d[0], SSD_HEAD_DIM).reshape(1, SSD_W)
    e_mat = jnp.asarray(_head_expand())
    w1p = jnp.pad(hy_w1[0], ((0, EMB_PAD - HY_EMB), (0, 0)))
    deltas = jnp.asarray(np.abs(np.linspace(HY_MIN_DECAY, HY_MAX_DECAY, HY_ORDER * HY_W))
                         .reshape(HY_ORDER, HY_W).astype(np.float32))

    (x1, z, xbc, dt, hy), mod, (wg2, wu2, wd2), wo = _ffn_in(
        x_prompt.reshape(t_ctx, D_MODEL), x_sample.reshape(n_lat * l_lat, D_MODEL), cond, w_ada, b_ada, n_lat, l_lat,
        row(norm_ffn1[0]), row(norm_mix[0]), tr(ffn1_w_gate), tr(ffn1_w_up), ffn1_w_down, tr(w_in),
        [tr(ffn2_w_gate), tr(ffn2_w_up), ffn2_w_down], w_out)

    lengths = (l_ctx, l_lat)
    dft = [tuple(jnp.asarray(a).astype(BF16) for a in _dft_mats(l // 2)) for l in lengths]
    filt = _filters(lengths,
                    [tuple(jnp.asarray(a) for a in _filter_feats(l)) + (dft[k][0], jnp.asarray(_shift_twiddles(l)))
                     for k, l in enumerate(lengths)],
                    w1p, row(hy_b1[0]), row(hy_freq[0]), hy_w2[0], row(hy_b2[0]), hy_w3[0], deltas)
    hyena_consts = [dft[k] + filt[k] for k in range(len(lengths))]

    lat_init = state_ssd[:, 0].reshape(n_lat, N_DIR, HP, SSD_STATE)
    ys, ctx_fin = _ssd(z, xbc, dt, n_ctx, l_ctx, l_ctx, l_lat, GRID_W, lat_init, ssd_conv_w[0], row(ssd_conv_b[0]),
                       dtb, alog, dexp, row(ssd_norm_w[0]), e_mat)
    yh = _hyena(hy, n_ctx, l_ctx, l_ctx, l_lat, GRID_W, hy_conv_w[0], row(hy_conv_b[0]),
                hyena_consts[0], hyena_consts[1], hy_skip[0])

    y_ctx, y_lat = _out_ffn(x1, ys, yh, t_ctx, mod, n_lat, l_lat, row(norm_ffn2[0]), row(norm_final),
                            wo, wg2, wu2, wd2)
    new_state = ctx_fin.reshape(n_ctx, 1, N_DIR, SSD_HEADS, SSD_HEAD_DIM, SSD_STATE).astype(x_prompt.dtype)
    return (y_ctx.reshape(n_ctx, l_ctx, D_MODEL), y_lat.reshape(n_lat, l_lat, D_MODEL), new_state)
```

```python
import functools
import math

import numpy as np
import jax
import jax.numpy as jnp
from jax import lax
from jax.experimental import pallas as pl
from jax.experimental.pallas import tpu as pltpu

F32 = jnp.float32
BF16 = jnp.bfloat16

D_MODEL = 1024
GRID_W = 64
N_MOD = 9
RMS_EPS = 1e-6
FFN_DIM = 2752
SSD_W = 512
SSD_HEADS = 8
SSD_HEAD_DIM = 64
SSD_STATE = 64
SSD_GROUPS = 2
SSD_CHUNK = 128
SSD_XBC = SSD_W + 2 * SSD_GROUPS * SSD_STATE
N_DIR = 2
HY_W = 512
HY_ORDER = 2
HY_EMB = 33
HY_BANDS = (HY_EMB - 1) // 2
HY_HIDDEN = 64
HY_MIN_DECAY = math.log(1e-2) / 1.5
HY_MAX_DECAY = math.log(1e-2) / 0.3
IN_SPLITS = (SSD_W, SSD_W + SSD_XBC, SSD_W + SSD_XBC + N_DIR * SSD_HEADS)
IN_COLS = IN_SPLITS[-1] + (HY_ORDER + 1) * HY_W

LANES = 128
SUBLANES = 8
FFN_PAD = 2816
FFN_CHUNK = 256
N_DT = N_DIR * SSD_HEADS
DT_PAD = LANES
IN_ROWS = IN_COLS + DT_PAD - N_DIR * SSD_HEADS
EMB_PAD = LANES
TOKEN_TILE = 512
HY_CH_TILE = 256
VMEM_LIMIT = 60 * 1024 * 1024
COND_ROWS = 16
NEG_BIG = -1e30
HP = SSD_HEADS * SSD_HEAD_DIM


def _silu(x):
    return x * jax.nn.sigmoid(x)


def _softplus(x):
    return jnp.maximum(x, 0.0) + jnp.log1p(jnp.exp(-jnp.abs(x)))


def _rms_mod(x, gain, shift, scale):
    ms = jnp.mean(x * x, axis=-1, keepdims=True)
    return x * lax.rsqrt(ms + RMS_EPS) * (gain * (1.0 + scale)) + shift


def _dot(a, b):
    return jnp.dot(a, b, preferred_element_type=F32)


def _dot_nt(a, b):
    return lax.dot_general(a, b, (((1,), (1,)), ((), ())), preferred_element_type=F32)


def _split3(x):
    hi = x.astype(BF16)
    r = x - hi.astype(F32)
    mid = r.astype(BF16)
    lo = (r - mid.astype(F32)).astype(BF16)
    return hi, mid, lo


def _dot_exact_lhs(x, m01):
    hi, mid, lo = _split3(x)
    return _dot(hi, m01) + _dot(mid, m01) + _dot(lo, m01)


def _seg_edges(l, seg):
    pos = lax.broadcasted_iota(jnp.int32, (l, 1), 0) & (seg - 1)
    return pos == 0, pos == seg - 1


def _swiglu_acc(h, wg_ref, wu_ref, wd_ref):
    acc = None
    for k in range(FFN_PAD // FFN_CHUNK):
        cols = slice(k * FFN_CHUNK, (k + 1) * FFN_CHUNK)
        g = _dot_nt(h, wg_ref[cols, :])
        u = _dot_nt(h, wu_ref[cols, :])
        a = (_silu(g) * u).astype(BF16)
        part = _dot(a, wd_ref[cols, :])
        acc = part if acc is None else acc + part
    return acc


def _params(*semantics):
    return pltpu.CompilerParams(dimension_semantics=semantics, vmem_limit_bytes=VMEM_LIMIT)


LOAD_ROWS = 256
LOAD_SLOTS = 4


def _row_pieces(src_lo, src_hi, dst_lo):
    return [(r, min(LOAD_ROWS, src_hi - r), dst_lo + r - src_lo) for r in range(src_lo, src_hi, LOAD_ROWS)]


def _stream_pieces(pieces, stage_ref, sem_ref):
    def load(k):
        src, n, _ = pieces[k]
        slot = k % LOAD_SLOTS
        return pltpu.make_async_copy(src, stage_ref.at[slot, pl.ds(0, n), :], sem_ref.at[slot])

    for k in range(min(LOAD_SLOTS, len(pieces))):
        load(k).start(priority=k % 2)
    for k, (_, n, consume) in enumerate(pieces):
        load(k).wait()
        consume(stage_ref[k % LOAD_SLOTS, 0:n, :])
        if k + LOAD_SLOTS < len(pieces):
            load(k + LOAD_SLOTS).start(priority=(k + LOAD_SLOTS) % 2)


def _cast_pieces(src_ref, dst_ref, src_lo, src_hi, dst_lo):
    def piece(r, n, dr):
        def consume(v):
            dst_ref[dr:dr + n, :] = v.astype(BF16)
        return (src_ref.at[0, pl.ds(r, n), :], n, consume)
    return [piece(*p) for p in _row_pieces(src_lo, src_hi, dst_lo)]


def _ada_pieces(cond_ref, w_hbm, b_ref, mod_ref):
    s = _silu(cond_ref[...]).astype(BF16)

    def piece(k, r):
        def consume(v):
            mod_ref[k] = mod_ref[k] + _dot(s[:, r:r + LOAD_ROWS], v.astype(BF16))
        return (w_hbm.at[0, pl.ds(r, LOAD_ROWS), pl.ds(k * D_MODEL, D_MODEL)], LOAD_ROWS, consume)

    for k in range(N_MOD):
        mod_ref[k] = jnp.broadcast_to(b_ref[:, k * D_MODEL:(k + 1) * D_MODEL], mod_ref.shape[1:])
    return [piece(k, r) for k in range(N_MOD) for r in range(0, D_MODEL, LOAD_ROWS)]


N_FILT_PLANES = 6


def _filter_kernel(fe_ref, fo_ref, w1_ref, b1_ref, fr_ref, w2_ref, b2_ref, w3_ref, dl_ref, f_ref, tw_ref,
                   kf_ref, kn_ref, *, m):
    freq = fr_ref[...]
    w1 = w1_ref[...].astype(BF16)
    w2 = w2_ref[...].astype(BF16)

    def hidden(feats):
        h = jnp.sin(freq * (_dot(feats.astype(BF16), w1) + b1_ref[...]))
        h = jnp.sin(freq * (_dot(h.astype(BF16), w2) + b2_ref[...]))
        return h.astype(BF16)

    feats_e = fe_ref[...]
    feats_o = fo_ref[...]
    hb_e = hidden(feats_e)
    hb_o = hidden(feats_o)
    t_e = feats_e[:, 0:1]
    t_o = feats_o[:, 0:1]
    row = lax.broadcasted_iota(jnp.int32, (m, 1), 0)
    sign = jnp.where((row & 1) == 0, 1.0, -1.0)
    cos_t = tw_ref[:, 0:1]
    sin_t = tw_ref[:, 1:2]
    f_top = f_ref[0:m, :]
    f_bot = f_ref[m:2 * m, :]

    def spectrum(k):
        kb = k.astype(BF16)
        return _dot(f_top, kb), jnp.where(row == 0, 0.0, _dot(f_bot, kb)), jnp.sum(k * sign, axis=0, keepdims=True)

    for i in range(HY_ORDER):
        dl = dl_ref[i:i + 1, :]
        c0 = (0 * HY_ORDER + i) * HY_W
        c1 = (1 * HY_ORDER + i) * HY_W
        w3f = w3_ref[:, c0:c0 + HY_W].astype(BF16)
        w3b = w3_ref[:, c1:c1 + HY_W].astype(BF16)
        win_e = jnp.exp(-t_e * dl)
        win_o = jnp.exp(-t_o * dl)
        k0e = _dot(hb_e, w3f) * win_e
        k0o = _dot(hb_o, w3f) * win_o
        k1e = jnp.where(row == 0, 0.0, _dot(hb_e, w3b) * win_e)
        k1o = _dot(hb_o, w3b) * win_o
        ker, _, ken = spectrum(k0e + k1e)
        _, kei, _ = spectrum(k0e - k1e)
        ar, ai, an = spectrum(k0o)
        br, bi, bn = spectrum(k1o)
        planes = (ker, kei,
                  ar + cos_t * br + sin_t * bi, ai + sin_t * br - cos_t * bi,
                  cos_t * ar + sin_t * ai + br, cos_t * ai - sin_t * ar - bi)
        for p, plane in enumerate(planes):
            kf_ref[i, p] = plane.astype(BF16)
        kn_ref[i] = jnp.concatenate([ken, an - bn, bn - an, jnp.zeros((SUBLANES - 3, HY_W), F32)], axis=0)


def _filters_kernel(w1_ref, b1_ref, fr_ref, w2_ref, b2_ref, w3_ref, dl_ref, *refs, ms):
    n = len(ms)
    for k, m in enumerate(ms):
        fe_ref, fo_ref, f_ref, tw_ref = refs[4 * k:4 * k + 4]
        kf_ref, kn_ref = refs[4 * n + 2 * k:4 * n + 2 * k + 2]
        _filter_kernel(fe_ref, fo_ref, w1_ref, b1_ref, fr_ref, w2_ref, b2_ref, w3_ref, dl_ref, f_ref, tw_ref,
                       kf_ref, kn_ref, m=m)


def _filters(lengths, per_length, w1p, b1, freq, w2, b2, w3, deltas):
    ms = [l // 2 for l in lengths]
    full = lambda *shape: pl.BlockSpec(shape, lambda: (0,) * len(shape))
    in_specs = [full(EMB_PAD, HY_HIDDEN), full(1, HY_HIDDEN), full(1, HY_HIDDEN), full(HY_HIDDEN, HY_HIDDEN),
                full(1, HY_HIDDEN), full(HY_HIDDEN, N_DIR * HY_ORDER * HY_W), full(HY_ORDER, HY_W)]
    out_specs, out_shape = [], []
    for m in ms:
        in_specs += [full(m, EMB_PAD), full(m, EMB_PAD), full(2 * m, m), full(m, 2)]
        out_specs += [full(HY_ORDER, N_FILT_PLANES, m, HY_W), full(HY_ORDER, SUBLANES, HY_W)]
        out_shape += [jax.ShapeDtypeStruct((HY_ORDER, N_FILT_PLANES, m, HY_W), BF16),
                      jax.ShapeDtypeStruct((HY_ORDER, SUBLANES, HY_W), F32)]
    outs = pl.pallas_call(
        functools.partial(_filters_kernel, ms=ms),
        in_specs=in_specs,
        out_specs=out_specs,
        out_shape=out_shape,
        compiler_params=pltpu.CompilerParams(vmem_limit_bytes=VMEM_LIMIT),
        name="filt",
    )(w1p, b1, freq, w2, b2, w3, deltas, *[a for group in per_length for a in group])
    return [(outs[2 * k], outs[2 * k + 1]) for k in range(len(ms))]


def _const_spec(shape):
    return pl.BlockSpec(shape, lambda i: (0,) * len(shape), pipeline_mode=pl.Buffered(1))


def _group_specs(ctx_tiles, width):
    tm = TOKEN_TILE
    return [pl.BlockSpec((tm, width), lambda i: (jnp.minimum(i, ctx_tiles - 1), 0)),
            pl.BlockSpec((tm, width), lambda i: (jnp.maximum(i - ctx_tiles, 0), 0))]


def _mod_rows(mod_ref, ctx_tiles, n_lat, tiles_per_seq):
    i = pl.program_id(0)
    r = jnp.where(i < ctx_tiles, n_lat, (i - ctx_tiles) // tiles_per_seq)
    return [mod_ref[k, pl.ds(r, 1), :] for k in range(N_MOD)]


NEXT_CAST_ROWS = 128
NEXT_CAST_ROWS_OUT = 64


def _ffn_in_kernel(xc_ref, xl_ref, cond_ref, ba_ref, n1_ref, nm_ref, wa_hbm, wg_hbm, wu_hbm, wd_hbm, wi_hbm,
                   ng_ref, nu_ref, nd_ref, no_ref,
                   x1_ref, z_ref, xbc_ref, dt_ref, hy_ref, mod_ref, ngb_ref, nub_ref, ndb_ref, nob_ref,
                   wg_ref, wu_ref, wd_ref, wi_ref, stage_ref, sem_ref, *, ctx_tiles, n_lat, tiles_per_seq):
    @pl.when(pl.program_id(0) == 0)
    def _():
        o1, o2, o3 = IN_SPLITS
        pieces = _ada_pieces(cond_ref, wa_hbm, ba_ref, mod_ref)
        for src, dst in ((wg_hbm, wg_ref), (wu_hbm, wu_ref), (wd_hbm, wd_ref)):
            pieces += _cast_pieces(src, dst, 0, FFN_DIM, 0)
            dst[FFN_DIM:, :] = jnp.zeros((FFN_PAD - FFN_DIM, D_MODEL), BF16)
        pieces += _cast_pieces(wi_hbm, wi_ref, 0, o3, 0) + _cast_pieces(wi_hbm, wi_ref, o3, IN_COLS, o2 + DT_PAD)
        wi_ref[o3:o2 + DT_PAD, :] = jnp.zeros((o2 + DT_PAD - o3, D_MODEL), BF16)
        _stream_pieces(pieces, stage_ref, sem_ref)

    blk = jnp.minimum(pl.program_id(0), FFN_PAD // NEXT_CAST_ROWS - 1)
    wrow = blk * NEXT_CAST_ROWS + lax.broadcasted_iota(jnp.int32, (NEXT_CAST_ROWS, 1), 0)
    for src, dst in ((ng_ref, ngb_ref), (nu_ref, nub_ref), (nd_ref, ndb_ref)):
        dst[...] = jnp.where(wrow < FFN_DIM, src[...], 0.0).astype(BF16)
    nob_ref[...] = no_ref[...].astype(BF16)

    x = jnp.where(pl.program_id(0) < ctx_tiles, xc_ref[...], xl_ref[...])
    mod = _mod_rows(mod_ref, ctx_tiles, n_lat, tiles_per_seq)
    h = _rms_mod(x, n1_ref[...], mod[0], mod[1]).astype(BF16)
    x1 = x + (0.5 * mod[2]) * _swiglu_acc(h, wg_ref, wu_ref, wd_ref)
    x1_ref[...] = x1
    h2 = _rms_mod(x1, nm_ref[...], mod[3], mod[4]).astype(BF16)
    o1, o2, _ = IN_SPLITS
    z_ref[...] = _dot_nt(h2, wi_ref[0:o1, :])
    xbc_ref[...] = _dot_nt(h2, wi_ref[o1:o2, :])
    dt_ref[...] = _dot_nt(wi_ref[o2:o2 + N_DT, :], h2)
    hy = _dot_nt(h2, wi_ref[o2 + DT_PAD:, :])
    for k in range(hy_ref.shape[0]):
        hy_ref[k] = hy[:, k * LANES:(k + 1) * LANES]


def _ffn_in(x_ctx, x_lat, cond, w_ada, b_ada, n_lat, l_lat, n1, nm, wg, wu, wd, wi, next_ffn, next_out):
    tm = TOKEN_TILE
    ctx_tiles = x_ctx.shape[0] // tm
    tokens = x_ctx.shape[0] + x_lat.shape[0]
    steps = tokens // tm
    ffn_blocks = FFN_PAD // NEXT_CAST_ROWS
    out_blocks = D_MODEL // NEXT_CAST_ROWS_OUT
    assert steps >= ffn_blocks and steps >= out_blocks
    row_spec = lambda n: pl.BlockSpec((tm, n), lambda i: (i, 0))
    widths = (D_MODEL, SSD_W, SSD_XBC)
    hy_tiles = (HY_ORDER + 1) * HY_W // LANES
    dt_spec = pl.BlockSpec((N_DT, tm), lambda i: (0, i))
    outs = pl.pallas_call(
        functools.partial(_ffn_in_kernel, ctx_tiles=ctx_tiles, n_lat=n_lat, tiles_per_seq=l_lat // tm),
        grid=(steps,),
        in_specs=_group_specs(ctx_tiles, D_MODEL) + [
            _const_spec(cond.shape), _const_spec(b_ada.shape), _const_spec((1, D_MODEL)), _const_spec((1, D_MODEL))]
        + [pl.BlockSpec(memory_space=pl.ANY)] * 5
        + [pl.BlockSpec((None, NEXT_CAST_ROWS, D_MODEL), lambda i: (0, jnp.minimum(i, ffn_blocks - 1), 0))] * 3
        + [pl.BlockSpec((None, NEXT_CAST_ROWS_OUT, D_MODEL), lambda i: (0, jnp.minimum(i, out_blocks - 1), 0))],
        out_specs=[row_spec(n) for n in widths] + [dt_spec, pl.BlockSpec((hy_tiles, tm, LANES), lambda i: (0, i, 0)),
                                                   pl.BlockSpec((N_MOD, COND_ROWS, D_MODEL), lambda i: (0, 0, 0))]
        + [pl.BlockSpec((NEXT_CAST_ROWS, D_MODEL), lambda i: (jnp.minimum(i, ffn_blocks - 1), 0))] * 3
        + [pl.BlockSpec((NEXT_CAST_ROWS_OUT, D_MODEL), lambda i: (jnp.minimum(i, out_blocks - 1), 0))],
        out_shape=[jax.ShapeDtypeStruct((tokens, n), F32) for n in widths]
        + [jax.ShapeDtypeStruct((N_DT, tokens), F32), jax.ShapeDtypeStruct((hy_tiles, tokens, LANES), F32),
           jax.ShapeDtypeStruct((N_MOD, COND_ROWS, D_MODEL), F32)]
        + [jax.ShapeDtypeStruct((FFN_PAD, D_MODEL), BF16)] * 3 + [jax.ShapeDtypeStruct((D_MODEL, D_MODEL), BF16)],
        scratch_shapes=[pltpu.VMEM((FFN_PAD, D_MODEL), BF16)] * 3 + [
            pltpu.VMEM((IN_ROWS, D_MODEL), BF16), pltpu.VMEM((LOAD_SLOTS, LOAD_ROWS, D_MODEL), F32),
            pltpu.SemaphoreType.DMA((LOAD_SLOTS,))],
        compiler_params=_params("arbitrary"),
        name="ffn_in",
    )(x_ctx, x_lat, cond, b_ada, n1, nm, w_ada, wg, wu, wd, wi, *next_ffn, next_out)
    return outs[:5], outs[5], outs[6:9], outs[9]


def _out_ffn_kernel(x1_ref, ys_ref, yh_ref, mod_ref, n3_ref, nf_ref,
                    wo_ref, wg_ref, wu_ref, wd_ref, oc_ref, ol_ref, *, ctx_tiles, n_lat, tiles_per_seq):
    is_ctx = pl.program_id(0) < ctx_tiles
    mod = _mod_rows(mod_ref, ctx_tiles, n_lat, tiles_per_seq)
    y = jnp.concatenate([ys_ref[...]] + [yh_ref[k] for k in range(yh_ref.shape[0])], axis=1).astype(BF16)
    x2 = x1_ref[...] + mod[5] * _dot(y, wo_ref[...])
    h = _rms_mod(x2, n3_ref[...], mod[6], mod[7]).astype(BF16)
    x3 = x2 + (0.5 * mod[8]) * _swiglu_acc(h, wg_ref, wu_ref, wd_ref)
    ms = jnp.mean(x3 * x3, axis=-1, keepdims=True)
    out = x3 * lax.rsqrt(ms + RMS_EPS) * nf_ref[...]

    @pl.when(is_ctx)
    def _():
        oc_ref[...] = out

    @pl.when(jnp.logical_not(is_ctx))
    def _():
        ol_ref[...] = out


def _out_ffn(x1, ys, yh, t_ctx, mod, n_lat, l_lat, n3, nf, wo, wg, wu, wd):
    tm = TOKEN_TILE
    ctx_tiles = t_ctx // tm
    tokens = x1.shape[0]
    return pl.pallas_call(
        functools.partial(_out_ffn_kernel, ctx_tiles=ctx_tiles, n_lat=n_lat, tiles_per_seq=l_lat // tm),
        grid=(tokens // tm,),
        in_specs=[pl.BlockSpec((tm, D_MODEL), lambda i: (i, 0)), pl.BlockSpec((tm, SSD_W), lambda i: (i, 0)),
                  pl.BlockSpec((HY_W // LANES, tm, LANES), lambda i: (0, i, 0)),
            _const_spec(mod.shape), _const_spec((1, D_MODEL)), _const_spec((1, D_MODEL)),
            _const_spec(wo.shape), _const_spec(wg.shape), _const_spec(wu.shape), _const_spec(wd.shape)],
        out_specs=_group_specs(ctx_tiles, D_MODEL),
        out_shape=[jax.ShapeDtypeStruct((t_ctx, D_MODEL), F32),
                   jax.ShapeDtypeStruct((tokens - t_ctx, D_MODEL), F32)],
        compiler_params=_params("arbitrary"),
        name="out_ffn",
    )(x1, ys, yh, mod, n3, nf, wo, wg, wu, wd)


def _ssd_kernel(z_ref, xbc_ref, dt_ref, init_ref, cw_ref, cb_ref, dtb_ref, alog_ref, dexp_ref, nw_ref, e_ref,
                y_ref, fin_ref,
                xs_s, b_s, c_s, ec_s, dst_s, et_s, cum_s, ct_s,
                *, l, seg, zero_init, write_final):
    q = SSD_CHUNK
    nc = l // q
    n_seq = y_ref.shape[0] // l
    cr = max(seg, q)
    e_mats = [e_ref[k].astype(BF16) for k in range(e_ref.shape[0])]
    dt_t = _softplus(dt_ref[...] + dtb_ref[...])
    a_t = dt_t * (-jnp.exp(alog_ref[...]))

    first, last = _seg_edges(cr, seg)

    for r0 in range(0, n_seq * l, cr):
        rows = slice(r0, r0 + cr)
        x = xbc_ref[rows, :]
        w = cw_ref[...]
        prev = jnp.where(first, 0.0, pltpu.roll(x, 1, 0))
        nxt = jnp.where(last, 0.0, pltpu.roll(x, cr - 1, 0))
        u = _silu(prev * w[0:1, :] + x * w[1:2, :] + nxt * w[2:3, :] + cb_ref[...])
        xs_s[rows, :] = u[:, :SSD_W]
        b_s[rows, :] = u[:, SSD_W:SSD_W + LANES]
        c_s[rows, :] = u[:, SSD_W + LANES:]

    row_g = lax.broadcasted_iota(jnp.int32, (LANES, N_DIR * HP), 0) // SSD_STATE
    lane_g = (lax.broadcasted_iota(jnp.int32, (LANES, N_DIR * HP), 1) % HP) // (HP // SSD_GROUPS)
    own_t = row_g == lane_g

    ii = lax.broadcasted_iota(jnp.int32, (q, q), 0)
    jj = lax.broadcasted_iota(jnp.int32, (q, q), 1)
    tri_upper = (ii <= jj).astype(BF16)
    lane = lax.broadcasted_iota(jnp.int32, (q, LANES), 1)
    low_half = lane < SSD_STATE
    is_fwd_row = lax.broadcasted_iota(jnp.int32, (N_DT, 1), 0) < SSD_HEADS

    chunks = range(n_seq * nc)

    for c in chunks:
        rows = slice(c * q, (c + 1) * q)
        a_c = a_t[:, rows]
        cum_f = _dot_exact_lhs(a_c, tri_upper)
        tot_c = cum_f[:, q - 1:q]
        cum_t = jnp.where(is_fwd_row, cum_f, tot_c - cum_f + a_c)
        ct_s[c] = cum_t
        stack = jnp.concatenate([cum_t, jnp.exp(cum_t), dt_t[:, rows] * jnp.exp(tot_c - cum_t),
                                 jnp.zeros((LANES - 3 * N_DT, q), F32)], axis=0)
        cum_s[c] = stack.T

    def expand_pass(c):
        rows = slice(c * q, (c + 1) * q)
        cum = cum_s[c]
        cum_b = cum.astype(BF16)
        ec_s[rows, :] = _dot(cum_b, e_mats[1])
        tot = jnp.where(lane[0:1, :] < SSD_HEADS, cum[q - 1:q, :], cum[0:1, :])
        tot = jnp.where(lane[0:1, :] < N_DT, tot, 0.0)
        et_s[c] = _dot_exact_lhs(jnp.broadcast_to(jnp.exp(tot), (SUBLANES, LANES)), e_mats[0])
        xs = xs_s[rows, :]
        w = (jnp.concatenate([xs, xs], axis=1) * _dot(cum_b, e_mats[2])).astype(BF16)
        dst_s[c] = jnp.where(own_t, _dot(b_s[rows, :].T.astype(BF16), w), 0.0)

    def local_pass(c):
        rows = slice(c * q, (c + 1) * q)
        cum = cum_s[c]
        cum_t = ct_s[c]
        dt_c = dt_t[:, rows]
        bcb = b_s[rows, :].astype(BF16)
        cc = c_s[rows, :]
        g_mats = [_dot_nt(jnp.where(low_half, cc, 0.0).astype(BF16), bcb),
                  _dot_nt(jnp.where(low_half, 0.0, cc).astype(BF16), bcb)]
        y_parts = []
        for pair in range(SSD_HEADS // 2):
            g = pair // (SSD_HEADS // 2 // SSD_GROUPS)
            s_mats = []
            for hh in (2 * pair, 2 * pair + 1):
                weight = None
                for d in range(N_DIR):
                    keep = (ii >= jj) if d == 0 else (ii <= jj)
                    col = d * SSD_HEADS + hh
                    diff = cum[:, col:col + 1] - cum_t[col:col + 1, :]
                    term = jnp.exp(jnp.where(keep, diff, NEG_BIG)) * dt_c[col:col + 1, :]
                    weight = term if weight is None else weight + term
                s_mats.append((g_mats[g] * weight).astype(BF16))
            lhs = jnp.concatenate(s_mats, axis=1)
            xp = xs_s[rows, pair * LANES:(pair + 1) * LANES]
            rhs = jnp.concatenate([jnp.where(low_half, xp, 0.0), jnp.where(low_half, 0.0, xp)],
                                  axis=0).astype(BF16)
            y_parts.append(_dot(lhs, rhs))
        y_ref[rows, :] = jnp.concatenate(y_parts, axis=1)

    for c in chunks:
        expand_pass(c)
        local_pass(c)

    half = HP // SSD_GROUPS
    for s in range(n_seq):
        states = []
        for d in range(N_DIR):
            if zero_init:
                states.append(jnp.zeros((LANES, HP), F32))
            else:
                s0 = init_ref[s, d]
                states.append(jnp.where(own_t[:, :HP], jnp.concatenate([s0, s0], axis=1).T, 0.0))

        for k in range(nc):
            for d in range(N_DIR):
                c = s * nc + (k if d == 0 else nc - 1 - k)
                rows = slice(c * q, (c + 1) * q)
                lanes = slice(d * HP, (d + 1) * HP)
                if zero_init and k == 0:
                    states[d] = dst_s[c, :, lanes]
                    continue
                y_off = _dot(c_s[rows, :].astype(BF16), states[d].astype(BF16)) * ec_s[rows, lanes]
                y_ref[rows, :] = y_ref[rows, :] + y_off
                states[d] = states[d] * et_s[c, 0:1, lanes] + dst_s[c, :, lanes]

        for d in range(N_DIR if write_final else 0):
            st = states[d].T
            fin_ref[s, d, 0:half, :] = st[0:half, 0:SSD_STATE]
            fin_ref[s, d, half:HP, :] = st[half:HP, SSD_STATE:2 * SSD_STATE]

        for c in range(s * nc, (s + 1) * nc):
            rows = slice(c * q, (c + 1) * q)
            y = y_ref[rows, :] + xs_s[rows, :] * dexp_ref[...]
            y = y * _silu(z_ref[rows, :])
            ms = jnp.mean(y * y, axis=-1, keepdims=True)
            y_ref[rows, :] = y * lax.rsqrt(ms + RMS_EPS) * nw_ref[...]


SSD_STEP_ROWS = 1024


def _ssd_groups_kernel(*refs, ctx_steps, ctx, lat):
    @pl.when(pl.program_id(0) < ctx_steps)
    def _():
        _ssd_kernel(*refs, l=ctx[0], seg=ctx[1], zero_init=True, write_final=True)

    @pl.when(pl.program_id(0) >= ctx_steps)
    def _():
        _ssd_kernel(*refs, l=lat[0], seg=lat[1], zero_init=False, write_final=False)


def _ssd(z, xbc, dt, n_ctx, l_ctx, seg_ctx, l_lat, seg_lat, init_lat, cw, cb, dtb, alog, dexp, nw, e_mat):
    rows = SSD_STEP_ROWS
    tokens = z.shape[0]
    ctx_steps = n_ctx * l_ctx // rows
    ctx_seqs, lat_seqs = rows // l_ctx, rows // l_lat
    seq_spec = lambda n: pl.BlockSpec((rows, n), lambda b: (b, 0))
    const = lambda *shape: pl.BlockSpec(shape, lambda b: (0,) * len(shape))
    nc = rows // SSD_CHUNK
    return pl.pallas_call(
        functools.partial(_ssd_groups_kernel, ctx_steps=ctx_steps, ctx=(l_ctx, seg_ctx), lat=(l_lat, seg_lat)),
        grid=(tokens // rows,),
        in_specs=[seq_spec(SSD_W), seq_spec(SSD_XBC), pl.BlockSpec((N_DT, rows), lambda b: (0, b)),
                  pl.BlockSpec((lat_seqs, N_DIR, HP, SSD_STATE), lambda b: (jnp.maximum(b - ctx_steps, 0), 0, 0, 0)),
                  const(3, SSD_XBC), const(1, SSD_XBC), const(N_DT, 1), const(N_DT, 1),
                  const(1, SSD_W), const(1, SSD_W), const(*e_mat.shape)],
        out_specs=[pl.BlockSpec((rows, SSD_W), lambda b: (b, 0)),
                   pl.BlockSpec((ctx_seqs, N_DIR, HP, SSD_STATE), lambda b: (jnp.minimum(b, ctx_steps - 1), 0, 0, 0))],
        out_shape=[jax.ShapeDtypeStruct((tokens, SSD_W), F32),
                   jax.ShapeDtypeStruct((n_ctx, N_DIR, HP, SSD_STATE), F32)],
        scratch_shapes=[pltpu.VMEM((rows, SSD_W), F32), pltpu.VMEM((rows, LANES), F32), pltpu.VMEM((rows, LANES), F32),
                        pltpu.VMEM((rows, N_DIR * HP), F32),
                        pltpu.VMEM((nc, LANES, N_DIR * HP), F32), pltpu.VMEM((nc, SUBLANES, N_DIR * HP), F32),
                        pltpu.VMEM((nc, SSD_CHUNK, LANES), F32), pltpu.VMEM((nc, N_DT, SSD_CHUNK), F32)],
        compiler_params=_params("arbitrary"),
        name="ssd",
    )(z, xbc, dt, init_lat, cw, cb, dtb, alog, dexp, nw, e_mat)


HY_ROW_BLOCK = 16
HY_STEP_ROWS = 2048


def _hyena_kernel(v_ref, x1_ref, x2_ref, wv_ref, w1_ref, w2_ref, bv_ref, b1_ref, b2_ref,
                  f_ref, g_ref, kf_ref, kn_ref, skip_ref, o_ref, spec_s, prod_s, *, l, seg):
    m = l // 2
    tiles = o_ref.shape[0]
    n_seq = o_ref.shape[1] // l
    ct = tiles * LANES
    rb = HY_ROW_BLOCK
    first, last = _seg_edges(m, seg // 2)
    row0 = lax.broadcasted_iota(jnp.int32, (rb, 1), 0) == 0
    ev = slice(0, ct)
    od = slice(ct, 2 * ct)

    def conv_eo(x_ref, w_ref, b_ref, base):
        xe, xo = (jnp.concatenate([x_ref[k, pl.ds(base + p, m, stride=2), :] for k in range(tiles)], axis=1)
                  for p in range(2))
        w = w_ref[...]
        b = b_ref[...]
        xo_prev = jnp.where(first, 0.0, pltpu.roll(xo, 1, 0))
        xe_next = jnp.where(last, 0.0, pltpu.roll(xe, m - 1, 0))
        ce = xo_prev * w[0:1, :] + xe * w[1:2, :] + xo * w[2:3, :] + b
        co = xe * w[0:1, :] + xo * w[1:2, :] + xe_next * w[2:3, :] + b
        return jnp.concatenate([ce, co], axis=1)

    def pointwise(s, i, r0):
        re = slice(s * l + r0, s * l + r0 + rb)
        im = slice(s * l + m + r0, s * l + m + r0 + rb)
        er, orr = spec_s[re, ev], spec_s[re, od]
        ei, oi = spec_s[im, ev], spec_s[im, od]
        ker, kei, kor, koi, vr, vi = (kf_ref[i, p, r0:r0 + rb, :] for p in range(N_FILT_PLANES))
        if r0 == 0:
            e_n, o_n = ei[0:1, :].astype(F32), oi[0:1, :].astype(F32)
            zero = jnp.zeros((), BF16)
            ei = jnp.where(row0, zero, ei)
            oi = jnp.where(row0, zero, oi)
        pe_r = er * ker - ei * kei + orr * vr - oi * vi
        pe_i = er * kei + ei * ker + orr * vi + oi * vr
        po_r = er * kor - ei * koi + orr * ker - oi * kei
        po_i = er * koi + ei * kor + orr * kei + oi * ker
        if r0 == 0:
            kn = kn_ref[i]
            pe_i = jnp.where(row0, (e_n * kn[0:1, :] + o_n * kn[2:3, :]).astype(BF16), pe_i)
            po_i = jnp.where(row0, (e_n * kn[1:2, :] + o_n * kn[0:1, :]).astype(BF16), po_i)
        prod_s[re, ev] = pe_r
        prod_s[re, od] = po_r
        prod_s[im, ev] = pe_i
        prod_s[im, od] = po_i

    seqs = range(n_seq)
    zz = [conv_eo(v_ref, wv_ref, bv_ref, s * l) for s in seqs]
    for i, (xg_ref, wg_ref, bg_ref) in enumerate(((x1_ref, w1_ref, b1_ref), (x2_ref, w2_ref, b2_ref))):
        for s in seqs:
            spec_s[s * l:(s + 1) * l, :] = _dot(f_ref[...], zz[s].astype(BF16)).astype(BF16)
        for s in seqs:
            for r0 in range(0, m, rb):
                pointwise(s, i, r0)
        skip = skip_ref[i:i + 1, :]
        skip2 = jnp.concatenate([skip, skip], axis=1)
        for s in seqs:
            conv = _dot(g_ref[...], prod_s[s * l:(s + 1) * l, :])
            zz[s] = conv_eo(xg_ref, wg_ref, bg_ref, s * l) * (conv + zz[s] * skip2)
    for s in seqs:
        for k in range(tiles):
            o_ref[k, pl.ds(s * l, m, stride=2), :] = zz[s][:, k * LANES:(k + 1) * LANES]
            o_ref[k, pl.ds(s * l + 1, m, stride=2), :] = zz[s][:, ct + k * LANES:ct + (k + 1) * LANES]


def _hyena_groups_kernel(v_ref, x1_ref, x2_ref, wv_ref, w1_ref, w2_ref, bv_ref, b1_ref, b2_ref,
                         fc_ref, gc_ref, kfc_ref, knc_ref, fl_ref, gl_ref, kfl_ref, knl_ref, skip_ref,
                         o_ref, spec_s, prod_s, *, ctx_steps, ctx, lat):
    common = (v_ref, x1_ref, x2_ref, wv_ref, w1_ref, w2_ref, bv_ref, b1_ref, b2_ref)

    @pl.when(pl.program_id(1) < ctx_steps)
    def _():
        _hyena_kernel(*common, fc_ref, gc_ref, kfc_ref, knc_ref, skip_ref, o_ref, spec_s, prod_s, l=ctx[0], seg=ctx[1])

    @pl.when(pl.program_id(1) >= ctx_steps)
    def _():
        _hyena_kernel(*common, fl_ref, gl_ref, kfl_ref, knl_ref, skip_ref, o_ref, spec_s, prod_s, l=lat[0], seg=lat[1])


def _hyena(hy, n_ctx, l_ctx, seg_ctx, l_lat, seg_lat, cw, cb, consts_ctx, consts_lat, skip):
    ct = HY_CH_TILE
    nct = HY_W // ct
    tiles = ct // LANES
    rows = HY_STEP_ROWS
    tokens = hy.shape[1]
    ctx_steps = n_ctx * l_ctx // rows
    part = lambda p: pl.BlockSpec((tiles, rows, LANES), lambda j, b: (p * nct + j, b, 0))
    wpart = lambda p: pl.BlockSpec((3, ct), lambda j, b: (0, p * nct + j))
    bpart = lambda p: pl.BlockSpec((1, ct), lambda j, b: (0, p * nct + j))

    def const_specs(l):
        m = l // 2
        return [pl.BlockSpec((l, m), lambda j, b: (0, 0)),
                pl.BlockSpec((m, l), lambda j, b: (0, 0)),
                pl.BlockSpec((HY_ORDER, N_FILT_PLANES, m, ct), lambda j, b: (0, 0, 0, j)),
                pl.BlockSpec((HY_ORDER, SUBLANES, ct), lambda j, b: (0, 0, j))]

    return pl.pallas_call(
        functools.partial(_hyena_groups_kernel, ctx_steps=ctx_steps, ctx=(l_ctx, seg_ctx), lat=(l_lat, seg_lat)),
        grid=(nct, tokens // rows),
        in_specs=[part(0), part(1), part(2), wpart(0), wpart(1), wpart(2), bpart(0), bpart(1), bpart(2)]
        + const_specs(l_ctx) + const_specs(l_lat) + [pl.BlockSpec((HY_ORDER, ct), lambda j, b: (0, j))],
        out_specs=pl.BlockSpec((tiles, rows, LANES), lambda j, b: (j, b, 0)),
        out_shape=jax.ShapeDtypeStruct((HY_W // LANES, tokens, LANES), F32),
        scratch_shapes=[pltpu.VMEM((rows, 2 * ct), BF16), pltpu.VMEM((rows, 2 * ct), BF16)],
        compiler_params=_params("arbitrary", "arbitrary"),
        name="hyena",
    )(hy, hy, hy, cw, cw, cw, cb, cb, cb, *consts_ctx, *consts_lat, skip)


def _dft_mats(l):
    n = 2 * l
    f = np.arange(l, dtype=np.int64)[:, None]
    t = np.arange(l, dtype=np.int64)[None, :]
    ang = 2.0 * np.pi * ((f * t) % n).astype(np.float64) / n
    alt = np.where(np.arange(l) % 2 == 0, 1.0, -1.0)
    top = np.cos(ang)
    bot = -np.sin(ang)
    bot[0, :] = alt
    fwd = np.concatenate([top, bot], axis=0)
    wf = np.full((l,), 2.0)
    wf[0] = 1.0
    gtop = np.cos(ang).T * wf[None, :] / n
    gbot = -np.sin(ang).T * 2.0 / n
    gbot[:, 0] = alt / n
    inv = np.concatenate([gtop, gbot], axis=1)
    return fwd.astype(np.float32), inv.astype(np.float32)


def _filter_feats(l):
    t = np.linspace(0.0, 1.0, l)[:, None]
    w = (2.0 * np.pi / l) * np.arange(l, dtype=np.float64)[:, None]
    f = np.linspace(1e-4, HY_BANDS - 1, HY_BANDS)[None, :]
    feats = np.concatenate([t, np.cos(f * w), -np.sin(f * w)], axis=-1)
    out = np.zeros((l, EMB_PAD), np.float32)
    out[:, :HY_EMB] = feats
    return out[0::2], out[1::2]


def _shift_twiddles(l):
    theta = 2.0 * np.pi * np.arange(l // 2, dtype=np.float64) / l
    return np.stack([np.cos(theta), np.sin(theta)], axis=1).astype(np.float32)


def _head_expand():
    n_blocks = LANES // N_DT // 2
    e = np.zeros((n_blocks, LANES, N_DIR * HP), np.float32)
    for k in range(n_blocks):
        for j in range(N_DT):
            e[k, k * N_DT + j, j * SSD_HEAD_DIM:(j + 1) * SSD_HEAD_DIM] = 1.0
    return e


def kernel(x_prompt, x_sample, state_ssd, c, c_ctx, w_ada, b_ada, norm_ffn1, ffn1_w_gate, ffn1_w_up, ffn1_w_down, norm_mix, w_in, w_out, ssd_conv_w, ssd_conv_b, ssd_dt_bias, ssd_a_log, ssd_d, ssd_norm_w, hy_conv_w, hy_conv_b, hy_w1, hy_b1, hy_freq, hy_w2, hy_b2, hy_w3, hy_skip, norm_ffn2, ffn2_w_gate, ffn2_w_up, ffn2_w_down, norm_final):
    assert w_ada.shape[0] == 1, "single layer"
    n_ctx, l_ctx, _ = x_prompt.shape
    n_lat, l_lat, _ = x_sample.shape
    t_ctx = n_ctx * l_ctx

    assert n_lat < COND_ROWS
    cond = jnp.zeros((COND_ROWS, D_MODEL), F32).at[:n_lat].set(c).at[n_lat].set(c_ctx)

    tr = lambda w: jnp.swapaxes(w, 1, 2)

    row = lambda v: v.reshape(1, -1)
    dtb, alog = ssd_dt_bias[0].reshape(N_DT, 1), ssd_a_log[0].reshape(N_DT, 1)
    dexp = jnp.repeat(ssd_d[0], SSD_HEAD_DIM).reshape(1, SSD_W)
    e_mat = jnp.asarray(_head_expand())
    w1p = jnp.pad(hy_w1[0], ((0, EMB_PAD - HY_EMB), (0, 0)))
    deltas = jnp.asarray(np.abs(np.linspace(HY_MIN_DECAY, HY_MAX_DECAY, HY_ORDER * HY_W))
                         .reshape(HY_ORDER, HY_W).astype(np.float32))

    (x1, z, xbc, dt, hy), mod, (wg2, wu2, wd2), wo = _ffn_in(
        x_prompt.reshape(t_ctx, D_MODEL), x_sample.reshape(n_lat * l_lat, D_MODEL), cond, w_ada, b_ada, n_lat, l_lat,
        row(norm_ffn1[0]), row(norm_mix[0]), tr(ffn1_w_gate), tr(ffn1_w_up), ffn1_w_down, tr(w_in),
        [tr(ffn2_w_gate), tr(ffn2_w_up), ffn2_w_down], w_out)

    lengths = (l_ctx, l_lat)
    dft = [tuple(jnp.asarray(a).astype(BF16) for a in _dft_mats(l // 2)) for l in lengths]
    filt = _filters(lengths,
                    [tuple(jnp.asarray(a) for a in _filter_feats(l)) + (dft[k][0], jnp.asarray(_shift_twiddles(l)))
                     for k, l in enumerate(lengths)],
                    w1p, row(hy_b1[0]), row(hy_freq[0]), hy_w2[0], row(hy_b2[0]), hy_w3[0], deltas)
    hyena_consts = [dft[k] + filt[k] for k in range(len(lengths))]

    lat_init = state_ssd[:, 0].reshape(n_lat, N_DIR, HP, SSD_STATE)
    ys, ctx_fin = _ssd(z, xbc, dt, n_ctx, l_ctx, l_ctx, l_lat, GRID_W, lat_init, ssd_conv_w[0], row(ssd_conv_b[0]),
                       dtb, alog, dexp, row(ssd_norm_w[0]), e_mat)
    yh = _hyena(hy, n_ctx, l_ctx, l_ctx, l_lat, GRID_W, hy_conv_w[0], row(hy_conv_b[0]),
                hyena_consts[0], hyena_consts[1], hy_skip[0])

    y_ctx, y_lat = _out_ffn(x1, ys, yh, t_ctx, mod, n_lat, l_lat, row(norm_ffn2[0]), row(norm_final),
                            wo, wg2, wu2, wd2)
    new_state = ctx_fin.reshape(n_ctx, 1, N_DIR, SSD_HEADS, SSD_HEAD_DIM, SSD_STATE).astype(x_prompt.dtype)
    return (y_ctx.reshape(n_ctx, l_ctx, D_MODEL), y_lat.reshape(n_lat, l_lat, D_MODEL), new_state)
```
